```python
import math
import jax, jax.numpy as jnp
from jax import lax
import numpy as np

D_MODEL = 2048
BATCH = 2
SEQ = 4096
DEPTH = 2
DEC_BATCH = 32
DEC_SEQ = 4
PAST_LEN = 16384
PAGE_SIZE = 128

N_MIXERS = 2
N_ATTN_LAYERS = (DEPTH + 1) // 2
N_GDN_LAYERS = DEPTH // 2

HEAD_DIM = 64
N_HEADS = D_MODEL // HEAD_DIM
N_KV_HEADS = N_HEADS // 8
GROUP = N_HEADS // N_KV_HEADS
WINDOW = 128
BLOCK = 128
N_BUCKETS = 32
MAX_DISTANCE = 128
NEG_INF = -1e30

GDN_K_HEADS = 16
GDN_V_HEADS = 32
GDN_DK = 128
GDN_DV = 128
GDN_KEY_DIM = GDN_K_HEADS * GDN_DK
GDN_VAL_DIM = GDN_V_HEADS * GDN_DV
GDN_CONV_DIM = 2 * GDN_KEY_DIM + GDN_VAL_DIM
GDN_IN_DIM = GDN_CONV_DIM + GDN_VAL_DIM + 2 * GDN_V_HEADS
GDN_CONV = 4
GDN_CHUNK = 64

D_FF = 5632
FFN_CONV = 3

NORM_EPS = 1e-6

kernel_name = 'hybrid_swa_sink_gdn_convffn_step'


def rms_norm(x, gain):
    xf = x.astype(jnp.float32)
    y = xf * lax.rsqrt(jnp.mean(xf * xf, axis=-1, keepdims=True) + NORM_EPS)
    return (y * gain.astype(jnp.float32)).astype(x.dtype)


def l2_norm(x):
    xf = x.astype(jnp.float32)
    return xf * lax.rsqrt(jnp.sum(xf * xf, axis=-1, keepdims=True) + NORM_EPS)


def modulate(h, shift, scale):
    return h * (1 + scale[:, None, :]) + shift[:, None, :]


def causal_dwconv(x, hist, w, bias=None):
    width = w.shape[0]
    L = x.shape[1]
    xp = jnp.concatenate([hist.astype(x.dtype), x], axis=1)
    y = xp[:, 0:L] * w[0]
    for j in range(1, width):
        y = y + xp[:, j:j + L] * w[j]
    if bias is not None:
        y = y + bias
    return y, xp[:, L:]


def t5_bucket(dist):
    max_exact = N_BUCKETS // 2
    d = jnp.maximum(dist, 0)
    df = jnp.maximum(d, 1).astype(jnp.float32)
    large = max_exact + (jnp.log(df / max_exact) / math.log(MAX_DISTANCE / max_exact)
                         * (N_BUCKETS - max_exact)).astype(jnp.int32)
    large = jnp.minimum(large, N_BUCKETS - 1)
    return jnp.where(d < max_exact, d, large)


def rel_bias(dist, table):
    b = table.astype(jnp.float32)[t5_bucket(dist)]
    return jnp.moveaxis(b, -1, 0).reshape(N_KV_HEADS, GROUP, *dist.shape)


def softmax_with_sink(s, sinks):
    sink = sinks.astype(jnp.float32).reshape(N_KV_HEADS, GROUP, 1, 1)
    m = jnp.maximum(jnp.max(s, axis=-1, keepdims=True), sink)
    p = jnp.exp(s - m)
    return p / (jnp.sum(p, axis=-1, keepdims=True) + jnp.exp(sink - m))


def attn_qkv(h, w_qkv, q_gain, k_gain):
    B, L, _ = h.shape
    qkv = h @ w_qkv
    q, k, v = jnp.split(qkv, [N_HEADS * HEAD_DIM, (N_HEADS + N_KV_HEADS) * HEAD_DIM], axis=-1)
    q = rms_norm(q.reshape(B, L, N_KV_HEADS, GROUP, HEAD_DIM), q_gain)
    k = rms_norm(k.reshape(B, L, N_KV_HEADS, HEAD_DIM), k_gain)
    v = v.reshape(B, L, N_KV_HEADS, HEAD_DIM)
    return q, k, v


def attn_prompt(h, w_qkv, q_gain, k_gain, sinks, w_o, rel_table):
    B, L, _ = h.shape
    nb = L // BLOCK
    q, k, v = attn_qkv(h, w_qkv, q_gain, k_gain)
    qb = q.reshape(B, nb, BLOCK, N_KV_HEADS, GROUP, HEAD_DIM)
    kb = k.reshape(B, nb, BLOCK, N_KV_HEADS, HEAD_DIM)
    vb = v.reshape(B, nb, BLOCK, N_KV_HEADS, HEAD_DIM)
    prev = lambda t: jnp.pad(t, ((0, 0), (1, 0), (0, 0), (0, 0), (0, 0)))[:, :-1]
    k2 = jnp.concatenate([prev(kb), kb], axis=2)
    v2 = jnp.concatenate([prev(vb), vb], axis=2)
    qi = jnp.arange(BLOCK)[:, None]
    sj = jnp.arange(2 * BLOCK)[None, :]
    dist = qi + BLOCK - sj
    in_band = (dist >= 0) & (dist <= WINDOW)
    has_prev = (jnp.arange(nb)[:, None, None] > 0) | (sj >= BLOCK)[None]
    mask = in_band[None] & has_prev
    s = jnp.einsum('bnqkgd,bnskd->bnkgqs', qb, k2,
                   preferred_element_type=jnp.float32) * HEAD_DIM ** -0.5
    s = s + rel_bias(dist, rel_table)
    s = jnp.where(mask[None, :, None, None], s, NEG_INF)
    p = softmax_with_sink(s, sinks)
    o = jnp.einsum('bnkgqs,bnskd->bnqkgd', p.astype(v.dtype), v2)
    y = o.reshape(B, L, N_HEADS * HEAD_DIM) @ w_o
    keep = min(WINDOW, PAST_LEN)
    return y, k[:, L - keep:], v[:, L - keep:]


def attn_sample(h, buf_k, buf_v, w_qkv, q_gain, k_gain, sinks, w_o, rel_table):
    B, T, _ = h.shape
    W = buf_k.shape[1]
    q, k, v = attn_qkv(h, w_qkv, q_gain, k_gain)
    k_all = jnp.concatenate([buf_k.astype(k.dtype), k], axis=1)
    v_all = jnp.concatenate([buf_v.astype(v.dtype), v], axis=1)
    q_pos = PAST_LEN + jnp.arange(T)
    k_pos = PAST_LEN - W + jnp.arange(W + T)
    dist = q_pos[:, None] - k_pos[None, :]
    mask = (dist >= 0) & (dist <= WINDOW)
    s = jnp.einsum('bqkgd,bskd->bkgqs', q, k_all,
                   preferred_element_type=jnp.float32) * HEAD_DIM ** -0.5
    s = jnp.where(mask, s + rel_bias(dist, rel_table), NEG_INF)
    p = softmax_with_sink(s, sinks)
    o = jnp.einsum('bkgqs,bskd->bqkgd', p.astype(v.dtype), v_all)
    y = o.reshape(B, T, N_HEADS * HEAD_DIM) @ w_o
    return y, k_all[:, T:], v_all[:, T:]


def gated_delta_chunked(q, k, v, beta, g, state0):
    f32 = jnp.float32
    B, L, H, DK = q.shape
    DV = v.shape[-1]
    C = min(GDN_CHUNK, L)
    pad = (-L) % C
    q = q.astype(f32) * DK ** -0.5
    k = k.astype(f32)
    v = v.astype(f32)
    beta = beta.astype(f32)
    g = g.astype(f32)
    if pad:
        p4 = ((0, 0), (0, pad), (0, 0), (0, 0))
        q, k, v = jnp.pad(q, p4), jnp.pad(k, p4), jnp.pad(v, p4)
        beta, g = jnp.pad(beta, p4[:3]), jnp.pad(g, p4[:3])
    N = (L + pad) // C
    chunk = lambda t: jnp.moveaxis(t.reshape(B, N, C, H, *t.shape[3:]), 3, 1)
    q, k, v, beta, g = chunk(q), chunk(k), chunk(v), chunk(beta), chunk(g)
    g = jnp.cumsum(g, axis=-1)
    k_beta = k * beta[..., None]
    v_beta = v * beta[..., None]
    tril = jnp.tril(jnp.ones((C, C), dtype=bool))
    strict = jnp.tril(jnp.ones((C, C), dtype=bool), -1)
    decay = jnp.exp(jnp.where(tril, g[..., :, None] - g[..., None, :], -jnp.inf))
    A = jnp.where(strict, jnp.einsum('bhnid,bhnjd->bhnij', k_beta, k) * decay, 0.0)
    eye = jnp.eye(C, dtype=f32)
    T = lax.linalg.triangular_solve(eye + A, jnp.broadcast_to(eye, A.shape),
                                    left_side=True, lower=True, unit_diagonal=True)
    u = T @ v_beta
    w = T @ (k_beta * jnp.exp(g)[..., None])
    qk = jnp.einsum('bhnid,bhnjd->bhnij', q, k) * decay

    def step(S, xs):
        q_c, k_c, u_c, w_c, g_c, qk_c = xs
        v_new = u_c - w_c @ S
        o = (q_c * jnp.exp(g_c)[..., None]) @ S + qk_c @ v_new
        g_last = g_c[..., -1]
        S = S * jnp.exp(g_last)[..., None, None] + jnp.einsum(
            'bhcd,bhce->bhde', k_c * jnp.exp(g_last[..., None] - g_c)[..., None], v_new)
        return S, o

    xs = tuple(jnp.moveaxis(t, 2, 0) for t in (q, k, u, w, g, qk))
    S, o = lax.scan(step, state0.astype(f32), xs)
    o = jnp.transpose(o, (1, 0, 3, 2, 4)).reshape(B, N * C, H, DV)[:, :L]
    return o, S


def gdn_mixer(h, S0, conv0, w_in, conv_w, a_log, dt_bias, out_norm, w_out):
    B, L, _ = h.shape
    proj = h @ w_in
    mixed, z, b, a = jnp.split(
        proj, [GDN_CONV_DIM, GDN_CONV_DIM + GDN_VAL_DIM, GDN_CONV_DIM + GDN_VAL_DIM + GDN_V_HEADS], axis=-1)
    mixed, conv1 = causal_dwconv(mixed, conv0, conv_w)
    mixed = jax.nn.silu(mixed)
    q, k, v = jnp.split(mixed, [GDN_KEY_DIM, 2 * GDN_KEY_DIM], axis=-1)
    rep = GDN_V_HEADS // GDN_K_HEADS
    q = jnp.repeat(l2_norm(q.reshape(B, L, GDN_K_HEADS, GDN_DK)), rep, axis=2)
    k = jnp.repeat(l2_norm(k.reshape(B, L, GDN_K_HEADS, GDN_DK)), rep, axis=2)
    v = v.reshape(B, L, GDN_V_HEADS, GDN_DV)
    beta = jax.nn.sigmoid(b.astype(jnp.float32))
    g = -jnp.exp(a_log.astype(jnp.float32)) * jax.nn.softplus(
        a.astype(jnp.float32) + dt_bias.astype(jnp.float32))
    o, S1 = gated_delta_chunked(q, k, v, beta, g, S0)
    o = rms_norm(o, out_norm) * jax.nn.silu(z.reshape(B, L, GDN_V_HEADS, GDN_DV).astype(jnp.float32))
    y = o.reshape(B, L, GDN_VAL_DIM).astype(h.dtype) @ w_out
    return y, S1, conv1


def conv_ffn(h, hist, w_up, conv_w, conv_b, w_down):
    u = h @ w_up
    y, hist1 = causal_dwconv(u, hist, conv_w, conv_b)
    gate, val = jnp.split(y, 2, axis=-1)
    return (jax.nn.silu(gate) * val) @ w_down, hist1


def trunk(x, c, win_k, win_v, gdn_state, gdn_conv, ffn_conv, P):
    sample = win_k is not None
    B = x.shape[0]
    new_k, new_v, new_s, new_gc, new_fc = [], [], [], [], []
    for layer in range(DEPTH):
        mod = jax.nn.silu(c) @ P['w_ada'][layer] + P['b_ada'][layer]
        sh_m, sc_m, g_m, sh_f, sc_f, g_f = jnp.split(mod, 6, axis=-1)
        h = modulate(rms_norm(x, P['norm_mix'][layer]), sh_m, sc_m)
        i = layer // N_MIXERS
        if layer % N_MIXERS == 0:
            args = (P['w_attn_qkv'][i], P['attn_q_norm'][i], P['attn_k_norm'][i],
                    P['attn_sinks'][i], P['w_attn_o'][i], P['rel_bias_table'])
            if sample:
                out, kk, vv = attn_sample(h, win_k[i], win_v[i], *args)
            else:
                out, kk, vv = attn_prompt(h, *args)
            new_k.append(kk)
            new_v.append(vv)
        else:
            if sample:
                s0, c0 = gdn_state[i], gdn_conv[i]
            else:
                s0 = jnp.zeros((B, GDN_V_HEADS, GDN_DK, GDN_DV), jnp.float32)
                c0 = jnp.zeros((B, GDN_CONV - 1, GDN_CONV_DIM), x.dtype)
            out, s1, c1 = gdn_mixer(h, s0, c0, P['w_gdn_in'][i], P['gdn_conv_w'][i], P['gdn_a_log'][i],
                                    P['gdn_dt_bias'][i], P['gdn_out_norm'][i], P['w_gdn_out'][i])
            new_s.append(s1)
            new_gc.append(c1)
        x = x + g_m[:, None, :] * out
        h = modulate(rms_norm(x, P['norm_ffn'][layer]), sh_f, sc_f)
        f0 = ffn_conv[layer] if sample else jnp.zeros((B, FFN_CONV - 1, 2 * D_FF), x.dtype)
        out, f1 = conv_ffn(h, f0, P['w_ffn_up'][layer], P['ffn_conv_w'][layer],
                           P['ffn_conv_b'][layer], P['w_ffn_down'][layer])
        new_fc.append(f1)
        x = x + g_f[:, None, :] * out
    return (x, jnp.stack(new_k), jnp.stack(new_v), jnp.stack(new_s),
            jnp.stack(new_gc), jnp.stack(new_fc))


def setup_inputs(seed: int = 0) -> dict:
    key = jax.random.key(seed)
    ks = jax.random.split(key, 32)
    f32 = jnp.float32
    nrm = lambda k, shape, scale: jax.random.normal(k, shape, f32) * scale
    win_len = min(WINDOW, PAST_LEN)
    dt = jax.random.uniform(ks[22], (N_GDN_LAYERS, GDN_V_HEADS), f32, 1e-3, 0.1)
    return {
        'x_prompt': nrm(ks[0], (BATCH, SEQ, D_MODEL), 1.0),
        'x_sample': nrm(ks[1], (DEC_BATCH, DEC_SEQ, D_MODEL), 1.0),
        'c_prompt': nrm(ks[2], (BATCH, D_MODEL), 1.0),
        'c_sample': nrm(ks[3], (DEC_BATCH, D_MODEL), 1.0),
        'cache_win_k': nrm(ks[4], (N_ATTN_LAYERS, DEC_BATCH, win_len, N_KV_HEADS, HEAD_DIM), 1.0),
        'cache_win_v': nrm(ks[5], (N_ATTN_LAYERS, DEC_BATCH, win_len, N_KV_HEADS, HEAD_DIM), 1.0),
        'state_gdn': nrm(ks[6], (N_GDN_LAYERS, DEC_BATCH, GDN_V_HEADS, GDN_DK, GDN_DV), 0.5),
        'state_gdn_conv': nrm(ks[7], (N_GDN_LAYERS, DEC_BATCH, GDN_CONV - 1, GDN_CONV_DIM), 1.0),
        'state_ffn_conv': nrm(ks[8], (DEPTH, DEC_BATCH, FFN_CONV - 1, 2 * D_FF), 1.0),
        'rel_bias_table': nrm(ks[9], (N_BUCKETS, N_HEADS), 0.5),
        'w_ada': nrm(ks[10], (DEPTH, D_MODEL, 6 * D_MODEL), 0.5 * D_MODEL ** -0.5),
        'b_ada': nrm(ks[11], (DEPTH, 6 * D_MODEL), 0.01),
        'norm_mix': 1.0 + nrm(ks[12], (DEPTH, D_MODEL), 0.01),
        'norm_ffn': 1.0 + nrm(ks[13], (DEPTH, D_MODEL), 0.01),
        'w_attn_qkv': nrm(ks[14], (N_ATTN_LAYERS, D_MODEL, (N_HEADS + 2 * N_KV_HEADS) * HEAD_DIM), D_MODEL ** -0.5),
        'attn_q_norm': 1.0 + nrm(ks[15], (N_ATTN_LAYERS, HEAD_DIM), 0.01),
        'attn_k_norm': 1.0 + nrm(ks[16], (N_ATTN_LAYERS, HEAD_DIM), 0.01),
        'attn_sinks': nrm(ks[17], (N_ATTN_LAYERS, N_HEADS), 0.5),
        'w_attn_o': nrm(ks[18], (N_ATTN_LAYERS, N_HEADS * HEAD_DIM, D_MODEL), (N_HEADS * HEAD_DIM) ** -0.5),
        'w_gdn_in': nrm(ks[19], (N_GDN_LAYERS, D_MODEL, GDN_IN_DIM), D_MODEL ** -0.5),
        'gdn_conv_w': nrm(ks[20], (N_GDN_LAYERS, GDN_CONV, GDN_CONV_DIM), GDN_CONV ** -0.5),
        'gdn_a_log': jnp.log(jax.random.uniform(ks[21], (N_GDN_LAYERS, GDN_V_HEADS), f32, 1.0, 16.0)),
        'gdn_dt_bias': dt + jnp.log(-jnp.expm1(-dt)),
        'gdn_out_norm': 1.0 + nrm(ks[23], (N_GDN_LAYERS, GDN_DV), 0.01),
        'w_gdn_out': nrm(ks[24], (N_GDN_LAYERS, GDN_VAL_DIM, D_MODEL), GDN_VAL_DIM ** -0.5),
        'w_ffn_up': nrm(ks[25], (DEPTH, D_MODEL, 2 * D_FF), D_MODEL ** -0.5),
        'ffn_conv_w': nrm(ks[26], (DEPTH, FFN_CONV, 2 * D_FF), FFN_CONV ** -0.5),
        'ffn_conv_b': nrm(ks[27], (DEPTH, 2 * D_FF), 0.01),
        'w_ffn_down': nrm(ks[28], (DEPTH, D_FF, D_MODEL), D_FF ** -0.5),
    }


def reference(x_prompt, x_sample, c_prompt, c_sample,
              cache_win_k, cache_win_v, state_gdn, state_gdn_conv, state_ffn_conv,
              rel_bias_table, w_ada, b_ada, norm_mix, norm_ffn,
              w_attn_qkv, attn_q_norm, attn_k_norm, attn_sinks, w_attn_o,
              w_gdn_in, gdn_conv_w, gdn_a_log, gdn_dt_bias, gdn_out_norm, w_gdn_out,
              w_ffn_up, ffn_conv_w, ffn_conv_b, w_ffn_down):
    P = {
        'rel_bias_table': rel_bias_table, 'w_ada': w_ada, 'b_ada': b_ada,
        'norm_mix': norm_mix, 'norm_ffn': norm_ffn,
        'w_attn_qkv': w_attn_qkv, 'attn_q_norm': attn_q_norm, 'attn_k_norm': attn_k_norm,
        'attn_sinks': attn_sinks, 'w_attn_o': w_attn_o,
        'w_gdn_in': w_gdn_in, 'gdn_conv_w': gdn_conv_w, 'gdn_a_log': gdn_a_log,
        'gdn_dt_bias': gdn_dt_bias, 'gdn_out_norm': gdn_out_norm, 'w_gdn_out': w_gdn_out,
        'w_ffn_up': w_ffn_up, 'ffn_conv_w': ffn_conv_w, 'ffn_conv_b': ffn_conv_b,
        'w_ffn_down': w_ffn_down,
    }
    y_prompt, wk_p, wv_p, gdn_p, gconv_p, fconv_p = trunk(
        x_prompt, c_prompt, None, None, None, None, None, P)
    y_sample, wk_s, wv_s, gdn_s, gconv_s, fconv_s = trunk(
        x_sample, c_sample, cache_win_k, cache_win_v, state_gdn, state_gdn_conv, state_ffn_conv, P)
    return (y_prompt, y_sample, wk_p, wv_p, wk_s, wv_s, gdn_p, gdn_s, gconv_p, gconv_s, fconv_p, fconv_s)
```

```python
import functools
import math

import numpy as np
import jax
import jax.numpy as jnp
from jax import lax
from jax.experimental import pallas as pl
from jax.experimental.pallas import tpu as pltpu

HEAD_DIM = 64
N_HEADS = 32
N_KV_HEADS = 4
ATT_GROUP = N_HEADS // N_KV_HEADS
WINDOW = 128
ATT_BLOCK = 128
N_BUCKETS = 32
MAX_DISTANCE = 128
NEG_INF = -1e30
PAST_LEN = 16384

GDN_K_HEADS = 16
GDN_V_HEADS = 32
GDN_DK = 128
GDN_DV = 128
GDN_KEY_DIM = GDN_K_HEADS * GDN_DK
GDN_VAL_DIM = GDN_V_HEADS * GDN_DV
GDN_CONV_DIM = 2 * GDN_KEY_DIM + GDN_VAL_DIM
GDN_CONV = 4
GDN_CHUNK = 64
GDN_HEAD_GROUPS = 4
GDN_VH_PER_GROUP = GDN_V_HEADS // GDN_HEAD_GROUPS
GDN_KH_PER_GROUP = GDN_K_HEADS // GDN_HEAD_GROUPS

FFN_CONV = 3
NORM_EPS = 1e-6

LANES = 128
SUBLANES = 8
VMEM_CAP_BYTES = 60 * 1024 * 1024
VMEM_SLACK_BYTES = 8 * 1024 * 1024

ROW_TILE_CAP = 1024
GDN_ROW_TILE_CAP = 512

BF16 = jnp.bfloat16
F32 = jnp.float32


def _vmem_limit(block_bytes, scratch_bytes=0):
    est = 2 * int(block_bytes) + int(scratch_bytes) + VMEM_SLACK_BYTES
    return int(min(max(est, 16 * 1024 * 1024), VMEM_CAP_BYTES))


def _params(n_grid, block_bytes, scratch_bytes=0):
    return pltpu.CompilerParams(
        dimension_semantics=("arbitrary",) * n_grid,
        vmem_limit_bytes=_vmem_limit(block_bytes, scratch_bytes))


def _nbytes(shape, dtype):
    return int(np.prod(shape)) * jnp.dtype(dtype).itemsize


def _silu(x):
    return x * (1.0 / (1.0 + jnp.exp(-x)))


def _dot(a, b):
    return jnp.dot(a, b, preferred_element_type=F32)


def _dot_nt(a, b):
    return lax.dot_general(a, b, (((1,), (1,)), ((), ())), preferred_element_type=F32)


def _dot_tn(a, b):
    return lax.dot_general(a, b, (((0,), (0,)), ((), ())), preferred_element_type=F32)


def _norm_mod(x, gain, shift, scale):
    ms = jnp.mean(x * x, axis=-1, keepdims=True)
    y = x * lax.rsqrt(ms + NORM_EPS) * gain
    return y * (1.0 + scale) + shift


def _mod_specs(shift, tm):
    d = shift.shape[-1]
    if shift.shape[1] == 1:
        return pl.BlockSpec((1, 1, d), lambda b, i, j: (b, 0, 0))
    return pl.BlockSpec((1, tm, d), lambda b, i, j: (b, i, 0))


def _ada_kernel(c_ref, w_ref, b_ref, o_ref):
    a = _silu(c_ref[...]).astype(BF16)
    o_ref[0] = _dot(a, w_ref[0].astype(BF16)) + b_ref[0]


def ada_modulation(c_all, w_ada, b_ada, tn=1024):
    rows, d = c_all.shape
    depth, _, n = w_ada.shape
    blocks = _nbytes((rows, d), F32) + _nbytes((d, tn), F32) + _nbytes((rows, tn), F32)
    return pl.pallas_call(
        _ada_kernel,
        grid=(depth, n // tn),
        in_specs=[pl.BlockSpec((rows, d), lambda l, j: (0, 0)),
                  pl.BlockSpec((1, d, tn), lambda l, j: (l, 0, j)),
                  pl.BlockSpec((1, 1, tn), lambda l, j: (l, 0, j))],
        out_specs=pl.BlockSpec((1, rows, tn), lambda l, j: (l, 0, j)),
        out_shape=jax.ShapeDtypeStruct((depth, rows, n), F32),
        compiler_params=_params(2, blocks, _nbytes((d, tn), BF16)),
        name="ada_modulation",
    )(c_all, w_ada, b_ada.reshape(depth, 1, n))


def _qkv_kernel(x_ref, gain_ref, shift_ref, scale_ref, w_ref, hgain_ref, hflag_ref, gmat_ref,
                o_ref, h_s):
    @pl.when(pl.program_id(2) == 0)
    def _():
        h_s[...] = _norm_mod(x_ref[0], gain_ref[...], shift_ref[0], scale_ref[0]).astype(BF16)

    y = _dot(h_s[...], w_ref[...].astype(BF16))
    sq = y * y
    sq_hi = sq.astype(BF16)
    sq_lo = (sq - sq_hi.astype(F32)).astype(BF16)
    ms = _dot(sq_hi, gmat_ref[...]) + _dot(sq_lo, gmat_ref[...])
    yn = y * lax.rsqrt(ms + NORM_EPS) * hgain_ref[...]
    o_ref[0] = jnp.where(hflag_ref[...] > 0.0, yn, y)


def qkv_projection(x, gain, shift, scale, w, q_gain, k_gain, tm, tn=512):
    b, l, d = x.shape
    n = w.shape[1]
    nq, nk = N_HEADS * HEAD_DIM, N_KV_HEADS * HEAD_DIM
    hgain = jnp.concatenate([jnp.tile(q_gain, N_HEADS), jnp.tile(k_gain, N_KV_HEADS),
                             jnp.ones((nk,), F32)]).reshape(1, n)
    hflag = jnp.concatenate([jnp.ones((nq + nk,), F32), jnp.zeros((nk,), F32)]).reshape(1, n)
    gidx = np.arange(tn) // HEAD_DIM
    gmat = jnp.asarray((gidx[:, None] == gidx[None, :]).astype(np.float32) / HEAD_DIM, BF16)
    blocks = (_nbytes((tm, d), F32) + _nbytes((d, tn), F32) + _nbytes((tm, tn), F32)
              + 2 * _nbytes((shift.shape[1] == 1 and 1 or tm, d), F32) + _nbytes((tn, tn), BF16))
    scratch = _nbytes((tm, d), BF16) + _nbytes((d, tn), BF16) + 4 * _nbytes((tm, tn), F32)
    return pl.pallas_call(
        _qkv_kernel,
        grid=(b, l // tm, n // tn),
        in_specs=[pl.BlockSpec((1, tm, d), lambda b_, i, j: (b_, i, 0)),
                  pl.BlockSpec((1, d), lambda b_, i, j: (0, 0)),
                  _mod_specs(shift, tm), _mod_specs(scale, tm),
                  pl.BlockSpec((d, tn), lambda b_, i, j: (0, j)),
                  pl.BlockSpec((1, tn), lambda b_, i, j: (0, j)),
                  pl.BlockSpec((1, tn), lambda b_, i, j: (0, j)),
                  pl.BlockSpec((tn, tn), lambda b_, i, j: (0, 0))],
        out_specs=pl.BlockSpec((1, tm, tn), lambda b_, i, j: (b_, i, j)),
        out_shape=jax.ShapeDtypeStruct((b, l, n), F32),
        scratch_shapes=[pltpu.VMEM((tm, d), BF16)],
        compiler_params=_params(3, blocks, scratch),
        name="qkv_projection",
    )(x, gain.reshape(1, d), shift, scale, w, hgain, hflag, gmat)


def _attn_prompt_kernel(sink_ref, q_ref, cur_ref, prev_ref, bprev_ref, bcur_ref, o_ref, kz_s, vz_s):
    n = pl.program_id(1)
    blk = ATT_BLOCK
    lane = lax.broadcasted_iota(jnp.int32, (2 * blk, LANES), 1)
    lo_half = lane < HEAD_DIM
    neg_prev = jnp.where(n == 0, NEG_INF, 0.0).astype(F32)

    kv_width = N_KV_HEADS * HEAD_DIM
    for pair in range(N_KV_HEADS // 2):
        for part, dst in ((0, kz_s), (1, vz_s)):
            col = part * kv_width + pair * LANES
            both = jnp.concatenate([prev_ref[0, :, col:col + LANES], cur_ref[0, :, col:col + LANES]], axis=0)
            swapped = pltpu.roll(both, HEAD_DIM, axis=1)
            zero = jnp.zeros_like(both)
            c0, c1 = 2 * pair, 2 * pair + 1
            dst[2 * c0 + 0] = jnp.where(lo_half, both, zero).astype(BF16)
            dst[2 * c0 + 1] = jnp.where(lo_half, zero, swapped).astype(BF16)
            dst[2 * c1 + 0] = jnp.where(lo_half, swapped, zero).astype(BF16)
            dst[2 * c1 + 1] = jnp.where(lo_half, zero, both).astype(BF16)

    scale = HEAD_DIM ** -0.5
    for p in range(N_HEADS // 2):
        c = (2 * p) // ATT_GROUP
        qp = (q_ref[0, :, p * LANES:(p + 1) * LANES] * scale).astype(BF16)
        o_pair = None
        for a in range(2):
            h = 2 * p + a
            sink = sink_ref[h]
            s = _dot_nt(qp, kz_s[2 * c + a])
            s_prev = s[:, :blk] + bprev_ref[h] + neg_prev
            s_cur = s[:, blk:] + bcur_ref[h]
            m = jnp.maximum(jnp.maximum(jnp.max(s_prev, axis=-1, keepdims=True),
                                        jnp.max(s_cur, axis=-1, keepdims=True)), sink)
            e_prev = jnp.exp(s_prev - m)
            e_cur = jnp.exp(s_cur - m)
            den = (jnp.sum(e_prev, axis=-1, keepdims=True) + jnp.sum(e_cur, axis=-1, keepdims=True)
                   + jnp.exp(sink - m))
            pm = jnp.concatenate([e_prev, e_cur], axis=1).astype(BF16)
            o_a = _dot(pm, vz_s[2 * c + a]) * (1.0 / den)
            o_pair = o_a if o_pair is None else o_pair + o_a
        o_ref[0, :, p * LANES:(p + 1) * LANES] = o_pair.astype(o_ref.dtype)


def _t5_bucket_np(dist):
    max_exact = N_BUCKETS // 2
    d = np.maximum(dist, 0)
    df = np.maximum(d, 1).astype(np.float32)
    large = max_exact + (np.log(df / np.float32(max_exact)) / np.float32(math.log(MAX_DISTANCE / max_exact))
                         * np.float32(N_BUCKETS - max_exact)).astype(np.int32)
    large = np.minimum(large, N_BUCKETS - 1)
    return np.where(d < max_exact, d, large)


def _bias_from_dist(dist, in_band, rel_table):
    bucket = jnp.asarray(_t5_bucket_np(dist).reshape(-1), jnp.int32)
    tab = jnp.take(rel_table.astype(F32).T, bucket, axis=1).reshape(N_HEADS, *dist.shape)
    return jnp.where(jnp.asarray(in_band)[None], tab, NEG_INF)


def attention_prompt(qkv, sinks, rel_table):
    b, l, n = qkv.shape
    blk = ATT_BLOCK
    nq = N_HEADS * HEAD_DIM
    kvw = 2 * N_KV_HEADS * HEAD_DIM
    kv_blk = nq // kvw
    qi = np.arange(blk)[:, None]
    sj = np.arange(blk)[None, :]
    d_prev = qi + blk - sj
    d_cur = qi - sj
    bias_prev = _bias_from_dist(d_prev, (d_prev >= 0) & (d_prev <= WINDOW), rel_table)
    bias_cur = _bias_from_dist(d_cur, (d_cur >= 0) & (d_cur <= WINDOW), rel_table)
    blocks = (_nbytes((blk, nq), F32) + 2 * _nbytes((blk, kvw), F32) + _nbytes((blk, nq), BF16))
    scratch = 2 * _nbytes((2 * N_KV_HEADS, 2 * blk, LANES), BF16) + 4 * _nbytes((N_HEADS, blk, blk), F32)
    return pl.pallas_call(
        _attn_prompt_kernel,
        grid=(b, l // blk),
        in_specs=[pl.BlockSpec(memory_space=pltpu.SMEM),
                  pl.BlockSpec((1, blk, nq), lambda b_, i: (b_, i, 0)),
                  pl.BlockSpec((1, blk, kvw), lambda b_, i: (b_, i, kv_blk)),
                  pl.BlockSpec((1, blk, kvw), lambda b_, i: (b_, jnp.maximum(i - 1, 0), kv_blk)),
                  pl.BlockSpec((N_HEADS, blk, blk), lambda b_, i: (0, 0, 0)),
                  pl.BlockSpec((N_HEADS, blk, blk), lambda b_, i: (0, 0, 0))],
        out_specs=pl.BlockSpec((1, blk, nq), lambda b_, i: (b_, i, 0)),
        out_shape=jax.ShapeDtypeStruct((b, l, nq), BF16),
        scratch_shapes=[pltpu.VMEM((2 * N_KV_HEADS, 2 * blk, LANES), BF16),
                        pltpu.VMEM((2 * N_KV_HEADS, 2 * blk, LANES), BF16)],
        compiler_params=_params(2, blocks, scratch),
        name="attention_prompt",
    )(sinks.astype(F32), qkv, qkv, qkv, bias_prev, bias_cur)


def _attn_sample_kernel(q_ref, k_ref, v_ref, bias_ref, sink_ref, o_ref):
    k = k_ref[0].astype(BF16)
    v = v_ref[0].astype(BF16)
    for c in range(N_KV_HEADS):
        s = _dot_nt(q_ref[0, c].astype(BF16), k) + bias_ref[c]
        sink = sink_ref[c]
        m = jnp.maximum(jnp.max(s, axis=-1, keepdims=True), sink)
        e = jnp.exp(s - m)
        den = jnp.sum(e, axis=-1, keepdims=True) + jnp.exp(sink - m)
        o_ref[0, c] = _dot(e.astype(BF16), v) * (1.0 / den)


def attention_sample(qkv, cache_k, cache_v, sinks, rel_table):
    bsz, w = cache_k.shape[0], cache_k.shape[1]
    t = qkv.shape[0] // bsz
    nq, nkv = N_HEADS * HEAD_DIM, N_KV_HEADS * HEAD_DIM
    keys = w + t
    keys_pad = -(-keys // 16) * 16
    rows = ATT_GROUP * t
    q = qkv[:, :nq].reshape(bsz, t, N_KV_HEADS, ATT_GROUP, HEAD_DIM) * HEAD_DIM ** -0.5
    q = jnp.transpose(q, (0, 2, 3, 1, 4)).reshape(bsz, N_KV_HEADS, rows, HEAD_DIM)
    qz = jnp.einsum('bcrd,ck->bcrkd', q, jnp.eye(N_KV_HEADS, dtype=F32)).reshape(bsz, N_KV_HEADS, rows, nkv)
    k_all = jnp.concatenate([cache_k.reshape(bsz, w, nkv), qkv[:, nq:nq + nkv].reshape(bsz, t, nkv)], axis=1)
    v_all = jnp.concatenate([cache_v.reshape(bsz, w, nkv), qkv[:, nq + nkv:].reshape(bsz, t, nkv)], axis=1)
    pad = ((0, 0), (0, keys_pad - keys), (0, 0))
    k_pad, v_pad = jnp.pad(k_all, pad), jnp.pad(v_all, pad)
    tq = np.arange(t)[:, None]
    sk = np.arange(keys_pad)[None, :]
    dist = tq + w - sk
    in_band = (dist >= 0) & (dist <= WINDOW) & (sk < keys)
    bias_t = _bias_from_dist(dist, in_band, rel_table)
    bias = bias_t.reshape(N_KV_HEADS, rows, keys_pad)
    sink_rows = jnp.repeat(sinks.astype(F32), t).reshape(N_KV_HEADS, rows, 1)
    blocks = (_nbytes((N_KV_HEADS, rows, nkv), F32) * 2 + 2 * _nbytes((keys_pad, nkv), F32))
    oz = pl.pallas_call(
        _attn_sample_kernel,
        grid=(bsz,),
        in_specs=[pl.BlockSpec((1, N_KV_HEADS, rows, nkv), lambda b_: (b_, 0, 0, 0)),
                  pl.BlockSpec((1, keys_pad, nkv), lambda b_: (b_, 0, 0)),
                  pl.BlockSpec((1, keys_pad, nkv), lambda b_: (b_, 0, 0)),
                  pl.BlockSpec((N_KV_HEADS, rows, keys_pad), lambda b_: (0, 0, 0)),
                  pl.BlockSpec((N_KV_HEADS, rows, 1), lambda b_: (0, 0, 0))],
        out_specs=pl.BlockSpec((1, N_KV_HEADS, rows, nkv), lambda b_: (b_, 0, 0, 0)),
        out_shape=jax.ShapeDtypeStruct((bsz, N_KV_HEADS, rows, nkv), F32),
        compiler_params=_params(1, blocks),
        name="attention_sample",
    )(qz, k_pad, v_pad, bias, sink_rows)
    o = jnp.stack([oz[:, c, :, c * HEAD_DIM:(c + 1) * HEAD_DIM] for c in range(N_KV_HEADS)], axis=1)
    o = o.reshape(bsz, N_KV_HEADS, ATT_GROUP, t, HEAD_DIM)
    o = jnp.transpose(o, (0, 3, 1, 2, 4)).reshape(bsz * t, nq)
    return o.astype(BF16), k_all[:, t:], v_all[:, t:]


def _mm_res_kernel(a_ref, w_ref, x_ref, g_ref, o_ref):
    y = _dot(a_ref[0], w_ref[...].astype(BF16))
    o_ref[0] = x_ref[0] + g_ref[0] * y


def matmul_gate_residual(a, w, x, gate, tm, tn):
    b, l, k = a.shape
    n = w.shape[1]
    grows = 1 if gate.shape[1] == 1 else tm
    if gate.shape[1] == 1:
        g_spec = pl.BlockSpec((1, 1, tn), lambda b_, i, j: (b_, 0, j))
    else:
        g_spec = pl.BlockSpec((1, tm, tn), lambda b_, i, j: (b_, i, j))
    blocks = (_nbytes((tm, k), a.dtype) + _nbytes((k, tn), F32) + 2 * _nbytes((tm, tn), F32)
              + _nbytes((grows, tn), F32))
    scratch = _nbytes((k, tn), BF16) + _nbytes((tm, tn), F32)
    return pl.pallas_call(
        _mm_res_kernel,
        grid=(b, l // tm, n // tn),
        in_specs=[pl.BlockSpec((1, tm, k), lambda b_, i, j: (b_, i, 0)),
                  pl.BlockSpec((k, tn), lambda b_, i, j: (0, j)),
                  pl.BlockSpec((1, tm, tn), lambda b_, i, j: (b_, i, j)),
                  g_spec],
        out_specs=pl.BlockSpec((1, tm, tn), lambda b_, i, j: (b_, i, j)),
        out_shape=jax.ShapeDtypeStruct((b, l, n), F32),
        compiler_params=_params(3, blocks, scratch),
        name="matmul_gate_residual",
    )(a, w, x, gate)


def _ffn_conv_rows(u, prev1, prev2, cw_ref, cb_ref):
    return u * cw_ref[2:3, :] + prev1 * cw_ref[1:2, :] + prev2 * cw_ref[0:1, :] + cb_ref[...]


def _ffn_up_prompt_kernel(x_ref, gain_ref, shift_ref, scale_ref, wg_ref, wv_ref, cwg_ref, cwv_ref,
                          cbg_ref, cbv_ref, o_ref, tail_ref, h_s, carry_s):
    i = pl.program_id(1)
    j = pl.program_id(2)
    tm = h_s.shape[0]

    @pl.when(j == 0)
    def _():
        h_s[...] = _norm_mod(x_ref[0], gain_ref[...], shift_ref[0], scale_ref[0]).astype(BF16)

    h = h_s[...]
    row8 = lax.broadcasted_iota(jnp.int32, (SUBLANES, 1), 0)
    ys, ys_top = [], []
    for half, (w_ref, cw_ref, cb_ref) in enumerate(((wg_ref, cwg_ref, cbg_ref), (wv_ref, cwv_ref, cbv_ref))):
        u = _dot(h, w_ref[...].astype(BF16))
        ys.append(_ffn_conv_rows(u, pltpu.roll(u, 1, axis=0), pltpu.roll(u, 2, axis=0), cw_ref, cb_ref))
        top = u[0:SUBLANES]

        @pl.when(i == 0)
        def _():
            carry_s[j, half] = jnp.zeros(carry_s.shape[2:], F32)

        hist = carry_s[j, half]
        p1 = jnp.where(row8 < 1, pltpu.roll(hist, 1, axis=0), pltpu.roll(top, 1, axis=0))
        p2 = jnp.where(row8 < 2, pltpu.roll(hist, 2, axis=0), pltpu.roll(top, 2, axis=0))
        ys_top.append(_ffn_conv_rows(top, p1, p2, cw_ref, cb_ref))
        tail = u[tm - SUBLANES:tm]
        carry_s[j, half] = tail
        tail_ref[0, 0, half] = tail
    o_ref[0] = (_silu(ys[0]) * ys[1]).astype(o_ref.dtype)
    o_ref[0, 0:SUBLANES, :] = (_silu(ys_top[0]) * ys_top[1]).astype(o_ref.dtype)


def ffn_up_prompt(x, gain, shift, scale, w_up, conv_w, conv_b, tm, tn=512):
    b, l, d = x.shape
    dff = w_up.shape[1] // 2
    nj = dff // tn
    conv_b = conv_b.reshape(1, 2 * dff)
    blocks = (_nbytes((tm, d), F32) + 2 * _nbytes((d, tn), F32) + _nbytes((tm, tn), BF16)
              + 2 * _nbytes((1, d), F32) + 8 * _nbytes((SUBLANES, tn), F32))
    scratch = (_nbytes((tm, d), BF16) + _nbytes((nj, 2, SUBLANES, tn), F32) + 2 * _nbytes((d, tn), BF16)
               + 8 * _nbytes((tm, tn), F32))
    act, tail = pl.pallas_call(
        _ffn_up_prompt_kernel,
        grid=(b, l // tm, nj),
        in_specs=[pl.BlockSpec((1, tm, d), lambda b_, i, j: (b_, i, 0)),
                  pl.BlockSpec((1, d), lambda b_, i, j: (0, 0)),
                  _mod_specs(shift, tm), _mod_specs(scale, tm),
                  pl.BlockSpec((d, tn), lambda b_, i, j: (0, j)),
                  pl.BlockSpec((d, tn), lambda b_, i, j: (0, j + nj)),
                  pl.BlockSpec((FFN_CONV, tn), lambda b_, i, j: (0, j)),
                  pl.BlockSpec((FFN_CONV, tn), lambda b_, i, j: (0, j + nj)),
                  pl.BlockSpec((1, tn), lambda b_, i, j: (0, j)),
                  pl.BlockSpec((1, tn), lambda b_, i, j: (0, j + nj))],
        out_specs=[pl.BlockSpec((1, tm, tn), lambda b_, i, j: (b_, i, j)),
                   pl.BlockSpec((1, 1, 2, SUBLANES, tn), lambda b_, i, j: (b_, i, 0, 0, j))],
        out_shape=[jax.ShapeDtypeStruct((b, l, dff), BF16),
                   jax.ShapeDtypeStruct((b, l // tm, 2, SUBLANES, dff), F32)],
        scratch_shapes=[pltpu.VMEM((tm, d), BF16), pltpu.VMEM((nj, 2, SUBLANES, tn), F32)],
        compiler_params=_params(3, blocks, scratch),
        name="ffn_up_prompt",
    )(x, gain.reshape(1, d), shift, scale, w_up, w_up, conv_w, conv_w, conv_b, conv_b)
    keep = FFN_CONV - 1
    hist = jnp.transpose(tail[:, -1, :, SUBLANES - keep:, :], (0, 2, 1, 3)).reshape(b, keep, 2 * dff)
    return act, hist


def _ffn_up_sample_kernel(x_ref, gain_ref, shift_ref, scale_ref, wg_ref, wv_ref, cwg_ref, cwv_ref,
                          cbg_ref, cbv_ref, e1g_ref, e1v_ref, e2g_ref, e2v_ref, o_ref, ug_ref, uv_ref, h_s,
                          *, seq_len):
    @pl.when(pl.program_id(0) == 0)
    def _():
        h_s[...] = _norm_mod(x_ref[...], gain_ref[...], shift_ref[...], scale_ref[...]).astype(BF16)

    h = h_s[...]
    rows = h_s.shape[0]
    t = lax.broadcasted_iota(jnp.int32, (rows, 1), 0) % seq_len
    ys = []
    for w_ref, cw_ref, cb_ref, e1_ref, e2_ref, u_ref in (
            (wg_ref, cwg_ref, cbg_ref, e1g_ref, e2g_ref, ug_ref),
            (wv_ref, cwv_ref, cbv_ref, e1v_ref, e2v_ref, uv_ref)):
        u = _dot(h, w_ref[...].astype(BF16))
        u_ref[...] = u
        p1 = jnp.where(t >= 1, pltpu.roll(u, 1, axis=0), e1_ref[...])
        p2 = jnp.where(t >= 2, pltpu.roll(u, 2, axis=0), e2_ref[...])
        ys.append(_ffn_conv_rows(u, p1, p2, cw_ref, cb_ref))
    o_ref[...] = (_silu(ys[0]) * ys[1]).astype(o_ref.dtype)


def ffn_up_sample(x, gain, shift, scale, w_up, conv_w, conv_b, hist, seq_len, tn=512):
    rows, d = x.shape
    dff = w_up.shape[1] // 2
    nj = dff // tn
    bsz = rows // seq_len
    conv_b = conv_b.reshape(1, 2 * dff)
    tpos = np.arange(seq_len)
    sel = lambda cond: jnp.asarray(cond.astype(np.float32))[None, :, None]
    h0, h1 = hist[:, 0:1, :], hist[:, 1:2, :]
    e1 = (h1 * sel(tpos == 0)).reshape(rows, 2 * dff)
    e2 = (h0 * sel(tpos == 0) + h1 * sel(tpos == 1)).reshape(rows, 2 * dff)
    col = lambda j: (0, j)
    col_hi = lambda j: (0, j + nj)
    blocks = (_nbytes((rows, d), F32) * 3 + 2 * _nbytes((d, tn), F32) + 7 * _nbytes((rows, tn), F32))
    scratch = _nbytes((rows, d), BF16) + 2 * _nbytes((d, tn), BF16) + 8 * _nbytes((rows, tn), F32)
    act, ug, uv = pl.pallas_call(
        functools.partial(_ffn_up_sample_kernel, seq_len=seq_len),
        grid=(nj,),
        in_specs=[pl.BlockSpec((rows, d), lambda j: (0, 0)),
                  pl.BlockSpec((1, d), lambda j: (0, 0)),
                  pl.BlockSpec((rows, d), lambda j: (0, 0)),
                  pl.BlockSpec((rows, d), lambda j: (0, 0)),
                  pl.BlockSpec((d, tn), col), pl.BlockSpec((d, tn), col_hi),
                  pl.BlockSpec((FFN_CONV, tn), col), pl.BlockSpec((FFN_CONV, tn), col_hi),
                  pl.BlockSpec((1, tn), col), pl.BlockSpec((1, tn), col_hi),
                  pl.BlockSpec((rows, tn), col), pl.BlockSpec((rows, tn), col_hi),
                  pl.BlockSpec((rows, tn), col), pl.BlockSpec((rows, tn), col_hi)],
        out_specs=[pl.BlockSpec((rows, tn), col), pl.BlockSpec((rows, tn), col), pl.BlockSpec((rows, tn), col)],
        out_shape=[jax.ShapeDtypeStruct((rows, dff), BF16),
                   jax.ShapeDtypeStruct((rows, dff), F32),
                   jax.ShapeDtypeStruct((rows, dff), F32)],
        scratch_shapes=[pltpu.VMEM((rows, d), BF16)],
        compiler_params=_params(1, blocks, scratch),
        name="ffn_up_sample",
    )(x, gain.reshape(1, d), shift, scale, w_up, w_up, conv_w, conv_w, conv_b, conv_b, e1, e1, e2, e2)
    keep = FFN_CONV - 1
    u_all = jnp.concatenate([ug, uv], axis=-1).reshape(bsz, seq_len, 2 * dff)
    new_hist = jnp.concatenate([hist, u_all], axis=1)[:, -keep:]
    return act, new_hist


def _gdn_in_kernel(x_ref, gain_ref, shift_ref, scale_ref, w_ref, wba_ref, alog_ref, dtb_ref,
                   o_ref, bg_ref, h_s):
    @pl.when(pl.program_id(2) == 0)
    def _():
        h = _norm_mod(x_ref[0], gain_ref[...], shift_ref[0], scale_ref[0]).astype(BF16)
        h_s[...] = h
        y = _dot(h, wba_ref[...].astype(BF16))
        lane = lax.broadcasted_iota(jnp.int32, y.shape, 1) % LANES
        beta = 1.0 / (1.0 + jnp.exp(-y))
        a = y + dtb_ref[...]
        softplus = jnp.maximum(a, 0.0) + jnp.log1p(jnp.exp(-jnp.abs(a)))
        g = -jnp.exp(alog_ref[...]) * softplus
        bg_ref[0] = jnp.where(lane < GDN_VH_PER_GROUP, beta, jnp.where(lane < 2 * GDN_VH_PER_GROUP, g, 0.0))

    o_ref[0] = _dot(h_s[...], w_ref[...].astype(BF16)).astype(o_ref.dtype)


def _group_lane_layout(vec_b, vec_a):
    lead = vec_b.shape[:-1]
    vb = vec_b.reshape(*lead, GDN_HEAD_GROUPS, GDN_VH_PER_GROUP)
    va = vec_a.reshape(*lead, GDN_HEAD_GROUPS, GDN_VH_PER_GROUP)
    pad = jnp.zeros((*lead, GDN_HEAD_GROUPS, LANES - 2 * GDN_VH_PER_GROUP), vec_b.dtype)
    return jnp.concatenate([vb, va, pad], axis=-1).reshape(*lead, GDN_HEAD_GROUPS * LANES)


def gdn_in_projection(x, gain, shift, scale, w_in, a_log, dt_bias, tm, tn=512):
    b, l, d = x.shape
    n_main = GDN_CONV_DIM + GDN_VAL_DIM
    w_b = w_in[:, n_main:n_main + GDN_V_HEADS]
    w_a = w_in[:, n_main + GDN_V_HEADS:]
    wba = _group_lane_layout(w_b, w_a)
    zeros = jnp.zeros((1, GDN_V_HEADS), F32)
    alog = _group_lane_layout(zeros, a_log.reshape(1, -1).astype(F32))
    dtb = _group_lane_layout(zeros, dt_bias.reshape(1, -1).astype(F32))
    nbg = wba.shape[1]
    blocks = (_nbytes((tm, d), F32) + _nbytes((d, tn), F32) + _nbytes((tm, tn), BF16)
              + _nbytes((d, nbg), F32) + _nbytes((tm, nbg), F32) + 2 * _nbytes((1, d), F32))
    scratch = _nbytes((tm, d), BF16) + _nbytes((d, tn), BF16) + 6 * _nbytes((tm, nbg), F32)
    return pl.pallas_call(
        _gdn_in_kernel,
        grid=(b, l // tm, n_main // tn),
        in_specs=[pl.BlockSpec((1, tm, d), lambda b_, i, j: (b_, i, 0)),
                  pl.BlockSpec((1, d), lambda b_, i, j: (0, 0)),
                  _mod_specs(shift, tm), _mod_specs(scale, tm),
                  pl.BlockSpec((d, tn), lambda b_, i, j: (0, j)),
                  pl.BlockSpec((d, nbg), lambda b_, i, j: (0, 0)),
                  pl.BlockSpec((1, nbg), lambda b_, i, j: (0, 0)),
                  pl.BlockSpec((1, nbg), lambda b_, i, j: (0, 0))],
        out_specs=[pl.BlockSpec((1, tm, tn), lambda b_, i, j: (b_, i, j)),
                   pl.BlockSpec((1, tm, nbg), lambda b_, i, j: (b_, i, 0))],
        out_shape=[jax.ShapeDtypeStruct((b, l, n_main), BF16),
                   jax.ShapeDtypeStruct((b, l, nbg), F32)],
        scratch_shapes=[pltpu.VMEM((tm, d), BF16)],
        compiler_params=_params(3, blocks, scratch),
        name="gdn_in_projection",
    )(x, gain.reshape(1, d), shift, scale, w_in, wba, alog, dtb)


def _causal_conv_silu(x, carry, w_ref):
    width = GDN_CONV
    row8 = lax.broadcasted_iota(jnp.int32, (SUBLANES, 1), 0)
    top = x[0:SUBLANES]
    y = x * w_ref[width - 1:width, :]
    y_top = top * w_ref[width - 1:width, :]
    for k in range(1, width):
        wk = w_ref[width - 1 - k:width - k, :]
        y = y + pltpu.roll(x, k, axis=0) * wk
        y_top = y_top + jnp.where(row8 < k, pltpu.roll(carry, k, axis=0), pltpu.roll(top, k, axis=0)) * wk
    return _silu(y), _silu(y_top)


def _l2_normalize_heads(x, scale):
    outs = []
    for hd in range(x.shape[1] // LANES):
        xh = x[:, hd * LANES:(hd + 1) * LANES]
        outs.append(xh * (lax.rsqrt(jnp.sum(xh * xh, axis=-1, keepdims=True) + NORM_EPS) * scale))
    return outs


def _gdn_core_kernel(q_ref, k_ref, v_ref, z_ref, bg_ref, hq_ref, hk_ref, hv_ref, cwq_ref, cwk_ref, cwv_ref,
                     onorm_ref, s0_ref, o_ref, s1_ref, s_s, cq_s, ck_s, cv_s, qs_s, ks_s, vs_s):
    r = pl.program_id(2)
    n_r = pl.num_programs(2)
    rows = q_ref.shape[1]
    chunk = GDN_CHUNK
    vpg = GDN_VH_PER_GROUP

    @pl.when(r == 0)
    def _():
        s_s[...] = s0_ref[0]
        cq_s[...] = hq_ref[0]
        ck_s[...] = hk_ref[0]
        cv_s[...] = hv_ref[0]

    for src_ref, carry_s, w_ref, dst_s, l2scale in ((q_ref, cq_s, cwq_ref, qs_s, GDN_DK ** -0.5),
                                                    (k_ref, ck_s, cwk_ref, ks_s, 1.0),
                                                    (v_ref, cv_s, cwv_ref, vs_s, None)):
        x = src_ref[0].astype(F32)
        y, y_top = _causal_conv_silu(x, carry_s[...], w_ref)
        carry_s[...] = x[rows - SUBLANES:rows]
        if l2scale is None:
            dst_s[...] = y
            dst_s[0:SUBLANES, :] = y_top
        else:
            for hd, (yh, yh_top) in enumerate(zip(_l2_normalize_heads(y, l2scale),
                                                  _l2_normalize_heads(y_top, l2scale))):
                dst_s[:, hd * LANES:(hd + 1) * LANES] = yh
                dst_s[0:SUBLANES, hd * LANES:(hd + 1) * LANES] = yh_top

    ri = lax.broadcasted_iota(jnp.int32, (chunk, chunk), 0)
    ci = lax.broadcasted_iota(jnp.int32, (chunk, chunk), 1)
    tril = ri >= ci
    strict = ri > ci
    eye = (ri == ci).astype(F32)
    ltri = tril.astype(BF16)
    n_levels = int(math.log2(chunk))
    off_masks = [((ri >> (lvl + 1)) == (ci >> (lvl + 1))) & ((ri >> lvl) != (ci >> lvl))
                 for lvl in range(n_levels)]

    def chunk_body(c, carry):
        rs = pl.ds(pl.multiple_of(c * chunk, chunk), chunk)
        bg = bg_ref[0, rs, :]
        b_hi = bg.astype(BF16)
        rem = bg - b_hi.astype(F32)
        b_mid = rem.astype(BF16)
        b_lo = (rem - b_mid.astype(F32)).astype(BF16)
        cum = _dot(ltri, b_hi) + _dot(ltri, b_mid) + _dot(ltri, b_lo)
        cum_t = jnp.transpose(cum)
        for kh in range(GDN_KH_PER_GROUP):
            ksl = slice(kh * LANES, (kh + 1) * LANES)
            k = ks_s[rs, ksl]
            q = qs_s[rs, ksl]
            k_bf = k.astype(BF16)
            gq = _dot_nt(jnp.concatenate([k, q], axis=0).astype(BF16), k_bf)
            gram, qk_raw = gq[:chunk], gq[chunk:]
            for e in range(GDN_V_HEADS // GDN_K_HEADS):
                hv = kh * (GDN_V_HEADS // GDN_K_HEADS) + e
                vsl = slice(hv * LANES, (hv + 1) * LANES)
                beta = bg[:, hv:hv + 1]
                gcol = cum[:, vpg + hv:vpg + hv + 1]
                grow = cum_t[vpg + hv:vpg + hv + 1, :]
                decay = jnp.exp(jnp.where(tril, gcol - grow, NEG_INF))
                a = jnp.where(strict, gram * beta * decay, 0.0)
                inv = eye - jnp.where(off_masks[0], a, 0.0)
                for lvl in range(1, n_levels):
                    a_off = jnp.where(off_masks[lvl], a, 0.0).astype(BF16)
                    inv_bf = inv.astype(BF16)
                    inv = inv - _dot(inv_bf, _dot(a_off, inv_bf).astype(BF16))
                eg = jnp.exp(gcol)
                v = vs_s[rs, vsl]
                s_old = s_s[hv]
                ks_qs = _dot(jnp.concatenate([k * (beta * eg), q * eg], axis=0).astype(BF16),
                             s_old.astype(BF16))
                v_new = _dot(inv.astype(BF16), (v * beta - ks_qs[:chunk]).astype(BF16))
                v_new_bf = v_new.astype(BF16)
                o = ks_qs[chunk:] + _dot((qk_raw * decay).astype(BF16), v_new_bf)
                g_last = gcol[chunk - 1:chunk, :]
                k_dec = (k * jnp.exp(g_last - gcol)).astype(BF16)
                s_s[hv] = s_old * jnp.exp(g_last) + _dot_tn(k_dec, v_new_bf)
                on = o * lax.rsqrt(jnp.mean(o * o, axis=-1, keepdims=True) + NORM_EPS) * onorm_ref[...]
                z = z_ref[0, rs, vsl].astype(F32)
                o_ref[0, rs, vsl] = (on * _silu(z)).astype(o_ref.dtype)
        return carry

    lax.fori_loop(0, rows // chunk, chunk_body, 0)

    @pl.when(r == n_r - 1)
    def _():
        s1_ref[0] = s_s[...]


def gdn_core(mixed_z, bg, hist, s0, conv_w, out_norm, rows):
    b, l, _ = mixed_z.shape
    hg = GDN_HEAD_GROUPS
    qw = GDN_KEY_DIM // hg
    vw = GDN_VAL_DIM // hg
    kq = GDN_KEY_DIM // qw
    kv = 2 * GDN_KEY_DIM // vw
    kz = GDN_CONV_DIM // vw
    vpg = GDN_VH_PER_GROUP
    blocks = (2 * _nbytes((rows, qw), BF16) + 3 * _nbytes((rows, vw), BF16) + _nbytes((rows, LANES), F32)
              + 2 * _nbytes((vpg, GDN_DK, GDN_DV), F32) + 4 * _nbytes((SUBLANES, vw), F32))
    scratch = (_nbytes((vpg, GDN_DK, GDN_DV), F32) + 2 * _nbytes((rows, qw), F32) + _nbytes((rows, vw), F32)
               + 6 * _nbytes((rows, vw), F32))
    return pl.pallas_call(
        _gdn_core_kernel,
        grid=(b, hg, l // rows),
        in_specs=[pl.BlockSpec((1, rows, qw), lambda b_, g, r: (b_, r, g)),
                  pl.BlockSpec((1, rows, qw), lambda b_, g, r: (b_, r, kq + g)),
                  pl.BlockSpec((1, rows, vw), lambda b_, g, r: (b_, r, kv + g)),
                  pl.BlockSpec((1, rows, vw), lambda b_, g, r: (b_, r, kz + g)),
                  pl.BlockSpec((1, rows, LANES), lambda b_, g, r: (b_, r, g)),
                  pl.BlockSpec((1, SUBLANES, qw), lambda b_, g, r: (b_, 0, g)),
                  pl.BlockSpec((1, SUBLANES, qw), lambda b_, g, r: (b_, 0, kq + g)),
                  pl.BlockSpec((1, SUBLANES, vw), lambda b_, g, r: (b_, 0, kv + g)),
                  pl.BlockSpec((GDN_CONV, qw), lambda b_, g, r: (0, g)),
                  pl.BlockSpec((GDN_CONV, qw), lambda b_, g, r: (0, kq + g)),
                  pl.BlockSpec((GDN_CONV, vw), lambda b_, g, r: (0, kv + g)),
                  pl.BlockSpec((1, GDN_DV), lambda b_, g, r: (0, 0)),
                  pl.BlockSpec((1, vpg, GDN_DK, GDN_DV), lambda b_, g, r: (b_, g, 0, 0))],
        out_specs=[pl.BlockSpec((1, rows, vw), lambda b_, g, r: (b_, r, g)),
                   pl.BlockSpec((1, vpg, GDN_DK, GDN_DV), lambda b_, g, r: (b_, g, 0, 0))],
        out_shape=[jax.ShapeDtypeStruct((b, l, GDN_VAL_DIM), BF16),
                   jax.ShapeDtypeStruct((b, GDN_V_HEADS, GDN_DK, GDN_DV), F32)],
        scratch_shapes=[pltpu.VMEM((vpg, GDN_DK, GDN_DV), F32),
                        pltpu.VMEM((SUBLANES, qw), F32), pltpu.VMEM((SUBLANES, qw), F32),
                        pltpu.VMEM((SUBLANES, vw), F32),
                        pltpu.VMEM((rows, qw), F32), pltpu.VMEM((rows, qw), F32), pltpu.VMEM((rows, vw), F32)],
        compiler_params=_params(3, blocks, scratch),
        name="gdn_core",
    )(mixed_z, mixed_z, mixed_z, mixed_z, bg, hist, hist, hist, conv_w, conv_w, conv_w,
      out_norm.reshape(1, GDN_DV).astype(F32), s0)


def _tile_rows(l, cap):
    t = min(l, cap)
    while l % t:
        t //= 2
    return t


def _trunk(x, mod, states, P, sample):
    bsz, l, d = x.shape
    tm = _tile_rows(l, ROW_TILE_CAP)
    new = {}
    depth = mod.shape[0]
    for layer in range(depth):
        if sample:
            parts = [mod[layer][None, :, k * d:(k + 1) * d] for k in range(6)]
        else:
            parts = [mod[layer][:, None, k * d:(k + 1) * d] for k in range(6)]
        sh_m, sc_m, g_m, sh_f, sc_f, g_f = parts
        i = layer // 2
        if layer % 2 == 0:
            qkv = qkv_projection(x, P['norm_mix'][layer], sh_m, sc_m, P['w_attn_qkv'][i],
                                 P['attn_q_norm'][i], P['attn_k_norm'][i], tm)
            nq, nkv = N_HEADS * HEAD_DIM, N_KV_HEADS * HEAD_DIM
            if sample:
                o, k_win, v_win = attention_sample(qkv[0], states['win_k'][i], states['win_v'][i],
                                                   P['attn_sinks'][i], P['rel_bias_table'])
                o = o[None]
                w_len = k_win.shape[1]
                new.setdefault('win_k', []).append(k_win.reshape(-1, w_len, N_KV_HEADS, HEAD_DIM))
                new.setdefault('win_v', []).append(v_win.reshape(-1, w_len, N_KV_HEADS, HEAD_DIM))
            else:
                o = attention_prompt(qkv, P['attn_sinks'][i], P['rel_bias_table'])
                keep = min(WINDOW, PAST_LEN)
                new.setdefault('win_k', []).append(
                    qkv[:, l - keep:, nq:nq + nkv].reshape(bsz, keep, N_KV_HEADS, HEAD_DIM))
                new.setdefault('win_v', []).append(
                    qkv[:, l - keep:, nq + nkv:].reshape(bsz, keep, N_KV_HEADS, HEAD_DIM))
            x = matmul_gate_residual(o, P['w_attn_o'][i], x, g_m, tm, 512)
        else:
            mixed_z, bg = gdn_in_projection(x, P['norm_mix'][layer], sh_m, sc_m, P['w_gdn_in'][i],
                                            P['gdn_a_log'][i], P['gdn_dt_bias'][i], tm)
            keep = GDN_CONV - 1
            if sample:
                seqs = states['gdn'].shape[1]
                t = l // seqs
                pad_rows = GDN_CHUNK - t
                pad3 = lambda a: jnp.pad(a.reshape(seqs, t, a.shape[-1]), ((0, 0), (0, pad_rows), (0, 0)))
                mz_seq, bg_seq = pad3(mixed_z[0]), pad3(bg[0])
                conv0 = states['gdn_conv'][i].astype(F32)
                hist = jnp.pad(conv0, ((0, 0), (SUBLANES - keep, 0), (0, 0)))
                o_seq, s1 = gdn_core(mz_seq, bg_seq, hist, states['gdn'][i].astype(F32),
                                     P['gdn_conv_w'][i], P['gdn_out_norm'][i], GDN_CHUNK)
                o = o_seq[:, :t].reshape(1, l, GDN_VAL_DIM)
                mixed_rows = mixed_z[0, :, :GDN_CONV_DIM].astype(F32).reshape(seqs, t, GDN_CONV_DIM)
                new.setdefault('gdn_conv', []).append(jnp.concatenate([conv0, mixed_rows], axis=1)[:, -keep:])
            else:
                hist = jnp.zeros((bsz, SUBLANES, GDN_CONV_DIM), F32)
                s0 = jnp.zeros((bsz, GDN_V_HEADS, GDN_DK, GDN_DV), F32)
                o, s1 = gdn_core(mixed_z, bg, hist, s0, P['gdn_conv_w'][i], P['gdn_out_norm'][i],
                                 _tile_rows(l, GDN_ROW_TILE_CAP))
                new.setdefault('gdn_conv', []).append(mixed_z[:, l - keep:, :GDN_CONV_DIM].astype(F32))
            new.setdefault('gdn', []).append(s1)
            x = matmul_gate_residual(o, P['w_gdn_out'][i], x, g_m, tm, 512)
        if sample:
            seqs = states['ffn_conv'].shape[1]
            act, f1 = ffn_up_sample(x[0], P['norm_ffn'][layer], sh_f[0], sc_f[0], P['w_ffn_up'][layer],
                                    P['ffn_conv_w'][layer], P['ffn_conv_b'][layer],
                                    states['ffn_conv'][layer].astype(F32), l // seqs)
            act = act[None]
        else:
            act, f1 = ffn_up_prompt(x, P['norm_ffn'][layer], sh_f, sc_f, P['w_ffn_up'][layer],
                                    P['ffn_conv_w'][layer], P['ffn_conv_b'][layer], tm)
        new.setdefault('ffn_conv', []).append(f1)
        x = matmul_gate_residual(act, P['w_ffn_down'][layer], x, g_f, tm, 256)
    return x, {k: jnp.stack(v) for k, v in new.items()}


def kernel(x_prompt, x_sample, c_prompt, c_sample, cache_win_k, cache_win_v, state_gdn, state_gdn_conv,
           state_ffn_conv, rel_bias_table, w_ada, b_ada, norm_mix, norm_ffn, w_attn_qkv, attn_q_norm,
           attn_k_norm, attn_sinks, w_attn_o, w_gdn_in, gdn_conv_w, gdn_a_log, gdn_dt_bias, gdn_out_norm,
           w_gdn_out, w_ffn_up, ffn_conv_w, ffn_conv_b, w_ffn_down):
    P = dict(rel_bias_table=rel_bias_table, norm_mix=norm_mix, norm_ffn=norm_ffn, w_attn_qkv=w_attn_qkv,
             attn_q_norm=attn_q_norm, attn_k_norm=attn_k_norm, attn_sinks=attn_sinks, w_attn_o=w_attn_o,
             w_gdn_in=w_gdn_in, gdn_conv_w=gdn_conv_w, gdn_a_log=gdn_a_log, gdn_dt_bias=gdn_dt_bias,
             gdn_out_norm=gdn_out_norm, w_gdn_out=w_gdn_out, w_ffn_up=w_ffn_up, ffn_conv_w=ffn_conv_w,
             ffn_conv_b=ffn_conv_b, w_ffn_down=w_ffn_down)
    bp = x_prompt.shape[0]
    bs, ts, d = x_sample.shape
    n_c = bp + bs
    c_rows = -(-n_c // SUBLANES) * SUBLANES
    c_all = jnp.pad(jnp.concatenate([c_prompt, c_sample], axis=0), ((0, c_rows - n_c), (0, 0)))
    mod = ada_modulation(c_all, w_ada, b_ada)
    mod_prompt = mod[:, :bp]
    mod_sample = jnp.repeat(mod[:, bp:n_c], ts, axis=1)

    y_p, new_p = _trunk(x_prompt, mod_prompt, None, P, sample=False)
    states = dict(win_k=cache_win_k, win_v=cache_win_v, gdn=state_gdn, gdn_conv=state_gdn_conv,
                  ffn_conv=state_ffn_conv)
    y_s, new_s = _trunk(x_sample.reshape(1, bs * ts, d), mod_sample, states, P, sample=True)
    y_s = y_s.reshape(bs, ts, d)
    return (y_p, y_s, new_p['win_k'], new_p['win_v'], new_s['win_k'], new_s['win_v'],
            new_p['gdn'], new_s['gdn'], new_p['gdn_conv'], new_s['gdn_conv'],
            new_p['ffn_conv'], new_s['ffn_conv'])
```

```python
import functools
import math

import numpy as np
import jax
import jax.numpy as jnp
from jax import lax
from jax.experimental import pallas as pl
from jax.experimental.pallas import tpu as pltpu

HEAD_DIM = 64
N_HEADS = 32
N_KV_HEADS = 4
ATT_GROUP = N_HEADS // N_KV_HEADS
WINDOW = 128
ATT_BLOCK = 128
N_BUCKETS = 32
MAX_DISTANCE = 128
NEG_INF = -1e30
PAST_LEN = 16384

GDN_K_HEADS = 16
GDN_V_HEADS = 32
GDN_DK = 128
GDN_DV = 128
GDN_KEY_DIM = GDN_K_HEADS * GDN_DK
GDN_VAL_DIM = GDN_V_HEADS * GDN_DV
GDN_CONV_DIM = 2 * GDN_KEY_DIM + GDN_VAL_DIM
GDN_CONV = 4
GDN_CHUNK = 64
GDN_HEAD_GROUPS = 4
GDN_VH_PER_GROUP = GDN_V_HEADS // GDN_HEAD_GROUPS
GDN_KH_PER_GROUP = GDN_K_HEADS // GDN_HEAD_GROUPS
GDN_PACK = 4
GDN_PHASE_A_CHUNKS = 4

FFN_CONV = 3
NORM_EPS = 1e-6

LANES = 128
SUBLANES = 8
VMEM_CAP_BYTES = 60 * 1024 * 1024
VMEM_SLACK_BYTES = 8 * 1024 * 1024

ROW_TILE_CAP = 1024
GDN_ROW_TILE_CAP = 512

BF16 = jnp.bfloat16
F32 = jnp.float32


def _vmem_limit(block_bytes, scratch_bytes=0):
    est = 2 * int(block_bytes) + int(scratch_bytes) + VMEM_SLACK_BYTES
    return int(min(max(est, 16 * 1024 * 1024), VMEM_CAP_BYTES))


def _params(n_grid, block_bytes, scratch_bytes=0):
    return pltpu.CompilerParams(
        dimension_semantics=("arbitrary",) * n_grid,
        vmem_limit_bytes=_vmem_limit(block_bytes, scratch_bytes))


def _nbytes(shape, dtype):
    return int(np.prod(shape)) * jnp.dtype(dtype).itemsize


def _silu(x):
    return x * (1.0 / (1.0 + jnp.exp(-x)))


def _dot(a, b):
    return jnp.dot(a, b, preferred_element_type=F32)


def _dot_nt(a, b):
    return lax.dot_general(a, b, (((1,), (1,)), ((), ())), preferred_element_type=F32)


def _dot_tn(a, b):
    return lax.dot_general(a, b, (((0,), (0,)), ((), ())), preferred_element_type=F32)


def _norm_mod(x, gain, shift, scale):
    ms = jnp.mean(x * x, axis=-1, keepdims=True)
    y = x * lax.rsqrt(ms + NORM_EPS) * gain
    return y * (1.0 + scale) + shift


def _mod_specs(shift, tm):
    d = shift.shape[-1]
    if shift.shape[1] == 1:
        return pl.BlockSpec((1, 1, d), lambda b, i, j: (b, 0, 0))
    return pl.BlockSpec((1, tm, d), lambda b, i, j: (b, i, 0))


def _ada_kernel(c_ref, w_ref, b_ref, o_ref):
    a = _silu(c_ref[...]).astype(BF16)
    o_ref[0] = _dot(a, w_ref[0].astype(BF16)) + b_ref[0]


def ada_modulation(c_all, w_ada, b_ada, tn=1024):
    rows, d = c_all.shape
    depth, _, n = w_ada.shape
    blocks = _nbytes((rows, d), F32) + _nbytes((d, tn), F32) + _nbytes((rows, tn), F32)
    return pl.pallas_call(
        _ada_kernel,
        grid=(depth, n // tn),
        in_specs=[pl.BlockSpec((rows, d), lambda l, j: (0, 0)),
                  pl.BlockSpec((1, d, tn), lambda l, j: (l, 0, j)),
                  pl.BlockSpec((1, 1, tn), lambda l, j: (l, 0, j))],
        out_specs=pl.BlockSpec((1, rows, tn), lambda l, j: (l, 0, j)),
        out_shape=jax.ShapeDtypeStruct((depth, rows, n), F32),
        compiler_params=_params(2, blocks, _nbytes((d, tn), BF16)),
        name="ada_modulation",
    )(c_all, w_ada, b_ada.reshape(depth, 1, n))


def _qkv_kernel(x_ref, gain_ref, shift_ref, scale_ref, w_ref, hgain_ref, hflag_ref, gmat_ref,
                o_ref, h_s):
    @pl.when(pl.program_id(2) == 0)
    def _():
        h_s[...] = _norm_mod(x_ref[0], gain_ref[...], shift_ref[0], scale_ref[0]).astype(BF16)

    y = _dot(h_s[...], w_ref[...].astype(BF16))
    sq = y * y
    sq_hi = sq.astype(BF16)
    sq_lo = (sq - sq_hi.astype(F32)).astype(BF16)
    ms = _dot(sq_hi, gmat_ref[...]) + _dot(sq_lo, gmat_ref[...])
    yn = y * lax.rsqrt(ms + NORM_EPS) * hgain_ref[...]
    o_ref[0] = jnp.where(hflag_ref[...] > 0.0, yn, y)


def qkv_projection(x, gain, shift, scale, w, q_gain, k_gain, tm, tn=512):
    b, l, d = x.shape
    n = w.shape[1]
    nq, nk = N_HEADS * HEAD_DIM, N_KV_HEADS * HEAD_DIM
    hgain = jnp.concatenate([jnp.tile(q_gain, N_HEADS), jnp.tile(k_gain, N_KV_HEADS),
                             jnp.ones((nk,), F32)]).reshape(1, n)
    hflag = jnp.concatenate([jnp.ones((nq + nk,), F32), jnp.zeros((nk,), F32)]).reshape(1, n)
    gidx = np.arange(tn) // HEAD_DIM
    gmat = jnp.asarray((gidx[:, None] == gidx[None, :]).astype(np.float32) / HEAD_DIM, BF16)
    blocks = (_nbytes((tm, d), F32) + _nbytes((d, tn), F32) + _nbytes((tm, tn), F32)
              + 2 * _nbytes((shift.shape[1] == 1 and 1 or tm, d), F32) + _nbytes((tn, tn), BF16))
    scratch = _nbytes((tm, d), BF16) + _nbytes((d, tn), BF16) + 4 * _nbytes((tm, tn), F32)
    return pl.pallas_call(
        _qkv_kernel,
        grid=(b, l // tm, n // tn),
        in_specs=[pl.BlockSpec((1, tm, d), lambda b_, i, j: (b_, i, 0)),
                  pl.BlockSpec((1, d), lambda b_, i, j: (0, 0)),
                  _mod_specs(shift, tm), _mod_specs(scale, tm),
                  pl.BlockSpec((d, tn), lambda b_, i, j: (0, j)),
                  pl.BlockSpec((1, tn), lambda b_, i, j: (0, j)),
                  pl.BlockSpec((1, tn), lambda b_, i, j: (0, j)),
                  pl.BlockSpec((tn, tn), lambda b_, i, j: (0, 0))],
        out_specs=pl.BlockSpec((1, tm, tn), lambda b_, i, j: (b_, i, j)),
        out_shape=jax.ShapeDtypeStruct((b, l, n), F32),
        scratch_shapes=[pltpu.VMEM((tm, d), BF16)],
        compiler_params=_params(3, blocks, scratch),
        name="qkv_projection",
    )(x, gain.reshape(1, d), shift, scale, w, hgain, hflag, gmat)


def _attn_prompt_kernel(sink_ref, q_ref, cur_ref, prev_ref, bprev_ref, bcur_ref, o_ref, kz_s, vz_s):
    n = pl.program_id(1)
    blk = ATT_BLOCK
    lane = lax.broadcasted_iota(jnp.int32, (2 * blk, LANES), 1)
    lo_half = lane < HEAD_DIM
    neg_prev = jnp.where(n == 0, NEG_INF, 0.0).astype(F32)

    kv_width = N_KV_HEADS * HEAD_DIM
    for pair in range(N_KV_HEADS // 2):
        for part, dst in ((0, kz_s), (1, vz_s)):
            col = part * kv_width + pair * LANES
            both = jnp.concatenate([prev_ref[0, :, col:col + LANES], cur_ref[0, :, col:col + LANES]], axis=0)
            swapped = pltpu.roll(both, HEAD_DIM, axis=1)
            zero = jnp.zeros_like(both)
            c0, c1 = 2 * pair, 2 * pair + 1
            dst[2 * c0 + 0] = jnp.where(lo_half, both, zero).astype(BF16)
            dst[2 * c0 + 1] = jnp.where(lo_half, zero, swapped).astype(BF16)
            dst[2 * c1 + 0] = jnp.where(lo_half, swapped, zero).astype(BF16)
            dst[2 * c1 + 1] = jnp.where(lo_half, zero, both).astype(BF16)

    scale = HEAD_DIM ** -0.5
    for p in range(N_HEADS // 2):
        c = (2 * p) // ATT_GROUP
        qp = (q_ref[0, :, p * LANES:(p + 1) * LANES] * scale).astype(BF16)
        o_pair = None
        for a in range(2):
            h = 2 * p + a
            sink = sink_ref[h]
            s = _dot_nt(qp, kz_s[2 * c + a])
            s_prev = s[:, :blk] + bprev_ref[h] + neg_prev
            s_cur = s[:, blk:] + bcur_ref[h]
            m = jnp.maximum(jnp.maximum(jnp.max(s_prev, axis=-1, keepdims=True),
                                        jnp.max(s_cur, axis=-1, keepdims=True)), sink)
            e_prev = jnp.exp(s_prev - m)
            e_cur = jnp.exp(s_cur - m)
            den = (jnp.sum(e_prev, axis=-1, keepdims=True) + jnp.sum(e_cur, axis=-1, keepdims=True)
                   + jnp.exp(sink - m))
            pm = jnp.concatenate([e_prev, e_cur], axis=1).astype(BF16)
            o_a = _dot(pm, vz_s[2 * c + a]) * (1.0 / den)
            o_pair = o_a if o_pair is None else o_pair + o_a
        o_ref[0, :, p * LANES:(p + 1) * LANES] = o_pair.astype(o_ref.dtype)


def _t5_bucket_np(dist):
    max_exact = N_BUCKETS // 2
    d = np.maximum(dist, 0)
    df = np.maximum(d, 1).astype(np.float32)
    large = max_exact + (np.log(df / np.float32(max_exact)) / np.float32(math.log(MAX_DISTANCE / max_exact))
                         * np.float32(N_BUCKETS - max_exact)).astype(np.int32)
    large = np.minimum(large, N_BUCKETS - 1)
    return np.where(d < max_exact, d, large)


def _bias_from_dist(dist, in_band, rel_table):
    bucket = jnp.asarray(_t5_bucket_np(dist).reshape(-1), jnp.int32)
    tab = jnp.take(rel_table.astype(F32).T, bucket, axis=1).reshape(N_HEADS, *dist.shape)
    return jnp.where(jnp.asarray(in_band)[None], tab, NEG_INF)


def attention_prompt(qkv, sinks, rel_table):
    b, l, n = qkv.shape
    blk = ATT_BLOCK
    nq = N_HEADS * HEAD_DIM
    kvw = 2 * N_KV_HEADS * HEAD_DIM
    kv_blk = nq // kvw
    qi = np.arange(blk)[:, None]
    sj = np.arange(blk)[None, :]
    d_prev = qi + blk - sj
    d_cur = qi - sj
    bias_prev = _bias_from_dist(d_prev, (d_prev >= 0) & (d_prev <= WINDOW), rel_table)
    bias_cur = _bias_from_dist(d_cur, (d_cur >= 0) & (d_cur <= WINDOW), rel_table)
    blocks = (_nbytes((blk, nq), F32) + 2 * _nbytes((blk, kvw), F32) + _nbytes((blk, nq), BF16))
    scratch = 2 * _nbytes((2 * N_KV_HEADS, 2 * blk, LANES), BF16) + 4 * _nbytes((N_HEADS, blk, blk), F32)
    return pl.pallas_call(
        _attn_prompt_kernel,
        grid=(b, l // blk),
        in_specs=[pl.BlockSpec(memory_space=pltpu.SMEM),
                  pl.BlockSpec((1, blk, nq), lambda b_, i: (b_, i, 0)),
                  pl.BlockSpec((1, blk, kvw), lambda b_, i: (b_, i, kv_blk)),
                  pl.BlockSpec((1, blk, kvw), lambda b_, i: (b_, jnp.maximum(i - 1, 0), kv_blk)),
                  pl.BlockSpec((N_HEADS, blk, blk), lambda b_, i: (0, 0, 0)),
                  pl.BlockSpec((N_HEADS, blk, blk), lambda b_, i: (0, 0, 0))],
        out_specs=pl.BlockSpec((1, blk, nq), lambda b_, i: (b_, i, 0)),
        out_shape=jax.ShapeDtypeStruct((b, l, nq), BF16),
        scratch_shapes=[pltpu.VMEM((2 * N_KV_HEADS, 2 * blk, LANES), BF16),
                        pltpu.VMEM((2 * N_KV_HEADS, 2 * blk, LANES), BF16)],
        compiler_params=_params(2, blocks, scratch),
        name="attention_prompt",
    )(sinks.astype(F32), qkv, qkv, qkv, bias_prev, bias_cur)


def _attn_sample_kernel(q_ref, k_ref, v_ref, bias_ref, sink_ref, o_ref):
    k = k_ref[0].astype(BF16)
    v = v_ref[0].astype(BF16)
    for c in range(N_KV_HEADS):
        s = _dot_nt(q_ref[0, c].astype(BF16), k) + bias_ref[c]
        sink = sink_ref[c]
        m = jnp.maximum(jnp.max(s, axis=-1, keepdims=True), sink)
        e = jnp.exp(s - m)
        den = jnp.sum(e, axis=-1, keepdims=True) + jnp.exp(sink - m)
        o_ref[0, c] = _dot(e.astype(BF16), v) * (1.0 / den)


def attention_sample(qkv, cache_k, cache_v, sinks, rel_table):
    bsz, w = cache_k.shape[0], cache_k.shape[1]
    t = qkv.shape[0] // bsz
    nq, nkv = N_HEADS * HEAD_DIM, N_KV_HEADS * HEAD_DIM
    keys = w + t
    keys_pad = -(-keys // 16) * 16
    rows = ATT_GROUP * t
    q = qkv[:, :nq].reshape(bsz, t, N_KV_HEADS, ATT_GROUP, HEAD_DIM) * HEAD_DIM ** -0.5
    q = jnp.transpose(q, (0, 2, 3, 1, 4)).reshape(bsz, N_KV_HEADS, rows, HEAD_DIM)
    qz = jnp.einsum('bcrd,ck->bcrkd', q, jnp.eye(N_KV_HEADS, dtype=F32)).reshape(bsz, N_KV_HEADS, rows, nkv)
    k_all = jnp.concatenate([cache_k.reshape(bsz, w, nkv), qkv[:, nq:nq + nkv].reshape(bsz, t, nkv)], axis=1)
    v_all = jnp.concatenate([cache_v.reshape(bsz, w, nkv), qkv[:, nq + nkv:].reshape(bsz, t, nkv)], axis=1)
    pad = ((0, 0), (0, keys_pad - keys), (0, 0))
    k_pad, v_pad = jnp.pad(k_all, pad), jnp.pad(v_all, pad)
    tq = np.arange(t)[:, None]
    sk = np.arange(keys_pad)[None, :]
    dist = tq + w - sk
    in_band = (dist >= 0) & (dist <= WINDOW) & (sk < keys)
    bias_t = _bias_from_dist(dist, in_band, rel_table)
    bias = bias_t.reshape(N_KV_HEADS, rows, keys_pad)
    sink_rows = jnp.repeat(sinks.astype(F32), t).reshape(N_KV_HEADS, rows, 1)
    blocks = (_nbytes((N_KV_HEADS, rows, nkv), F32) * 2 + 2 * _nbytes((keys_pad, nkv), F32))
    oz = pl.pallas_call(
        _attn_sample_kernel,
        grid=(bsz,),
        in_specs=[pl.BlockSpec((1, N_KV_HEADS, rows, nkv), lambda b_: (b_, 0, 0, 0)),
                  pl.BlockSpec((1, keys_pad, nkv), lambda b_: (b_, 0, 0)),
                  pl.BlockSpec((1, keys_pad, nkv), lambda b_: (b_, 0, 0)),
                  pl.BlockSpec((N_KV_HEADS, rows, keys_pad), lambda b_: (0, 0, 0)),
                  pl.BlockSpec((N_KV_HEADS, rows, 1), lambda b_: (0, 0, 0))],
        out_specs=pl.BlockSpec((1, N_KV_HEADS, rows, nkv), lambda b_: (b_, 0, 0, 0)),
        out_shape=jax.ShapeDtypeStruct((bsz, N_KV_HEADS, rows, nkv), F32),
        compiler_params=_params(1, blocks),
        name="attention_sample",
    )(qz, k_pad, v_pad, bias, sink_rows)
    o = jnp.stack([oz[:, c, :, c * HEAD_DIM:(c + 1) * HEAD_DIM] for c in range(N_KV_HEADS)], axis=1)
    o = o.reshape(bsz, N_KV_HEADS, ATT_GROUP, t, HEAD_DIM)
    o = jnp.transpose(o, (0, 3, 1, 2, 4)).reshape(bsz * t, nq)
    return o.astype(BF16), k_all[:, t:], v_all[:, t:]


def _mm_res_kernel(a_ref, w_ref, x_ref, g_ref, o_ref):
    y = _dot(a_ref[0], w_ref[...].astype(BF16))
    o_ref[0] = x_ref[0] + g_ref[0] * y


def matmul_gate_residual(a, w, x, gate, tm, tn):
    b, l, k = a.shape
    n = w.shape[1]
    grows = 1 if gate.shape[1] == 1 else tm
    if gate.shape[1] == 1:
        g_spec = pl.BlockSpec((1, 1, tn), lambda b_, i, j: (b_, 0, j))
    else:
        g_spec = pl.BlockSpec((1, tm, tn), lambda b_, i, j: (b_, i, j))
    blocks = (_nbytes((tm, k), a.dtype) + _nbytes((k, tn), F32) + 2 * _nbytes((tm, tn), F32)
              + _nbytes((grows, tn), F32))
    scratch = _nbytes((k, tn), BF16) + _nbytes((tm, tn), F32)
    return pl.pallas_call(
        _mm_res_kernel,
        grid=(b, l // tm, n // tn),
        in_specs=[pl.BlockSpec((1, tm, k), lambda b_, i, j: (b_, i, 0)),
                  pl.BlockSpec((k, tn), lambda b_, i, j: (0, j)),
                  pl.BlockSpec((1, tm, tn), lambda b_, i, j: (b_, i, j)),
                  g_spec],
        out_specs=pl.BlockSpec((1, tm, tn), lambda b_, i, j: (b_, i, j)),
        out_shape=jax.ShapeDtypeStruct((b, l, n), F32),
        compiler_params=_params(3, blocks, scratch),
        name="matmul_gate_residual",
    )(a, w, x, gate)


def _ffn_conv_rows(u, prev1, prev2, cw_ref, cb_ref):
    return u * cw_ref[2:3, :] + prev1 * cw_ref[1:2, :] + prev2 * cw_ref[0:1, :] + cb_ref[...]


def _ffn_up_prompt_kernel(x_ref, gain_ref, shift_ref, scale_ref, wg_ref, wv_ref, cwg_ref, cwv_ref,
                          cbg_ref, cbv_ref, o_ref, tail_ref, h_s, carry_s):
    i = pl.program_id(1)
    j = pl.program_id(2)
    tm = h_s.shape[0]

    @pl.when(j == 0)
    def _():
        h_s[...] = _norm_mod(x_ref[0], gain_ref[...], shift_ref[0], scale_ref[0]).astype(BF16)

    h = h_s[...]
    row8 = lax.broadcasted_iota(jnp.int32, (SUBLANES, 1), 0)
    ys, ys_top = [], []
    for half, (w_ref, cw_ref, cb_ref) in enumerate(((wg_ref, cwg_ref, cbg_ref), (wv_ref, cwv_ref, cbv_ref))):
        u = _dot(h, w_ref[...].astype(BF16))
        ys.append(_ffn_conv_rows(u, pltpu.roll(u, 1, axis=0), pltpu.roll(u, 2, axis=0), cw_ref, cb_ref))
        top = u[0:SUBLANES]

        @pl.when(i == 0)
        def _():
            carry_s[j, half] = jnp.zeros(carry_s.shape[2:], F32)

        hist = carry_s[j, half]
        p1 = jnp.where(row8 < 1, pltpu.roll(hist, 1, axis=0), pltpu.roll(top, 1, axis=0))
        p2 = jnp.where(row8 < 2, pltpu.roll(hist, 2, axis=0), pltpu.roll(top, 2, axis=0))
        ys_top.append(_ffn_conv_rows(top, p1, p2, cw_ref, cb_ref))
        tail = u[tm - SUBLANES:tm]
        carry_s[j, half] = tail
        tail_ref[0, 0, half] = tail
    o_ref[0] = (_silu(ys[0]) * ys[1]).astype(o_ref.dtype)
    o_ref[0, 0:SUBLANES, :] = (_silu(ys_top[0]) * ys_top[1]).astype(o_ref.dtype)


def ffn_up_prompt(x, gain, shift, scale, w_up, conv_w, conv_b, tm, tn=512):
    b, l, d = x.shape
    dff = w_up.shape[1] // 2
    nj = dff // tn
    conv_b = conv_b.reshape(1, 2 * dff)
    blocks = (_nbytes((tm, d), F32) + 2 * _nbytes((d, tn), F32) + _nbytes((tm, tn), BF16)
              + 2 * _nbytes((1, d), F32) + 8 * _nbytes((SUBLANES, tn), F32))
    scratch = (_nbytes((tm, d), BF16) + _nbytes((nj, 2, SUBLANES, tn), F32) + 2 * _nbytes((d, tn), BF16)
               + 8 * _nbytes((tm, tn), F32))
    act, tail = pl.pallas_call(
        _ffn_up_prompt_kernel,
        grid=(b, l // tm, nj),
        in_specs=[pl.BlockSpec((1, tm, d), lambda b_, i, j: (b_, i, 0)),
                  pl.BlockSpec((1, d), lambda b_, i, j: (0, 0)),
                  _mod_specs(shift, tm), _mod_specs(scale, tm),
                  pl.BlockSpec((d, tn), lambda b_, i, j: (0, j)),
                  pl.BlockSpec((d, tn), lambda b_, i, j: (0, j + nj)),
                  pl.BlockSpec((FFN_CONV, tn), lambda b_, i, j: (0, j)),
                  pl.BlockSpec((FFN_CONV, tn), lambda b_, i, j: (0, j + nj)),
                  pl.BlockSpec((1, tn), lambda b_, i, j: (0, j)),
                  pl.BlockSpec((1, tn), lambda b_, i, j: (0, j + nj))],
        out_specs=[pl.BlockSpec((1, tm, tn), lambda b_, i, j: (b_, i, j)),
                   pl.BlockSpec((1, 1, 2, SUBLANES, tn), lambda b_, i, j: (b_, i, 0, 0, j))],
        out_shape=[jax.ShapeDtypeStruct((b, l, dff), BF16),
                   jax.ShapeDtypeStruct((b, l // tm, 2, SUBLANES, dff), F32)],
        scratch_shapes=[pltpu.VMEM((tm, d), BF16), pltpu.VMEM((nj, 2, SUBLANES, tn), F32)],
        compiler_params=_params(3, blocks, scratch),
        name="ffn_up_prompt",
    )(x, gain.reshape(1, d), shift, scale, w_up, w_up, conv_w, conv_w, conv_b, conv_b)
    keep = FFN_CONV - 1
    hist = jnp.transpose(tail[:, -1, :, SUBLANES - keep:, :], (0, 2, 1, 3)).reshape(b, keep, 2 * dff)
    return act, hist


def _ffn_up_sample_kernel(x_ref, gain_ref, shift_ref, scale_ref, wg_ref, wv_ref, cwg_ref, cwv_ref,
                          cbg_ref, cbv_ref, e1g_ref, e1v_ref, e2g_ref, e2v_ref, o_ref, ug_ref, uv_ref, h_s,
                          *, seq_len):
    @pl.when(pl.program_id(0) == 0)
    def _():
        h_s[...] = _norm_mod(x_ref[...], gain_ref[...], shift_ref[...], scale_ref[...]).astype(BF16)

    h = h_s[...]
    rows = h_s.shape[0]
    t = lax.broadcasted_iota(jnp.int32, (rows, 1), 0) % seq_len
    ys = []
    for w_ref, cw_ref, cb_ref, e1_ref, e2_ref, u_ref in (
            (wg_ref, cwg_ref, cbg_ref, e1g_ref, e2g_ref, ug_ref),
            (wv_ref, cwv_ref, cbv_ref, e1v_ref, e2v_ref, uv_ref)):
        u = _dot(h, w_ref[...].astype(BF16))
        u_ref[...] = u
        p1 = jnp.where(t >= 1, pltpu.roll(u, 1, axis=0), e1_ref[...])
        p2 = jnp.where(t >= 2, pltpu.roll(u, 2, axis=0), e2_ref[...])
        ys.append(_ffn_conv_rows(u, p1, p2, cw_ref, cb_ref))
    o_ref[...] = (_silu(ys[0]) * ys[1]).astype(o_ref.dtype)


def ffn_up_sample(x, gain, shift, scale, w_up, conv_w, conv_b, hist, seq_len, tn=512):
    rows, d = x.shape
    dff = w_up.shape[1] // 2
    nj = dff // tn
    bsz = rows // seq_len
    conv_b = conv_b.reshape(1, 2 * dff)
    tpos = np.arange(seq_len)
    sel = lambda cond: jnp.asarray(cond.astype(np.float32))[None, :, None]
    h0, h1 = hist[:, 0:1, :], hist[:, 1:2, :]
    e1 = (h1 * sel(tpos == 0)).reshape(rows, 2 * dff)
    e2 = (h0 * sel(tpos == 0) + h1 * sel(tpos == 1)).reshape(rows, 2 * dff)
    col = lambda j: (0, j)
    col_hi = lambda j: (0, j + nj)
    blocks = (_nbytes((rows, d), F32) * 3 + 2 * _nbytes((d, tn), F32) + 7 * _nbytes((rows, tn), F32))
    scratch = _nbytes((rows, d), BF16) + 2 * _nbytes((d, tn), BF16) + 8 * _nbytes((rows, tn), F32)
    act, ug, uv = pl.pallas_call(
        functools.partial(_ffn_up_sample_kernel, seq_len=seq_len),
        grid=(nj,),
        in_specs=[pl.BlockSpec((rows, d), lambda j: (0, 0)),
                  pl.BlockSpec((1, d), lambda j: (0, 0)),
                  pl.BlockSpec((rows, d), lambda j: (0, 0)),
                  pl.BlockSpec((rows, d), lambda j: (0, 0)),
                  pl.BlockSpec((d, tn), col), pl.BlockSpec((d, tn), col_hi),
                  pl.BlockSpec((FFN_CONV, tn), col), pl.BlockSpec((FFN_CONV, tn), col_hi),
                  pl.BlockSpec((1, tn), col), pl.BlockSpec((1, tn), col_hi),
                  pl.BlockSpec((rows, tn), col), pl.BlockSpec((rows, tn), col_hi),
                  pl.BlockSpec((rows, tn), col), pl.BlockSpec((rows, tn), col_hi)],
        out_specs=[pl.BlockSpec((rows, tn), col), pl.BlockSpec((rows, tn), col), pl.BlockSpec((rows, tn), col)],
        out_shape=[jax.ShapeDtypeStruct((rows, dff), BF16),
                   jax.ShapeDtypeStruct((rows, dff), F32),
                   jax.ShapeDtypeStruct((rows, dff), F32)],
        scratch_shapes=[pltpu.VMEM((rows, d), BF16)],
        compiler_params=_params(1, blocks, scratch),
        name="ffn_up_sample",
    )(x, gain.reshape(1, d), shift, scale, w_up, w_up, conv_w, conv_w, conv_b, conv_b, e1, e1, e2, e2)
    keep = FFN_CONV - 1
    u_all = jnp.concatenate([ug, uv], axis=-1).reshape(bsz, seq_len, 2 * dff)
    new_hist = jnp.concatenate([hist, u_all], axis=1)[:, -keep:]
    return act, new_hist


def _gdn_in_kernel(x_ref, gain_ref, shift_ref, scale_ref, w_ref, wba_ref, alog_ref, dtb_ref,
                   o_ref, bg_ref, h_s):
    @pl.when(pl.program_id(2) == 0)
    def _():
        h = _norm_mod(x_ref[0], gain_ref[...], shift_ref[0], scale_ref[0]).astype(BF16)
        h_s[...] = h
        y = _dot(h, wba_ref[...].astype(BF16))
        lane = lax.broadcasted_iota(jnp.int32, y.shape, 1) % LANES
        beta = 1.0 / (1.0 + jnp.exp(-y))
        a = y + dtb_ref[...]
        softplus = jnp.maximum(a, 0.0) + jnp.log1p(jnp.exp(-jnp.abs(a)))
        g = -jnp.exp(alog_ref[...]) * softplus
        bg_ref[0] = jnp.where(lane < GDN_VH_PER_GROUP, beta, jnp.where(lane < 2 * GDN_VH_PER_GROUP, g, 0.0))

    o_ref[0] = _dot(h_s[...], w_ref[...].astype(BF16)).astype(o_ref.dtype)


def _group_lane_layout(vec_b, vec_a):
    lead = vec_b.shape[:-1]
    vb = vec_b.reshape(*lead, GDN_HEAD_GROUPS, GDN_VH_PER_GROUP)
    va = vec_a.reshape(*lead, GDN_HEAD_GROUPS, GDN_VH_PER_GROUP)
    pad = jnp.zeros((*lead, GDN_HEAD_GROUPS, LANES - 2 * GDN_VH_PER_GROUP), vec_b.dtype)
    return jnp.concatenate([vb, va, pad], axis=-1).reshape(*lead, GDN_HEAD_GROUPS * LANES)


def gdn_in_projection(x, gain, shift, scale, w_in, a_log, dt_bias, tm, tn=512):
    b, l, d = x.shape
    n_main = GDN_CONV_DIM + GDN_VAL_DIM
    w_b = w_in[:, n_main:n_main + GDN_V_HEADS]
    w_a = w_in[:, n_main + GDN_V_HEADS:]
    wba = _group_lane_layout(w_b, w_a)
    zeros = jnp.zeros((1, GDN_V_HEADS), F32)
    alog = _group_lane_layout(zeros, a_log.reshape(1, -1).astype(F32))
    dtb = _group_lane_layout(zeros, dt_bias.reshape(1, -1).astype(F32))
    nbg = wba.shape[1]
    blocks = (_nbytes((tm, d), F32) + _nbytes((d, tn), F32) + _nbytes((tm, tn), BF16)
              + _nbytes((d, nbg), F32) + _nbytes((tm, nbg), F32) + 2 * _nbytes((1, d), F32))
    scratch = _nbytes((tm, d), BF16) + _nbytes((d, tn), BF16) + 6 * _nbytes((tm, nbg), F32)
    return pl.pallas_call(
        _gdn_in_kernel,
        grid=(b, l // tm, n_main // tn),
        in_specs=[pl.BlockSpec((1, tm, d), lambda b_, i, j: (b_, i, 0)),
                  pl.BlockSpec((1, d), lambda b_, i, j: (0, 0)),
                  _mod_specs(shift, tm), _mod_specs(scale, tm),
                  pl.BlockSpec((d, tn), lambda b_, i, j: (0, j)),
                  pl.BlockSpec((d, nbg), lambda b_, i, j: (0, 0)),
                  pl.BlockSpec((1, nbg), lambda b_, i, j: (0, 0)),
                  pl.BlockSpec((1, nbg), lambda b_, i, j: (0, 0))],
        out_specs=[pl.BlockSpec((1, tm, tn), lambda b_, i, j: (b_, i, j)),
                   pl.BlockSpec((1, tm, nbg), lambda b_, i, j: (b_, i, 0))],
        out_shape=[jax.ShapeDtypeStruct((b, l, n_main), BF16),
                   jax.ShapeDtypeStruct((b, l, nbg), F32)],
        scratch_shapes=[pltpu.VMEM((tm, d), BF16)],
        compiler_params=_params(3, blocks, scratch),
        name="gdn_in_projection",
    )(x, gain.reshape(1, d), shift, scale, w_in, wba, alog, dtb)


def _causal_conv_silu(x, carry, w_ref):
    width = GDN_CONV
    row8 = lax.broadcasted_iota(jnp.int32, (SUBLANES, 1), 0)
    top = x[0:SUBLANES]
    y = x * w_ref[width - 1:width, :]
    y_top = top * w_ref[width - 1:width, :]
    for k in range(1, width):
        wk = w_ref[width - 1 - k:width - k, :]
        y = y + pltpu.roll(x, k, axis=0) * wk
        y_top = y_top + jnp.where(row8 < k, pltpu.roll(carry, k, axis=0), pltpu.roll(top, k, axis=0)) * wk
    return _silu(y), _silu(y_top)


def _l2_normalize_heads(x, scale):
    outs = []
    for hd in range(x.shape[1] // LANES):
        xh = x[:, hd * LANES:(hd + 1) * LANES]
        outs.append(xh * (lax.rsqrt(jnp.sum(xh * xh, axis=-1, keepdims=True) + NORM_EPS) * scale))
    return outs


def _gdn_core_kernel(q_ref, k_ref, v_ref, z_ref, bg_ref, hq_ref, hk_ref, hv_ref, cwq_ref, cwk_ref, cwv_ref,
                     onorm_ref, s0_ref, o_ref, s1_ref, s_s, cq_s, ck_s, cv_s, qs_s, ks_s, vs_s,
                     gc_s, tbd_s, qkbd_s):
    r = pl.program_id(2)
    n_r = pl.num_programs(2)
    rows = q_ref.shape[1]
    chunk = GDN_CHUNK
    vpg = GDN_VH_PER_GROUP

    @pl.when(r == 0)
    def _():
        s_s[...] = s0_ref[0]
        cq_s[...] = hq_ref[0]
        ck_s[...] = hk_ref[0]
        cv_s[...] = hv_ref[0]

    for src_ref, carry_s, w_ref, dst_s, l2scale in ((q_ref, cq_s, cwq_ref, qs_s, GDN_DK ** -0.5),
                                                    (k_ref, ck_s, cwk_ref, ks_s, 1.0),
                                                    (v_ref, cv_s, cwv_ref, vs_s, None)):
        x = src_ref[0].astype(F32)
        y, y_top = _causal_conv_silu(x, carry_s[...], w_ref)
        carry_s[...] = x[rows - SUBLANES:rows]
        if l2scale is None:
            dst_s[...] = y
            dst_s[0:SUBLANES, :] = y_top
        else:
            for hd, (yh, yh_top) in enumerate(zip(_l2_normalize_heads(y, l2scale),
                                                  _l2_normalize_heads(y_top, l2scale))):
                dst_s[:, hd * LANES:(hd + 1) * LANES] = yh
                dst_s[0:SUBLANES, hd * LANES:(hd + 1) * LANES] = yh_top

    n_chunks = rows // chunk
    pw = GDN_PACK * chunk
    n_packs = vpg // GDN_PACK
    rep = GDN_V_HEADS // GDN_K_HEADS
    ri = lax.broadcasted_iota(jnp.int32, (chunk, pw), 0)
    li = lax.broadcasted_iota(jnp.int32, (chunk, pw), 1) % chunk
    tril_p = ri >= li
    strict_p = ri > li
    eye_p = (ri == li).astype(F32)
    n_levels = int(math.log2(chunk))
    off_masks = [((ri >> (lvl + 1)) == (li >> (lvl + 1))) & ((ri >> lvl) != (li >> lvl))
                 for lvl in range(n_levels)]
    bd_rows = lax.broadcasted_iota(jnp.int32, (pw, pw), 0) // chunk
    bd_cols = lax.broadcasted_iota(jnp.int32, (pw, pw), 1) // chunk
    bd_mask = (bd_rows == bd_cols).astype(BF16)
    row_c = lax.broadcasted_iota(jnp.int32, (chunk, LANES), 0)
    lane_lo = lax.broadcasted_iota(jnp.int32, (chunk, LANES), 1) < chunk

    def block_diag(xp):
        return jnp.concatenate([xp.astype(BF16)] * GDN_PACK, axis=0) * bd_mask

    def pack_cols(arr, first_col):
        tiles = []
        for t in range(pw // LANES):
            even = jnp.broadcast_to(arr[:, first_col + 2 * t:first_col + 2 * t + 1], (chunk, LANES))
            odd = jnp.broadcast_to(arr[:, first_col + 2 * t + 1:first_col + 2 * t + 2], (chunk, LANES))
            tiles.append(jnp.where(lane_lo, even, odd))
        return jnp.concatenate(tiles, axis=1)

    for c0 in range(0, n_chunks, GDN_PHASE_A_CHUNKS):
        group = [(c, p) for c in range(c0, min(c0 + GDN_PHASE_A_CHUNKS, n_chunks)) for p in range(n_packs)]
        a_list = {}
        for c in range(c0, min(c0 + GDN_PHASE_A_CHUNKS, n_chunks)):
            rs = slice(c * chunk, (c + 1) * chunk)
            bg = bg_ref[0, rs, :]
            cum = bg
            shift = 1
            while shift < chunk:
                cum = cum + jnp.where(row_c >= shift, pltpu.roll(cum, shift, axis=0), 0.0)
                shift *= 2
            gc_s[rs, :] = cum
            gq = []
            for kh in range(GDN_KH_PER_GROUP):
                ksl = slice(kh * LANES, (kh + 1) * LANES)
                k = ks_s[rs, ksl]
                kq = jnp.concatenate([k, qs_s[rs, ksl]], axis=0).astype(BF16)
                kk = jnp.concatenate([k, k], axis=0).astype(BF16)
                gq.append(_dot_nt(kq, kk))
            for p in range(n_packs):
                khs = [(p * GDN_PACK + 2 * t) // rep for t in range(pw // LANES)]
                gram = jnp.concatenate([gq[kh][:chunk] for kh in khs], axis=1)
                qk = jnp.concatenate([gq[kh][chunk:] for kh in khs], axis=1)
                beta_p = pack_cols(bg, p * GDN_PACK)
                gcol_p = pack_cols(cum, vpg + p * GDN_PACK)
                grow_p = jnp.sum(gcol_p * eye_p, axis=0, keepdims=True)
                decay = jnp.exp(jnp.where(tril_p, gcol_p - grow_p, NEG_INF))
                a_list[(c, p)] = jnp.where(strict_p, gram * beta_p * decay, 0.0)
                qkbd_s[c, p] = block_diag(qk * decay)
        inv = {cp: eye_p - jnp.where(off_masks[0], a_list[cp], 0.0) for cp in group}
        for lvl in range(1, n_levels):
            w = {cp: _dot(jnp.where(off_masks[lvl], a_list[cp], 0.0).astype(BF16), block_diag(inv[cp]))
                 for cp in group}
            inv = {cp: inv[cp] - _dot(inv[cp].astype(BF16), block_diag(w[cp])) for cp in group}
        for c, p in group:
            tbd_s[c, p] = block_diag(inv[(c, p)])

    def chunk_body(c, carry):
        rs = pl.ds(pl.multiple_of(c * chunk, chunk), chunk)
        bg = bg_ref[0, rs, :]
        cum = gc_s[rs, :]
        beta, gcol, g_last = [], [], []
        ks_qs = []
        for hv in range(vpg):
            kh = hv // rep
            ksl = slice(kh * LANES, (kh + 1) * LANES)
            beta.append(bg[:, hv:hv + 1])
            gcol.append(cum[:, vpg + hv:vpg + hv + 1])
            g_last.append(gcol[hv][chunk - 1:chunk, :])
            eg = jnp.exp(gcol[hv])
            lhs = jnp.concatenate([ks_s[rs, ksl] * (beta[hv] * eg), qs_s[rs, ksl] * eg], axis=0)
            ks_qs.append(_dot(lhs.astype(BF16), s_s[hv].astype(BF16)))
        v_new, o_intra = [], []
        for p in range(n_packs):
            heads = range(p * GDN_PACK, (p + 1) * GDN_PACK)
            rhs = jnp.concatenate([vs_s[rs, hv * LANES:(hv + 1) * LANES] * beta[hv] - ks_qs[hv][:chunk]
                                   for hv in heads], axis=0).astype(BF16)
            vn = _dot(tbd_s[c, p], rhs)
            oi = _dot(qkbd_s[c, p], vn.astype(BF16))
            for j in range(GDN_PACK):
                v_new.append(vn[j * chunk:(j + 1) * chunk])
                o_intra.append(oi[j * chunk:(j + 1) * chunk])
        for hv in range(vpg):
            kh = hv // rep
            vsl = slice(hv * LANES, (hv + 1) * LANES)
            k_bf = ks_s[rs, kh * LANES:(kh + 1) * LANES].astype(BF16)
            v_dec = (v_new[hv] * jnp.exp(g_last[hv] - gcol[hv])).astype(BF16)
            s_s[hv] = s_s[hv] * jnp.exp(g_last[hv]) + _dot_tn(k_bf, v_dec)
            o = ks_qs[hv][chunk:] + o_intra[hv]
            on = o * lax.rsqrt(jnp.mean(o * o, axis=-1, keepdims=True) + NORM_EPS) * onorm_ref[...]
            z = z_ref[0, rs, vsl].astype(F32)
            o_ref[0, rs, vsl] = (on * _silu(z)).astype(o_ref.dtype)
        return carry

    lax.fori_loop(0, n_chunks, chunk_body, 0)

    @pl.when(r == n_r - 1)
    def _():
        s1_ref[0] = s_s[...]


def gdn_core(mixed_z, bg, hist, s0, conv_w, out_norm, rows):
    b, l, _ = mixed_z.shape
    hg = GDN_HEAD_GROUPS
    qw = GDN_KEY_DIM // hg
    vw = GDN_VAL_DIM // hg
    kq = GDN_KEY_DIM // qw
    kv = 2 * GDN_KEY_DIM // vw
    kz = GDN_CONV_DIM // vw
    vpg = GDN_VH_PER_GROUP
    blocks = (2 * _nbytes((rows, qw), BF16) + 3 * _nbytes((rows, vw), BF16) + _nbytes((rows, LANES), F32)
              + 2 * _nbytes((vpg, GDN_DK, GDN_DV), F32) + 4 * _nbytes((SUBLANES, vw), F32))
    n_chunks = rows // GDN_CHUNK
    n_packs = vpg // GDN_PACK
    pw = GDN_PACK * GDN_CHUNK
    scratch = (_nbytes((vpg, GDN_DK, GDN_DV), F32) + 2 * _nbytes((rows, qw), F32) + _nbytes((rows, vw), F32)
               + 2 * _nbytes((n_chunks, n_packs, pw, pw), BF16) + 6 * _nbytes((rows, vw), F32))
    return pl.pallas_call(
        _gdn_core_kernel,
        grid=(b, hg, l // rows),
        in_specs=[pl.BlockSpec((1, rows, qw), lambda b_, g, r: (b_, r, g)),
                  pl.BlockSpec((1, rows, qw), lambda b_, g, r: (b_, r, kq + g)),
                  pl.BlockSpec((1, rows, vw), lambda b_, g, r: (b_, r, kv + g)),
                  pl.BlockSpec((1, rows, vw), lambda b_, g, r: (b_, r, kz + g)),
                  pl.BlockSpec((1, rows, LANES), lambda b_, g, r: (b_, r, g)),
                  pl.BlockSpec((1, SUBLANES, qw), lambda b_, g, r: (b_, 0, g)),
                  pl.BlockSpec((1, SUBLANES, qw), lambda b_, g, r: (b_, 0, kq + g)),
                  pl.BlockSpec((1, SUBLANES, vw), lambda b_, g, r: (b_, 0, kv + g)),
                  pl.BlockSpec((GDN_CONV, qw), lambda b_, g, r: (0, g)),
                  pl.BlockSpec((GDN_CONV, qw), lambda b_, g, r: (0, kq + g)),
                  pl.BlockSpec((GDN_CONV, vw), lambda b_, g, r: (0, kv + g)),
                  pl.BlockSpec((1, GDN_DV), lambda b_, g, r: (0, 0)),
                  pl.BlockSpec((1, vpg, GDN_DK, GDN_DV), lambda b_, g, r: (b_, g, 0, 0))],
        out_specs=[pl.BlockSpec((1, rows, vw), lambda b_, g, r: (b_, r, g)),
                   pl.BlockSpec((1, vpg, GDN_DK, GDN_DV), lambda b_, g, r: (b_, g, 0, 0))],
        out_shape=[jax.ShapeDtypeStruct((b, l, GDN_VAL_DIM), BF16),
                   jax.ShapeDtypeStruct((b, GDN_V_HEADS, GDN_DK, GDN_DV), F32)],
        scratch_shapes=[pltpu.VMEM((vpg, GDN_DK, GDN_DV), F32),
                        pltpu.VMEM((SUBLANES, qw), F32), pltpu.VMEM((SUBLANES, qw), F32),
                        pltpu.VMEM((SUBLANES, vw), F32),
                        pltpu.VMEM((rows, qw), F32), pltpu.VMEM((rows, qw), F32), pltpu.VMEM((rows, vw), F32),
                        pltpu.VMEM((rows, LANES), F32),
                        pltpu.VMEM((n_chunks, n_packs, pw, pw), BF16),
                        pltpu.VMEM((n_chunks, n_packs, pw, pw), BF16)],
        compiler_params=_params(3, blocks, scratch),
        name="gdn_core",
    )(mixed_z, mixed_z, mixed_z, mixed_z, bg, hist, hist, hist, conv_w, conv_w, conv_w,
      out_norm.reshape(1, GDN_DV).astype(F32), s0)


def _tile_rows(l, cap):
    t = min(l, cap)
    while l % t:
        t //= 2
    return t


def _trunk(x, mod, states, P, sample):
    bsz, l, d = x.shape
    tm = _tile_rows(l, ROW_TILE_CAP)
    new = {}
    depth = mod.shape[0]
    for layer in range(depth):
        if sample:
            parts = [mod[layer][None, :, k * d:(k + 1) * d] for k in range(6)]
        else:
            parts = [mod[layer][:, None, k * d:(k + 1) * d] for k in range(6)]
        sh_m, sc_m, g_m, sh_f, sc_f, g_f = parts
        i = layer // 2
        if layer % 2 == 0:
            qkv = qkv_projection(x, P['norm_mix'][layer], sh_m, sc_m, P['w_attn_qkv'][i],
                                 P['attn_q_norm'][i], P['attn_k_norm'][i], tm)
            nq, nkv = N_HEADS * HEAD_DIM, N_KV_HEADS * HEAD_DIM
            if sample:
                o, k_win, v_win = attention_sample(qkv[0], states['win_k'][i], states['win_v'][i],
                                                   P['attn_sinks'][i], P['rel_bias_table'])
                o = o[None]
                w_len = k_win.shape[1]
                new.setdefault('win_k', []).append(k_win.reshape(-1, w_len, N_KV_HEADS, HEAD_DIM))
                new.setdefault('win_v', []).append(v_win.reshape(-1, w_len, N_KV_HEADS, HEAD_DIM))
            else:
                o = attention_prompt(qkv, P['attn_sinks'][i], P['rel_bias_table'])
                keep = min(WINDOW, PAST_LEN)
                new.setdefault('win_k', []).append(
                    qkv[:, l - keep:, nq:nq + nkv].reshape(bsz, keep, N_KV_HEADS, HEAD_DIM))
                new.setdefault('win_v', []).append(
                    qkv[:, l - keep:, nq + nkv:].reshape(bsz, keep, N_KV_HEADS, HEAD_DIM))
            x = matmul_gate_residual(o, P['w_attn_o'][i], x, g_m, tm, 512)
        else:
            mixed_z, bg = gdn_in_projection(x, P['norm_mix'][layer], sh_m, sc_m, P['w_gdn_in'][i],
                                            P['gdn_a_log'][i], P['gdn_dt_bias'][i], tm)
            keep = GDN_CONV - 1
            if sample:
                seqs = states['gdn'].shape[1]
                t = l // seqs
                pad_rows = GDN_CHUNK - t
                pad3 = lambda a: jnp.pad(a.reshape(seqs, t, a.shape[-1]), ((0, 0), (0, pad_rows), (0, 0)))
                mz_seq, bg_seq = pad3(mixed_z[0]), pad3(bg[0])
                conv0 = states['gdn_conv'][i].astype(F32)
                hist = jnp.pad(conv0, ((0, 0), (SUBLANES - keep, 0), (0, 0)))
                o_seq, s1 = gdn_core(mz_seq, bg_seq, hist, states['gdn'][i].astype(F32),
                                     P['gdn_conv_w'][i], P['gdn_out_norm'][i], GDN_CHUNK)
                o = o_seq[:, :t].reshape(1, l, GDN_VAL_DIM)
                mixed_rows = mixed_z[0, :, :GDN_CONV_DIM].astype(F32).reshape(seqs, t, GDN_CONV_DIM)
                new.setdefault('gdn_conv', []).append(jnp.concatenate([conv0, mixed_rows], axis=1)[:, -keep:])
            else:
                hist = jnp.zeros((bsz, SUBLANES, GDN_CONV_DIM), F32)
                s0 = jnp.zeros((bsz, GDN_V_HEADS, GDN_DK, GDN_DV), F32)
                o, s1 = gdn_core(mixed_z, bg, hist, s0, P['gdn_conv_w'][i], P['gdn_out_norm'][i],
                                 _tile_rows(l, GDN_ROW_TILE_CAP))
                new.setdefault('gdn_conv', []).append(mixed_z[:, l - keep:, :GDN_CONV_DIM].astype(F32))
            new.setdefault('gdn', []).append(s1)
            x = matmul_gate_residual(o, P['w_gdn_out'][i], x, g_m, tm, 512)
        if sample:
            seqs = states['ffn_conv'].shape[1]
            act, f1 = ffn_up_sample(x[0], P['norm_ffn'][layer], sh_f[0], sc_f[0], P['w_ffn_up'][layer],
                                    P['ffn_conv_w'][layer], P['ffn_conv_b'][layer],
                                    states['ffn_conv'][layer].astype(F32), l // seqs)
            act = act[None]
        else:
            act, f1 = ffn_up_prompt(x, P['norm_ffn'][layer], sh_f, sc_f, P['w_ffn_up'][layer],
                                    P['ffn_conv_w'][layer], P['ffn_conv_b'][layer], tm)
        new.setdefault('ffn_conv', []).append(f1)
        x = matmul_gate_residual(act, P['w_ffn_down'][layer], x, g_f, tm, 256)
    return x, {k: jnp.stack(v) for k, v in new.items()}


def kernel(x_prompt, x_sample, c_prompt, c_sample, cache_win_k, cache_win_v, state_gdn, state_gdn_conv,
           state_ffn_conv, rel_bias_table, w_ada, b_ada, norm_mix, norm_ffn, w_attn_qkv, attn_q_norm,
           attn_k_norm, attn_sinks, w_attn_o, w_gdn_in, gdn_conv_w, gdn_a_log, gdn_dt_bias, gdn_out_norm,
           w_gdn_out, w_ffn_up, ffn_conv_w, ffn_conv_b, w_ffn_down):
    P = dict(rel_bias_table=rel_bias_table, norm_mix=norm_mix, norm_ffn=norm_ffn, w_attn_qkv=w_attn_qkv,
             attn_q_norm=attn_q_norm, attn_k_norm=attn_k_norm, attn_sinks=attn_sinks, w_attn_o=w_attn_o,
             w_gdn_in=w_gdn_in, gdn_conv_w=gdn_conv_w, gdn_a_log=gdn_a_log, gdn_dt_bias=gdn_dt_bias,
             gdn_out_norm=gdn_out_norm, w_gdn_out=w_gdn_out, w_ffn_up=w_ffn_up, ffn_conv_w=ffn_conv_w,
             ffn_conv_b=ffn_conv_b, w_ffn_down=w_ffn_down)
    bp = x_prompt.shape[0]
    bs, ts, d = x_sample.shape
    n_c = bp + bs
    c_rows = -(-n_c // SUBLANES) * SUBLANES
    c_all = jnp.pad(jnp.concatenate([c_prompt, c_sample], axis=0), ((0, c_rows - n_c), (0, 0)))
    mod = ada_modulation(c_all, w_ada, b_ada)
    mod_prompt = mod[:, :bp]
    mod_sample = jnp.repeat(mod[:, bp:n_c], ts, axis=1)

    y_p, new_p = _trunk(x_prompt, mod_prompt, None, P, sample=False)
    states = dict(win_k=cache_win_k, win_v=cache_win_v, gdn=state_gdn, gdn_conv=state_gdn_conv,
                  ffn_conv=state_ffn_conv)
    y_s, new_s = _trunk(x_sample.reshape(1, bs * ts, d), mod_sample, states, P, sample=True)
    y_s = y_s.reshape(bs, ts, d)
    return (y_p, y_s, new_p['win_k'], new_p['win_v'], new_s['win_k'], new_s['win_v'],
            new_p['gdn'], new_s['gdn'], new_p['gdn_conv'], new_s['gdn_conv'],
            new_p['ffn_conv'], new_s['ffn_conv'])
```

```python
import functools
import math

import numpy as np
import jax
import jax.numpy as jnp
from jax import lax
from jax.experimental import pallas as pl
from jax.experimental.pallas import tpu as pltpu

HEAD_DIM = 64
N_HEADS = 32
N_KV_HEADS = 4
ATT_GROUP = N_HEADS // N_KV_HEADS
WINDOW = 128
ATT_BLOCK = 128
N_BUCKETS = 32
MAX_DISTANCE = 128
NEG_INF = -1e30
PAST_LEN = 16384

GDN_K_HEADS = 16
GDN_V_HEADS = 32
GDN_DK = 128
GDN_DV = 128
GDN_KEY_DIM = GDN_K_HEADS * GDN_DK
GDN_VAL_DIM = GDN_V_HEADS * GDN_DV
GDN_CONV_DIM = 2 * GDN_KEY_DIM + GDN_VAL_DIM
GDN_CONV = 4
GDN_CHUNK = 64
GDN_HEAD_GROUPS = 4
GDN_VH_PER_GROUP = GDN_V_HEADS // GDN_HEAD_GROUPS
GDN_KH_PER_GROUP = GDN_K_HEADS // GDN_HEAD_GROUPS
GDN_PACK = 4
GDN_PHASE_A_CHUNKS = 4

FFN_CONV = 3
NORM_EPS = 1e-6

LANES = 128
SUBLANES = 8
VMEM_CAP_BYTES = 60 * 1024 * 1024
VMEM_SLACK_BYTES = 8 * 1024 * 1024

ROW_TILE_CAP = 1024
FFN_SUB_ROWS = 256
GDN_ROW_TILE_CAP = 512

BF16 = jnp.bfloat16
F32 = jnp.float32


def _vmem_limit(block_bytes, scratch_bytes=0):
    est = 2 * int(block_bytes) + int(scratch_bytes) + VMEM_SLACK_BYTES
    return int(min(max(est, 16 * 1024 * 1024), VMEM_CAP_BYTES))


def _params(n_grid, block_bytes, scratch_bytes=0):
    return pltpu.CompilerParams(
        dimension_semantics=("arbitrary",) * n_grid,
        vmem_limit_bytes=_vmem_limit(block_bytes, scratch_bytes))


def _nbytes(shape, dtype):
    return int(np.prod(shape)) * jnp.dtype(dtype).itemsize


def _silu(x):
    return x * (1.0 / (1.0 + jnp.exp(-x)))


def _dot(a, b):
    return jnp.dot(a, b, preferred_element_type=F32)


def _dot_nt(a, b):
    return lax.dot_general(a, b, (((1,), (1,)), ((), ())), preferred_element_type=F32)


def _dot_tn(a, b):
    return lax.dot_general(a, b, (((0,), (0,)), ((), ())), preferred_element_type=F32)


def _norm_mod(x, gain, shift, scale):
    ms = jnp.mean(x * x, axis=-1, keepdims=True)
    y = x * lax.rsqrt(ms + NORM_EPS) * gain
    return y * (1.0 + scale) + shift


def _mod_specs(shift, tm):
    d = shift.shape[-1]
    if shift.shape[1] == 1:
        return pl.BlockSpec((1, 1, d), lambda b, i, j: (b, 0, 0))
    return pl.BlockSpec((1, tm, d), lambda b, i, j: (b, i, 0))


def _ada_kernel(c_ref, w_ref, b_ref, o_ref):
    a = _silu(c_ref[...]).astype(BF16)
    o_ref[0] = _dot(a, w_ref[0].astype(BF16)) + b_ref[0]


def ada_modulation(c_all, w_ada, b_ada, tn=1024):
    rows, d = c_all.shape
    depth, _, n = w_ada.shape
    blocks = _nbytes((rows, d), F32) + _nbytes((d, tn), F32) + _nbytes((rows, tn), F32)
    return pl.pallas_call(
        _ada_kernel,
        grid=(depth, n // tn),
        in_specs=[pl.BlockSpec((rows, d), lambda l, j: (0, 0)),
                  pl.BlockSpec((1, d, tn), lambda l, j: (l, 0, j)),
                  pl.BlockSpec((1, 1, tn), lambda l, j: (l, 0, j))],
        out_specs=pl.BlockSpec((1, rows, tn), lambda l, j: (l, 0, j)),
        out_shape=jax.ShapeDtypeStruct((depth, rows, n), F32),
        compiler_params=_params(2, blocks, _nbytes((d, tn), BF16)),
        name="ada_modulation",
    )(c_all, w_ada, b_ada.reshape(depth, 1, n))


def _qkv_kernel(x_ref, gain_ref, shift_ref, scale_ref, w_ref, hgain_ref, hflag_ref, gmat_ref,
                o_ref, h_s):
    @pl.when(pl.program_id(2) == 0)
    def _():
        h_s[...] = _norm_mod(x_ref[0], gain_ref[...], shift_ref[0], scale_ref[0]).astype(BF16)

    y = _dot(h_s[...], w_ref[...].astype(BF16))
    sq = y * y
    sq_hi = sq.astype(BF16)
    sq_lo = (sq - sq_hi.astype(F32)).astype(BF16)
    ms = _dot(sq_hi, gmat_ref[...]) + _dot(sq_lo, gmat_ref[...])
    yn = y * lax.rsqrt(ms + NORM_EPS) * hgain_ref[...]
    o_ref[0] = jnp.where(hflag_ref[...] > 0.0, yn, y)


def qkv_projection(x, gain, shift, scale, w, q_gain, k_gain, tm, tn=512):
    b, l, d = x.shape
    n = w.shape[1]
    nq, nk = N_HEADS * HEAD_DIM, N_KV_HEADS * HEAD_DIM
    hgain = jnp.concatenate([jnp.tile(q_gain, N_HEADS), jnp.tile(k_gain, N_KV_HEADS),
                             jnp.ones((nk,), F32)]).reshape(1, n)
    hflag = jnp.concatenate([jnp.ones((nq + nk,), F32), jnp.zeros((nk,), F32)]).reshape(1, n)
    gidx = np.arange(tn) // HEAD_DIM
    gmat = jnp.asarray((gidx[:, None] == gidx[None, :]).astype(np.float32) / HEAD_DIM, BF16)
    blocks = (_nbytes((tm, d), F32) + _nbytes((d, tn), F32) + _nbytes((tm, tn), F32)
              + 2 * _nbytes((shift.shape[1] == 1 and 1 or tm, d), F32) + _nbytes((tn, tn), BF16))
    scratch = _nbytes((tm, d), BF16) + _nbytes((d, tn), BF16) + 4 * _nbytes((tm, tn), F32)
    return pl.pallas_call(
        _qkv_kernel,
        grid=(b, l // tm, n // tn),
        in_specs=[pl.BlockSpec((1, tm, d), lambda b_, i, j: (b_, i, 0)),
                  pl.BlockSpec((1, d), lambda b_, i, j: (0, 0)),
                  _mod_specs(shift, tm), _mod_specs(scale, tm),
                  pl.BlockSpec((d, tn), lambda b_, i, j: (0, j)),
                  pl.BlockSpec((1, tn), lambda b_, i, j: (0, j)),
                  pl.BlockSpec((1, tn), lambda b_, i, j: (0, j)),
                  pl.BlockSpec((tn, tn), lambda b_, i, j: (0, 0))],
        out_specs=pl.BlockSpec((1, tm, tn), lambda b_, i, j: (b_, i, j)),
        out_shape=jax.ShapeDtypeStruct((b, l, n), F32),
        scratch_shapes=[pltpu.VMEM((tm, d), BF16)],
        compiler_params=_params(3, blocks, scratch),
        name="qkv_projection",
    )(x, gain.reshape(1, d), shift, scale, w, hgain, hflag, gmat)


def _attn_prompt_kernel(sink_ref, q_ref, cur_ref, prev_ref, bprev_ref, bcur_ref, o_ref, kz_s, vz_s):
    n = pl.program_id(1)
    blk = ATT_BLOCK
    lane = lax.broadcasted_iota(jnp.int32, (2 * blk, LANES), 1)
    lo_half = lane < HEAD_DIM
    neg_prev = jnp.where(n == 0, NEG_INF, 0.0).astype(F32)

    kv_width = N_KV_HEADS * HEAD_DIM
    for pair in range(N_KV_HEADS // 2):
        for part, dst in ((0, kz_s), (1, vz_s)):
            col = part * kv_width + pair * LANES
            both = jnp.concatenate([prev_ref[0, :, col:col + LANES], cur_ref[0, :, col:col + LANES]], axis=0)
            swapped = pltpu.roll(both, HEAD_DIM, axis=1)
            zero = jnp.zeros_like(both)
            c0, c1 = 2 * pair, 2 * pair + 1
            dst[2 * c0 + 0] = jnp.where(lo_half, both, zero).astype(BF16)
            dst[2 * c0 + 1] = jnp.where(lo_half, zero, swapped).astype(BF16)
            dst[2 * c1 + 0] = jnp.where(lo_half, swapped, zero).astype(BF16)
            dst[2 * c1 + 1] = jnp.where(lo_half, zero, both).astype(BF16)

    scale = HEAD_DIM ** -0.5
    for p in range(N_HEADS // 2):
        c = (2 * p) // ATT_GROUP
        qp = (q_ref[0, :, p * LANES:(p + 1) * LANES] * scale).astype(BF16)
        o_pair = None
        for a in range(2):
            h = 2 * p + a
            sink = sink_ref[h]
            s = _dot_nt(qp, kz_s[2 * c + a])
            s_prev = s[:, :blk] + bprev_ref[h] + neg_prev
            s_cur = s[:, blk:] + bcur_ref[h]
            m = jnp.maximum(jnp.max(jnp.maximum(s_prev, s_cur), axis=-1, keepdims=True), sink)
            e_prev = jnp.exp(s_prev - m)
            e_cur = jnp.exp(s_cur - m)
            den = jnp.sum(e_prev + e_cur, axis=-1, keepdims=True) + jnp.exp(sink - m)
            pm = jnp.concatenate([e_prev, e_cur], axis=1).astype(BF16)
            o_a = _dot(pm, vz_s[2 * c + a]) * (1.0 / den)
            o_pair = o_a if o_pair is None else o_pair + o_a
        o_ref[0, :, p * LANES:(p + 1) * LANES] = o_pair.astype(o_ref.dtype)


def _t5_bucket_np(dist):
    max_exact = N_BUCKETS // 2
    d = np.maximum(dist, 0)
    df = np.maximum(d, 1).astype(np.float32)
    large = max_exact + (np.log(df / np.float32(max_exact)) / np.float32(math.log(MAX_DISTANCE / max_exact))
                         * np.float32(N_BUCKETS - max_exact)).astype(np.int32)
    large = np.minimum(large, N_BUCKETS - 1)
    return np.where(d < max_exact, d, large)


def _bias_from_dist(dist, in_band, rel_table):
    onehot = (_t5_bucket_np(dist)[..., None] == np.arange(N_BUCKETS)).astype(np.float32)
    tab = jnp.einsum('qsb,bh->hqs', jnp.asarray(onehot), rel_table.astype(F32), precision=lax.Precision.HIGHEST)
    return jnp.where(jnp.asarray(in_band)[None], tab, NEG_INF)


def attention_prompt(qkv, sinks, rel_table):
    b, l, n = qkv.shape
    blk = ATT_BLOCK
    nq = N_HEADS * HEAD_DIM
    kvw = 2 * N_KV_HEADS * HEAD_DIM
    kv_blk = nq // kvw
    qi = np.arange(blk)[:, None]
    sj = np.arange(blk)[None, :]
    d_prev = qi + blk - sj
    d_cur = qi - sj
    bias_prev = _bias_from_dist(d_prev, (d_prev >= 0) & (d_prev <= WINDOW), rel_table)
    bias_cur = _bias_from_dist(d_cur, (d_cur >= 0) & (d_cur <= WINDOW), rel_table)
    blocks = (_nbytes((blk, nq), F32) + 2 * _nbytes((blk, kvw), F32) + _nbytes((blk, nq), BF16))
    scratch = 2 * _nbytes((2 * N_KV_HEADS, 2 * blk, LANES), BF16) + 4 * _nbytes((N_HEADS, blk, blk), F32)
    return pl.pallas_call(
        _attn_prompt_kernel,
        grid=(b, l // blk),
        in_specs=[pl.BlockSpec(memory_space=pltpu.SMEM),
                  pl.BlockSpec((1, blk, nq), lambda b_, i: (b_, i, 0)),
                  pl.BlockSpec((1, blk, kvw), lambda b_, i: (b_, i, kv_blk)),
                  pl.BlockSpec((1, blk, kvw), lambda b_, i: (b_, jnp.maximum(i - 1, 0), kv_blk)),
                  pl.BlockSpec((N_HEADS, blk, blk), lambda b_, i: (0, 0, 0)),
                  pl.BlockSpec((N_HEADS, blk, blk), lambda b_, i: (0, 0, 0))],
        out_specs=pl.BlockSpec((1, blk, nq), lambda b_, i: (b_, i, 0)),
        out_shape=jax.ShapeDtypeStruct((b, l, nq), BF16),
        scratch_shapes=[pltpu.VMEM((2 * N_KV_HEADS, 2 * blk, LANES), BF16),
                        pltpu.VMEM((2 * N_KV_HEADS, 2 * blk, LANES), BF16)],
        compiler_params=_params(2, blocks, scratch),
        name="attention_prompt",
    )(sinks.astype(F32), qkv, qkv, qkv, bias_prev, bias_cur)


def _attn_sample_kernel(q_ref, k_ref, v_ref, bias_ref, sink_ref, o_ref):
    k = k_ref[0].astype(BF16)
    v = v_ref[0].astype(BF16)
    for c in range(N_KV_HEADS):
        s = _dot_nt(q_ref[0, c].astype(BF16), k) + bias_ref[c]
        sink = sink_ref[c]
        m = jnp.maximum(jnp.max(s, axis=-1, keepdims=True), sink)
        e = jnp.exp(s - m)
        den = jnp.sum(e, axis=-1, keepdims=True) + jnp.exp(sink - m)
        o_ref[0, c] = _dot(e.astype(BF16), v) * (1.0 / den)


def attention_sample(qkv, cache_k, cache_v, sinks, rel_table):
    bsz, w = cache_k.shape[0], cache_k.shape[1]
    t = qkv.shape[0] // bsz
    nq, nkv = N_HEADS * HEAD_DIM, N_KV_HEADS * HEAD_DIM
    keys = w + t
    keys_pad = -(-keys // 16) * 16
    rows = ATT_GROUP * t
    q = qkv[:, :nq].reshape(bsz, t, N_KV_HEADS, ATT_GROUP, HEAD_DIM) * HEAD_DIM ** -0.5
    q = jnp.transpose(q, (0, 2, 3, 1, 4)).reshape(bsz, N_KV_HEADS, rows, HEAD_DIM)
    qz = jnp.einsum('bcrd,ck->bcrkd', q, jnp.eye(N_KV_HEADS, dtype=F32)).reshape(bsz, N_KV_HEADS, rows, nkv)
    k_all = jnp.concatenate([cache_k.reshape(bsz, w, nkv), qkv[:, nq:nq + nkv].reshape(bsz, t, nkv)], axis=1)
    v_all = jnp.concatenate([cache_v.reshape(bsz, w, nkv), qkv[:, nq + nkv:].reshape(bsz, t, nkv)], axis=1)
    pad = ((0, 0), (0, keys_pad - keys), (0, 0))
    k_pad, v_pad = jnp.pad(k_all, pad), jnp.pad(v_all, pad)
    tq = np.arange(t)[:, None]
    sk = np.arange(keys_pad)[None, :]
    dist = tq + w - sk
    in_band = (dist >= 0) & (dist <= WINDOW) & (sk < keys)
    bias_t = _bias_from_dist(dist, in_band, rel_table)
    bias = bias_t.reshape(N_KV_HEADS, rows, keys_pad)
    sink_rows = jnp.repeat(sinks.astype(F32), t).reshape(N_KV_HEADS, rows, 1)
    blocks = (_nbytes((N_KV_HEADS, rows, nkv), F32) * 2 + 2 * _nbytes((keys_pad, nkv), F32))
    oz = pl.pallas_call(
        _attn_sample_kernel,
        grid=(bsz,),
        in_specs=[pl.BlockSpec((1, N_KV_HEADS, rows, nkv), lambda b_: (b_, 0, 0, 0)),
                  pl.BlockSpec((1, keys_pad, nkv), lambda b_: (b_, 0, 0)),
                  pl.BlockSpec((1, keys_pad, nkv), lambda b_: (b_, 0, 0)),
                  pl.BlockSpec((N_KV_HEADS, rows, keys_pad), lambda b_: (0, 0, 0)),
                  pl.BlockSpec((N_KV_HEADS, rows, 1), lambda b_: (0, 0, 0))],
        out_specs=pl.BlockSpec((1, N_KV_HEADS, rows, nkv), lambda b_: (b_, 0, 0, 0)),
        out_shape=jax.ShapeDtypeStruct((bsz, N_KV_HEADS, rows, nkv), F32),
        compiler_params=_params(1, blocks),
        name="attention_sample",
    )(qz, k_pad, v_pad, bias, sink_rows)
    o = jnp.stack([oz[:, c, :, c * HEAD_DIM:(c + 1) * HEAD_DIM] for c in range(N_KV_HEADS)], axis=1)
    o = o.reshape(bsz, N_KV_HEADS, ATT_GROUP, t, HEAD_DIM)
    o = jnp.transpose(o, (0, 3, 1, 2, 4)).reshape(bsz * t, nq)
    return o.astype(BF16), k_all[:, t:], v_all[:, t:]


def _mm_res_kernel(a_ref, w_ref, x_ref, g_ref, o_ref):
    y = _dot(a_ref[0], w_ref[...].astype(BF16))
    o_ref[0] = x_ref[0] + g_ref[0] * y


def matmul_gate_residual(a, w_stack, layer, x, gate, tm, tn):
    b, l, k = a.shape
    n = w_stack.shape[2]
    grows = 1 if gate.shape[1] == 1 else tm
    if gate.shape[1] == 1:
        g_spec = pl.BlockSpec((1, 1, tn), lambda b_, i, j: (b_, 0, j))
    else:
        g_spec = pl.BlockSpec((1, tm, tn), lambda b_, i, j: (b_, i, j))
    blocks = (_nbytes((tm, k), a.dtype) + _nbytes((k, tn), F32) + 2 * _nbytes((tm, tn), F32)
              + _nbytes((grows, tn), F32))
    scratch = _nbytes((k, tn), BF16) + _nbytes((tm, tn), F32)
    return pl.pallas_call(
        _mm_res_kernel,
        grid=(b, l // tm, n // tn),
        in_specs=[pl.BlockSpec((1, tm, k), lambda b_, i, j: (b_, i, 0)),
                  pl.BlockSpec((None, k, tn), lambda b_, i, j: (layer, 0, j)),
                  pl.BlockSpec((1, tm, tn), lambda b_, i, j: (b_, i, j)),
                  g_spec],
        out_specs=pl.BlockSpec((1, tm, tn), lambda b_, i, j: (b_, i, j)),
        out_shape=jax.ShapeDtypeStruct((b, l, n), F32),
        compiler_params=_params(3, blocks, scratch),
        name="matmul_gate_residual",
    )(a, w_stack, x, gate)


def _ffn_conv_rows(u, prev1, prev2, cw_ref, cb_ref):
    return u * cw_ref[2:3, :] + prev1 * cw_ref[1:2, :] + prev2 * cw_ref[0:1, :] + cb_ref[...]


def _ffn_up_prompt_kernel(x_ref, gain_ref, shift_ref, scale_ref, wg_ref, wv_ref, cwg_ref, cwv_ref,
                          cbg_ref, cbv_ref, o_ref, tail_ref, h_s, carry_s):
    i = pl.program_id(1)
    j = pl.program_id(2)
    tm = h_s.shape[0]

    @pl.when(j == 0)
    def _():
        h_s[...] = _norm_mod(x_ref[0], gain_ref[...], shift_ref[0], scale_ref[0]).astype(BF16)

    @pl.when(i == 0)
    def _():
        carry_s[j] = jnp.zeros(carry_s.shape[1:], F32)

    halves = ((wg_ref, cwg_ref, cbg_ref), (wv_ref, cwv_ref, cbv_ref))
    w_bf = [w_ref[...].astype(BF16) for w_ref, _, _ in halves]
    prev_tail = [carry_s[j, half] for half in range(2)]
    row8 = lax.broadcasted_iota(jnp.int32, (SUBLANES, 1), 0)
    sub = min(FFN_SUB_ROWS, tm)
    for m in range(tm // sub):
        hm = h_s[m * sub:(m + 1) * sub, :]
        ys = []
        for half, (_, cw_ref, cb_ref) in enumerate(halves):
            u = _dot(hm, w_bf[half])
            y = _ffn_conv_rows(u, pltpu.roll(u, 1, axis=0), pltpu.roll(u, 2, axis=0), cw_ref, cb_ref)
            top, hist = u[0:SUBLANES], prev_tail[half]
            p1 = jnp.where(row8 < 1, pltpu.roll(hist, 1, axis=0), pltpu.roll(top, 1, axis=0))
            p2 = jnp.where(row8 < 2, pltpu.roll(hist, 2, axis=0), pltpu.roll(top, 2, axis=0))
            y_top = _ffn_conv_rows(top, p1, p2, cw_ref, cb_ref)
            ys.append(jnp.concatenate([y_top, y[SUBLANES:]], axis=0))
            prev_tail[half] = u[sub - SUBLANES:sub]
        o_ref[0, m * sub:(m + 1) * sub, :] = (_silu(ys[0]) * ys[1]).astype(o_ref.dtype)
    for half in range(2):
        carry_s[j, half] = prev_tail[half]
        tail_ref[0, 0, half] = prev_tail[half]


def ffn_up_prompt(x, gain, shift, scale, w_up, conv_w, conv_b, layer, tm, tn=512):
    b, l, d = x.shape
    dff = w_up.shape[2] // 2
    nj = dff // tn
    conv_b = conv_b.reshape(conv_b.shape[0], 1, 2 * dff)
    blocks = (_nbytes((tm, d), F32) + 2 * _nbytes((d, tn), F32) + _nbytes((tm, tn), BF16)
              + 2 * _nbytes((1, d), F32) + 8 * _nbytes((SUBLANES, tn), F32))
    scratch = (_nbytes((tm, d), BF16) + _nbytes((nj, 2, SUBLANES, tn), F32) + 2 * _nbytes((d, tn), BF16)
               + 8 * _nbytes((tm, tn), F32))
    act, tail = pl.pallas_call(
        _ffn_up_prompt_kernel,
        grid=(b, l // tm, nj),
        in_specs=[pl.BlockSpec((1, tm, d), lambda b_, i, j: (b_, i, 0)),
                  pl.BlockSpec((1, d), lambda b_, i, j: (0, 0)),
                  _mod_specs(shift, tm), _mod_specs(scale, tm),
                  pl.BlockSpec((None, d, tn), lambda b_, i, j: (layer, 0, j)),
                  pl.BlockSpec((None, d, tn), lambda b_, i, j: (layer, 0, j + nj)),
                  pl.BlockSpec((None, FFN_CONV, tn), lambda b_, i, j: (layer, 0, j)),
                  pl.BlockSpec((None, FFN_CONV, tn), lambda b_, i, j: (layer, 0, j + nj)),
                  pl.BlockSpec((None, 1, tn), lambda b_, i, j: (layer, 0, j)),
                  pl.BlockSpec((None, 1, tn), lambda b_, i, j: (layer, 0, j + nj))],
        out_specs=[pl.BlockSpec((1, tm, tn), lambda b_, i, j: (b_, i, j)),
                   pl.BlockSpec((1, 1, 2, SUBLANES, tn), lambda b_, i, j: (b_, i, 0, 0, j))],
        out_shape=[jax.ShapeDtypeStruct((b, l, dff), BF16),
                   jax.ShapeDtypeStruct((b, l // tm, 2, SUBLANES, dff), F32)],
        scratch_shapes=[pltpu.VMEM((tm, d), BF16), pltpu.VMEM((nj, 2, SUBLANES, tn), F32)],
        compiler_params=_params(3, blocks, scratch),
        name="ffn_up_prompt",
    )(x, gain.reshape(1, d), shift, scale, w_up, w_up, conv_w, conv_w, conv_b, conv_b)
    keep = FFN_CONV - 1
    hist = jnp.transpose(tail[:, -1, :, SUBLANES - keep:, :], (0, 2, 1, 3)).reshape(b, keep, 2 * dff)
    return act, hist


def _ffn_up_sample_kernel(x_ref, gain_ref, shift_ref, scale_ref, wg_ref, wv_ref, cwg_ref, cwv_ref,
                          cbg_ref, cbv_ref, e1g_ref, e1v_ref, e2g_ref, e2v_ref, o_ref, ug_ref, uv_ref, h_s,
                          *, seq_len):
    @pl.when(pl.program_id(0) == 0)
    def _():
        h_s[...] = _norm_mod(x_ref[...], gain_ref[...], shift_ref[...], scale_ref[...]).astype(BF16)

    h = h_s[...]
    rows = h_s.shape[0]
    t = lax.broadcasted_iota(jnp.int32, (rows, 1), 0) % seq_len
    ys = []
    for w_ref, cw_ref, cb_ref, e1_ref, e2_ref, u_ref in (
            (wg_ref, cwg_ref, cbg_ref, e1g_ref, e2g_ref, ug_ref),
            (wv_ref, cwv_ref, cbv_ref, e1v_ref, e2v_ref, uv_ref)):
        u = _dot(h, w_ref[...].astype(BF16))
        u_ref[...] = u
        p1 = jnp.where(t >= 1, pltpu.roll(u, 1, axis=0), e1_ref[...])
        p2 = jnp.where(t >= 2, pltpu.roll(u, 2, axis=0), e2_ref[...])
        ys.append(_ffn_conv_rows(u, p1, p2, cw_ref, cb_ref))
    o_ref[...] = (_silu(ys[0]) * ys[1]).astype(o_ref.dtype)


def ffn_up_sample(x, gain, shift, scale, w_up, conv_w, conv_b, layer, hist, seq_len, tn=512):
    rows, d = x.shape
    dff = w_up.shape[2] // 2
    nj = dff // tn
    bsz = rows // seq_len
    conv_b = conv_b.reshape(conv_b.shape[0], 1, 2 * dff)
    tpos = np.arange(seq_len)
    sel = lambda cond: jnp.asarray(cond.astype(np.float32))[None, :, None]
    h0, h1 = hist[:, 0:1, :], hist[:, 1:2, :]
    e1 = (h1 * sel(tpos == 0)).reshape(rows, 2 * dff)
    e2 = (h0 * sel(tpos == 0) + h1 * sel(tpos == 1)).reshape(rows, 2 * dff)
    col = lambda j: (0, j)
    col_hi = lambda j: (0, j + nj)
    lcol = lambda j: (layer, 0, j)
    lcol_hi = lambda j: (layer, 0, j + nj)
    blocks = (_nbytes((rows, d), F32) * 3 + 2 * _nbytes((d, tn), F32) + 7 * _nbytes((rows, tn), F32))
    scratch = _nbytes((rows, d), BF16) + 2 * _nbytes((d, tn), BF16) + 8 * _nbytes((rows, tn), F32)
    act, ug, uv = pl.pallas_call(
        functools.partial(_ffn_up_sample_kernel, seq_len=seq_len),
        grid=(nj,),
        in_specs=[pl.BlockSpec((rows, d), lambda j: (0, 0)),
                  pl.BlockSpec((1, d), lambda j: (0, 0)),
                  pl.BlockSpec((rows, d), lambda j: (0, 0)),
                  pl.BlockSpec((rows, d), lambda j: (0, 0)),
                  pl.BlockSpec((None, d, tn), lcol), pl.BlockSpec((None, d, tn), lcol_hi),
                  pl.BlockSpec((None, FFN_CONV, tn), lcol), pl.BlockSpec((None, FFN_CONV, tn), lcol_hi),
                  pl.BlockSpec((None, 1, tn), lcol), pl.BlockSpec((None, 1, tn), lcol_hi),
                  pl.BlockSpec((rows, tn), col), pl.BlockSpec((rows, tn), col_hi),
                  pl.BlockSpec((rows, tn), col), pl.BlockSpec((rows, tn), col_hi)],
        out_specs=[pl.BlockSpec((rows, tn), col), pl.BlockSpec((rows, tn), col), pl.BlockSpec((rows, tn), col)],
        out_shape=[jax.ShapeDtypeStruct((rows, dff), BF16),
                   jax.ShapeDtypeStruct((rows, dff), F32),
                   jax.ShapeDtypeStruct((rows, dff), F32)],
        scratch_shapes=[pltpu.VMEM((rows, d), BF16)],
        compiler_params=_params(1, blocks, scratch),
        name="ffn_up_sample",
    )(x, gain.reshape(1, d), shift, scale, w_up, w_up, conv_w, conv_w, conv_b, conv_b, e1, e1, e2, e2)
    keep = FFN_CONV - 1
    u_all = jnp.concatenate([ug, uv], axis=-1).reshape(bsz, seq_len, 2 * dff)
    new_hist = jnp.concatenate([hist, u_all], axis=1)[:, -keep:]
    return act, new_hist


def _gdn_in_kernel(x_ref, gain_ref, shift_ref, scale_ref, w_ref, wba_ref, alog_ref, dtb_ref,
                   o_ref, bg_ref, h_s):
    @pl.when(pl.program_id(2) == 0)
    def _():
        h = _norm_mod(x_ref[0], gain_ref[...], shift_ref[0], scale_ref[0]).astype(BF16)
        h_s[...] = h
        y = _dot(h, wba_ref[...].astype(BF16))
        lane = lax.broadcasted_iota(jnp.int32, y.shape, 1) % LANES
        beta = 1.0 / (1.0 + jnp.exp(-y))
        a = y + dtb_ref[...]
        softplus = jnp.maximum(a, 0.0) + jnp.log1p(jnp.exp(-jnp.abs(a)))
        g = -jnp.exp(alog_ref[...]) * softplus
        bg_ref[0] = jnp.where(lane < GDN_VH_PER_GROUP, beta, jnp.where(lane < 2 * GDN_VH_PER_GROUP, g, 0.0))

    o_ref[0] = _dot(h_s[...], w_ref[...].astype(BF16)).astype(o_ref.dtype)


def _group_lane_layout(vec_b, vec_a):
    lead = vec_b.shape[:-1]
    vb = vec_b.reshape(*lead, GDN_HEAD_GROUPS, GDN_VH_PER_GROUP)
    va = vec_a.reshape(*lead, GDN_HEAD_GROUPS, GDN_VH_PER_GROUP)
    pad = jnp.zeros((*lead, GDN_HEAD_GROUPS, LANES - 2 * GDN_VH_PER_GROUP), vec_b.dtype)
    return jnp.concatenate([vb, va, pad], axis=-1).reshape(*lead, GDN_HEAD_GROUPS * LANES)


def gdn_in_projection(x, gain, shift, scale, w_in, a_log, dt_bias, tm, tn=512):
    b, l, d = x.shape
    n_main = GDN_CONV_DIM + GDN_VAL_DIM
    w_b = w_in[:, n_main:n_main + GDN_V_HEADS]
    w_a = w_in[:, n_main + GDN_V_HEADS:]
    wba = _group_lane_layout(w_b, w_a)
    zeros = jnp.zeros((1, GDN_V_HEADS), F32)
    alog = _group_lane_layout(zeros, a_log.reshape(1, -1).astype(F32))
    dtb = _group_lane_layout(zeros, dt_bias.reshape(1, -1).astype(F32))
    nbg = wba.shape[1]
    blocks = (_nbytes((tm, d), F32) + _nbytes((d, tn), F32) + _nbytes((tm, tn), BF16)
              + _nbytes((d, nbg), F32) + _nbytes((tm, nbg), F32) + 2 * _nbytes((1, d), F32))
    scratch = _nbytes((tm, d), BF16) + _nbytes((d, tn), BF16) + 6 * _nbytes((tm, nbg), F32)
    return pl.pallas_call(
        _gdn_in_kernel,
        grid=(b, l // tm, n_main // tn),
        in_specs=[pl.BlockSpec((1, tm, d), lambda b_, i, j: (b_, i, 0)),
                  pl.BlockSpec((1, d), lambda b_, i, j: (0, 0)),
                  _mod_specs(shift, tm), _mod_specs(scale, tm),
                  pl.BlockSpec((d, tn), lambda b_, i, j: (0, j)),
                  pl.BlockSpec((d, nbg), lambda b_, i, j: (0, 0)),
                  pl.BlockSpec((1, nbg), lambda b_, i, j: (0, 0)),
                  pl.BlockSpec((1, nbg), lambda b_, i, j: (0, 0))],
        out_specs=[pl.BlockSpec((1, tm, tn), lambda b_, i, j: (b_, i, j)),
                   pl.BlockSpec((1, tm, nbg), lambda b_, i, j: (b_, i, 0))],
        out_shape=[jax.ShapeDtypeStruct((b, l, n_main), BF16),
                   jax.ShapeDtypeStruct((b, l, nbg), F32)],
        scratch_shapes=[pltpu.VMEM((tm, d), BF16)],
        compiler_params=_params(3, blocks, scratch),
        name="gdn_in_projection",
    )(x, gain.reshape(1, d), shift, scale, w_in, wba, alog, dtb)


def _causal_conv_silu(x, carry, w_ref):
    width = GDN_CONV
    row8 = lax.broadcasted_iota(jnp.int32, (SUBLANES, 1), 0)
    top = x[0:SUBLANES]
    y = x * w_ref[width - 1:width, :]
    y_top = top * w_ref[width - 1:width, :]
    for k in range(1, width):
        wk = w_ref[width - 1 - k:width - k, :]
        y = y + pltpu.roll(x, k, axis=0) * wk
        y_top = y_top + jnp.where(row8 < k, pltpu.roll(carry, k, axis=0), pltpu.roll(top, k, axis=0)) * wk
    return _silu(y), _silu(y_top)


def _l2_normalize_heads(x, scale):
    outs = []
    for hd in range(x.shape[1] // LANES):
        xh = x[:, hd * LANES:(hd + 1) * LANES]
        outs.append(xh * (lax.rsqrt(jnp.sum(xh * xh, axis=-1, keepdims=True) + NORM_EPS) * scale))
    return outs


def _gdn_core_kernel(q_ref, k_ref, v_ref, z_ref, bg_ref, hq_ref, hk_ref, hv_ref, cwq_ref, cwk_ref, cwv_ref,
                     onorm_ref, s0_ref, o_ref, s1_ref, s_s, cq_s, ck_s, cv_s, qs_s, ks_s, vs_s,
                     gc_s, tbd_s, qkbd_s):
    r = pl.program_id(2)
    n_r = pl.num_programs(2)
    rows = q_ref.shape[1]
    chunk = GDN_CHUNK
    vpg = GDN_VH_PER_GROUP

    @pl.when(r == 0)
    def _():
        s_s[...] = s0_ref[0]
        cq_s[...] = hq_ref[0]
        ck_s[...] = hk_ref[0]
        cv_s[...] = hv_ref[0]

    for src_ref, carry_s, w_ref, dst_s, l2scale in ((q_ref, cq_s, cwq_ref, qs_s, GDN_DK ** -0.5),
                                                    (k_ref, ck_s, cwk_ref, ks_s, 1.0),
                                                    (v_ref, cv_s, cwv_ref, vs_s, None)):
        x = src_ref[0].astype(F32)
        y, y_top = _causal_conv_silu(x, carry_s[...], w_ref)
        carry_s[...] = x[rows - SUBLANES:rows]
        if l2scale is None:
            dst_s[...] = y
            dst_s[0:SUBLANES, :] = y_top
        else:
            for hd, (yh, yh_top) in enumerate(zip(_l2_normalize_heads(y, l2scale),
                                                  _l2_normalize_heads(y_top, l2scale))):
                dst_s[:, hd * LANES:(hd + 1) * LANES] = yh
                dst_s[0:SUBLANES, hd * LANES:(hd + 1) * LANES] = yh_top

    n_chunks = rows // chunk
    pw = GDN_PACK * chunk
    n_packs = vpg // GDN_PACK
    rep = GDN_V_HEADS // GDN_K_HEADS
    ri = lax.broadcasted_iota(jnp.int32, (chunk, pw), 0)
    li = lax.broadcasted_iota(jnp.int32, (chunk, pw), 1) % chunk
    tril_p = ri >= li
    strict_p = ri > li
    eye_p = (ri == li).astype(F32)
    n_levels = int(math.log2(chunk))
    off_masks = [((ri >> (lvl + 1)) == (li >> (lvl + 1))) & ((ri >> lvl) != (li >> lvl))
                 for lvl in range(n_levels)]
    bd_rows = lax.broadcasted_iota(jnp.int32, (pw, pw), 0) // chunk
    bd_cols = lax.broadcasted_iota(jnp.int32, (pw, pw), 1) // chunk
    bd_mask = (bd_rows == bd_cols).astype(BF16)
    row_c = lax.broadcasted_iota(jnp.int32, (chunk, LANES), 0)
    lane_lo = lax.broadcasted_iota(jnp.int32, (chunk, LANES), 1) < chunk

    def block_diag(xp):
        return jnp.concatenate([xp.astype(BF16)] * GDN_PACK, axis=0) * bd_mask

    def pack_cols(arr, first_col):
        tiles = []
        for t in range(pw // LANES):
            even = jnp.broadcast_to(arr[:, first_col + 2 * t:first_col + 2 * t + 1], (chunk, LANES))
            odd = jnp.broadcast_to(arr[:, first_col + 2 * t + 1:first_col + 2 * t + 2], (chunk, LANES))
            tiles.append(jnp.where(lane_lo, even, odd))
        return jnp.concatenate(tiles, axis=1)

    for c0 in range(0, n_chunks, GDN_PHASE_A_CHUNKS):
        group = [(c, p) for c in range(c0, min(c0 + GDN_PHASE_A_CHUNKS, n_chunks)) for p in range(n_packs)]
        a_list = {}
        for c in range(c0, min(c0 + GDN_PHASE_A_CHUNKS, n_chunks)):
            rs = slice(c * chunk, (c + 1) * chunk)
            bg = bg_ref[0, rs, :]
            cum = bg
            shift = 1
            while shift < chunk:
                cum = cum + jnp.where(row_c >= shift, pltpu.roll(cum, shift, axis=0), 0.0)
                shift *= 2
            gc_s[rs, :] = cum
            gq = []
            for kh in range(GDN_KH_PER_GROUP):
                ksl = slice(kh * LANES, (kh + 1) * LANES)
                k = ks_s[rs, ksl]
                kq = jnp.concatenate([k, qs_s[rs, ksl]], axis=0).astype(BF16)
                kk = jnp.concatenate([k, k], axis=0).astype(BF16)
                gq.append(_dot_nt(kq, kk))
            for p in range(n_packs):
                khs = [(p * GDN_PACK + 2 * t) // rep for t in range(pw // LANES)]
                gram = jnp.concatenate([gq[kh][:chunk] for kh in khs], axis=1)
                qk = jnp.concatenate([gq[kh][chunk:] for kh in khs], axis=1)
                beta_p = pack_cols(bg, p * GDN_PACK)
                gcol_p = pack_cols(cum, vpg + p * GDN_PACK)
                grow_p = jnp.sum(gcol_p * eye_p, axis=0, keepdims=True)
                decay = jnp.exp(jnp.where(tril_p, gcol_p - grow_p, NEG_INF))
                a_list[(c, p)] = jnp.where(strict_p, gram * beta_p * decay, 0.0)
                qkbd_s[c, p] = block_diag(qk * decay)
        inv = {cp: eye_p - jnp.where(off_masks[0], a_list[cp], 0.0) for cp in group}
        for lvl in range(1, n_levels):
            w = {cp: _dot(jnp.where(off_masks[lvl], a_list[cp], 0.0).astype(BF16), block_diag(inv[cp]))
                 for cp in group}
            inv = {cp: inv[cp] - _dot(inv[cp].astype(BF16), block_diag(w[cp])) for cp in group}
        for c, p in group:
            tbd_s[c, p] = block_diag(inv[(c, p)])

    def chunk_body(c, carry):
        rs = pl.ds(pl.multiple_of(c * chunk, chunk), chunk)
        bg = bg_ref[0, rs, :]
        cum = gc_s[rs, :]
        beta, gcol, g_last = [], [], []
        ks_qs = []
        for hv in range(vpg):
            kh = hv // rep
            ksl = slice(kh * LANES, (kh + 1) * LANES)
            beta.append(bg[:, hv:hv + 1])
            gcol.append(cum[:, vpg + hv:vpg + hv + 1])
            g_last.append(gcol[hv][chunk - 1:chunk, :])
            eg = jnp.exp(gcol[hv])
            lhs = jnp.concatenate([ks_s[rs, ksl] * (beta[hv] * eg), qs_s[rs, ksl] * eg], axis=0)
            ks_qs.append(_dot(lhs.astype(BF16), s_s[hv].astype(BF16)))
        v_new, o_intra = [], []
        for p in range(n_packs):
            heads = range(p * GDN_PACK, (p + 1) * GDN_PACK)
            rhs = jnp.concatenate([vs_s[rs, hv * LANES:(hv + 1) * LANES] * beta[hv] - ks_qs[hv][:chunk]
                                   for hv in heads], axis=0).astype(BF16)
            vn = _dot(tbd_s[c, p], rhs)
            oi = _dot(qkbd_s[c, p], vn.astype(BF16))
            for j in range(GDN_PACK):
                v_new.append(vn[j * chunk:(j + 1) * chunk])
                o_intra.append(oi[j * chunk:(j + 1) * chunk])
        for hv in range(vpg):
            kh = hv // rep
            vsl = slice(hv * LANES, (hv + 1) * LANES)
            k_bf = ks_s[rs, kh * LANES:(kh + 1) * LANES].astype(BF16)
            v_dec = (v_new[hv] * jnp.exp(g_last[hv] - gcol[hv])).astype(BF16)
            s_s[hv] = s_s[hv] * jnp.exp(g_last[hv]) + _dot_tn(k_bf, v_dec)
            o = ks_qs[hv][chunk:] + o_intra[hv]
            on = o * lax.rsqrt(jnp.mean(o * o, axis=-1, keepdims=True) + NORM_EPS) * onorm_ref[...]
            z = z_ref[0, rs, vsl].astype(F32)
            o_ref[0, rs, vsl] = (on * _silu(z)).astype(o_ref.dtype)
        return carry

    lax.fori_loop(0, n_chunks, chunk_body, 0)

    @pl.when(r == n_r - 1)
    def _():
        s1_ref[0] = s_s[...]


def gdn_core(mixed_z, bg, hist, s0, conv_w, out_norm, rows):
    b, l, _ = mixed_z.shape
    hg = GDN_HEAD_GROUPS
    qw = GDN_KEY_DIM // hg
    vw = GDN_VAL_DIM // hg
    kq = GDN_KEY_DIM // qw
    kv = 2 * GDN_KEY_DIM // vw
    kz = GDN_CONV_DIM // vw
    vpg = GDN_VH_PER_GROUP
    blocks = (2 * _nbytes((rows, qw), BF16) + 3 * _nbytes((rows, vw), BF16) + _nbytes((rows, LANES), F32)
              + 2 * _nbytes((vpg, GDN_DK, GDN_DV), F32) + 4 * _nbytes((SUBLANES, vw), F32))
    n_chunks = rows // GDN_CHUNK
    n_packs = vpg // GDN_PACK
    pw = GDN_PACK * GDN_CHUNK
    scratch = (_nbytes((vpg, GDN_DK, GDN_DV), F32) + 2 * _nbytes((rows, qw), F32) + _nbytes((rows, vw), F32)
               + 2 * _nbytes((n_chunks, n_packs, pw, pw), BF16) + 6 * _nbytes((rows, vw), F32))
    return pl.pallas_call(
        _gdn_core_kernel,
        grid=(b, hg, l // rows),
        in_specs=[pl.BlockSpec((1, rows, qw), lambda b_, g, r: (b_, r, g)),
                  pl.BlockSpec((1, rows, qw), lambda b_, g, r: (b_, r, kq + g)),
                  pl.BlockSpec((1, rows, vw), lambda b_, g, r: (b_, r, kv + g)),
                  pl.BlockSpec((1, rows, vw), lambda b_, g, r: (b_, r, kz + g)),
                  pl.BlockSpec((1, rows, LANES), lambda b_, g, r: (b_, r, g)),
                  pl.BlockSpec((1, SUBLANES, qw), lambda b_, g, r: (b_, 0, g)),
                  pl.BlockSpec((1, SUBLANES, qw), lambda b_, g, r: (b_, 0, kq + g)),
                  pl.BlockSpec((1, SUBLANES, vw), lambda b_, g, r: (b_, 0, kv + g)),
                  pl.BlockSpec((GDN_CONV, qw), lambda b_, g, r: (0, g)),
                  pl.BlockSpec((GDN_CONV, qw), lambda b_, g, r: (0, kq + g)),
                  pl.BlockSpec((GDN_CONV, vw), lambda b_, g, r: (0, kv + g)),
                  pl.BlockSpec((1, GDN_DV), lambda b_, g, r: (0, 0)),
                  pl.BlockSpec((1, vpg, GDN_DK, GDN_DV), lambda b_, g, r: (b_, g, 0, 0))],
        out_specs=[pl.BlockSpec((1, rows, vw), lambda b_, g, r: (b_, r, g)),
                   pl.BlockSpec((1, vpg, GDN_DK, GDN_DV), lambda b_, g, r: (b_, g, 0, 0))],
        out_shape=[jax.ShapeDtypeStruct((b, l, GDN_VAL_DIM), BF16),
                   jax.ShapeDtypeStruct((b, GDN_V_HEADS, GDN_DK, GDN_DV), F32)],
        scratch_shapes=[pltpu.VMEM((vpg, GDN_DK, GDN_DV), F32),
                        pltpu.VMEM((SUBLANES, qw), F32), pltpu.VMEM((SUBLANES, qw), F32),
                        pltpu.VMEM((SUBLANES, vw), F32),
                        pltpu.VMEM((rows, qw), F32), pltpu.VMEM((rows, qw), F32), pltpu.VMEM((rows, vw), F32),
                        pltpu.VMEM((rows, LANES), F32),
                        pltpu.VMEM((n_chunks, n_packs, pw, pw), BF16),
                        pltpu.VMEM((n_chunks, n_packs, pw, pw), BF16)],
        compiler_params=_params(3, blocks, scratch),
        name="gdn_core",
    )(mixed_z, mixed_z, mixed_z, mixed_z, bg, hist, hist, hist, conv_w, conv_w, conv_w,
      out_norm.reshape(1, GDN_DV).astype(F32), s0)


def _tile_rows(l, cap):
    t = min(l, cap)
    while l % t:
        t //= 2
    return t


def _trunk(x, mod, states, P, sample):
    bsz, l, d = x.shape
    tm = _tile_rows(l, ROW_TILE_CAP)
    new = {}
    depth = mod.shape[0]
    for layer in range(depth):
        if sample:
            parts = [mod[layer][None, :, k * d:(k + 1) * d] for k in range(6)]
        else:
            parts = [mod[layer][:, None, k * d:(k + 1) * d] for k in range(6)]
        sh_m, sc_m, g_m, sh_f, sc_f, g_f = parts
        i = layer // 2
        if layer % 2 == 0:
            qkv = qkv_projection(x, P['norm_mix'][layer], sh_m, sc_m, P['w_attn_qkv'][i],
                                 P['attn_q_norm'][i], P['attn_k_norm'][i], tm)
            nq, nkv = N_HEADS * HEAD_DIM, N_KV_HEADS * HEAD_DIM
            if sample:
                o, k_win, v_win = attention_sample(qkv[0], states['win_k'][i], states['win_v'][i],
                                                   P['attn_sinks'][i], P['rel_bias_table'])
                o = o[None]
                w_len = k_win.shape[1]
                new.setdefault('win_k', []).append(k_win.reshape(-1, w_len, N_KV_HEADS, HEAD_DIM))
                new.setdefault('win_v', []).append(v_win.reshape(-1, w_len, N_KV_HEADS, HEAD_DIM))
            else:
                o = attention_prompt(qkv, P['attn_sinks'][i], P['rel_bias_table'])
                keep = min(WINDOW, PAST_LEN)
                new.setdefault('win_k', []).append(
                    qkv[:, l - keep:, nq:nq + nkv].reshape(bsz, keep, N_KV_HEADS, HEAD_DIM))
                new.setdefault('win_v', []).append(
                    qkv[:, l - keep:, nq + nkv:].reshape(bsz, keep, N_KV_HEADS, HEAD_DIM))
            x = matmul_gate_residual(o, P['w_attn_o'], i, x, g_m, tm, 512)
        else:
            mixed_z, bg = gdn_in_projection(x, P['norm_mix'][layer], sh_m, sc_m, P['w_gdn_in'][i],
                                            P['gdn_a_log'][i], P['gdn_dt_bias'][i], tm)
            keep = GDN_CONV - 1
            if sample:
                seqs = states['gdn'].shape[1]
                t = l // seqs
                pad_rows = GDN_CHUNK - t
                pad3 = lambda a: jnp.pad(a.reshape(seqs, t, a.shape[-1]), ((0, 0), (0, pad_rows), (0, 0)))
                mz_seq, bg_seq = pad3(mixed_z[0]), pad3(bg[0])
                conv0 = states['gdn_conv'][i].astype(F32)
                hist = jnp.pad(conv0, ((0, 0), (SUBLANES - keep, 0), (0, 0)))
                o_seq, s1 = gdn_core(mz_seq, bg_seq, hist, states['gdn'][i].astype(F32),
                                     P['gdn_conv_w'][i], P['gdn_out_norm'][i], GDN_CHUNK)
                o = o_seq[:, :t].reshape(1, l, GDN_VAL_DIM)
                mixed_rows = mixed_z[0, :, :GDN_CONV_DIM].astype(F32).reshape(seqs, t, GDN_CONV_DIM)
                new.setdefault('gdn_conv', []).append(jnp.concatenate([conv0, mixed_rows], axis=1)[:, -keep:])
            else:
                hist = jnp.zeros((bsz, SUBLANES, GDN_CONV_DIM), F32)
                s0 = jnp.zeros((bsz, GDN_V_HEADS, GDN_DK, GDN_DV), F32)
                o, s1 = gdn_core(mixed_z, bg, hist, s0, P['gdn_conv_w'][i], P['gdn_out_norm'][i],
                                 _tile_rows(l, GDN_ROW_TILE_CAP))
                new.setdefault('gdn_conv', []).append(mixed_z[:, l - keep:, :GDN_CONV_DIM].astype(F32))
            new.setdefault('gdn', []).append(s1)
            x = matmul_gate_residual(o, P['w_gdn_out'], i, x, g_m, tm, 512)
        if sample:
            seqs = states['ffn_conv'].shape[1]
            act, f1 = ffn_up_sample(x[0], P['norm_ffn'][layer], sh_f[0], sc_f[0], P['w_ffn_up'],
                                    P['ffn_conv_w'], P['ffn_conv_b'], layer,
                                    states['ffn_conv'][layer].astype(F32), l // seqs)
            act = act[None]
        else:
            act, f1 = ffn_up_prompt(x, P['norm_ffn'][layer], sh_f, sc_f, P['w_ffn_up'],
                                    P['ffn_conv_w'], P['ffn_conv_b'], layer, tm)
        new.setdefault('ffn_conv', []).append(f1)
        x = matmul_gate_residual(act, P['w_ffn_down'], layer, x, g_f, tm, 256)
    return x, {k: jnp.stack(v) for k, v in new.items()}


def kernel(x_prompt, x_sample, c_prompt, c_sample, cache_win_k, cache_win_v, state_gdn, state_gdn_conv,
           state_ffn_conv, rel_bias_table, w_ada, b_ada, norm_mix, norm_ffn, w_attn_qkv, attn_q_norm,
           attn_k_norm, attn_sinks, w_attn_o, w_gdn_in, gdn_conv_w, gdn_a_log, gdn_dt_bias, gdn_out_norm,
           w_gdn_out, w_ffn_up, ffn_conv_w, ffn_conv_b, w_ffn_down):
    P = dict(rel_bias_table=rel_bias_table, norm_mix=norm_mix, norm_ffn=norm_ffn, w_attn_qkv=w_attn_qkv,
             attn_q_norm=attn_q_norm, attn_k_norm=attn_k_norm, attn_sinks=attn_sinks, w_attn_o=w_attn_o,
             w_gdn_in=w_gdn_in, gdn_conv_w=gdn_conv_w, gdn_a_log=gdn_a_log, gdn_dt_bias=gdn_dt_bias,
             gdn_out_norm=gdn_out_norm, w_gdn_out=w_gdn_out, w_ffn_up=w_ffn_up, ffn_conv_w=ffn_conv_w,
             ffn_conv_b=ffn_conv_b, w_ffn_down=w_ffn_down)
    bp = x_prompt.shape[0]
    bs, ts, d = x_sample.shape
    n_c = bp + bs
    c_rows = -(-n_c // SUBLANES) * SUBLANES
    c_all = jnp.pad(jnp.concatenate([c_prompt, c_sample], axis=0), ((0, c_rows - n_c), (0, 0)))
    mod = ada_modulation(c_all, w_ada, b_ada)
    mod_prompt = mod[:, :bp]
    mod_sample = jnp.repeat(mod[:, bp:n_c], ts, axis=1)

    y_p, new_p = _trunk(x_prompt, mod_prompt, None, P, sample=False)
    states = dict(win_k=cache_win_k, win_v=cache_win_v, gdn=state_gdn, gdn_conv=state_gdn_conv,
                  ffn_conv=state_ffn_conv)
    y_s, new_s = _trunk(x_sample.reshape(1, bs * ts, d), mod_sample, states, P, sample=True)
    y_s = y_s.reshape(bs, ts, d)
    return (y_p, y_s, new_p['win_k'], new_p['win_v'], new_s['win_k'], new_s['win_v'],
            new_p['gdn'], new_s['gdn'], new_p['gdn_conv'], new_s['gdn_conv'],
            new_p['ffn_conv'], new_s['ffn_conv'])
```

```python
import functools
import math

import numpy as np
import jax
import jax.numpy as jnp
from jax import lax
from jax.experimental import pallas as pl
from jax.experimental.pallas import tpu as pltpu

HEAD_DIM = 64
N_HEADS = 32
N_KV_HEADS = 4
ATT_GROUP = N_HEADS // N_KV_HEADS
WINDOW = 128
ATT_BLOCK = 128
N_BUCKETS = 32
MAX_DISTANCE = 128
NEG_INF = -1e30
PAST_LEN = 16384

GDN_K_HEADS = 16
GDN_V_HEADS = 32
GDN_DK = 128
GDN_DV = 128
GDN_KEY_DIM = GDN_K_HEADS * GDN_DK
GDN_VAL_DIM = GDN_V_HEADS * GDN_DV
GDN_CONV_DIM = 2 * GDN_KEY_DIM + GDN_VAL_DIM
GDN_CONV = 4
GDN_CHUNK = 64
GDN_HEAD_GROUPS = 4
GDN_VH_PER_GROUP = GDN_V_HEADS // GDN_HEAD_GROUPS
GDN_KH_PER_GROUP = GDN_K_HEADS // GDN_HEAD_GROUPS
GDN_PACK = 4
GDN_PHASE_A_CHUNKS = 4

FFN_CONV = 3
NORM_EPS = 1e-6

LANES = 128
SUBLANES = 8
VMEM_CAP_BYTES = 60 * 1024 * 1024
VMEM_SLACK_BYTES = 8 * 1024 * 1024

ROW_TILE_CAP = 1024
FFN_SUB_ROWS = 256
GDN_ROW_TILE_CAP = 512

BF16 = jnp.bfloat16
F32 = jnp.float32


def _vmem_limit(block_bytes, scratch_bytes=0):
    est = 2 * int(block_bytes) + int(scratch_bytes) + VMEM_SLACK_BYTES
    return int(min(max(est, 16 * 1024 * 1024), VMEM_CAP_BYTES))


def _params(n_grid, block_bytes, scratch_bytes=0):
    return pltpu.CompilerParams(
        dimension_semantics=("arbitrary",) * n_grid,
        vmem_limit_bytes=_vmem_limit(block_bytes, scratch_bytes))


def _nbytes(shape, dtype):
    return int(np.prod(shape)) * jnp.dtype(dtype).itemsize


def _silu(x):
    return x * (1.0 / (1.0 + jnp.exp(-x)))


def _dot(a, b):
    return jnp.dot(a, b, preferred_element_type=F32)


def _dot_nt(a, b):
    return lax.dot_general(a, b, (((1,), (1,)), ((), ())), preferred_element_type=F32)


def _dot_tn(a, b):
    return lax.dot_general(a, b, (((0,), (0,)), ((), ())), preferred_element_type=F32)


def _norm_mod(x, gain, shift, scale):
    ms = jnp.mean(x * x, axis=-1, keepdims=True)
    y = x * lax.rsqrt(ms + NORM_EPS) * gain
    return y * (1.0 + scale) + shift


def _mod_specs(shift, tm):
    d = shift.shape[-1]
    if shift.shape[1] == 1:
        return pl.BlockSpec((1, 1, d), lambda b, i, j: (b, 0, 0))
    return pl.BlockSpec((1, tm, d), lambda b, i, j: (b, i, 0))


def _ada_kernel(c_ref, w_ref, b_ref, o_ref):
    a = _silu(c_ref[...]).astype(BF16)
    o_ref[0] = _dot(a, w_ref[0].astype(BF16)) + b_ref[0]


def ada_modulation(c_all, w_ada, b_ada, tn=1024):
    rows, d = c_all.shape
    depth, _, n = w_ada.shape
    blocks = _nbytes((rows, d), F32) + _nbytes((d, tn), F32) + _nbytes((rows, tn), F32)
    return pl.pallas_call(
        _ada_kernel,
        grid=(depth, n // tn),
        in_specs=[pl.BlockSpec((rows, d), lambda l, j: (0, 0)),
                  pl.BlockSpec((1, d, tn), lambda l, j: (l, 0, j)),
                  pl.BlockSpec((1, 1, tn), lambda l, j: (l, 0, j))],
        out_specs=pl.BlockSpec((1, rows, tn), lambda l, j: (l, 0, j)),
        out_shape=jax.ShapeDtypeStruct((depth, rows, n), F32),
        compiler_params=_params(2, blocks, _nbytes((d, tn), BF16)),
        name="ada_modulation",
    )(c_all, w_ada, b_ada.reshape(depth, 1, n))


def _qkv_kernel(x_ref, gain_ref, shift_ref, scale_ref, w_ref, hgain_ref, hflag_ref, gmat_ref,
                o_ref, h_s):
    @pl.when(pl.program_id(2) == 0)
    def _():
        h_s[...] = _norm_mod(x_ref[0], gain_ref[...], shift_ref[0], scale_ref[0]).astype(BF16)

    y = _dot(h_s[...], w_ref[...].astype(BF16))
    sq = y * y
    sq_hi = sq.astype(BF16)
    sq_lo = (sq - sq_hi.astype(F32)).astype(BF16)
    ms = _dot(sq_hi, gmat_ref[...]) + _dot(sq_lo, gmat_ref[...])
    yn = y * lax.rsqrt(ms + NORM_EPS) * hgain_ref[...]
    o_ref[0] = jnp.where(hflag_ref[...] > 0.0, yn, y)


def qkv_projection(x, gain, shift, scale, w, q_gain, k_gain, tm, tn=512):
    b, l, d = x.shape
    n = w.shape[1]
    nq, nk = N_HEADS * HEAD_DIM, N_KV_HEADS * HEAD_DIM
    hgain = jnp.concatenate([jnp.tile(q_gain, N_HEADS), jnp.tile(k_gain, N_KV_HEADS),
                             jnp.ones((nk,), F32)]).reshape(1, n)
    hflag = jnp.concatenate([jnp.ones((nq + nk,), F32), jnp.zeros((nk,), F32)]).reshape(1, n)
    gidx = np.arange(tn) // HEAD_DIM
    gmat = jnp.asarray((gidx[:, None] == gidx[None, :]).astype(np.float32) / HEAD_DIM, BF16)
    blocks = (_nbytes((tm, d), F32) + _nbytes((d, tn), F32) + _nbytes((tm, tn), F32)
              + 2 * _nbytes((shift.shape[1] == 1 and 1 or tm, d), F32) + _nbytes((tn, tn), BF16))
    scratch = _nbytes((tm, d), BF16) + _nbytes((d, tn), BF16) + 4 * _nbytes((tm, tn), F32)
    return pl.pallas_call(
        _qkv_kernel,
        grid=(b, l // tm, n // tn),
        in_specs=[pl.BlockSpec((1, tm, d), lambda b_, i, j: (b_, i, 0)),
                  pl.BlockSpec((1, d), lambda b_, i, j: (0, 0)),
                  _mod_specs(shift, tm), _mod_specs(scale, tm),
                  pl.BlockSpec((d, tn), lambda b_, i, j: (0, j)),
                  pl.BlockSpec((1, tn), lambda b_, i, j: (0, j)),
                  pl.BlockSpec((1, tn), lambda b_, i, j: (0, j)),
                  pl.BlockSpec((tn, tn), lambda b_, i, j: (0, 0))],
        out_specs=pl.BlockSpec((1, tm, tn), lambda b_, i, j: (b_, i, j)),
        out_shape=jax.ShapeDtypeStruct((b, l, n), F32),
        scratch_shapes=[pltpu.VMEM((tm, d), BF16)],
        compiler_params=_params(3, blocks, scratch),
        name="qkv_projection",
    )(x, gain.reshape(1, d), shift, scale, w, hgain, hflag, gmat)


def _attn_prompt_kernel(sink_ref, q_ref, cur_ref, prev_ref, bprev_ref, bcur_ref, o_ref, kz_s, vz_s):
    n = pl.program_id(1)
    blk = ATT_BLOCK
    lane = lax.broadcasted_iota(jnp.int32, (2 * blk, LANES), 1)
    lo_half = lane < HEAD_DIM
    neg_prev = jnp.where(n == 0, NEG_INF, 0.0).astype(F32)

    kv_width = N_KV_HEADS * HEAD_DIM
    for pair in range(N_KV_HEADS // 2):
        for part, dst in ((0, kz_s), (1, vz_s)):
            col = part * kv_width + pair * LANES
            both = jnp.concatenate([prev_ref[0, :, col:col + LANES], cur_ref[0, :, col:col + LANES]], axis=0)
            swapped = pltpu.roll(both, HEAD_DIM, axis=1)
            zero = jnp.zeros_like(both)
            c0, c1 = 2 * pair, 2 * pair + 1
            dst[2 * c0 + 0] = jnp.where(lo_half, both, zero).astype(BF16)
            dst[2 * c0 + 1] = jnp.where(lo_half, zero, swapped).astype(BF16)
            dst[2 * c1 + 0] = jnp.where(lo_half, swapped, zero).astype(BF16)
            dst[2 * c1 + 1] = jnp.where(lo_half, zero, both).astype(BF16)

    scale = HEAD_DIM ** -0.5
    for p in range(N_HEADS // 2):
        c = (2 * p) // ATT_GROUP
        qp = (q_ref[0, :, p * LANES:(p + 1) * LANES] * scale).astype(BF16)
        o_pair = None
        for a in range(2):
            h = 2 * p + a
            sink = sink_ref[h]
            s = _dot_nt(qp, kz_s[2 * c + a])
            s_prev = s[:, :blk] + bprev_ref[h] + neg_prev
            s_cur = s[:, blk:] + bcur_ref[h]
            m = jnp.maximum(jnp.max(jnp.maximum(s_prev, s_cur), axis=-1, keepdims=True), sink)
            e_prev = jnp.exp(s_prev - m)
            e_cur = jnp.exp(s_cur - m)
            den = jnp.sum(e_prev + e_cur, axis=-1, keepdims=True) + jnp.exp(sink - m)
            pm = jnp.concatenate([e_prev, e_cur], axis=1).astype(BF16)
            o_a = _dot(pm, vz_s[2 * c + a]) * (1.0 / den)
            o_pair = o_a if o_pair is None else o_pair + o_a
        o_ref[0, :, p * LANES:(p + 1) * LANES] = o_pair.astype(o_ref.dtype)


def _t5_bucket_np(dist):
    max_exact = N_BUCKETS // 2
    d = np.maximum(dist, 0)
    df = np.maximum(d, 1).astype(np.float32)
    large = max_exact + (np.log(df / np.float32(max_exact)) / np.float32(math.log(MAX_DISTANCE / max_exact))
                         * np.float32(N_BUCKETS - max_exact)).astype(np.int32)
    large = np.minimum(large, N_BUCKETS - 1)
    return np.where(d < max_exact, d, large)


def _bias_from_dist(dist, in_band, rel_table):
    onehot = (_t5_bucket_np(dist)[..., None] == np.arange(N_BUCKETS)).astype(np.float32)
    tab = jnp.einsum('qsb,bh->hqs', jnp.asarray(onehot), rel_table.astype(F32), precision=lax.Precision.HIGHEST)
    return jnp.where(jnp.asarray(in_band)[None], tab, NEG_INF)


def attention_prompt(qkv, sinks, rel_table):
    b, l, n = qkv.shape
    blk = ATT_BLOCK
    nq = N_HEADS * HEAD_DIM
    kvw = 2 * N_KV_HEADS * HEAD_DIM
    kv_blk = nq // kvw
    qi = np.arange(blk)[:, None]
    sj = np.arange(blk)[None, :]
    d_prev = qi + blk - sj
    d_cur = qi - sj
    bias_prev = _bias_from_dist(d_prev, (d_prev >= 0) & (d_prev <= WINDOW), rel_table)
    bias_cur = _bias_from_dist(d_cur, (d_cur >= 0) & (d_cur <= WINDOW), rel_table)
    blocks = (_nbytes((blk, nq), F32) + 2 * _nbytes((blk, kvw), F32) + _nbytes((blk, nq), BF16))
    scratch = 2 * _nbytes((2 * N_KV_HEADS, 2 * blk, LANES), BF16) + 4 * _nbytes((N_HEADS, blk, blk), F32)
    return pl.pallas_call(
        _attn_prompt_kernel,
        grid=(b, l // blk),
        in_specs=[pl.BlockSpec(memory_space=pltpu.SMEM),
                  pl.BlockSpec((1, blk, nq), lambda b_, i: (b_, i, 0)),
                  pl.BlockSpec((1, blk, kvw), lambda b_, i: (b_, i, kv_blk)),
                  pl.BlockSpec((1, blk, kvw), lambda b_, i: (b_, jnp.maximum(i - 1, 0), kv_blk)),
                  pl.BlockSpec((N_HEADS, blk, blk), lambda b_, i: (0, 0, 0)),
                  pl.BlockSpec((N_HEADS, blk, blk), lambda b_, i: (0, 0, 0))],
        out_specs=pl.BlockSpec((1, blk, nq), lambda b_, i: (b_, i, 0)),
        out_shape=jax.ShapeDtypeStruct((b, l, nq), BF16),
        scratch_shapes=[pltpu.VMEM((2 * N_KV_HEADS, 2 * blk, LANES), BF16),
                        pltpu.VMEM((2 * N_KV_HEADS, 2 * blk, LANES), BF16)],
        compiler_params=_params(2, blocks, scratch),
        name="attention_prompt",
    )(sinks.astype(F32), qkv, qkv, qkv, bias_prev, bias_cur)


def _attn_sample_kernel(q_ref, k_ref, v_ref, bias_ref, sink_ref, o_ref):
    k = k_ref[0].astype(BF16)
    v = v_ref[0].astype(BF16)
    for c in range(N_KV_HEADS):
        s = _dot_nt(q_ref[0, c].astype(BF16), k) + bias_ref[c]
        sink = sink_ref[c]
        m = jnp.maximum(jnp.max(s, axis=-1, keepdims=True), sink)
        e = jnp.exp(s - m)
        den = jnp.sum(e, axis=-1, keepdims=True) + jnp.exp(sink - m)
        o_ref[0, c] = _dot(e.astype(BF16), v) * (1.0 / den)


def attention_sample(qkv, cache_k, cache_v, sinks, rel_table):
    bsz, w = cache_k.shape[0], cache_k.shape[1]
    t = qkv.shape[0] // bsz
    nq, nkv = N_HEADS * HEAD_DIM, N_KV_HEADS * HEAD_DIM
    keys = w + t
    keys_pad = -(-keys // 16) * 16
    rows = ATT_GROUP * t
    q = qkv[:, :nq].reshape(bsz, t, N_KV_HEADS, ATT_GROUP, HEAD_DIM) * HEAD_DIM ** -0.5
    q = jnp.transpose(q, (0, 2, 3, 1, 4)).reshape(bsz, N_KV_HEADS, rows, HEAD_DIM)
    qz = jnp.einsum('bcrd,ck->bcrkd', q, jnp.eye(N_KV_HEADS, dtype=F32)).reshape(bsz, N_KV_HEADS, rows, nkv)
    k_all = jnp.concatenate([cache_k.reshape(bsz, w, nkv), qkv[:, nq:nq + nkv].reshape(bsz, t, nkv)], axis=1)
    v_all = jnp.concatenate([cache_v.reshape(bsz, w, nkv), qkv[:, nq + nkv:].reshape(bsz, t, nkv)], axis=1)
    pad = ((0, 0), (0, keys_pad - keys), (0, 0))
    k_pad, v_pad = jnp.pad(k_all, pad), jnp.pad(v_all, pad)
    tq = np.arange(t)[:, None]
    sk = np.arange(keys_pad)[None, :]
    dist = tq + w - sk
    in_band = (dist >= 0) & (dist <= WINDOW) & (sk < keys)
    bias_t = _bias_from_dist(dist, in_band, rel_table)
    bias = bias_t.reshape(N_KV_HEADS, rows, keys_pad)
    sink_rows = jnp.repeat(sinks.astype(F32), t).reshape(N_KV_HEADS, rows, 1)
    blocks = (_nbytes((N_KV_HEADS, rows, nkv), F32) * 2 + 2 * _nbytes((keys_pad, nkv), F32))
    oz = pl.pallas_call(
        _attn_sample_kernel,
        grid=(bsz,),
        in_specs=[pl.BlockSpec((1, N_KV_HEADS, rows, nkv), lambda b_: (b_, 0, 0, 0)),
                  pl.BlockSpec((1, keys_pad, nkv), lambda b_: (b_, 0, 0)),
                  pl.BlockSpec((1, keys_pad, nkv), lambda b_: (b_, 0, 0)),
                  pl.BlockSpec((N_KV_HEADS, rows, keys_pad), lambda b_: (0, 0, 0)),
                  pl.BlockSpec((N_KV_HEADS, rows, 1), lambda b_: (0, 0, 0))],
        out_specs=pl.BlockSpec((1, N_KV_HEADS, rows, nkv), lambda b_: (b_, 0, 0, 0)),
        out_shape=jax.ShapeDtypeStruct((bsz, N_KV_HEADS, rows, nkv), F32),
        compiler_params=_params(1, blocks),
        name="attention_sample",
    )(qz, k_pad, v_pad, bias, sink_rows)
    o = jnp.stack([oz[:, c, :, c * HEAD_DIM:(c + 1) * HEAD_DIM] for c in range(N_KV_HEADS)], axis=1)
    o = o.reshape(bsz, N_KV_HEADS, ATT_GROUP, t, HEAD_DIM)
    o = jnp.transpose(o, (0, 3, 1, 2, 4)).reshape(bsz * t, nq)
    return o.astype(BF16), k_all[:, t:], v_all[:, t:]


def _mm_res_kernel(a_ref, w_ref, x_ref, g_ref, o_ref):
    y = _dot(a_ref[0], w_ref[...].astype(BF16))
    o_ref[0] = x_ref[0] + g_ref[0] * y


def matmul_gate_residual(a, w_stack, layer, x, gate, tm, tn):
    b, l, k = a.shape
    n = w_stack.shape[2]
    grows = 1 if gate.shape[1] == 1 else tm
    if gate.shape[1] == 1:
        g_spec = pl.BlockSpec((1, 1, tn), lambda b_, i, j: (b_, 0, j))
    else:
        g_spec = pl.BlockSpec((1, tm, tn), lambda b_, i, j: (b_, i, j))
    blocks = (_nbytes((tm, k), a.dtype) + _nbytes((k, tn), F32) + 2 * _nbytes((tm, tn), F32)
              + _nbytes((grows, tn), F32))
    scratch = _nbytes((k, tn), BF16) + _nbytes((tm, tn), F32)
    return pl.pallas_call(
        _mm_res_kernel,
        grid=(b, l // tm, n // tn),
        in_specs=[pl.BlockSpec((1, tm, k), lambda b_, i, j: (b_, i, 0)),
                  pl.BlockSpec((None, k, tn), lambda b_, i, j: (layer, 0, j)),
                  pl.BlockSpec((1, tm, tn), lambda b_, i, j: (b_, i, j)),
                  g_spec],
        out_specs=pl.BlockSpec((1, tm, tn), lambda b_, i, j: (b_, i, j)),
        out_shape=jax.ShapeDtypeStruct((b, l, n), F32),
        compiler_params=_params(3, blocks, scratch),
        name="matmul_gate_residual",
    )(a, w_stack, x, gate)


def _ffn_conv_rows(u, prev1, prev2, cw_ref, cb_ref):
    return u * cw_ref[2:3, :] + prev1 * cw_ref[1:2, :] + prev2 * cw_ref[0:1, :] + cb_ref[...]


def _ffn_up_prompt_kernel(x_ref, gain_ref, shift_ref, scale_ref, wg_ref, wv_ref, cwg_ref, cwv_ref,
                          cbg_ref, cbv_ref, o_ref, tail_ref, h_s, carry_s):
    i = pl.program_id(1)
    j = pl.program_id(2)
    tm = h_s.shape[0]

    @pl.when(j == 0)
    def _():
        h_s[...] = _norm_mod(x_ref[0], gain_ref[...], shift_ref[0], scale_ref[0]).astype(BF16)

    @pl.when(i == 0)
    def _():
        carry_s[j] = jnp.zeros(carry_s.shape[1:], F32)

    halves = ((wg_ref, cwg_ref, cbg_ref), (wv_ref, cwv_ref, cbv_ref))
    w_bf = [w_ref[...].astype(BF16) for w_ref, _, _ in halves]
    prev_tail = [carry_s[j, half] for half in range(2)]
    row8 = lax.broadcasted_iota(jnp.int32, (SUBLANES, 1), 0)
    sub = min(FFN_SUB_ROWS, tm)
    for m in range(tm // sub):
        hm = h_s[m * sub:(m + 1) * sub, :]
        ys = []
        for half, (_, cw_ref, cb_ref) in enumerate(halves):
            u = _dot(hm, w_bf[half])
            y = _ffn_conv_rows(u, pltpu.roll(u, 1, axis=0), pltpu.roll(u, 2, axis=0), cw_ref, cb_ref)
            top, hist = u[0:SUBLANES], prev_tail[half]
            p1 = jnp.where(row8 < 1, pltpu.roll(hist, 1, axis=0), pltpu.roll(top, 1, axis=0))
            p2 = jnp.where(row8 < 2, pltpu.roll(hist, 2, axis=0), pltpu.roll(top, 2, axis=0))
            y_top = _ffn_conv_rows(top, p1, p2, cw_ref, cb_ref)
            ys.append(jnp.concatenate([y_top, y[SUBLANES:]], axis=0))
            prev_tail[half] = u[sub - SUBLANES:sub]
        o_ref[0, m * sub:(m + 1) * sub, :] = (_silu(ys[0]) * ys[1]).astype(o_ref.dtype)
    for half in range(2):
        carry_s[j, half] = prev_tail[half]
        tail_ref[0, 0, half] = prev_tail[half]


def ffn_up_prompt(x, gain, shift, scale, w_up, conv_w, conv_b, layer, tm, tn=512):
    b, l, d = x.shape
    dff = w_up.shape[2] // 2
    nj = dff // tn
    conv_b = conv_b.reshape(conv_b.shape[0], 1, 2 * dff)
    blocks = (_nbytes((tm, d), F32) + 2 * _nbytes((d, tn), F32) + _nbytes((tm, tn), BF16)
              + 2 * _nbytes((1, d), F32) + 8 * _nbytes((SUBLANES, tn), F32))
    scratch = (_nbytes((tm, d), BF16) + _nbytes((nj, 2, SUBLANES, tn), F32) + 2 * _nbytes((d, tn), BF16)
               + 8 * _nbytes((tm, tn), F32))
    act, tail = pl.pallas_call(
        _ffn_up_prompt_kernel,
        grid=(b, l // tm, nj),
        in_specs=[pl.BlockSpec((1, tm, d), lambda b_, i, j: (b_, i, 0)),
                  pl.BlockSpec((1, d), lambda b_, i, j: (0, 0)),
                  _mod_specs(shift, tm), _mod_specs(scale, tm),
                  pl.BlockSpec((None, d, tn), lambda b_, i, j: (layer, 0, j)),
                  pl.BlockSpec((None, d, tn), lambda b_, i, j: (layer, 0, j + nj)),
                  pl.BlockSpec((None, FFN_CONV, tn), lambda b_, i, j: (layer, 0, j)),
                  pl.BlockSpec((None, FFN_CONV, tn), lambda b_, i, j: (layer, 0, j + nj)),
                  pl.BlockSpec((None, 1, tn), lambda b_, i, j: (layer, 0, j)),
                  pl.BlockSpec((None, 1, tn), lambda b_, i, j: (layer, 0, j + nj))],
        out_specs=[pl.BlockSpec((1, tm, tn), lambda b_, i, j: (b_, i, j)),
                   pl.BlockSpec((1, 1, 2, SUBLANES, tn), lambda b_, i, j: (b_, i, 0, 0, j))],
        out_shape=[jax.ShapeDtypeStruct((b, l, dff), BF16),
                   jax.ShapeDtypeStruct((b, l // tm, 2, SUBLANES, dff), F32)],
        scratch_shapes=[pltpu.VMEM((tm, d), BF16), pltpu.VMEM((nj, 2, SUBLANES, tn), F32)],
        compiler_params=_params(3, blocks, scratch),
        name="ffn_up_prompt",
    )(x, gain.reshape(1, d), shift, scale, w_up, w_up, conv_w, conv_w, conv_b, conv_b)
    keep = FFN_CONV - 1
    hist = jnp.transpose(tail[:, -1, :, SUBLANES - keep:, :], (0, 2, 1, 3)).reshape(b, keep, 2 * dff)
    return act, hist


def _ffn_up_sample_kernel(x_ref, gain_ref, shift_ref, scale_ref, wg_ref, wv_ref, cwg_ref, cwv_ref,
                          cbg_ref, cbv_ref, e1g_ref, e1v_ref, e2g_ref, e2v_ref, o_ref, ug_ref, uv_ref, h_s,
                          *, seq_len):
    @pl.when(pl.program_id(0) == 0)
    def _():
        h_s[...] = _norm_mod(x_ref[...], gain_ref[...], shift_ref[...], scale_ref[...]).astype(BF16)

    h = h_s[...]
    rows = h_s.shape[0]
    t = lax.broadcasted_iota(jnp.int32, (rows, 1), 0) % seq_len
    ys = []
    for w_ref, cw_ref, cb_ref, e1_ref, e2_ref, u_ref in (
            (wg_ref, cwg_ref, cbg_ref, e1g_ref, e2g_ref, ug_ref),
            (wv_ref, cwv_ref, cbv_ref, e1v_ref, e2v_ref, uv_ref)):
        u = _dot(h, w_ref[...].astype(BF16))
        u_ref[...] = u
        p1 = jnp.where(t >= 1, pltpu.roll(u, 1, axis=0), e1_ref[...])
        p2 = jnp.where(t >= 2, pltpu.roll(u, 2, axis=0), e2_ref[...])
        ys.append(_ffn_conv_rows(u, p1, p2, cw_ref, cb_ref))
    o_ref[...] = (_silu(ys[0]) * ys[1]).astype(o_ref.dtype)


def ffn_up_sample(x, gain, shift, scale, w_up, conv_w, conv_b, layer, hist, seq_len, tn=512):
    rows, d = x.shape
    dff = w_up.shape[2] // 2
    nj = dff // tn
    bsz = rows // seq_len
    conv_b = conv_b.reshape(conv_b.shape[0], 1, 2 * dff)
    tpos = np.arange(seq_len)
    sel = lambda cond: jnp.asarray(cond.astype(np.float32))[None, :, None]
    h0, h1 = hist[:, 0:1, :], hist[:, 1:2, :]
    e1 = (h1 * sel(tpos == 0)).reshape(rows, 2 * dff)
    e2 = (h0 * sel(tpos == 0) + h1 * sel(tpos == 1)).reshape(rows, 2 * dff)
    col = lambda j: (0, j)
    col_hi = lambda j: (0, j + nj)
    lcol = lambda j: (layer, 0, j)
    lcol_hi = lambda j: (layer, 0, j + nj)
    blocks = (_nbytes((rows, d), F32) * 3 + 2 * _nbytes((d, tn), F32) + 7 * _nbytes((rows, tn), F32))
    scratch = _nbytes((rows, d), BF16) + 2 * _nbytes((d, tn), BF16) + 8 * _nbytes((rows, tn), F32)
    act, ug, uv = pl.pallas_call(
        functools.partial(_ffn_up_sample_kernel, seq_len=seq_len),
        grid=(nj,),
        in_specs=[pl.BlockSpec((rows, d), lambda j: (0, 0)),
                  pl.BlockSpec((1, d), lambda j: (0, 0)),
                  pl.BlockSpec((rows, d), lambda j: (0, 0)),
                  pl.BlockSpec((rows, d), lambda j: (0, 0)),
                  pl.BlockSpec((None, d, tn), lcol), pl.BlockSpec((None, d, tn), lcol_hi),
                  pl.BlockSpec((None, FFN_CONV, tn), lcol), pl.BlockSpec((None, FFN_CONV, tn), lcol_hi),
                  pl.BlockSpec((None, 1, tn), lcol), pl.BlockSpec((None, 1, tn), lcol_hi),
                  pl.BlockSpec((rows, tn), col), pl.BlockSpec((rows, tn), col_hi),
                  pl.BlockSpec((rows, tn), col), pl.BlockSpec((rows, tn), col_hi)],
        out_specs=[pl.BlockSpec((rows, tn), col), pl.BlockSpec((rows, tn), col), pl.BlockSpec((rows, tn), col)],
        out_shape=[jax.ShapeDtypeStruct((rows, dff), BF16),
                   jax.ShapeDtypeStruct((rows, dff), F32),
                   jax.ShapeDtypeStruct((rows, dff), F32)],
        scratch_shapes=[pltpu.VMEM((rows, d), BF16)],
        compiler_params=_params(1, blocks, scratch),
        name="ffn_up_sample",
    )(x, gain.reshape(1, d), shift, scale, w_up, w_up, conv_w, conv_w, conv_b, conv_b, e1, e1, e2, e2)
    keep = FFN_CONV - 1
    u_all = jnp.concatenate([ug, uv], axis=-1).reshape(bsz, seq_len, 2 * dff)
    new_hist = jnp.concatenate([hist, u_all], axis=1)[:, -keep:]
    return act, new_hist


def _gdn_in_kernel(*refs, n_qk_tiles, n_conv_tiles, seq_len):
    if seq_len is None:
        (x_ref, gain_ref, shift_ref, scale_ref, w_ref, wba_ref, alog_ref, dtb_ref, cw_ref,
         o_ref, bg_ref, raw_ref, h_s, carry_s) = refs
        inj_refs = None
    else:
        (x_ref, gain_ref, shift_ref, scale_ref, w_ref, wba_ref, alog_ref, dtb_ref, cw_ref, *inj_refs,
         o_ref, bg_ref, raw_ref, h_s) = refs
    i = pl.program_id(1)
    j = pl.program_id(2)
    tm = h_s.shape[0]
    width = GDN_CONV

    @pl.when(j == 0)
    def _():
        h = _norm_mod(x_ref[0], gain_ref[...], shift_ref[0], scale_ref[0]).astype(BF16)
        h_s[...] = h
        y = _dot_nt(h, wba_ref[...].astype(BF16))
        lane = lax.broadcasted_iota(jnp.int32, y.shape, 1) % LANES
        beta = 1.0 / (1.0 + jnp.exp(-y))
        a = y + dtb_ref[...]
        softplus = jnp.maximum(a, 0.0) + jnp.log1p(jnp.exp(-jnp.abs(a)))
        g = -jnp.exp(alog_ref[...]) * softplus
        bg_ref[0] = jnp.where(lane < GDN_VH_PER_GROUP, beta, jnp.where(lane < 2 * GDN_VH_PER_GROUP, g, 0.0))

    sub = min(FFN_SUB_ROWS, tm)
    row8 = lax.broadcasted_iota(jnp.int32, (SUBLANES, 1), 0)

    def conv_tile(l2_scale):
        w_bf = w_ref[...].astype(BF16)
        jc = jnp.minimum(j, n_conv_tiles - 1)
        if seq_len is None:
            @pl.when(i == 0)
            def _():
                carry_s[jc] = jnp.zeros(carry_s.shape[1:], F32)
            prev_tail = carry_s[jc]
        for m in range(tm // sub):
            rs = slice(m * sub, (m + 1) * sub)
            u = _dot_nt(h_s[rs, :], w_bf)
            y = u * cw_ref[width - 1:width, :]
            if seq_len is None:
                top = u[0:SUBLANES]
                y_top = top * cw_ref[width - 1:width, :]
                for k in range(1, width):
                    wk = cw_ref[width - 1 - k:width - k, :]
                    y = y + pltpu.roll(u, k, axis=0) * wk
                    y_top = y_top + jnp.where(row8 < k, pltpu.roll(prev_tail, k, axis=0),
                                              pltpu.roll(top, k, axis=0)) * wk
                y = jnp.concatenate([y_top, y[SUBLANES:]], axis=0)
                prev_tail = u[sub - SUBLANES:sub]
            else:
                t = lax.broadcasted_iota(jnp.int32, (sub, 1), 0) % seq_len
                for k in range(1, width):
                    wk = cw_ref[width - 1 - k:width - k, :]
                    y = y + jnp.where(t >= k, pltpu.roll(u, k, axis=0), inj_refs[k - 1][rs, :]) * wk
                raw_ref[rs, :] = u
            y = _silu(y)
            if l2_scale is not None:
                heads = [y[:, hd * LANES:(hd + 1) * LANES] for hd in range(y.shape[1] // LANES)]
                y = jnp.concatenate(
                    [yh * (lax.rsqrt(jnp.sum(yh * yh, axis=-1, keepdims=True) + NORM_EPS) * l2_scale)
                     for yh in heads], axis=1)
            o_ref[0, rs, :] = y.astype(o_ref.dtype)
        if seq_len is None:
            carry_s[jc] = prev_tail
            raw_ref[0, 0] = prev_tail

    @pl.when(j < n_qk_tiles)
    def _():
        conv_tile(jnp.where(j < n_qk_tiles // 2, GDN_DK ** -0.5, 1.0).astype(F32))

    @pl.when((j >= n_qk_tiles) & (j < n_conv_tiles))
    def _():
        conv_tile(None)

    @pl.when(j >= n_conv_tiles)
    def _():
        w_bf = w_ref[...].astype(BF16)
        for m in range(tm // sub):
            rs = slice(m * sub, (m + 1) * sub)
            o_ref[0, rs, :] = _silu(_dot_nt(h_s[rs, :], w_bf)).astype(o_ref.dtype)


def _group_lane_layout(vec_b, vec_a):
    lead = vec_b.shape[:-1]
    vb = vec_b.reshape(*lead, GDN_HEAD_GROUPS, GDN_VH_PER_GROUP)
    va = vec_a.reshape(*lead, GDN_HEAD_GROUPS, GDN_VH_PER_GROUP)
    pad = jnp.zeros((*lead, GDN_HEAD_GROUPS, LANES - 2 * GDN_VH_PER_GROUP), vec_b.dtype)
    return jnp.concatenate([vb, va, pad], axis=-1).reshape(*lead, GDN_HEAD_GROUPS * LANES)


def gdn_in_projection(x, gain, shift, scale, w_in, conv_w, a_log, dt_bias, tm, conv_hist=None, tn=512):
    b, l, d = x.shape
    n_main = GDN_CONV_DIM + GDN_VAL_DIM
    w_t = jnp.transpose(w_in)
    wba = _group_lane_layout(w_in[:, n_main:n_main + GDN_V_HEADS], w_in[:, n_main + GDN_V_HEADS:])
    wba_t = jnp.transpose(wba)
    zeros = jnp.zeros((1, GDN_V_HEADS), F32)
    alog = _group_lane_layout(zeros, a_log.reshape(1, -1).astype(F32))
    dtb = _group_lane_layout(zeros, dt_bias.reshape(1, -1).astype(F32))
    nbg = wba_t.shape[0]
    n_conv_tiles = GDN_CONV_DIM // tn
    n_qk_tiles = 2 * GDN_KEY_DIM // tn
    conv_col = lambda b_, i, j: (0, jnp.minimum(j, n_conv_tiles - 1))
    in_specs = [pl.BlockSpec((1, tm, d), lambda b_, i, j: (b_, i, 0)),
                pl.BlockSpec((1, d), lambda b_, i, j: (0, 0)),
                _mod_specs(shift, tm), _mod_specs(scale, tm),
                pl.BlockSpec((tn, d), lambda b_, i, j: (j, 0)),
                pl.BlockSpec((nbg, d), lambda b_, i, j: (0, 0)),
                pl.BlockSpec((1, nbg), lambda b_, i, j: (0, 0)),
                pl.BlockSpec((1, nbg), lambda b_, i, j: (0, 0)),
                pl.BlockSpec((GDN_CONV, tn), conv_col)]
    args = [x, gain.reshape(1, d), shift, scale, w_t, wba_t, alog, dtb, conv_w]
    scratch_shapes = [pltpu.VMEM((tm, d), BF16)]
    blocks = (_nbytes((tm, d), F32) + _nbytes((tn, d), F32) + _nbytes((tm, tn), BF16)
              + _nbytes((nbg, d), F32) + _nbytes((tm, nbg), F32) + 2 * _nbytes((1, d), F32))
    scratch = _nbytes((tm, d), BF16) + _nbytes((tn, d), BF16) + 6 * _nbytes((tm, nbg), F32)
    if conv_hist is None:
        seq_len = None
        raw_spec = pl.BlockSpec((1, 1, SUBLANES, tn), lambda b_, i, j: (b_, i, 0, jnp.minimum(j, n_conv_tiles - 1)))
        raw_shape = jax.ShapeDtypeStruct((b, l // tm, SUBLANES, GDN_CONV_DIM), F32)
        scratch_shapes.append(pltpu.VMEM((n_conv_tiles, SUBLANES, tn), F32))
    else:
        seqs = conv_hist.shape[0]
        seq_len = l // seqs
        tpos = np.arange(seq_len)
        for k in range(1, GDN_CONV):
            inj = sum(conv_hist[:, GDN_CONV - 1 - k + t:GDN_CONV - k + t, :]
                      * jnp.asarray((tpos == t).astype(np.float32))[None, :, None] for t in range(k))
            args.append(inj.reshape(l, GDN_CONV_DIM))
            in_specs.append(pl.BlockSpec((tm, tn), conv_col))
        raw_spec = pl.BlockSpec((tm, tn), conv_col)
        raw_shape = jax.ShapeDtypeStruct((l, GDN_CONV_DIM), F32)
        blocks += 4 * _nbytes((tm, tn), F32)
    return pl.pallas_call(
        functools.partial(_gdn_in_kernel, n_qk_tiles=n_qk_tiles, n_conv_tiles=n_conv_tiles, seq_len=seq_len),
        grid=(b, l // tm, n_main // tn),
        in_specs=in_specs,
        out_specs=[pl.BlockSpec((1, tm, tn), lambda b_, i, j: (b_, i, j)),
                   pl.BlockSpec((1, tm, nbg), lambda b_, i, j: (b_, i, 0)),
                   raw_spec],
        out_shape=[jax.ShapeDtypeStruct((b, l, n_main), BF16),
                   jax.ShapeDtypeStruct((b, l, nbg), F32),
                   raw_shape],
        scratch_shapes=scratch_shapes,
        compiler_params=_params(3, blocks, scratch),
        name="gdn_in_projection",
    )(*args)


def _gdn_core_kernel(q_ref, k_ref, v_ref, zg_ref, bg_ref, onorm_ref, s0_ref, o_ref, s1_ref,
                     s_s, gc_s, tdec_s, mo_s):
    r = pl.program_id(2)
    n_r = pl.num_programs(2)
    rows = q_ref.shape[1]
    chunk = GDN_CHUNK
    vpg = GDN_VH_PER_GROUP
    rep = GDN_V_HEADS // GDN_K_HEADS

    @pl.when(r == 0)
    def _():
        for kh in range(GDN_KH_PER_GROUP):
            s_s[kh] = jnp.concatenate([s0_ref[0, kh * rep + e] for e in range(rep)], axis=1)

    n_chunks = rows // chunk
    pw = GDN_PACK * chunk
    n_packs = vpg // GDN_PACK
    ri = lax.broadcasted_iota(jnp.int32, (chunk, pw), 0)
    li = lax.broadcasted_iota(jnp.int32, (chunk, pw), 1) % chunk
    tril_p = ri >= li
    strict_p = ri > li
    eye_p = (ri == li).astype(F32)
    n_levels = int(math.log2(chunk))
    off_masks = [((ri >> (lvl + 1)) == (li >> (lvl + 1))) & ((ri >> lvl) != (li >> lvl))
                 for lvl in range(n_levels)]
    bd_rows = lax.broadcasted_iota(jnp.int32, (pw, pw), 0) // chunk
    bd_cols = lax.broadcasted_iota(jnp.int32, (pw, pw), 1) // chunk
    bd_mask = (bd_rows == bd_cols).astype(BF16)
    row_c = lax.broadcasted_iota(jnp.int32, (chunk, LANES), 0)
    lane_lo = lax.broadcasted_iota(jnp.int32, (chunk, LANES), 1) < chunk

    def block_diag(xp):
        return jnp.concatenate([xp.astype(BF16)] * GDN_PACK, axis=0) * bd_mask

    def pack_cols(arr, first_col):
        tiles = []
        for t in range(pw // LANES):
            even = jnp.broadcast_to(arr[:, first_col + 2 * t:first_col + 2 * t + 1], (chunk, LANES))
            odd = jnp.broadcast_to(arr[:, first_col + 2 * t + 1:first_col + 2 * t + 2], (chunk, LANES))
            tiles.append(jnp.where(lane_lo, even, odd))
        return jnp.concatenate(tiles, axis=1)

    for c0 in range(0, n_chunks, GDN_PHASE_A_CHUNKS):
        group = [(c, p) for c in range(c0, min(c0 + GDN_PHASE_A_CHUNKS, n_chunks)) for p in range(n_packs)]
        a_list, qkd, brow, dec = {}, {}, {}, {}
        for c in range(c0, min(c0 + GDN_PHASE_A_CHUNKS, n_chunks)):
            rs = slice(c * chunk, (c + 1) * chunk)
            bg = bg_ref[0, rs, :]
            cum = bg
            shift = 1
            while shift < chunk:
                cum = cum + jnp.where(row_c >= shift, pltpu.roll(cum, shift, axis=0), 0.0)
                shift *= 2
            gc_s[rs, :] = cum
            gq = []
            for kh in range(GDN_KH_PER_GROUP):
                ksl = slice(kh * LANES, (kh + 1) * LANES)
                k = k_ref[0, rs, ksl]
                kq = jnp.concatenate([k, q_ref[0, rs, ksl]], axis=0)
                kk = jnp.concatenate([k, k], axis=0)
                gq.append(_dot_nt(kq, kk))
            for p in range(n_packs):
                khs = [(p * GDN_PACK + 2 * t) // rep for t in range(pw // LANES)]
                gram = jnp.concatenate([gq[kh][:chunk] for kh in khs], axis=1)
                qk = jnp.concatenate([gq[kh][chunk:] for kh in khs], axis=1)
                beta_p = pack_cols(bg, p * GDN_PACK)
                gcol_p = pack_cols(cum, vpg + p * GDN_PACK)
                grow_p = jnp.sum(gcol_p * eye_p, axis=0, keepdims=True)
                decay = jnp.exp(jnp.where(tril_p, gcol_p - grow_p, NEG_INF))
                a_list[(c, p)] = jnp.where(strict_p, gram * beta_p * decay, 0.0)
                qkd[(c, p)] = (qk * decay).astype(BF16)
                brow[(c, p)] = jnp.sum(beta_p * eye_p, axis=0, keepdims=True)
                dec[(c, p)] = jnp.exp(gcol_p[chunk - 1:chunk, :] - gcol_p)
        inv = {cp: eye_p - jnp.where(off_masks[0], a_list[cp], 0.0) for cp in group}
        for lvl in range(1, n_levels):
            w = {cp: _dot(jnp.where(off_masks[lvl], a_list[cp], 0.0).astype(BF16), block_diag(inv[cp]))
                 for cp in group}
            inv = {cp: inv[cp] - _dot(inv[cp].astype(BF16), block_diag(w[cp])) for cp in group}
        for cp in group:
            t_beta = inv[cp] * brow[cp]
            tdec_s[cp[0], cp[1]] = block_diag(dec[cp] * t_beta)
            mo_s[cp[0], cp[1]] = block_diag(_dot(qkd[cp], block_diag(t_beta)))

    def chunk_body(c, carry):
        rs = pl.ds(pl.multiple_of(c * chunk, chunk), chunk)
        cum = gc_s[rs, :]
        eg, g_last, ks, qs = [], [], [], []
        for kh in range(GDN_KH_PER_GROUP):
            ksl = slice(kh * LANES, (kh + 1) * LANES)
            kq = jnp.concatenate([k_ref[0, rs, ksl], q_ref[0, rs, ksl]], axis=0)
            kq_s = _dot(kq, s_s[kh].astype(BF16))
            for e in range(rep):
                hv = kh * rep + e
                gcol = cum[:, vpg + hv:vpg + hv + 1]
                g_last.append(gcol[chunk - 1:chunk, :])
                eg.append(jnp.exp(gcol))
                ks.append(kq_s[:chunk, e * LANES:(e + 1) * LANES] * eg[hv])
                qs.append(kq_s[chunk:, e * LANES:(e + 1) * LANES] * eg[hv])
        v_dec, o_intra = [], []
        for p in range(n_packs):
            heads = range(p * GDN_PACK, (p + 1) * GDN_PACK)
            rhs = jnp.concatenate([v_ref[0, rs, hv * LANES:(hv + 1) * LANES].astype(F32) - ks[hv]
                                   for hv in heads], axis=0).astype(BF16)
            vd = _dot(tdec_s[c, p], rhs)
            oi = _dot(mo_s[c, p], rhs)
            for j in range(GDN_PACK):
                v_dec.append(vd[j * chunk:(j + 1) * chunk])
                o_intra.append(oi[j * chunk:(j + 1) * chunk])
        for hv in range(vpg):
            vsl = slice(hv * LANES, (hv + 1) * LANES)
            o = qs[hv] + o_intra[hv]
            on = o * lax.rsqrt(jnp.mean(o * o, axis=-1, keepdims=True) + NORM_EPS) * onorm_ref[...]
            o_ref[0, rs, vsl] = (on * zg_ref[0, rs, vsl].astype(F32)).astype(o_ref.dtype)
        lane_v = lax.broadcasted_iota(jnp.int32, (1, rep * LANES), 1) // LANES
        for kh in range(GDN_KH_PER_GROUP):
            heads = range(kh * rep, (kh + 1) * rep)
            ds = _dot_tn(k_ref[0, rs, kh * LANES:(kh + 1) * LANES],
                         jnp.concatenate([v_dec[hv] for hv in heads], axis=1).astype(BF16))
            keep = jnp.exp(g_last[kh * rep])
            for e in range(1, rep):
                keep = jnp.where(lane_v == e, jnp.exp(g_last[kh * rep + e]), keep)
            s_s[kh] = s_s[kh] * keep + ds
        return carry

    lax.fori_loop(0, n_chunks, chunk_body, 0)

    @pl.when(r == n_r - 1)
    def _():
        for hv in range(vpg):
            s1_ref[0, hv] = s_s[hv // rep][:, (hv % rep) * LANES:(hv % rep + 1) * LANES]


def gdn_core(qkvz, bg, s0, out_norm, rows):
    b, l, _ = qkvz.shape
    hg = GDN_HEAD_GROUPS
    qw = GDN_KEY_DIM // hg
    vw = GDN_VAL_DIM // hg
    kq = GDN_KEY_DIM // qw
    kv = 2 * GDN_KEY_DIM // vw
    kz = GDN_CONV_DIM // vw
    vpg = GDN_VH_PER_GROUP
    blocks = (2 * _nbytes((rows, qw), BF16) + 3 * _nbytes((rows, vw), BF16) + _nbytes((rows, LANES), F32)
              + 2 * _nbytes((vpg, GDN_DK, GDN_DV), F32))
    n_chunks = rows // GDN_CHUNK
    n_packs = vpg // GDN_PACK
    pw = GDN_PACK * GDN_CHUNK
    scratch = (_nbytes((vpg, GDN_DK, GDN_DV), F32) + _nbytes((rows, LANES), F32)
               + 2 * _nbytes((n_chunks, n_packs, pw, pw), BF16) + 6 * _nbytes((rows, vw), F32))
    return pl.pallas_call(
        _gdn_core_kernel,
        grid=(b, hg, l // rows),
        in_specs=[pl.BlockSpec((1, rows, qw), lambda b_, g, r: (b_, r, g)),
                  pl.BlockSpec((1, rows, qw), lambda b_, g, r: (b_, r, kq + g)),
                  pl.BlockSpec((1, rows, vw), lambda b_, g, r: (b_, r, kv + g)),
                  pl.BlockSpec((1, rows, vw), lambda b_, g, r: (b_, r, kz + g)),
                  pl.BlockSpec((1, rows, LANES), lambda b_, g, r: (b_, r, g)),
                  pl.BlockSpec((1, GDN_DV), lambda b_, g, r: (0, 0)),
                  pl.BlockSpec((1, vpg, GDN_DK, GDN_DV), lambda b_, g, r: (b_, g, 0, 0))],
        out_specs=[pl.BlockSpec((1, rows, vw), lambda b_, g, r: (b_, r, g)),
                   pl.BlockSpec((1, vpg, GDN_DK, GDN_DV), lambda b_, g, r: (b_, g, 0, 0))],
        out_shape=[jax.ShapeDtypeStruct((b, l, GDN_VAL_DIM), BF16),
                   jax.ShapeDtypeStruct((b, GDN_V_HEADS, GDN_DK, GDN_DV), F32)],
        scratch_shapes=[pltpu.VMEM((GDN_KH_PER_GROUP, GDN_DK, (GDN_V_HEADS // GDN_K_HEADS) * GDN_DV), F32),
                        pltpu.VMEM((rows, LANES), F32),
                        pltpu.VMEM((n_chunks, n_packs, pw, pw), BF16),
                        pltpu.VMEM((n_chunks, n_packs, pw, pw), BF16)],
        compiler_params=_params(3, blocks, scratch),
        name="gdn_core",
    )(qkvz, qkvz, qkvz, qkvz, bg, out_norm.reshape(1, GDN_DV).astype(F32), s0)


def _tile_rows(l, cap):
    t = min(l, cap)
    while l % t:
        t //= 2
    return t


def _trunk(x, mod, states, P, sample):
    bsz, l, d = x.shape
    tm = _tile_rows(l, ROW_TILE_CAP)
    new = {}
    depth = mod.shape[0]
    for layer in range(depth):
        if sample:
            parts = [mod[layer][None, :, k * d:(k + 1) * d] for k in range(6)]
        else:
            parts = [mod[layer][:, None, k * d:(k + 1) * d] for k in range(6)]
        sh_m, sc_m, g_m, sh_f, sc_f, g_f = parts
        i = layer // 2
        if layer % 2 == 0:
            qkv = qkv_projection(x, P['norm_mix'][layer], sh_m, sc_m, P['w_attn_qkv'][i],
                                 P['attn_q_norm'][i], P['attn_k_norm'][i], tm)
            nq, nkv = N_HEADS * HEAD_DIM, N_KV_HEADS * HEAD_DIM
            if sample:
                o, k_win, v_win = attention_sample(qkv[0], states['win_k'][i], states['win_v'][i],
                                                   P['attn_sinks'][i], P['rel_bias_table'])
                o = o[None]
                w_len = k_win.shape[1]
                new.setdefault('win_k', []).append(k_win.reshape(-1, w_len, N_KV_HEADS, HEAD_DIM))
                new.setdefault('win_v', []).append(v_win.reshape(-1, w_len, N_KV_HEADS, HEAD_DIM))
            else:
                o = attention_prompt(qkv, P['attn_sinks'][i], P['rel_bias_table'])
                keep = min(WINDOW, PAST_LEN)
                new.setdefault('win_k', []).append(
                    qkv[:, l - keep:, nq:nq + nkv].reshape(bsz, keep, N_KV_HEADS, HEAD_DIM))
                new.setdefault('win_v', []).append(
                    qkv[:, l - keep:, nq + nkv:].reshape(bsz, keep, N_KV_HEADS, HEAD_DIM))
            x = matmul_gate_residual(o, P['w_attn_o'], i, x, g_m, tm, 512)
        else:
            keep = GDN_CONV - 1
            gdn_args = (x, P['norm_mix'][layer], sh_m, sc_m, P['w_gdn_in'][i], P['gdn_conv_w'][i],
                        P['gdn_a_log'][i], P['gdn_dt_bias'][i], tm)
            if sample:
                seqs = states['gdn'].shape[1]
                t = l // seqs
                conv0 = states['gdn_conv'][i].astype(F32)
                qkvz, bg, raw = gdn_in_projection(*gdn_args, conv_hist=conv0)
                pad_rows = GDN_CHUNK - t
                pad3 = lambda a: jnp.pad(a.reshape(seqs, t, a.shape[-1]), ((0, 0), (0, pad_rows), (0, 0)))
                o_seq, s1 = gdn_core(pad3(qkvz[0]), pad3(bg[0]), states['gdn'][i].astype(F32),
                                     P['gdn_out_norm'][i], GDN_CHUNK)
                o = o_seq[:, :t].reshape(1, l, GDN_VAL_DIM)
                raw_rows = raw.reshape(seqs, t, GDN_CONV_DIM)
                new.setdefault('gdn_conv', []).append(jnp.concatenate([conv0, raw_rows], axis=1)[:, -keep:])
            else:
                qkvz, bg, raw = gdn_in_projection(*gdn_args)
                s0 = jnp.zeros((bsz, GDN_V_HEADS, GDN_DK, GDN_DV), F32)
                o, s1 = gdn_core(qkvz, bg, s0, P['gdn_out_norm'][i], _tile_rows(l, GDN_ROW_TILE_CAP))
                new.setdefault('gdn_conv', []).append(raw[:, -1, SUBLANES - keep:, :])
            new.setdefault('gdn', []).append(s1)
            x = matmul_gate_residual(o, P['w_gdn_out'], i, x, g_m, tm, 512)
        if sample:
            seqs = states['ffn_conv'].shape[1]
            act, f1 = ffn_up_sample(x[0], P['norm_ffn'][layer], sh_f[0], sc_f[0], P['w_ffn_up'],
                                    P['ffn_conv_w'], P['ffn_conv_b'], layer,
                                    states['ffn_conv'][layer].astype(F32), l // seqs)
            act = act[None]
        else:
            act, f1 = ffn_up_prompt(x, P['norm_ffn'][layer], sh_f, sc_f, P['w_ffn_up'],
                                    P['ffn_conv_w'], P['ffn_conv_b'], layer, tm)
        new.setdefault('ffn_conv', []).append(f1)
        x = matmul_gate_residual(act, P['w_ffn_down'], layer, x, g_f, tm, 256)
    return x, {k: jnp.stack(v) for k, v in new.items()}


def kernel(x_prompt, x_sample, c_prompt, c_sample, cache_win_k, cache_win_v, state_gdn, state_gdn_conv,
           state_ffn_conv, rel_bias_table, w_ada, b_ada, norm_mix, norm_ffn, w_attn_qkv, attn_q_norm,
           attn_k_norm, attn_sinks, w_attn_o, w_gdn_in, gdn_conv_w, gdn_a_log, gdn_dt_bias, gdn_out_norm,
           w_gdn_out, w_ffn_up, ffn_conv_w, ffn_conv_b, w_ffn_down):
    P = dict(rel_bias_table=rel_bias_table, norm_mix=norm_mix, norm_ffn=norm_ffn, w_attn_qkv=w_attn_qkv,
             attn_q_norm=attn_q_norm, attn_k_norm=attn_k_norm, attn_sinks=attn_sinks, w_attn_o=w_attn_o,
             w_gdn_in=w_gdn_in, gdn_conv_w=gdn_conv_w, gdn_a_log=gdn_a_log, gdn_dt_bias=gdn_dt_bias,
             gdn_out_norm=gdn_out_norm, w_gdn_out=w_gdn_out, w_ffn_up=w_ffn_up, ffn_conv_w=ffn_conv_w,
             ffn_conv_b=ffn_conv_b, w_ffn_down=w_ffn_down)
    bp = x_prompt.shape[0]
    bs, ts, d = x_sample.shape
    n_c = bp + bs
    c_rows = -(-n_c // SUBLANES) * SUBLANES
    c_all = jnp.pad(jnp.concatenate([c_prompt, c_sample], axis=0), ((0, c_rows - n_c), (0, 0)))
    mod = ada_modulation(c_all, w_ada, b_ada)
    mod_prompt = mod[:, :bp]
    mod_sample = jnp.repeat(mod[:, bp:n_c], ts, axis=1)

    y_p, new_p = _trunk(x_prompt, mod_prompt, None, P, sample=False)
    states = dict(win_k=cache_win_k, win_v=cache_win_v, gdn=state_gdn, gdn_conv=state_gdn_conv,
                  ffn_conv=state_ffn_conv)
    y_s, new_s = _trunk(x_sample.reshape(1, bs * ts, d), mod_sample, states, P, sample=True)
    y_s = y_s.reshape(bs, ts, d)
    return (y_p, y_s, new_p['win_k'], new_p['win_v'], new_s['win_k'], new_s['win_v'],
            new_p['gdn'], new_s['gdn'], new_p['gdn_conv'], new_s['gdn_conv'],
            new_p['ffn_conv'], new_s['ffn_conv'])
```

```python
import functools
import math

import numpy as np
import jax
import jax.numpy as jnp
from jax import lax
from jax.experimental import pallas as pl
from jax.experimental.pallas import tpu as pltpu

HEAD_DIM = 64
N_HEADS = 32
N_KV_HEADS = 4
ATT_GROUP = N_HEADS // N_KV_HEADS
WINDOW = 128
ATT_BLOCK = 128
N_BUCKETS = 32
MAX_DISTANCE = 128
NEG_INF = -1e30
PAST_LEN = 16384

GDN_K_HEADS = 16
GDN_V_HEADS = 32
GDN_DK = 128
GDN_DV = 128
GDN_KEY_DIM = GDN_K_HEADS * GDN_DK
GDN_VAL_DIM = GDN_V_HEADS * GDN_DV
GDN_CONV_DIM = 2 * GDN_KEY_DIM + GDN_VAL_DIM
GDN_CONV = 4
GDN_CHUNK = 64
GDN_HEAD_GROUPS = 4
GDN_VH_PER_GROUP = GDN_V_HEADS // GDN_HEAD_GROUPS
GDN_KH_PER_GROUP = GDN_K_HEADS // GDN_HEAD_GROUPS
GDN_PACK = 4
GDN_PHASE_A_CHUNKS = 4

FFN_CONV = 3
NORM_EPS = 1e-6

LANES = 128
SUBLANES = 8
VMEM_CAP_BYTES = 60 * 1024 * 1024
VMEM_SLACK_BYTES = 8 * 1024 * 1024

ROW_TILE_CAP = 1024
FFN_SUB_ROWS = 256
GDN_ROW_TILE_CAP = 512

BF16 = jnp.bfloat16
F32 = jnp.float32


def _vmem_limit(block_bytes, scratch_bytes=0):
    est = 2 * int(block_bytes) + int(scratch_bytes) + VMEM_SLACK_BYTES
    return int(min(max(est, 16 * 1024 * 1024), VMEM_CAP_BYTES))


def _params(n_grid, block_bytes, scratch_bytes=0):
    return pltpu.CompilerParams(
        dimension_semantics=("arbitrary",) * n_grid,
        vmem_limit_bytes=_vmem_limit(block_bytes, scratch_bytes))


def _nbytes(shape, dtype):
    return int(np.prod(shape)) * jnp.dtype(dtype).itemsize


def _silu(x):
    return x * (1.0 / (1.0 + jnp.exp(-x)))


def _dot(a, b):
    return jnp.dot(a, b, preferred_element_type=F32)


def _dot_nt(a, b):
    return lax.dot_general(a, b, (((1,), (1,)), ((), ())), preferred_element_type=F32)


def _dot_tn(a, b):
    return lax.dot_general(a, b, (((0,), (0,)), ((), ())), preferred_element_type=F32)


def _norm_mod(x, gain, shift, scale):
    ms = jnp.mean(x * x, axis=-1, keepdims=True)
    y = x * lax.rsqrt(ms + NORM_EPS) * gain
    return y * (1.0 + scale) + shift


def _mod_specs(shift, tm):
    d = shift.shape[-1]
    if shift.shape[1] == 1:
        return pl.BlockSpec((1, 1, d), lambda b, i, j: (b, 0, 0))
    return pl.BlockSpec((1, tm, d), lambda b, i, j: (b, i, 0))


def _ada_kernel(c_ref, w_ref, b_ref, o_ref):
    a = _silu(c_ref[...]).astype(BF16)
    o_ref[0] = _dot(a, w_ref[0].astype(BF16)) + b_ref[0]


def ada_modulation(c_all, w_ada, b_ada, tn=1024):
    rows, d = c_all.shape
    depth, _, n = w_ada.shape
    blocks = _nbytes((rows, d), F32) + _nbytes((d, tn), F32) + _nbytes((rows, tn), F32)
    return pl.pallas_call(
        _ada_kernel,
        grid=(depth, n // tn),
        in_specs=[pl.BlockSpec((rows, d), lambda l, j: (0, 0)),
                  pl.BlockSpec((1, d, tn), lambda l, j: (l, 0, j)),
                  pl.BlockSpec((1, 1, tn), lambda l, j: (l, 0, j))],
        out_specs=pl.BlockSpec((1, rows, tn), lambda l, j: (l, 0, j)),
        out_shape=jax.ShapeDtypeStruct((depth, rows, n), F32),
        compiler_params=_params(2, blocks, _nbytes((d, tn), BF16)),
        name="ada_modulation",
    )(c_all, w_ada, b_ada.reshape(depth, 1, n))


def _qkv_kernel(x_ref, gain_ref, shift_ref, scale_ref, w_ref, hgain_ref, hflag_ref, gmat_ref,
                o_ref, h_s):
    @pl.when(pl.program_id(2) == 0)
    def _():
        h_s[...] = _norm_mod(x_ref[0], gain_ref[...], shift_ref[0], scale_ref[0]).astype(BF16)

    y = _dot(h_s[...], w_ref[...].astype(BF16))
    sq = y * y
    sq_hi = sq.astype(BF16)
    sq_lo = (sq - sq_hi.astype(F32)).astype(BF16)
    ms = _dot(sq_hi, gmat_ref[...]) + _dot(sq_lo, gmat_ref[...])
    yn = y * lax.rsqrt(ms + NORM_EPS) * hgain_ref[...]
    o_ref[0] = jnp.where(hflag_ref[...] > 0.0, yn, y)


def qkv_projection(x, gain, shift, scale, w, q_gain, k_gain, tm, tn=512):
    b, l, d = x.shape
    n = w.shape[1]
    nq, nk = N_HEADS * HEAD_DIM, N_KV_HEADS * HEAD_DIM
    hgain = jnp.concatenate([jnp.tile(q_gain, N_HEADS), jnp.tile(k_gain, N_KV_HEADS),
                             jnp.ones((nk,), F32)]).reshape(1, n)
    hflag = jnp.concatenate([jnp.ones((nq + nk,), F32), jnp.zeros((nk,), F32)]).reshape(1, n)
    gidx = np.arange(tn) // HEAD_DIM
    gmat = jnp.asarray((gidx[:, None] == gidx[None, :]).astype(np.float32) / HEAD_DIM, BF16)
    blocks = (_nbytes((tm, d), F32) + _nbytes((d, tn), F32) + _nbytes((tm, tn), F32)
              + 2 * _nbytes((shift.shape[1] == 1 and 1 or tm, d), F32) + _nbytes((tn, tn), BF16))
    scratch = _nbytes((tm, d), BF16) + _nbytes((d, tn), BF16) + 4 * _nbytes((tm, tn), F32)
    return pl.pallas_call(
        _qkv_kernel,
        grid=(b, l // tm, n // tn),
        in_specs=[pl.BlockSpec((1, tm, d), lambda b_, i, j: (b_, i, 0)),
                  pl.BlockSpec((1, d), lambda b_, i, j: (0, 0)),
                  _mod_specs(shift, tm), _mod_specs(scale, tm),
                  pl.BlockSpec((d, tn), lambda b_, i, j: (0, j)),
                  pl.BlockSpec((1, tn), lambda b_, i, j: (0, j)),
                  pl.BlockSpec((1, tn), lambda b_, i, j: (0, j)),
                  pl.BlockSpec((tn, tn), lambda b_, i, j: (0, 0))],
        out_specs=pl.BlockSpec((1, tm, tn), lambda b_, i, j: (b_, i, j)),
        out_shape=jax.ShapeDtypeStruct((b, l, n), F32),
        scratch_shapes=[pltpu.VMEM((tm, d), BF16)],
        compiler_params=_params(3, blocks, scratch),
        name="qkv_projection",
    )(x, gain.reshape(1, d), shift, scale, w, hgain, hflag, gmat)


def _attn_prompt_kernel(sink_ref, q_ref, cur_ref, prev_ref, bprev_ref, bcur_ref, o_ref, kz_s, vz_s):
    n = pl.program_id(1)
    blk = ATT_BLOCK
    lane = lax.broadcasted_iota(jnp.int32, (2 * blk, LANES), 1)
    lo_half = lane < HEAD_DIM
    neg_prev = jnp.where(n == 0, NEG_INF, 0.0).astype(F32)

    kv_width = N_KV_HEADS * HEAD_DIM
    for pair in range(N_KV_HEADS // 2):
        for part, dst in ((0, kz_s), (1, vz_s)):
            col = part * kv_width + pair * LANES
            both = jnp.concatenate([prev_ref[0, :, col:col + LANES], cur_ref[0, :, col:col + LANES]], axis=0)
            swapped = pltpu.roll(both, HEAD_DIM, axis=1)
            zero = jnp.zeros_like(both)
            c0, c1 = 2 * pair, 2 * pair + 1
            dst[2 * c0 + 0] = jnp.where(lo_half, both, zero).astype(BF16)
            dst[2 * c0 + 1] = jnp.where(lo_half, zero, swapped).astype(BF16)
            dst[2 * c1 + 0] = jnp.where(lo_half, swapped, zero).astype(BF16)
            dst[2 * c1 + 1] = jnp.where(lo_half, zero, both).astype(BF16)

    scale = HEAD_DIM ** -0.5
    for p in range(N_HEADS // 2):
        c = (2 * p) // ATT_GROUP
        qp = (q_ref[0, :, p * LANES:(p + 1) * LANES] * scale).astype(BF16)
        o_pair = None
        for a in range(2):
            h = 2 * p + a
            sink = sink_ref[h]
            s = _dot_nt(qp, kz_s[2 * c + a])
            s_prev = s[:, :blk] + bprev_ref[h] + neg_prev
            s_cur = s[:, blk:] + bcur_ref[h]
            m = jnp.maximum(jnp.max(jnp.maximum(s_prev, s_cur), axis=-1, keepdims=True), sink)
            e_prev = jnp.exp(s_prev - m)
            e_cur = jnp.exp(s_cur - m)
            den = jnp.sum(e_prev + e_cur, axis=-1, keepdims=True) + jnp.exp(sink - m)
            pm = jnp.concatenate([e_prev, e_cur], axis=1).astype(BF16)
            o_a = _dot(pm, vz_s[2 * c + a]) * (1.0 / den)
            o_pair = o_a if o_pair is None else o_pair + o_a
        o_ref[0, :, p * LANES:(p + 1) * LANES] = o_pair.astype(o_ref.dtype)


def _t5_bucket_np(dist):
    max_exact = N_BUCKETS // 2
    d = np.maximum(dist, 0)
    df = np.maximum(d, 1).astype(np.float32)
    large = max_exact + (np.log(df / np.float32(max_exact)) / np.float32(math.log(MAX_DISTANCE / max_exact))
                         * np.float32(N_BUCKETS - max_exact)).astype(np.int32)
    large = np.minimum(large, N_BUCKETS - 1)
    return np.where(d < max_exact, d, large)


def _bias_from_dist(dist, in_band, rel_table):
    onehot = (_t5_bucket_np(dist)[..., None] == np.arange(N_BUCKETS)).astype(np.float32)
    tab = jnp.einsum('qsb,bh->hqs', jnp.asarray(onehot), rel_table.astype(F32), precision=lax.Precision.HIGHEST)
    return jnp.where(jnp.asarray(in_band)[None], tab, NEG_INF)


def attention_prompt(qkv, sinks, rel_table):
    b, l, n = qkv.shape
    blk = ATT_BLOCK
    nq = N_HEADS * HEAD_DIM
    kvw = 2 * N_KV_HEADS * HEAD_DIM
    kv_blk = nq // kvw
    qi = np.arange(blk)[:, None]
    sj = np.arange(blk)[None, :]
    d_prev = qi + blk - sj
    d_cur = qi - sj
    bias_prev = _bias_from_dist(d_prev, (d_prev >= 0) & (d_prev <= WINDOW), rel_table)
    bias_cur = _bias_from_dist(d_cur, (d_cur >= 0) & (d_cur <= WINDOW), rel_table)
    blocks = (_nbytes((blk, nq), F32) + 2 * _nbytes((blk, kvw), F32) + _nbytes((blk, nq), BF16))
    scratch = 2 * _nbytes((2 * N_KV_HEADS, 2 * blk, LANES), BF16) + 4 * _nbytes((N_HEADS, blk, blk), F32)
    return pl.pallas_call(
        _attn_prompt_kernel,
        grid=(b, l // blk),
        in_specs=[pl.BlockSpec(memory_space=pltpu.SMEM),
                  pl.BlockSpec((1, blk, nq), lambda b_, i: (b_, i, 0)),
                  pl.BlockSpec((1, blk, kvw), lambda b_, i: (b_, i, kv_blk)),
                  pl.BlockSpec((1, blk, kvw), lambda b_, i: (b_, jnp.maximum(i - 1, 0), kv_blk)),
                  pl.BlockSpec((N_HEADS, blk, blk), lambda b_, i: (0, 0, 0)),
                  pl.BlockSpec((N_HEADS, blk, blk), lambda b_, i: (0, 0, 0))],
        out_specs=pl.BlockSpec((1, blk, nq), lambda b_, i: (b_, i, 0)),
        out_shape=jax.ShapeDtypeStruct((b, l, nq), BF16),
        scratch_shapes=[pltpu.VMEM((2 * N_KV_HEADS, 2 * blk, LANES), BF16),
                        pltpu.VMEM((2 * N_KV_HEADS, 2 * blk, LANES), BF16)],
        compiler_params=_params(2, blocks, scratch),
        name="attention_prompt",
    )(sinks.astype(F32), qkv, qkv, qkv, bias_prev, bias_cur)


def _attn_sample_kernel(q_ref, k_ref, v_ref, bias_ref, sink_ref, o_ref):
    k = k_ref[0].astype(BF16)
    v = v_ref[0].astype(BF16)
    for c in range(N_KV_HEADS):
        s = _dot_nt(q_ref[0, c].astype(BF16), k) + bias_ref[c]
        sink = sink_ref[c]
        m = jnp.maximum(jnp.max(s, axis=-1, keepdims=True), sink)
        e = jnp.exp(s - m)
        den = jnp.sum(e, axis=-1, keepdims=True) + jnp.exp(sink - m)
        o_ref[0, c] = _dot(e.astype(BF16), v) * (1.0 / den)


def attention_sample(qkv, cache_k, cache_v, sinks, rel_table):
    bsz, w = cache_k.shape[0], cache_k.shape[1]
    t = qkv.shape[0] // bsz
    nq, nkv = N_HEADS * HEAD_DIM, N_KV_HEADS * HEAD_DIM
    keys = w + t
    keys_pad = -(-keys // 16) * 16
    rows = ATT_GROUP * t
    q = qkv[:, :nq].reshape(t, bsz, N_KV_HEADS, ATT_GROUP, HEAD_DIM) * HEAD_DIM ** -0.5
    q = jnp.transpose(q, (1, 2, 3, 0, 4)).reshape(bsz, N_KV_HEADS, rows, HEAD_DIM)
    qz = jnp.einsum('bcrd,ck->bcrkd', q, jnp.eye(N_KV_HEADS, dtype=F32)).reshape(bsz, N_KV_HEADS, rows, nkv)
    new_kv = jnp.transpose(qkv[:, nq:].reshape(t, bsz, 2 * nkv), (1, 0, 2))
    k_all = jnp.concatenate([cache_k.reshape(bsz, w, nkv), new_kv[:, :, :nkv]], axis=1)
    v_all = jnp.concatenate([cache_v.reshape(bsz, w, nkv), new_kv[:, :, nkv:]], axis=1)
    pad = ((0, 0), (0, keys_pad - keys), (0, 0))
    k_pad, v_pad = jnp.pad(k_all, pad), jnp.pad(v_all, pad)
    tq = np.arange(t)[:, None]
    sk = np.arange(keys_pad)[None, :]
    dist = tq + w - sk
    in_band = (dist >= 0) & (dist <= WINDOW) & (sk < keys)
    bias_t = _bias_from_dist(dist, in_band, rel_table)
    bias = bias_t.reshape(N_KV_HEADS, rows, keys_pad)
    sink_rows = jnp.repeat(sinks.astype(F32), t).reshape(N_KV_HEADS, rows, 1)
    blocks = (_nbytes((N_KV_HEADS, rows, nkv), F32) * 2 + 2 * _nbytes((keys_pad, nkv), F32))
    oz = pl.pallas_call(
        _attn_sample_kernel,
        grid=(bsz,),
        in_specs=[pl.BlockSpec((1, N_KV_HEADS, rows, nkv), lambda b_: (b_, 0, 0, 0)),
                  pl.BlockSpec((1, keys_pad, nkv), lambda b_: (b_, 0, 0)),
                  pl.BlockSpec((1, keys_pad, nkv), lambda b_: (b_, 0, 0)),
                  pl.BlockSpec((N_KV_HEADS, rows, keys_pad), lambda b_: (0, 0, 0)),
                  pl.BlockSpec((N_KV_HEADS, rows, 1), lambda b_: (0, 0, 0))],
        out_specs=pl.BlockSpec((1, N_KV_HEADS, rows, nkv), lambda b_: (b_, 0, 0, 0)),
        out_shape=jax.ShapeDtypeStruct((bsz, N_KV_HEADS, rows, nkv), F32),
        compiler_params=_params(1, blocks),
        name="attention_sample",
    )(qz, k_pad, v_pad, bias, sink_rows)
    o = jnp.stack([oz[:, c, :, c * HEAD_DIM:(c + 1) * HEAD_DIM] for c in range(N_KV_HEADS)], axis=1)
    o = o.reshape(bsz, N_KV_HEADS, ATT_GROUP, t, HEAD_DIM)
    o = jnp.transpose(o, (3, 0, 1, 2, 4)).reshape(t * bsz, nq)
    return o.astype(BF16), k_all[:, t:], v_all[:, t:]


def _mm_res_kernel(a_ref, w_ref, x_ref, g_ref, o_ref):
    y = _dot(a_ref[0], w_ref[...].astype(BF16))
    o_ref[0] = x_ref[0] + g_ref[0] * y


def matmul_gate_residual(a, w_stack, layer, x, gate, tm, tn):
    b, l, k = a.shape
    n = w_stack.shape[2]
    grows = 1 if gate.shape[1] == 1 else tm
    if gate.shape[1] == 1:
        g_spec = pl.BlockSpec((1, 1, tn), lambda b_, i, j: (b_, 0, j))
    else:
        g_spec = pl.BlockSpec((1, tm, tn), lambda b_, i, j: (b_, i, j))
    blocks = (_nbytes((tm, k), a.dtype) + _nbytes((k, tn), F32) + 2 * _nbytes((tm, tn), F32)
              + _nbytes((grows, tn), F32))
    scratch = _nbytes((k, tn), BF16) + _nbytes((tm, tn), F32)
    return pl.pallas_call(
        _mm_res_kernel,
        grid=(b, l // tm, n // tn),
        in_specs=[pl.BlockSpec((1, tm, k), lambda b_, i, j: (b_, i, 0)),
                  pl.BlockSpec((None, k, tn), lambda b_, i, j: (layer, 0, j)),
                  pl.BlockSpec((1, tm, tn), lambda b_, i, j: (b_, i, j)),
                  g_spec],
        out_specs=pl.BlockSpec((1, tm, tn), lambda b_, i, j: (b_, i, j)),
        out_shape=jax.ShapeDtypeStruct((b, l, n), F32),
        compiler_params=_params(3, blocks, scratch),
        name="matmul_gate_residual",
    )(a, w_stack, x, gate)


def _ffn_conv_rows(u, prev1, prev2, cw_ref, cb_ref):
    return u * cw_ref[2:3, :] + prev1 * cw_ref[1:2, :] + prev2 * cw_ref[0:1, :] + cb_ref[...]


def _ffn_up_prompt_kernel(x_ref, gain_ref, shift_ref, scale_ref, wg_ref, wv_ref, cwg_ref, cwv_ref,
                          cbg_ref, cbv_ref, o_ref, tail_ref, h_s, carry_s):
    i = pl.program_id(1)
    j = pl.program_id(2)
    tm = h_s.shape[0]

    @pl.when(j == 0)
    def _():
        h_s[...] = _norm_mod(x_ref[0], gain_ref[...], shift_ref[0], scale_ref[0]).astype(BF16)

    @pl.when(i == 0)
    def _():
        carry_s[j] = jnp.zeros(carry_s.shape[1:], F32)

    halves = ((wg_ref, cwg_ref, cbg_ref), (wv_ref, cwv_ref, cbv_ref))
    w_bf = [w_ref[...].astype(BF16) for w_ref, _, _ in halves]
    prev_tail = [carry_s[j, half] for half in range(2)]
    row8 = lax.broadcasted_iota(jnp.int32, (SUBLANES, 1), 0)
    sub = min(FFN_SUB_ROWS, tm)
    for m in range(tm // sub):
        hm = h_s[m * sub:(m + 1) * sub, :]
        ys = []
        for half, (_, cw_ref, cb_ref) in enumerate(halves):
            u = _dot(hm, w_bf[half])
            y = _ffn_conv_rows(u, pltpu.roll(u, 1, axis=0), pltpu.roll(u, 2, axis=0), cw_ref, cb_ref)
            top, hist = u[0:SUBLANES], prev_tail[half]
            p1 = jnp.where(row8 < 1, pltpu.roll(hist, 1, axis=0), pltpu.roll(top, 1, axis=0))
            p2 = jnp.where(row8 < 2, pltpu.roll(hist, 2, axis=0), pltpu.roll(top, 2, axis=0))
            y_top = _ffn_conv_rows(top, p1, p2, cw_ref, cb_ref)
            ys.append(jnp.concatenate([y_top, y[SUBLANES:]], axis=0))
            prev_tail[half] = u[sub - SUBLANES:sub]
        o_ref[0, m * sub:(m + 1) * sub, :] = (_silu(ys[0]) * ys[1]).astype(o_ref.dtype)
    for half in range(2):
        carry_s[j, half] = prev_tail[half]
        tail_ref[0, 0, half] = prev_tail[half]


def ffn_up_prompt(x, gain, shift, scale, w_up, conv_w, conv_b, layer, tm, tn=512):
    b, l, d = x.shape
    dff = w_up.shape[2] // 2
    nj = dff // tn
    conv_b = conv_b.reshape(conv_b.shape[0], 1, 2 * dff)
    blocks = (_nbytes((tm, d), F32) + 2 * _nbytes((d, tn), F32) + _nbytes((tm, tn), BF16)
              + 2 * _nbytes((1, d), F32) + 8 * _nbytes((SUBLANES, tn), F32))
    scratch = (_nbytes((tm, d), BF16) + _nbytes((nj, 2, SUBLANES, tn), F32) + 2 * _nbytes((d, tn), BF16)
               + 8 * _nbytes((tm, tn), F32))
    act, tail = pl.pallas_call(
        _ffn_up_prompt_kernel,
        grid=(b, l // tm, nj),
        in_specs=[pl.BlockSpec((1, tm, d), lambda b_, i, j: (b_, i, 0)),
                  pl.BlockSpec((1, d), lambda b_, i, j: (0, 0)),
                  _mod_specs(shift, tm), _mod_specs(scale, tm),
                  pl.BlockSpec((None, d, tn), lambda b_, i, j: (layer, 0, j)),
                  pl.BlockSpec((None, d, tn), lambda b_, i, j: (layer, 0, j + nj)),
                  pl.BlockSpec((None, FFN_CONV, tn), lambda b_, i, j: (layer, 0, j)),
                  pl.BlockSpec((None, FFN_CONV, tn), lambda b_, i, j: (layer, 0, j + nj)),
                  pl.BlockSpec((None, 1, tn), lambda b_, i, j: (layer, 0, j)),
                  pl.BlockSpec((None, 1, tn), lambda b_, i, j: (layer, 0, j + nj))],
        out_specs=[pl.BlockSpec((1, tm, tn), lambda b_, i, j: (b_, i, j)),
                   pl.BlockSpec((1, 1, 2, SUBLANES, tn), lambda b_, i, j: (b_, i, 0, 0, j))],
        out_shape=[jax.ShapeDtypeStruct((b, l, dff), BF16),
                   jax.ShapeDtypeStruct((b, l // tm, 2, SUBLANES, dff), F32)],
        scratch_shapes=[pltpu.VMEM((tm, d), BF16), pltpu.VMEM((nj, 2, SUBLANES, tn), F32)],
        compiler_params=_params(3, blocks, scratch),
        name="ffn_up_prompt",
    )(x, gain.reshape(1, d), shift, scale, w_up, w_up, conv_w, conv_w, conv_b, conv_b)
    keep = FFN_CONV - 1
    hist = jnp.transpose(tail[:, -1, :, SUBLANES - keep:, :], (0, 2, 1, 3)).reshape(b, keep, 2 * dff)
    return act, hist


def _ffn_up_sample_kernel(x_ref, gain_ref, shift_ref, scale_ref, wg_ref, wv_ref, cwg_ref, cwv_ref,
                          cbg_ref, cbv_ref, hg_ref, hv_ref, o_ref, ng_ref, nv_ref, h_s, *, bsz):
    @pl.when(pl.program_id(0) == 0)
    def _():
        h_s[...] = _norm_mod(x_ref[...], gain_ref[...], shift_ref[...], scale_ref[...]).astype(BF16)

    h = h_s[...]
    rows = h_s.shape[0]
    keep = FFN_CONV - 1
    ys = []
    for w_ref, cw_ref, cb_ref, hist_ref, new_ref in ((wg_ref, cwg_ref, cbg_ref, hg_ref, ng_ref),
                                                     (wv_ref, cwv_ref, cbv_ref, hv_ref, nv_ref)):
        u = _dot(h, w_ref[...].astype(BF16))
        ext = jnp.concatenate([hist_ref[...], u], axis=0)
        prev = [ext[(keep - k) * bsz:(keep - k) * bsz + rows] for k in (1, 2)]
        ys.append(_ffn_conv_rows(u, prev[0], prev[1], cw_ref, cb_ref))
        new_ref[...] = ext[rows:rows + keep * bsz]
    o_ref[...] = (_silu(ys[0]) * ys[1]).astype(o_ref.dtype)


def ffn_up_sample(x, gain, shift, scale, w_up, conv_w, conv_b, layer, hist, tn=512):
    rows, d = x.shape
    dff = w_up.shape[2] // 2
    nj = dff // tn
    keep = FFN_CONV - 1
    bsz = hist.shape[0] // keep
    conv_b = conv_b.reshape(conv_b.shape[0], 1, 2 * dff)
    col = lambda j: (0, j)
    col_hi = lambda j: (0, j + nj)
    lcol = lambda j: (layer, 0, j)
    lcol_hi = lambda j: (layer, 0, j + nj)
    blocks = (_nbytes((rows, d), F32) * 3 + 2 * _nbytes((d, tn), F32) + 5 * _nbytes((rows, tn), F32))
    scratch = _nbytes((rows, d), BF16) + 2 * _nbytes((d, tn), BF16) + 8 * _nbytes((rows, tn), F32)
    act, new_g, new_v = pl.pallas_call(
        functools.partial(_ffn_up_sample_kernel, bsz=bsz),
        grid=(nj,),
        in_specs=[pl.BlockSpec((rows, d), lambda j: (0, 0)),
                  pl.BlockSpec((1, d), lambda j: (0, 0)),
                  pl.BlockSpec((rows, d), lambda j: (0, 0)),
                  pl.BlockSpec((rows, d), lambda j: (0, 0)),
                  pl.BlockSpec((None, d, tn), lcol), pl.BlockSpec((None, d, tn), lcol_hi),
                  pl.BlockSpec((None, FFN_CONV, tn), lcol), pl.BlockSpec((None, FFN_CONV, tn), lcol_hi),
                  pl.BlockSpec((None, 1, tn), lcol), pl.BlockSpec((None, 1, tn), lcol_hi),
                  pl.BlockSpec((keep * bsz, tn), col), pl.BlockSpec((keep * bsz, tn), col_hi)],
        out_specs=[pl.BlockSpec((rows, tn), col), pl.BlockSpec((keep * bsz, tn), col),
                   pl.BlockSpec((keep * bsz, tn), col)],
        out_shape=[jax.ShapeDtypeStruct((rows, dff), BF16),
                   jax.ShapeDtypeStruct((keep * bsz, dff), F32),
                   jax.ShapeDtypeStruct((keep * bsz, dff), F32)],
        scratch_shapes=[pltpu.VMEM((rows, d), BF16)],
        compiler_params=_params(1, blocks, scratch),
        name="ffn_up_sample",
    )(x, gain.reshape(1, d), shift, scale, w_up, w_up, conv_w, conv_w, conv_b, conv_b, hist, hist)
    return act, jnp.concatenate([new_g, new_v], axis=-1)


def _gdn_in_kernel(*refs, n_qk_tiles, n_conv_tiles, bsz):
    if bsz is None:
        (x_ref, gain_ref, shift_ref, scale_ref, w_ref, wba_ref, alog_ref, dtb_ref, cw_ref,
         o_ref, bg_ref, raw_ref, h_s, carry_s) = refs
    else:
        (x_ref, gain_ref, shift_ref, scale_ref, w_ref, wba_ref, alog_ref, dtb_ref, cw_ref, hist_ref,
         o_ref, bg_ref, raw_ref, h_s) = refs
    i = pl.program_id(1)
    j = pl.program_id(2)
    tm = h_s.shape[0]
    width = GDN_CONV

    @pl.when(j == 0)
    def _():
        h = _norm_mod(x_ref[0], gain_ref[...], shift_ref[0], scale_ref[0]).astype(BF16)
        h_s[...] = h
        y = _dot_nt(h, wba_ref[...].astype(BF16))
        lane = lax.broadcasted_iota(jnp.int32, y.shape, 1) % LANES
        beta = 1.0 / (1.0 + jnp.exp(-y))
        a = y + dtb_ref[...]
        softplus = jnp.maximum(a, 0.0) + jnp.log1p(jnp.exp(-jnp.abs(a)))
        g = -jnp.exp(alog_ref[...]) * softplus
        bg_ref[0] = jnp.where(lane < GDN_VH_PER_GROUP, beta, jnp.where(lane < 2 * GDN_VH_PER_GROUP, g, 0.0))

    sub = min(FFN_SUB_ROWS, tm)
    row8 = lax.broadcasted_iota(jnp.int32, (SUBLANES, 1), 0)

    def conv_tile(l2_scale):
        w_bf = w_ref[...].astype(BF16)
        jc = jnp.minimum(j, n_conv_tiles - 1)
        if bsz is None:
            @pl.when(i == 0)
            def _():
                carry_s[jc] = jnp.zeros(carry_s.shape[1:], F32)
            prev_tail = carry_s[jc]
        for m in range(tm // sub):
            rs = slice(m * sub, (m + 1) * sub)
            u = _dot_nt(h_s[rs, :], w_bf)
            y = u * cw_ref[width - 1:width, :]
            if bsz is None:
                top = u[0:SUBLANES]
                y_top = top * cw_ref[width - 1:width, :]
                for k in range(1, width):
                    wk = cw_ref[width - 1 - k:width - k, :]
                    y = y + pltpu.roll(u, k, axis=0) * wk
                    y_top = y_top + jnp.where(row8 < k, pltpu.roll(prev_tail, k, axis=0),
                                              pltpu.roll(top, k, axis=0)) * wk
                y = jnp.concatenate([y_top, y[SUBLANES:]], axis=0)
                prev_tail = u[sub - SUBLANES:sub]
            else:
                ext = jnp.concatenate([hist_ref[...], u], axis=0)
                for k in range(1, width):
                    wk = cw_ref[width - 1 - k:width - k, :]
                    y = y + ext[(width - 1 - k) * bsz:(width - 1 - k) * bsz + sub] * wk
                raw_ref[...] = ext[sub:sub + (width - 1) * bsz]
            y = _silu(y)
            if l2_scale is not None:
                heads = [y[:, hd * LANES:(hd + 1) * LANES] for hd in range(y.shape[1] // LANES)]
                y = jnp.concatenate(
                    [yh * (lax.rsqrt(jnp.sum(yh * yh, axis=-1, keepdims=True) + NORM_EPS) * l2_scale)
                     for yh in heads], axis=1)
            o_ref[0, rs, :] = y.astype(o_ref.dtype)
        if bsz is None:
            carry_s[jc] = prev_tail
            raw_ref[0, 0] = prev_tail

    @pl.when(j < n_qk_tiles)
    def _():
        conv_tile(jnp.where(j < n_qk_tiles // 2, GDN_DK ** -0.5, 1.0).astype(F32))

    @pl.when((j >= n_qk_tiles) & (j < n_conv_tiles))
    def _():
        conv_tile(None)

    @pl.when(j >= n_conv_tiles)
    def _():
        w_bf = w_ref[...].astype(BF16)
        for m in range(tm // sub):
            rs = slice(m * sub, (m + 1) * sub)
            o_ref[0, rs, :] = _silu(_dot_nt(h_s[rs, :], w_bf)).astype(o_ref.dtype)


def _group_lane_layout(vec_b, vec_a):
    lead = vec_b.shape[:-1]
    vb = vec_b.reshape(*lead, GDN_HEAD_GROUPS, GDN_VH_PER_GROUP)
    va = vec_a.reshape(*lead, GDN_HEAD_GROUPS, GDN_VH_PER_GROUP)
    pad = jnp.zeros((*lead, GDN_HEAD_GROUPS, LANES - 2 * GDN_VH_PER_GROUP), vec_b.dtype)
    return jnp.concatenate([vb, va, pad], axis=-1).reshape(*lead, GDN_HEAD_GROUPS * LANES)


def gdn_in_projection(x, gain, shift, scale, w_in, conv_w, a_log, dt_bias, tm, conv_hist=None, tn=512):
    b, l, d = x.shape
    n_main = GDN_CONV_DIM + GDN_VAL_DIM
    w_t = jnp.transpose(w_in)
    wba = _group_lane_layout(w_in[:, n_main:n_main + GDN_V_HEADS], w_in[:, n_main + GDN_V_HEADS:])
    wba_t = jnp.transpose(wba)
    zeros = jnp.zeros((1, GDN_V_HEADS), F32)
    alog = _group_lane_layout(zeros, a_log.reshape(1, -1).astype(F32))
    dtb = _group_lane_layout(zeros, dt_bias.reshape(1, -1).astype(F32))
    nbg = wba_t.shape[0]
    n_conv_tiles = GDN_CONV_DIM // tn
    n_qk_tiles = 2 * GDN_KEY_DIM // tn
    conv_col = lambda b_, i, j: (0, jnp.minimum(j, n_conv_tiles - 1))
    in_specs = [pl.BlockSpec((1, tm, d), lambda b_, i, j: (b_, i, 0)),
                pl.BlockSpec((1, d), lambda b_, i, j: (0, 0)),
                _mod_specs(shift, tm), _mod_specs(scale, tm),
                pl.BlockSpec((tn, d), lambda b_, i, j: (j, 0)),
                pl.BlockSpec((nbg, d), lambda b_, i, j: (0, 0)),
                pl.BlockSpec((1, nbg), lambda b_, i, j: (0, 0)),
                pl.BlockSpec((1, nbg), lambda b_, i, j: (0, 0)),
                pl.BlockSpec((GDN_CONV, tn), conv_col)]
    args = [x, gain.reshape(1, d), shift, scale, w_t, wba_t, alog, dtb, conv_w]
    scratch_shapes = [pltpu.VMEM((tm, d), BF16)]
    blocks = (_nbytes((tm, d), F32) + _nbytes((tn, d), F32) + _nbytes((tm, tn), BF16)
              + _nbytes((nbg, d), F32) + _nbytes((tm, nbg), F32) + 2 * _nbytes((1, d), F32))
    scratch = _nbytes((tm, d), BF16) + _nbytes((tn, d), BF16) + 6 * _nbytes((tm, nbg), F32)
    if conv_hist is None:
        bsz = None
        raw_spec = pl.BlockSpec((1, 1, SUBLANES, tn), lambda b_, i, j: (b_, i, 0, jnp.minimum(j, n_conv_tiles - 1)))
        raw_shape = jax.ShapeDtypeStruct((b, l // tm, SUBLANES, GDN_CONV_DIM), F32)
        scratch_shapes.append(pltpu.VMEM((n_conv_tiles, SUBLANES, tn), F32))
    else:
        assert b == 1 and l == tm <= FFN_SUB_ROWS
        hist_rows = conv_hist.shape[0]
        bsz = hist_rows // (GDN_CONV - 1)
        args.append(conv_hist)
        in_specs.append(pl.BlockSpec((hist_rows, tn), conv_col))
        raw_spec = pl.BlockSpec((hist_rows, tn), conv_col)
        raw_shape = jax.ShapeDtypeStruct((hist_rows, GDN_CONV_DIM), F32)
        blocks += 2 * _nbytes((hist_rows, tn), F32)
    return pl.pallas_call(
        functools.partial(_gdn_in_kernel, n_qk_tiles=n_qk_tiles, n_conv_tiles=n_conv_tiles, bsz=bsz),
        grid=(b, l // tm, n_main // tn),
        in_specs=in_specs,
        out_specs=[pl.BlockSpec((1, tm, tn), lambda b_, i, j: (b_, i, j)),
                   pl.BlockSpec((1, tm, nbg), lambda b_, i, j: (b_, i, 0)),
                   raw_spec],
        out_shape=[jax.ShapeDtypeStruct((b, l, n_main), BF16),
                   jax.ShapeDtypeStruct((b, l, nbg), F32),
                   raw_shape],
        scratch_shapes=scratch_shapes,
        compiler_params=_params(3, blocks, scratch),
        name="gdn_in_projection",
    )(*args)


def _gdn_core_kernel(q_ref, k_ref, v_ref, zg_ref, bg_ref, onorm_ref, s0_ref, o_ref, s1_ref,
                     s_s, gc_s, tdec_s, mo_s, *, seq_rows):
    r = pl.program_id(2)
    n_r = pl.num_programs(2)
    rows = q_ref.shape[1]
    chunk = GDN_CHUNK
    vpg = GDN_VH_PER_GROUP
    rep = GDN_V_HEADS // GDN_K_HEADS
    n_kh = GDN_KH_PER_GROUP
    seg = chunk if seq_rows is None else seq_rows
    n_seq = chunk // seg

    def load_states():
        for s in range(n_seq):
            for kh in range(n_kh):
                s_s[s * n_kh + kh] = jnp.concatenate([s0_ref[s, kh * rep + e] for e in range(rep)], axis=1)

    if seq_rows is None:
        pl.when(r == 0)(load_states)
    else:
        load_states()

    def seg_last(x):
        if n_seq == 1:
            return x[chunk - 1:chunk, :]
        blocks = x.reshape(n_seq, seg, x.shape[1])
        return jnp.broadcast_to(blocks[:, seg - 1:seg, :], blocks.shape).reshape(x.shape)

    n_chunks = rows // chunk
    pw = GDN_PACK * chunk
    n_packs = vpg // GDN_PACK
    ri = lax.broadcasted_iota(jnp.int32, (chunk, pw), 0)
    li = lax.broadcasted_iota(jnp.int32, (chunk, pw), 1) % chunk
    seg_shift = int(math.log2(seg))
    same_seq = (ri >> seg_shift) == (li >> seg_shift)
    tril_p = (ri >= li) & same_seq
    strict_p = (ri > li) & same_seq
    eye_p = (ri == li).astype(F32)
    n_levels = seg_shift
    off_masks = [((ri >> (lvl + 1)) == (li >> (lvl + 1))) & ((ri >> lvl) != (li >> lvl))
                 for lvl in range(n_levels)]
    bd_rows = lax.broadcasted_iota(jnp.int32, (pw, pw), 0) // chunk
    bd_cols = lax.broadcasted_iota(jnp.int32, (pw, pw), 1) // chunk
    bd_mask = (bd_rows == bd_cols).astype(BF16)
    row_c = lax.broadcasted_iota(jnp.int32, (chunk, LANES), 0)
    lane_lo = lax.broadcasted_iota(jnp.int32, (chunk, LANES), 1) < chunk

    def block_diag(xp):
        return jnp.concatenate([xp.astype(BF16)] * GDN_PACK, axis=0) * bd_mask

    def pack_cols(arr, first_col):
        tiles = []
        for t in range(pw // LANES):
            even = jnp.broadcast_to(arr[:, first_col + 2 * t:first_col + 2 * t + 1], (chunk, LANES))
            odd = jnp.broadcast_to(arr[:, first_col + 2 * t + 1:first_col + 2 * t + 2], (chunk, LANES))
            tiles.append(jnp.where(lane_lo, even, odd))
        return jnp.concatenate(tiles, axis=1)

    for c0 in range(0, n_chunks, GDN_PHASE_A_CHUNKS):
        group = [(c, p) for c in range(c0, min(c0 + GDN_PHASE_A_CHUNKS, n_chunks)) for p in range(n_packs)]
        a_list, qkd, brow, dec = {}, {}, {}, {}
        for c in range(c0, min(c0 + GDN_PHASE_A_CHUNKS, n_chunks)):
            rs = slice(c * chunk, (c + 1) * chunk)
            bg = bg_ref[0, rs, :]
            cum = bg
            shift = 1
            while shift < seg:
                cum = cum + jnp.where((row_c & (seg - 1)) >= shift, pltpu.roll(cum, shift, axis=0), 0.0)
                shift *= 2
            gc_s[rs, :] = cum
            gq = []
            for kh in range(GDN_KH_PER_GROUP):
                ksl = slice(kh * LANES, (kh + 1) * LANES)
                k = k_ref[0, rs, ksl]
                kq = jnp.concatenate([k, q_ref[0, rs, ksl]], axis=0)
                kk = jnp.concatenate([k, k], axis=0)
                gq.append(_dot_nt(kq, kk))
            for p in range(n_packs):
                khs = [(p * GDN_PACK + 2 * t) // rep for t in range(pw // LANES)]
                gram = jnp.concatenate([gq[kh][:chunk] for kh in khs], axis=1)
                qk = jnp.concatenate([gq[kh][chunk:] for kh in khs], axis=1)
                beta_p = pack_cols(bg, p * GDN_PACK)
                gcol_p = pack_cols(cum, vpg + p * GDN_PACK)
                grow_p = jnp.sum(gcol_p * eye_p, axis=0, keepdims=True)
                decay = jnp.exp(jnp.where(tril_p, gcol_p - grow_p, NEG_INF))
                a_list[(c, p)] = jnp.where(strict_p, gram * beta_p * decay, 0.0)
                qkd[(c, p)] = (qk * decay).astype(BF16)
                brow[(c, p)] = jnp.sum(beta_p * eye_p, axis=0, keepdims=True)
                dec[(c, p)] = jnp.exp(seg_last(gcol_p) - gcol_p)
        inv = {cp: eye_p - jnp.where(off_masks[0], a_list[cp], 0.0) for cp in group}
        for lvl in range(1, n_levels):
            w = {cp: _dot(jnp.where(off_masks[lvl], a_list[cp], 0.0).astype(BF16), block_diag(inv[cp]))
                 for cp in group}
            inv = {cp: inv[cp] - _dot(inv[cp].astype(BF16), block_diag(w[cp])) for cp in group}
        for cp in group:
            t_beta = inv[cp] * brow[cp]
            tdec_s[cp[0], cp[1]] = block_diag(dec[cp] * t_beta)
            mo_s[cp[0], cp[1]] = block_diag(_dot(qkd[cp], block_diag(t_beta)))

    def chunk_body(c, carry):
        rs = pl.ds(pl.multiple_of(c * chunk, chunk), chunk)
        cum = gc_s[rs, :]
        eg, g_last, ks, qs = [], [], [], []
        for kh in range(n_kh):
            ksl = slice(kh * LANES, (kh + 1) * LANES)
            kq = jnp.concatenate([k_ref[0, rs, ksl], q_ref[0, rs, ksl]], axis=0)
            per_seq = [_dot(kq, s_s[s * n_kh + kh].astype(BF16)) for s in range(n_seq)]
            kq_s = jnp.concatenate([per_seq[s][half * chunk + s * seg:half * chunk + (s + 1) * seg]
                                    for half in range(2) for s in range(n_seq)], axis=0)
            for e in range(rep):
                hv = kh * rep + e
                gcol = cum[:, vpg + hv:vpg + hv + 1]
                g_last.append([gcol[(s + 1) * seg - 1:(s + 1) * seg, :] for s in range(n_seq)])
                eg.append(jnp.exp(gcol))
                ks.append(kq_s[:chunk, e * LANES:(e + 1) * LANES] * eg[hv])
                qs.append(kq_s[chunk:, e * LANES:(e + 1) * LANES] * eg[hv])
        v_dec, o_intra = [], []
        for p in range(n_packs):
            heads = range(p * GDN_PACK, (p + 1) * GDN_PACK)
            rhs = jnp.concatenate([v_ref[0, rs, hv * LANES:(hv + 1) * LANES].astype(F32) - ks[hv]
                                   for hv in heads], axis=0).astype(BF16)
            vd = _dot(tdec_s[c, p], rhs)
            oi = _dot(mo_s[c, p], rhs)
            for j in range(GDN_PACK):
                v_dec.append(vd[j * chunk:(j + 1) * chunk])
                o_intra.append(oi[j * chunk:(j + 1) * chunk])
        for hv in range(vpg):
            vsl = slice(hv * LANES, (hv + 1) * LANES)
            o = qs[hv] + o_intra[hv]
            on = o * lax.rsqrt(jnp.mean(o * o, axis=-1, keepdims=True) + NORM_EPS) * onorm_ref[...]
            o_ref[0, rs, vsl] = (on * zg_ref[0, rs, vsl].astype(F32)).astype(o_ref.dtype)
        lane_v = lax.broadcasted_iota(jnp.int32, (1, rep * LANES), 1) // LANES
        row_seq = lax.broadcasted_iota(jnp.int32, (chunk, 1), 0) >> seg_shift
        for kh in range(n_kh):
            heads = range(kh * rep, (kh + 1) * rep)
            k_bf = k_ref[0, rs, kh * LANES:(kh + 1) * LANES]
            vd_pair = jnp.concatenate([v_dec[hv] for hv in heads], axis=1)
            for s in range(n_seq):
                vd_s = vd_pair if n_seq == 1 else jnp.where(row_seq == s, vd_pair, 0.0)
                ds = _dot_tn(k_bf, vd_s.astype(BF16))
                keep = jnp.exp(g_last[kh * rep][s])
                for e in range(1, rep):
                    keep = jnp.where(lane_v == e, jnp.exp(g_last[kh * rep + e][s]), keep)
                s_s[s * n_kh + kh] = s_s[s * n_kh + kh] * keep + ds
        return carry

    lax.fori_loop(0, n_chunks, chunk_body, 0)

    def store_states():
        for s in range(n_seq):
            for hv in range(vpg):
                s1_ref[s, hv] = s_s[s * n_kh + hv // rep][:, (hv % rep) * LANES:(hv % rep + 1) * LANES]

    if seq_rows is None:
        pl.when(r == n_r - 1)(store_states)
    else:
        store_states()


def gdn_core(qkvz, bg, s0, out_norm, rows, seq_rows=None):
    b, l, _ = qkvz.shape
    n_state = 1 if seq_rows is None else GDN_CHUNK // seq_rows
    if seq_rows is None:
        state_idx = lambda b_, g, r: (b_, g, 0, 0)
    else:
        assert b == 1 and rows == GDN_CHUNK
        state_idx = lambda b_, g, r: (r, g, 0, 0)
    hg = GDN_HEAD_GROUPS
    qw = GDN_KEY_DIM // hg
    vw = GDN_VAL_DIM // hg
    kq = GDN_KEY_DIM // qw
    kv = 2 * GDN_KEY_DIM // vw
    kz = GDN_CONV_DIM // vw
    vpg = GDN_VH_PER_GROUP
    blocks = (2 * _nbytes((rows, qw), BF16) + 3 * _nbytes((rows, vw), BF16) + _nbytes((rows, LANES), F32)
              + 2 * _nbytes((n_state, vpg, GDN_DK, GDN_DV), F32))
    n_chunks = rows // GDN_CHUNK
    n_packs = vpg // GDN_PACK
    pw = GDN_PACK * GDN_CHUNK
    scratch = (_nbytes((n_state, vpg, GDN_DK, GDN_DV), F32) + _nbytes((rows, LANES), F32)
               + 2 * _nbytes((n_chunks, n_packs, pw, pw), BF16) + 6 * _nbytes((rows, vw), F32))
    return pl.pallas_call(
        functools.partial(_gdn_core_kernel, seq_rows=seq_rows),
        grid=(b, hg, l // rows),
        in_specs=[pl.BlockSpec((1, rows, qw), lambda b_, g, r: (b_, r, g)),
                  pl.BlockSpec((1, rows, qw), lambda b_, g, r: (b_, r, kq + g)),
                  pl.BlockSpec((1, rows, vw), lambda b_, g, r: (b_, r, kv + g)),
                  pl.BlockSpec((1, rows, vw), lambda b_, g, r: (b_, r, kz + g)),
                  pl.BlockSpec((1, rows, LANES), lambda b_, g, r: (b_, r, g)),
                  pl.BlockSpec((1, GDN_DV), lambda b_, g, r: (0, 0)),
                  pl.BlockSpec((n_state, vpg, GDN_DK, GDN_DV), state_idx)],
        out_specs=[pl.BlockSpec((1, rows, vw), lambda b_, g, r: (b_, r, g)),
                   pl.BlockSpec((n_state, vpg, GDN_DK, GDN_DV), state_idx)],
        out_shape=[jax.ShapeDtypeStruct((b, l, GDN_VAL_DIM), BF16),
                   jax.ShapeDtypeStruct(s0.shape, F32)],
        scratch_shapes=[pltpu.VMEM((n_state * GDN_KH_PER_GROUP, GDN_DK,
                                    (GDN_V_HEADS // GDN_K_HEADS) * GDN_DV), F32),
                        pltpu.VMEM((rows, LANES), F32),
                        pltpu.VMEM((n_chunks, n_packs, pw, pw), BF16),
                        pltpu.VMEM((n_chunks, n_packs, pw, pw), BF16)],
        compiler_params=_params(3, blocks, scratch),
        name="gdn_core",
    )(qkvz, qkvz, qkvz, qkvz, bg, out_norm.reshape(1, GDN_DV).astype(F32), s0)


def _tile_rows(l, cap):
    t = min(l, cap)
    while l % t:
        t //= 2
    return t


def _to_time_major(a):
    return jnp.transpose(a, (1, 0, 2)).reshape(a.shape[0] * a.shape[1], a.shape[2])


def _from_time_major(a, bsz):
    return jnp.transpose(a.reshape(a.shape[0] // bsz, bsz, a.shape[1]), (1, 0, 2))


def _trunk(x, mod, states, P, sample):
    bsz, l, d = x.shape
    tm = _tile_rows(l, ROW_TILE_CAP)
    new = {}
    depth = mod.shape[0]
    for layer in range(depth):
        if sample:
            parts = [mod[layer][None, :, k * d:(k + 1) * d] for k in range(6)]
        else:
            parts = [mod[layer][:, None, k * d:(k + 1) * d] for k in range(6)]
        sh_m, sc_m, g_m, sh_f, sc_f, g_f = parts
        i = layer // 2
        if layer % 2 == 0:
            qkv = qkv_projection(x, P['norm_mix'][layer], sh_m, sc_m, P['w_attn_qkv'][i],
                                 P['attn_q_norm'][i], P['attn_k_norm'][i], tm)
            nq, nkv = N_HEADS * HEAD_DIM, N_KV_HEADS * HEAD_DIM
            if sample:
                o, k_win, v_win = attention_sample(qkv[0], states['win_k'][i], states['win_v'][i],
                                                   P['attn_sinks'][i], P['rel_bias_table'])
                o = o[None]
                w_len = k_win.shape[1]
                new.setdefault('win_k', []).append(k_win.reshape(-1, w_len, N_KV_HEADS, HEAD_DIM))
                new.setdefault('win_v', []).append(v_win.reshape(-1, w_len, N_KV_HEADS, HEAD_DIM))
            else:
                o = attention_prompt(qkv, P['attn_sinks'][i], P['rel_bias_table'])
                keep = min(WINDOW, PAST_LEN)
                new.setdefault('win_k', []).append(
                    qkv[:, l - keep:, nq:nq + nkv].reshape(bsz, keep, N_KV_HEADS, HEAD_DIM))
                new.setdefault('win_v', []).append(
                    qkv[:, l - keep:, nq + nkv:].reshape(bsz, keep, N_KV_HEADS, HEAD_DIM))
            x = matmul_gate_residual(o, P['w_attn_o'], i, x, g_m, tm, 512)
        else:
            keep = GDN_CONV - 1
            gdn_args = (x, P['norm_mix'][layer], sh_m, sc_m, P['w_gdn_in'][i], P['gdn_conv_w'][i],
                        P['gdn_a_log'][i], P['gdn_dt_bias'][i], tm)
            if sample:
                seqs = states['gdn'].shape[1]
                t = l // seqs
                qkvz, bg, hist1 = gdn_in_projection(
                    *gdn_args, conv_hist=_to_time_major(states['gdn_conv'][i].astype(F32)))
                new.setdefault('gdn_conv', []).append(_from_time_major(hist1, seqs))
                seq_rows = -(-t // SUBLANES) * SUBLANES
                per_seq = lambda a: jnp.pad(_from_time_major(a, seqs), ((0, 0), (0, seq_rows - t), (0, 0))
                                            ).reshape(1, seqs * seq_rows, a.shape[-1])
                o_seq, s1 = gdn_core(per_seq(qkvz[0]), per_seq(bg[0]), states['gdn'][i].astype(F32),
                                     P['gdn_out_norm'][i], GDN_CHUNK, seq_rows=seq_rows)
                o = _to_time_major(o_seq.reshape(seqs, seq_rows, GDN_VAL_DIM)[:, :t])[None]
            else:
                qkvz, bg, raw = gdn_in_projection(*gdn_args)
                s0 = jnp.zeros((bsz, GDN_V_HEADS, GDN_DK, GDN_DV), F32)
                o, s1 = gdn_core(qkvz, bg, s0, P['gdn_out_norm'][i], _tile_rows(l, GDN_ROW_TILE_CAP))
                new.setdefault('gdn_conv', []).append(raw[:, -1, SUBLANES - keep:, :])
            new.setdefault('gdn', []).append(s1)
            x = matmul_gate_residual(o, P['w_gdn_out'], i, x, g_m, tm, 512)
        if sample:
            seqs = states['ffn_conv'].shape[1]
            act, f1 = ffn_up_sample(x[0], P['norm_ffn'][layer], sh_f[0], sc_f[0], P['w_ffn_up'],
                                    P['ffn_conv_w'], P['ffn_conv_b'], layer,
                                    _to_time_major(states['ffn_conv'][layer].astype(F32)))
            act, f1 = act[None], _from_time_major(f1, seqs)
        else:
            act, f1 = ffn_up_prompt(x, P['norm_ffn'][layer], sh_f, sc_f, P['w_ffn_up'],
                                    P['ffn_conv_w'], P['ffn_conv_b'], layer, tm)
        new.setdefault('ffn_conv', []).append(f1)
        x = matmul_gate_residual(act, P['w_ffn_down'], layer, x, g_f, tm, 256)
    return x, {k: jnp.stack(v) for k, v in new.items()}


def kernel(x_prompt, x_sample, c_prompt, c_sample, cache_win_k, cache_win_v, state_gdn, state_gdn_conv,
           state_ffn_conv, rel_bias_table, w_ada, b_ada, norm_mix, norm_ffn, w_attn_qkv, attn_q_norm,
           attn_k_norm, attn_sinks, w_attn_o, w_gdn_in, gdn_conv_w, gdn_a_log, gdn_dt_bias, gdn_out_norm,
           w_gdn_out, w_ffn_up, ffn_conv_w, ffn_conv_b, w_ffn_down):
    P = dict(rel_bias_table=rel_bias_table, norm_mix=norm_mix, norm_ffn=norm_ffn, w_attn_qkv=w_attn_qkv,
             attn_q_norm=attn_q_norm, attn_k_norm=attn_k_norm, attn_sinks=attn_sinks, w_attn_o=w_attn_o,
             w_gdn_in=w_gdn_in, gdn_conv_w=gdn_conv_w, gdn_a_log=gdn_a_log, gdn_dt_bias=gdn_dt_bias,
             gdn_out_norm=gdn_out_norm, w_gdn_out=w_gdn_out, w_ffn_up=w_ffn_up, ffn_conv_w=ffn_conv_w,
             ffn_conv_b=ffn_conv_b, w_ffn_down=w_ffn_down)
    bp = x_prompt.shape[0]
    bs, ts, d = x_sample.shape
    n_c = bp + bs
    c_rows = -(-n_c // SUBLANES) * SUBLANES
    c_all = jnp.pad(jnp.concatenate([c_prompt, c_sample], axis=0), ((0, c_rows - n_c), (0, 0)))
    mod = ada_modulation(c_all, w_ada, b_ada)
    mod_prompt = mod[:, :bp]
    mod_sample = jnp.tile(mod[:, bp:n_c], (1, ts, 1))

    y_p, new_p = _trunk(x_prompt, mod_prompt, None, P, sample=False)
    states = dict(win_k=cache_win_k, win_v=cache_win_v, gdn=state_gdn, gdn_conv=state_gdn_conv,
                  ffn_conv=state_ffn_conv)
    y_s, new_s = _trunk(_to_time_major(x_sample)[None], mod_sample, states, P, sample=True)
    y_s = _from_time_major(y_s[0], bs)
    return (y_p, y_s, new_p['win_k'], new_p['win_v'], new_s['win_k'], new_s['win_v'],
            new_p['gdn'], new_s['gdn'], new_p['gdn_conv'], new_s['gdn_conv'],
            new_p['ffn_conv'], new_s['ffn_conv'])
```

```python
import functools
import math

import numpy as np
import jax
import jax.numpy as jnp
from jax import lax
from jax.experimental import pallas as pl
from jax.experimental.pallas import tpu as pltpu

HEAD_DIM = 64
N_HEADS = 32
N_KV_HEADS = 4
ATT_GROUP = N_HEADS // N_KV_HEADS
WINDOW = 128
ATT_BLOCK = 128
ATT_SAMPLE_SEQS = 8
N_BUCKETS = 32
MAX_DISTANCE = 128
NEG_INF = -1e30
PAST_LEN = 16384

GDN_K_HEADS = 16
GDN_V_HEADS = 32
GDN_DK = 128
GDN_DV = 128
GDN_KEY_DIM = GDN_K_HEADS * GDN_DK
GDN_VAL_DIM = GDN_V_HEADS * GDN_DV
GDN_CONV_DIM = 2 * GDN_KEY_DIM + GDN_VAL_DIM
GDN_CONV = 4
GDN_CHUNK = 64
GDN_HEAD_GROUPS = 4
GDN_VH_PER_GROUP = GDN_V_HEADS // GDN_HEAD_GROUPS
GDN_KH_PER_GROUP = GDN_K_HEADS // GDN_HEAD_GROUPS
GDN_PACK = 4
GDN_PHASE_A_CHUNKS = 4

FFN_CONV = 3
NORM_EPS = 1e-6

LANES = 128
SUBLANES = 8
VMEM_CAP_BYTES = 60 * 1024 * 1024
VMEM_SLACK_BYTES = 8 * 1024 * 1024

ROW_TILE_CAP = 1024
WEIGHT_CAST_K_PIECE = 512
GDN_IN_SUB_ROWS = 256
FFN_SUB_ROWS = 256
GDN_ROW_TILE_CAP = 512

BF16 = jnp.bfloat16
F32 = jnp.float32


def _vmem_limit(block_bytes, scratch_bytes=0):
    est = 2 * int(block_bytes) + int(scratch_bytes) + VMEM_SLACK_BYTES
    return int(min(max(est, 16 * 1024 * 1024), VMEM_CAP_BYTES))


def _params(n_grid, block_bytes, scratch_bytes=0):
    return pltpu.CompilerParams(
        dimension_semantics=("arbitrary",) * n_grid,
        vmem_limit_bytes=_vmem_limit(block_bytes, scratch_bytes))


def _nbytes(shape, dtype):
    return int(np.prod(shape)) * jnp.dtype(dtype).itemsize


def _silu(x):
    return x * (1.0 / (1.0 + jnp.exp(-x)))


def _dot(a, b):
    return jnp.dot(a, b, preferred_element_type=F32)


def _dot_nt(a, b):
    return lax.dot_general(a, b, (((1,), (1,)), ((), ())), preferred_element_type=F32)


def _dot_tn(a, b):
    return lax.dot_general(a, b, (((0,), (0,)), ((), ())), preferred_element_type=F32)


def _dot_f32_weight(lhs, w_ref):
    k = lhs.shape[1]
    step = WEIGHT_CAST_K_PIECE if k % WEIGHT_CAST_K_PIECE == 0 else k
    acc = None
    for k0 in range(0, k, step):
        part = _dot(lhs[:, k0:k0 + step], w_ref[k0:k0 + step, :].astype(BF16))
        acc = part if acc is None else acc + part
    return acc


def _norm_mod(x, gain, shift, scale):
    ms = jnp.mean(x * x, axis=-1, keepdims=True)
    y = x * lax.rsqrt(ms + NORM_EPS) * gain
    return y * (1.0 + scale) + shift


def _mod_specs(shift, tm):
    d = shift.shape[-1]
    if shift.shape[1] == 1:
        return pl.BlockSpec((1, 1, d), lambda b, i, j: (b, 0, 0))
    return pl.BlockSpec((1, tm, d), lambda b, i, j: (b, i, 0))


def _ada_kernel(c_ref, w_ref, b_ref, o_ref):
    a = _silu(c_ref[...]).astype(BF16)
    o_ref[0] = _dot(a, w_ref[0].astype(BF16)) + b_ref[0]


def ada_modulation(c_all, w_ada, b_ada, tn=1024):
    rows, d = c_all.shape
    depth, _, n = w_ada.shape
    blocks = _nbytes((rows, d), F32) + _nbytes((d, tn), F32) + _nbytes((rows, tn), F32)
    return pl.pallas_call(
        _ada_kernel,
        grid=(depth, n // tn),
        in_specs=[pl.BlockSpec((rows, d), lambda l, j: (0, 0)),
                  pl.BlockSpec((1, d, tn), lambda l, j: (l, 0, j)),
                  pl.BlockSpec((1, 1, tn), lambda l, j: (l, 0, j))],
        out_specs=pl.BlockSpec((1, rows, tn), lambda l, j: (l, 0, j)),
        out_shape=jax.ShapeDtypeStruct((depth, rows, n), F32),
        compiler_params=_params(2, blocks, _nbytes((d, tn), BF16)),
        name="ada_modulation",
    )(c_all, w_ada, b_ada.reshape(depth, 1, n))


def _qkv_kernel(x_ref, gain_ref, shift_ref, scale_ref, w_ref, hgain_ref, hflag_ref, gmat_ref,
                o_ref, h_s):
    @pl.when(pl.program_id(2) == 0)
    def _():
        h_s[...] = _norm_mod(x_ref[0], gain_ref[...], shift_ref[0], scale_ref[0]).astype(BF16)

    y = _dot(h_s[...], w_ref[...].astype(BF16))
    ms = _dot((y * y).astype(BF16), gmat_ref[...])
    yn = y * lax.rsqrt(ms + NORM_EPS) * hgain_ref[...]
    o_ref[0] = jnp.where(hflag_ref[...] > 0.0, yn, y)


def qkv_projection(x, gain, shift, scale, w, q_gain, k_gain, tm, tn=512):
    b, l, d = x.shape
    n = w.shape[1]
    nq, nk = N_HEADS * HEAD_DIM, N_KV_HEADS * HEAD_DIM
    hgain = jnp.concatenate([jnp.tile(q_gain, N_HEADS), jnp.tile(k_gain, N_KV_HEADS),
                             jnp.ones((nk,), F32)]).reshape(1, n)
    hflag = jnp.concatenate([jnp.ones((nq + nk,), F32), jnp.zeros((nk,), F32)]).reshape(1, n)
    gidx = np.arange(tn) // HEAD_DIM
    gmat = jnp.asarray((gidx[:, None] == gidx[None, :]).astype(np.float32) / HEAD_DIM, BF16)
    blocks = (_nbytes((tm, d), F32) + _nbytes((d, tn), F32) + _nbytes((tm, tn), F32)
              + 2 * _nbytes((shift.shape[1] == 1 and 1 or tm, d), F32) + _nbytes((tn, tn), BF16))
    scratch = _nbytes((tm, d), BF16) + _nbytes((d, tn), BF16) + 4 * _nbytes((tm, tn), F32)
    return pl.pallas_call(
        _qkv_kernel,
        grid=(b, l // tm, n // tn),
        in_specs=[pl.BlockSpec((1, tm, d), lambda b_, i, j: (b_, i, 0)),
                  pl.BlockSpec((1, d), lambda b_, i, j: (0, 0)),
                  _mod_specs(shift, tm), _mod_specs(scale, tm),
                  pl.BlockSpec((d, tn), lambda b_, i, j: (0, j)),
                  pl.BlockSpec((1, tn), lambda b_, i, j: (0, j)),
                  pl.BlockSpec((1, tn), lambda b_, i, j: (0, j)),
                  pl.BlockSpec((tn, tn), lambda b_, i, j: (0, 0))],
        out_specs=pl.BlockSpec((1, tm, tn), lambda b_, i, j: (b_, i, j)),
        out_shape=jax.ShapeDtypeStruct((b, l, n), F32),
        scratch_shapes=[pltpu.VMEM((tm, d), BF16)],
        compiler_params=_params(3, blocks, scratch),
        name="qkv_projection",
    )(x, gain.reshape(1, d), shift, scale, w, hgain, hflag, gmat)


def _attn_prompt_kernel(sink_ref, q_ref, cur_ref, prev_ref, bprev_ref, bcur_ref, o_ref, kz_s, vz_s):
    n = pl.program_id(1)
    blk = ATT_BLOCK
    lane = lax.broadcasted_iota(jnp.int32, (2 * blk, LANES), 1)
    lo_half = lane < HEAD_DIM
    neg_prev = jnp.where(n == 0, NEG_INF, 0.0).astype(F32)

    kv_width = N_KV_HEADS * HEAD_DIM
    for pair in range(N_KV_HEADS // 2):
        for part, dst in ((0, kz_s), (1, vz_s)):
            col = part * kv_width + pair * LANES
            both = jnp.concatenate([prev_ref[0, :, col:col + LANES], cur_ref[0, :, col:col + LANES]], axis=0)
            swapped = pltpu.roll(both, HEAD_DIM, axis=1)
            zero = jnp.zeros_like(both)
            c0, c1 = 2 * pair, 2 * pair + 1
            dst[2 * c0 + 0] = jnp.where(lo_half, both, zero).astype(BF16)
            dst[2 * c0 + 1] = jnp.where(lo_half, zero, swapped).astype(BF16)
            dst[2 * c1 + 0] = jnp.where(lo_half, swapped, zero).astype(BF16)
            dst[2 * c1 + 1] = jnp.where(lo_half, zero, both).astype(BF16)

    scale = HEAD_DIM ** -0.5
    for p in range(N_HEADS // 2):
        c = (2 * p) // ATT_GROUP
        qp = (q_ref[0, :, p * LANES:(p + 1) * LANES] * scale).astype(BF16)
        o_pair = None
        for a in range(2):
            h = 2 * p + a
            sink = sink_ref[h]
            s = _dot_nt(qp, kz_s[2 * c + a])
            s_prev = s[:, :blk] + bprev_ref[h] + neg_prev
            s_cur = s[:, blk:] + bcur_ref[h]
            m = jnp.maximum(jnp.max(jnp.maximum(s_prev, s_cur), axis=-1, keepdims=True), sink)
            e_prev = jnp.exp(s_prev - m)
            e_cur = jnp.exp(s_cur - m)
            den = jnp.sum(e_prev + e_cur, axis=-1, keepdims=True) + jnp.exp(sink - m)
            pm = jnp.concatenate([e_prev, e_cur], axis=1).astype(BF16)
            o_a = _dot(pm, vz_s[2 * c + a]) * (1.0 / den)
            o_pair = o_a if o_pair is None else o_pair + o_a
        o_ref[0, :, p * LANES:(p + 1) * LANES] = o_pair.astype(o_ref.dtype)


def _t5_bucket_np(dist):
    max_exact = N_BUCKETS // 2
    d = np.maximum(dist, 0)
    df = np.maximum(d, 1).astype(np.float32)
    large = max_exact + (np.log(df / np.float32(max_exact)) / np.float32(math.log(MAX_DISTANCE / max_exact))
                         * np.float32(N_BUCKETS - max_exact)).astype(np.int32)
    large = np.minimum(large, N_BUCKETS - 1)
    return np.where(d < max_exact, d, large)


def _bias_from_dist(dist, in_band, rel_table):
    onehot = (_t5_bucket_np(dist)[..., None] == np.arange(N_BUCKETS)).astype(np.float32)
    tab = jnp.einsum('qsb,bh->hqs', jnp.asarray(onehot), rel_table.astype(F32), precision=lax.Precision.HIGHEST)
    return jnp.where(jnp.asarray(in_band)[None], tab, NEG_INF)


def attention_prompt(qkv, sinks, rel_table):
    b, l, n = qkv.shape
    blk = ATT_BLOCK
    nq = N_HEADS * HEAD_DIM
    kvw = 2 * N_KV_HEADS * HEAD_DIM
    kv_blk = nq // kvw
    qi = np.arange(blk)[:, None]
    sj = np.arange(blk)[None, :]
    d_prev = qi + blk - sj
    d_cur = qi - sj
    bias_prev = _bias_from_dist(d_prev, (d_prev >= 0) & (d_prev <= WINDOW), rel_table)
    bias_cur = _bias_from_dist(d_cur, (d_cur >= 0) & (d_cur <= WINDOW), rel_table)
    blocks = (_nbytes((blk, nq), F32) + 2 * _nbytes((blk, kvw), F32) + _nbytes((blk, nq), BF16))
    scratch = 2 * _nbytes((2 * N_KV_HEADS, 2 * blk, LANES), BF16) + 4 * _nbytes((N_HEADS, blk, blk), F32)
    return pl.pallas_call(
        _attn_prompt_kernel,
        grid=(b, l // blk),
        in_specs=[pl.BlockSpec(memory_space=pltpu.SMEM),
                  pl.BlockSpec((1, blk, nq), lambda b_, i: (b_, i, 0)),
                  pl.BlockSpec((1, blk, kvw), lambda b_, i: (b_, i, kv_blk)),
                  pl.BlockSpec((1, blk, kvw), lambda b_, i: (b_, jnp.maximum(i - 1, 0), kv_blk)),
                  pl.BlockSpec((N_HEADS, blk, blk), lambda b_, i: (0, 0, 0)),
                  pl.BlockSpec((N_HEADS, blk, blk), lambda b_, i: (0, 0, 0))],
        out_specs=pl.BlockSpec((1, blk, nq), lambda b_, i: (b_, i, 0)),
        out_shape=jax.ShapeDtypeStruct((b, l, nq), BF16),
        scratch_shapes=[pltpu.VMEM((2 * N_KV_HEADS, 2 * blk, LANES), BF16),
                        pltpu.VMEM((2 * N_KV_HEADS, 2 * blk, LANES), BF16)],
        compiler_params=_params(2, blocks, scratch),
        name="attention_prompt",
    )(sinks.astype(F32), qkv, qkv, qkv, bias_prev, bias_cur)


def _attn_sample_kernel(q_ref, k_ref, v_ref, bias_ref, sink_ref, o_ref):
    pairs = [(b, c) for b in range(q_ref.shape[0]) for c in range(N_KV_HEADS)]
    k = [k_ref[b].astype(BF16) for b in range(q_ref.shape[0])]
    s = {bc: _dot_nt(q_ref[bc[0], bc[1]].astype(BF16), k[bc[0]]) + bias_ref[bc[1]] for bc in pairs}
    e, den = {}, {}
    for b, c in pairs:
        sink = sink_ref[c]
        m = jnp.maximum(jnp.max(s[(b, c)], axis=-1, keepdims=True), sink)
        e[(b, c)] = jnp.exp(s[(b, c)] - m)
        den[(b, c)] = jnp.sum(e[(b, c)], axis=-1, keepdims=True) + jnp.exp(sink - m)
    for b, c in pairs:
        o_ref[b, c] = _dot(e[(b, c)].astype(BF16), v_ref[b].astype(BF16)) * (1.0 / den[(b, c)])


def attention_sample(qkv, cache_k, cache_v, sinks, rel_table):
    bsz, w = cache_k.shape[0], cache_k.shape[1]
    t = qkv.shape[0] // bsz
    nq, nkv = N_HEADS * HEAD_DIM, N_KV_HEADS * HEAD_DIM
    keys = w + t
    keys_pad = -(-keys // 16) * 16
    rows = ATT_GROUP * t
    q = qkv[:, :nq].reshape(t, bsz, N_KV_HEADS, ATT_GROUP, HEAD_DIM) * HEAD_DIM ** -0.5
    q = jnp.transpose(q, (1, 2, 3, 0, 4)).reshape(bsz, N_KV_HEADS, rows, HEAD_DIM)
    qz = jnp.einsum('bcrd,ck->bcrkd', q, jnp.eye(N_KV_HEADS, dtype=F32)).reshape(bsz, N_KV_HEADS, rows, nkv)
    new_kv = jnp.transpose(qkv[:, nq:].reshape(t, bsz, 2 * nkv), (1, 0, 2))
    k_all = jnp.concatenate([cache_k.reshape(bsz, w, nkv), new_kv[:, :, :nkv]], axis=1)
    v_all = jnp.concatenate([cache_v.reshape(bsz, w, nkv), new_kv[:, :, nkv:]], axis=1)
    pad = ((0, 0), (0, keys_pad - keys), (0, 0))
    k_pad, v_pad = jnp.pad(k_all, pad), jnp.pad(v_all, pad)
    tq = np.arange(t)[:, None]
    sk = np.arange(keys_pad)[None, :]
    dist = tq + w - sk
    in_band = (dist >= 0) & (dist <= WINDOW) & (sk < keys)
    bias_t = _bias_from_dist(dist, in_band, rel_table)
    bias = bias_t.reshape(N_KV_HEADS, rows, keys_pad)
    sink_rows = jnp.repeat(sinks.astype(F32), t).reshape(N_KV_HEADS, rows, 1)
    sb = math.gcd(bsz, ATT_SAMPLE_SEQS)
    blocks = sb * (_nbytes((N_KV_HEADS, rows, nkv), F32) * 2 + 2 * _nbytes((keys_pad, nkv), F32))
    oz = pl.pallas_call(
        _attn_sample_kernel,
        grid=(bsz // sb,),
        in_specs=[pl.BlockSpec((sb, N_KV_HEADS, rows, nkv), lambda b_: (b_, 0, 0, 0)),
                  pl.BlockSpec((sb, keys_pad, nkv), lambda b_: (b_, 0, 0)),
                  pl.BlockSpec((sb, keys_pad, nkv), lambda b_: (b_, 0, 0)),
                  pl.BlockSpec((N_KV_HEADS, rows, keys_pad), lambda b_: (0, 0, 0)),
                  pl.BlockSpec((N_KV_HEADS, rows, 1), lambda b_: (0, 0, 0))],
        out_specs=pl.BlockSpec((sb, N_KV_HEADS, rows, nkv), lambda b_: (b_, 0, 0, 0)),
        out_shape=jax.ShapeDtypeStruct((bsz, N_KV_HEADS, rows, nkv), F32),
        compiler_params=_params(1, blocks),
        name="attention_sample",
    )(qz, k_pad, v_pad, bias, sink_rows)
    o = jnp.stack([oz[:, c, :, c * HEAD_DIM:(c + 1) * HEAD_DIM] for c in range(N_KV_HEADS)], axis=1)
    o = o.reshape(bsz, N_KV_HEADS, ATT_GROUP, t, HEAD_DIM)
    o = jnp.transpose(o, (3, 0, 1, 2, 4)).reshape(t * bsz, nq)
    return o.astype(BF16), k_all[:, t:], v_all[:, t:]


def _mm_res_kernel(a_ref, w_ref, x_ref, g_ref, o_ref):
    y = _dot_f32_weight(a_ref[0], w_ref)
    o_ref[0] = x_ref[0] + g_ref[0] * y


def matmul_gate_residual(a, w_stack, layer, x, gate, tm, tn):
    b, l, k = a.shape
    n = w_stack.shape[2]
    grows = 1 if gate.shape[1] == 1 else tm
    if gate.shape[1] == 1:
        g_spec = pl.BlockSpec((1, 1, tn), lambda b_, i, j: (b_, 0, j))
    else:
        g_spec = pl.BlockSpec((1, tm, tn), lambda b_, i, j: (b_, i, j))
    blocks = (_nbytes((tm, k), a.dtype) + _nbytes((k, tn), F32) + 2 * _nbytes((tm, tn), F32)
              + _nbytes((grows, tn), F32))
    scratch = _nbytes((k, tn), BF16) + _nbytes((tm, tn), F32)
    return pl.pallas_call(
        _mm_res_kernel,
        grid=(b, l // tm, n // tn),
        in_specs=[pl.BlockSpec((1, tm, k), lambda b_, i, j: (b_, i, 0)),
                  pl.BlockSpec((None, k, tn), lambda b_, i, j: (layer, 0, j)),
                  pl.BlockSpec((1, tm, tn), lambda b_, i, j: (b_, i, j)),
                  g_spec],
        out_specs=pl.BlockSpec((1, tm, tn), lambda b_, i, j: (b_, i, j)),
        out_shape=jax.ShapeDtypeStruct((b, l, n), F32),
        compiler_params=_params(3, blocks, scratch),
        name="matmul_gate_residual",
    )(a, w_stack, x, gate)


def _ffn_conv_rows(u, prev1, prev2, cw_ref, cb_ref):
    return u * cw_ref[2:3, :] + prev1 * cw_ref[1:2, :] + prev2 * cw_ref[0:1, :] + cb_ref[...]


def _ffn_up_prompt_kernel(x_ref, gain_ref, shift_ref, scale_ref, wg_ref, wv_ref, cwg_ref, cwv_ref,
                          cbg_ref, cbv_ref, o_ref, tail_ref, h_s, carry_s, ext_s):
    i = pl.program_id(1)
    j = pl.program_id(2)
    tm = h_s.shape[0]

    @pl.when(j == 0)
    def _():
        h_s[...] = _norm_mod(x_ref[0], gain_ref[...], shift_ref[0], scale_ref[0]).astype(BF16)

    @pl.when(i == 0)
    def _():
        carry_s[j] = jnp.zeros(carry_s.shape[1:], F32)

    halves = ((wg_ref, cwg_ref, cbg_ref), (wv_ref, cwv_ref, cbv_ref))
    w_bf = [w_ref[...].astype(BF16) for w_ref, _, _ in halves]
    for half in range(2):
        ext_s[half, 0:SUBLANES, :] = carry_s[j, half]
    sub = min(FFN_SUB_ROWS, tm)
    for m in range(tm // sub):
        hm = h_s[m * sub:(m + 1) * sub, :]
        r0 = SUBLANES + m * sub
        ys = []
        for half, (_, cw_ref, cb_ref) in enumerate(halves):
            u = _dot(hm, w_bf[half])
            ext_s[half, r0:r0 + sub, :] = u
            ys.append(_ffn_conv_rows(u, ext_s[half, r0 - 1:r0 - 1 + sub, :], ext_s[half, r0 - 2:r0 - 2 + sub, :],
                                     cw_ref, cb_ref))
        o_ref[0, m * sub:(m + 1) * sub, :] = (_silu(ys[0]) * ys[1]).astype(o_ref.dtype)
    for half in range(2):
        tail = ext_s[half, tm:tm + SUBLANES, :]
        carry_s[j, half] = tail
        tail_ref[0, 0, half] = tail


def ffn_up_prompt(x, gain, shift, scale, w_up, conv_w, conv_b, layer, tm, tn=512):
    b, l, d = x.shape
    dff = w_up.shape[2] // 2
    nj = dff // tn
    conv_b = conv_b.reshape(conv_b.shape[0], 1, 2 * dff)
    blocks = (_nbytes((tm, d), F32) + 2 * _nbytes((d, tn), F32) + _nbytes((tm, tn), BF16)
              + 2 * _nbytes((1, d), F32) + 8 * _nbytes((SUBLANES, tn), F32))
    scratch = (_nbytes((tm, d), BF16) + _nbytes((nj, 2, SUBLANES, tn), F32) + 2 * _nbytes((d, tn), BF16)
               + 8 * _nbytes((tm, tn), F32))
    act, tail = pl.pallas_call(
        _ffn_up_prompt_kernel,
        grid=(b, l // tm, nj),
        in_specs=[pl.BlockSpec((1, tm, d), lambda b_, i, j: (b_, i, 0)),
                  pl.BlockSpec((1, d), lambda b_, i, j: (0, 0)),
                  _mod_specs(shift, tm), _mod_specs(scale, tm),
                  pl.BlockSpec((None, d, tn), lambda b_, i, j: (layer, 0, j)),
                  pl.BlockSpec((None, d, tn), lambda b_, i, j: (layer, 0, j + nj)),
                  pl.BlockSpec((None, FFN_CONV, tn), lambda b_, i, j: (layer, 0, j)),
                  pl.BlockSpec((None, FFN_CONV, tn), lambda b_, i, j: (layer, 0, j + nj)),
                  pl.BlockSpec((None, 1, tn), lambda b_, i, j: (layer, 0, j)),
                  pl.BlockSpec((None, 1, tn), lambda b_, i, j: (layer, 0, j + nj))],
        out_specs=[pl.BlockSpec((1, tm, tn), lambda b_, i, j: (b_, i, j)),
                   pl.BlockSpec((1, 1, 2, SUBLANES, tn), lambda b_, i, j: (b_, i, 0, 0, j))],
        out_shape=[jax.ShapeDtypeStruct((b, l, dff), BF16),
                   jax.ShapeDtypeStruct((b, l // tm, 2, SUBLANES, dff), F32)],
        scratch_shapes=[pltpu.VMEM((tm, d), BF16), pltpu.VMEM((nj, 2, SUBLANES, tn), F32),
                        pltpu.VMEM((2, SUBLANES + tm, tn), F32)],
        compiler_params=_params(3, blocks, scratch),
        name="ffn_up_prompt",
    )(x, gain.reshape(1, d), shift, scale, w_up, w_up, conv_w, conv_w, conv_b, conv_b)
    keep = FFN_CONV - 1
    hist = jnp.transpose(tail[:, -1, :, SUBLANES - keep:, :], (0, 2, 1, 3)).reshape(b, keep, 2 * dff)
    return act, hist


def _ffn_up_sample_kernel(x_ref, gain_ref, shift_ref, scale_ref, wg_ref, wv_ref, cwg_ref, cwv_ref,
                          cbg_ref, cbv_ref, hg_ref, hv_ref, o_ref, ng_ref, nv_ref, h_s, *, bsz):
    @pl.when(pl.program_id(0) == 0)
    def _():
        h_s[...] = _norm_mod(x_ref[...], gain_ref[...], shift_ref[...], scale_ref[...]).astype(BF16)

    h = h_s[...]
    rows = h_s.shape[0]
    keep = FFN_CONV - 1
    ys = []
    for w_ref, cw_ref, cb_ref, hist_ref, new_ref in ((wg_ref, cwg_ref, cbg_ref, hg_ref, ng_ref),
                                                     (wv_ref, cwv_ref, cbv_ref, hv_ref, nv_ref)):
        u = _dot(h, w_ref[...].astype(BF16))
        ext = jnp.concatenate([hist_ref[...], u], axis=0)
        prev = [ext[(keep - k) * bsz:(keep - k) * bsz + rows] for k in (1, 2)]
        ys.append(_ffn_conv_rows(u, prev[0], prev[1], cw_ref, cb_ref))
        new_ref[...] = ext[rows:rows + keep * bsz]
    o_ref[...] = (_silu(ys[0]) * ys[1]).astype(o_ref.dtype)


def ffn_up_sample(x, gain, shift, scale, w_up, conv_w, conv_b, layer, hist, tn=512):
    rows, d = x.shape
    dff = w_up.shape[2] // 2
    nj = dff // tn
    keep = FFN_CONV - 1
    bsz = hist.shape[0] // keep
    conv_b = conv_b.reshape(conv_b.shape[0], 1, 2 * dff)
    col = lambda j: (0, j)
    col_hi = lambda j: (0, j + nj)
    lcol = lambda j: (layer, 0, j)
    lcol_hi = lambda j: (layer, 0, j + nj)
    blocks = (_nbytes((rows, d), F32) * 3 + 2 * _nbytes((d, tn), F32) + 5 * _nbytes((rows, tn), F32))
    scratch = _nbytes((rows, d), BF16) + 2 * _nbytes((d, tn), BF16) + 8 * _nbytes((rows, tn), F32)
    act, new_g, new_v = pl.pallas_call(
        functools.partial(_ffn_up_sample_kernel, bsz=bsz),
        grid=(nj,),
        in_specs=[pl.BlockSpec((rows, d), lambda j: (0, 0)),
                  pl.BlockSpec((1, d), lambda j: (0, 0)),
                  pl.BlockSpec((rows, d), lambda j: (0, 0)),
                  pl.BlockSpec((rows, d), lambda j: (0, 0)),
                  pl.BlockSpec((None, d, tn), lcol), pl.BlockSpec((None, d, tn), lcol_hi),
                  pl.BlockSpec((None, FFN_CONV, tn), lcol), pl.BlockSpec((None, FFN_CONV, tn), lcol_hi),
                  pl.BlockSpec((None, 1, tn), lcol), pl.BlockSpec((None, 1, tn), lcol_hi),
                  pl.BlockSpec((keep * bsz, tn), col), pl.BlockSpec((keep * bsz, tn), col_hi)],
        out_specs=[pl.BlockSpec((rows, tn), col), pl.BlockSpec((keep * bsz, tn), col),
                   pl.BlockSpec((keep * bsz, tn), col)],
        out_shape=[jax.ShapeDtypeStruct((rows, dff), BF16),
                   jax.ShapeDtypeStruct((keep * bsz, dff), F32),
                   jax.ShapeDtypeStruct((keep * bsz, dff), F32)],
        scratch_shapes=[pltpu.VMEM((rows, d), BF16)],
        compiler_params=_params(1, blocks, scratch),
        name="ffn_up_sample",
    )(x, gain.reshape(1, d), shift, scale, w_up, w_up, conv_w, conv_w, conv_b, conv_b, hist, hist)
    return act, jnp.concatenate([new_g, new_v], axis=-1)


def _gdn_in_kernel(*refs, n_qk_tiles, n_conv_tiles, bsz):
    if bsz is None:
        (x_ref, gain_ref, shift_ref, scale_ref, w_ref, wba_ref, alog_ref, dtb_ref, cw_ref,
         o_ref, bg_ref, raw_ref, h_s, carry_s) = refs
    else:
        (x_ref, gain_ref, shift_ref, scale_ref, w_ref, wba_ref, alog_ref, dtb_ref, cw_ref, hist_ref,
         o_ref, bg_ref, raw_ref, h_s) = refs
    i = pl.program_id(1)
    j = pl.program_id(2)
    tm = h_s.shape[0]
    width = GDN_CONV

    @pl.when(j == 0)
    def _():
        h = _norm_mod(x_ref[0], gain_ref[...], shift_ref[0], scale_ref[0]).astype(BF16)
        h_s[...] = h
        y = _dot_nt(h, wba_ref[...].astype(BF16))
        lane = lax.broadcasted_iota(jnp.int32, y.shape, 1) % LANES
        beta = 1.0 / (1.0 + jnp.exp(-y))
        a = y + dtb_ref[...]
        softplus = jnp.maximum(a, 0.0) + jnp.log1p(jnp.exp(-jnp.abs(a)))
        g = -jnp.exp(alog_ref[...]) * softplus
        bg_ref[0] = jnp.where(lane < GDN_VH_PER_GROUP, beta, jnp.where(lane < 2 * GDN_VH_PER_GROUP, g, 0.0))

    sub = min(GDN_IN_SUB_ROWS, tm)
    row8 = lax.broadcasted_iota(jnp.int32, (SUBLANES, 1), 0)

    def conv_tile(l2_scale):
        w_bf = w_ref[...].astype(BF16)
        jc = jnp.minimum(j, n_conv_tiles - 1)
        if bsz is None:
            @pl.when(i == 0)
            def _():
                carry_s[jc] = jnp.zeros(carry_s.shape[1:], F32)
            prev_tail = carry_s[jc]
        for m in range(tm // sub):
            rs = slice(m * sub, (m + 1) * sub)
            u = _dot_nt(h_s[rs, :], w_bf)
            y = u * cw_ref[width - 1:width, :]
            if bsz is None:
                top = u[0:SUBLANES]
                y_top = top * cw_ref[width - 1:width, :]
                for k in range(1, width):
                    wk = cw_ref[width - 1 - k:width - k, :]
                    y = y + pltpu.roll(u, k, axis=0) * wk
                    y_top = y_top + jnp.where(row8 < k, pltpu.roll(prev_tail, k, axis=0),
                                              pltpu.roll(top, k, axis=0)) * wk
                y = jnp.concatenate([y_top, y[SUBLANES:]], axis=0)
                prev_tail = u[sub - SUBLANES:sub]
            else:
                ext = jnp.concatenate([hist_ref[...], u], axis=0)
                for k in range(1, width):
                    wk = cw_ref[width - 1 - k:width - k, :]
                    y = y + ext[(width - 1 - k) * bsz:(width - 1 - k) * bsz + sub] * wk
                raw_ref[...] = ext[sub:sub + (width - 1) * bsz]
            y = _silu(y)
            if l2_scale is not None:
                heads = [y[:, hd * LANES:(hd + 1) * LANES] for hd in range(y.shape[1] // LANES)]
                y = jnp.concatenate(
                    [yh * (lax.rsqrt(jnp.sum(yh * yh, axis=-1, keepdims=True) + NORM_EPS) * l2_scale)
                     for yh in heads], axis=1)
            o_ref[0, rs, :] = y.astype(o_ref.dtype)
        if bsz is None:
            carry_s[jc] = prev_tail
            raw_ref[0, 0] = prev_tail

    @pl.when(j < n_qk_tiles)
    def _():
        conv_tile(jnp.where(j < n_qk_tiles // 2, GDN_DK ** -0.5, 1.0).astype(F32))

    @pl.when((j >= n_qk_tiles) & (j < n_conv_tiles))
    def _():
        conv_tile(None)

    @pl.when(j >= n_conv_tiles)
    def _():
        w_bf = w_ref[...].astype(BF16)
        for m in range(tm // sub):
            rs = slice(m * sub, (m + 1) * sub)
            o_ref[0, rs, :] = _silu(_dot_nt(h_s[rs, :], w_bf)).astype(o_ref.dtype)


def _group_lane_layout(vec_b, vec_a):
    lead = vec_b.shape[:-1]
    vb = vec_b.reshape(*lead, GDN_HEAD_GROUPS, GDN_VH_PER_GROUP)
    va = vec_a.reshape(*lead, GDN_HEAD_GROUPS, GDN_VH_PER_GROUP)
    pad = jnp.zeros((*lead, GDN_HEAD_GROUPS, LANES - 2 * GDN_VH_PER_GROUP), vec_b.dtype)
    return jnp.concatenate([vb, va, pad], axis=-1).reshape(*lead, GDN_HEAD_GROUPS * LANES)


def gdn_in_projection(x, gain, shift, scale, w_in, conv_w, a_log, dt_bias, tm, conv_hist=None, tn=512):
    b, l, d = x.shape
    n_main = GDN_CONV_DIM + GDN_VAL_DIM
    w_t = jnp.transpose(w_in)
    wba = _group_lane_layout(w_in[:, n_main:n_main + GDN_V_HEADS], w_in[:, n_main + GDN_V_HEADS:])
    wba_t = jnp.transpose(wba)
    zeros = jnp.zeros((1, GDN_V_HEADS), F32)
    alog = _group_lane_layout(zeros, a_log.reshape(1, -1).astype(F32))
    dtb = _group_lane_layout(zeros, dt_bias.reshape(1, -1).astype(F32))
    nbg = wba_t.shape[0]
    n_conv_tiles = GDN_CONV_DIM // tn
    n_qk_tiles = 2 * GDN_KEY_DIM // tn
    conv_col = lambda b_, i, j: (0, jnp.minimum(j, n_conv_tiles - 1))
    in_specs = [pl.BlockSpec((1, tm, d), lambda b_, i, j: (b_, i, 0)),
                pl.BlockSpec((1, d), lambda b_, i, j: (0, 0)),
                _mod_specs(shift, tm), _mod_specs(scale, tm),
                pl.BlockSpec((tn, d), lambda b_, i, j: (j, 0)),
                pl.BlockSpec((nbg, d), lambda b_, i, j: (0, 0)),
                pl.BlockSpec((1, nbg), lambda b_, i, j: (0, 0)),
                pl.BlockSpec((1, nbg), lambda b_, i, j: (0, 0)),
                pl.BlockSpec((GDN_CONV, tn), conv_col)]
    args = [x, gain.reshape(1, d), shift, scale, w_t, wba_t, alog, dtb, conv_w]
    scratch_shapes = [pltpu.VMEM((tm, d), BF16)]
    blocks = (_nbytes((tm, d), F32) + _nbytes((tn, d), F32) + _nbytes((tm, tn), BF16)
              + _nbytes((nbg, d), F32) + _nbytes((tm, nbg), F32) + 2 * _nbytes((1, d), F32))
    scratch = _nbytes((tm, d), BF16) + _nbytes((tn, d), BF16) + 6 * _nbytes((tm, nbg), F32)
    if conv_hist is None:
        bsz = None
        raw_spec = pl.BlockSpec((1, 1, SUBLANES, tn), lambda b_, i, j: (b_, i, 0, jnp.minimum(j, n_conv_tiles - 1)))
        raw_shape = jax.ShapeDtypeStruct((b, l // tm, SUBLANES, GDN_CONV_DIM), F32)
        scratch_shapes.append(pltpu.VMEM((n_conv_tiles, SUBLANES, tn), F32))
    else:
        assert b == 1 and l == tm <= GDN_IN_SUB_ROWS
        hist_rows = conv_hist.shape[0]
        bsz = hist_rows // (GDN_CONV - 1)
        args.append(conv_hist)
        in_specs.append(pl.BlockSpec((hist_rows, tn), conv_col))
        raw_spec = pl.BlockSpec((hist_rows, tn), conv_col)
        raw_shape = jax.ShapeDtypeStruct((hist_rows, GDN_CONV_DIM), F32)
        blocks += 2 * _nbytes((hist_rows, tn), F32)
    return pl.pallas_call(
        functools.partial(_gdn_in_kernel, n_qk_tiles=n_qk_tiles, n_conv_tiles=n_conv_tiles, bsz=bsz),
        grid=(b, l // tm, n_main // tn),
        in_specs=in_specs,
        out_specs=[pl.BlockSpec((1, tm, tn), lambda b_, i, j: (b_, i, j)),
                   pl.BlockSpec((1, tm, nbg), lambda b_, i, j: (b_, i, 0)),
                   raw_spec],
        out_shape=[jax.ShapeDtypeStruct((b, l, n_main), BF16),
                   jax.ShapeDtypeStruct((b, l, nbg), F32),
                   raw_shape],
        scratch_shapes=scratch_shapes,
        compiler_params=_params(3, blocks, scratch),
        name="gdn_in_projection",
    )(*args)


def _gdn_core_kernel(q_ref, k_ref, v_ref, zg_ref, bg_ref, onorm_ref, s0_ref, o_ref, s1_ref,
                     s_s, gc_s, tdec_s, mo_s, *, seq_rows):
    r = pl.program_id(2)
    n_r = pl.num_programs(2)
    rows = q_ref.shape[1]
    chunk = GDN_CHUNK
    vpg = GDN_VH_PER_GROUP
    rep = GDN_V_HEADS // GDN_K_HEADS
    n_kh = GDN_KH_PER_GROUP
    seg = chunk if seq_rows is None else seq_rows
    n_seq = chunk // seg

    def load_states():
        for s in range(n_seq):
            for kh in range(n_kh):
                s_s[s * n_kh + kh] = jnp.concatenate([s0_ref[s, kh * rep + e] for e in range(rep)], axis=1)

    if seq_rows is None:
        pl.when(r == 0)(load_states)
    else:
        load_states()

    def seg_last(x):
        if n_seq == 1:
            return x[chunk - 1:chunk, :]
        blocks = x.reshape(n_seq, seg, x.shape[1])
        return jnp.broadcast_to(blocks[:, seg - 1:seg, :], blocks.shape).reshape(x.shape)

    n_chunks = rows // chunk
    pw = GDN_PACK * chunk
    n_packs = vpg // GDN_PACK
    ri = lax.broadcasted_iota(jnp.int32, (chunk, pw), 0)
    li = lax.broadcasted_iota(jnp.int32, (chunk, pw), 1) % chunk
    seg_shift = int(math.log2(seg))
    same_seq = (ri >> seg_shift) == (li >> seg_shift)
    tril_p = (ri >= li) & same_seq
    strict_p = (ri > li) & same_seq
    eye_p = (ri == li).astype(F32)
    n_levels = seg_shift
    off_masks = [((ri >> (lvl + 1)) == (li >> (lvl + 1))) & ((ri >> lvl) != (li >> lvl))
                 for lvl in range(n_levels)]
    bd_rows = lax.broadcasted_iota(jnp.int32, (pw, pw), 0) // chunk
    bd_cols = lax.broadcasted_iota(jnp.int32, (pw, pw), 1) // chunk
    bd_mask = (bd_rows == bd_cols).astype(BF16)
    row_c = lax.broadcasted_iota(jnp.int32, (chunk, LANES), 0)
    lane_lo = lax.broadcasted_iota(jnp.int32, (chunk, LANES), 1) < chunk

    def block_diag(xp):
        return jnp.concatenate([xp.astype(BF16)] * GDN_PACK, axis=0) * bd_mask

    def pack_cols(arr, first_col):
        tiles = []
        for t in range(pw // LANES):
            even = jnp.broadcast_to(arr[:, first_col + 2 * t:first_col + 2 * t + 1], (chunk, LANES))
            odd = jnp.broadcast_to(arr[:, first_col + 2 * t + 1:first_col + 2 * t + 2], (chunk, LANES))
            tiles.append(jnp.where(lane_lo, even, odd))
        return jnp.concatenate(tiles, axis=1)

    for c0 in range(0, n_chunks, GDN_PHASE_A_CHUNKS):
        group = [(c, p) for c in range(c0, min(c0 + GDN_PHASE_A_CHUNKS, n_chunks)) for p in range(n_packs)]
        a_list, qkd, brow, dec = {}, {}, {}, {}
        for c in range(c0, min(c0 + GDN_PHASE_A_CHUNKS, n_chunks)):
            rs = slice(c * chunk, (c + 1) * chunk)
            bg = bg_ref[0, rs, :]
            cum = bg
            shift = 1
            while shift < seg:
                cum = cum + jnp.where((row_c & (seg - 1)) >= shift, pltpu.roll(cum, shift, axis=0), 0.0)
                shift *= 2
            gc_s[rs, :] = cum
            gq = []
            for kh in range(GDN_KH_PER_GROUP):
                ksl = slice(kh * LANES, (kh + 1) * LANES)
                k = k_ref[0, rs, ksl]
                kq = jnp.concatenate([k, q_ref[0, rs, ksl]], axis=0)
                kk = jnp.concatenate([k, k], axis=0)
                gq.append(_dot_nt(kq, kk))
            for p in range(n_packs):
                khs = [(p * GDN_PACK + 2 * t) // rep for t in range(pw // LANES)]
                gram = jnp.concatenate([gq[kh][:chunk] for kh in khs], axis=1)
                qk = jnp.concatenate([gq[kh][chunk:] for kh in khs], axis=1)
                beta_p = pack_cols(bg, p * GDN_PACK)
                gcol_p = pack_cols(cum, vpg + p * GDN_PACK)
                grow_p = jnp.sum(gcol_p * eye_p, axis=0, keepdims=True)
                decay = jnp.exp(jnp.where(tril_p, gcol_p - grow_p, NEG_INF))
                a_list[(c, p)] = jnp.where(strict_p, gram * beta_p * decay, 0.0)
                qkd[(c, p)] = (qk * decay).astype(BF16)
                brow[(c, p)] = jnp.sum(beta_p * eye_p, axis=0, keepdims=True)
                dec[(c, p)] = jnp.exp(seg_last(gcol_p) - gcol_p)
        inv = {cp: eye_p - jnp.where(off_masks[0], a_list[cp], 0.0) for cp in group}
        for lvl in range(1, n_levels):
            w = {cp: _dot(jnp.where(off_masks[lvl], a_list[cp], 0.0).astype(BF16), block_diag(inv[cp]))
                 for cp in group}
            inv = {cp: inv[cp] - _dot(inv[cp].astype(BF16), block_diag(w[cp])) for cp in group}
        for cp in group:
            t_beta = inv[cp] * brow[cp]
            tdec_s[cp[0], cp[1]] = block_diag(dec[cp] * t_beta)
            mo_s[cp[0], cp[1]] = block_diag(_dot(qkd[cp], block_diag(t_beta)))

    def chunk_body(c, carry):
        rs = pl.ds(pl.multiple_of(c * chunk, chunk), chunk)
        cum = gc_s[rs, :]
        eg, g_last, ks, qs = [], [], [], []
        for kh in range(n_kh):
            ksl = slice(kh * LANES, (kh + 1) * LANES)
            kq = jnp.concatenate([k_ref[0, rs, ksl], q_ref[0, rs, ksl]], axis=0)
            per_seq = [_dot(kq, s_s[s * n_kh + kh].astype(BF16)) for s in range(n_seq)]
            kq_s = jnp.concatenate([per_seq[s][half * chunk + s * seg:half * chunk + (s + 1) * seg]
                                    for half in range(2) for s in range(n_seq)], axis=0)
            for e in range(rep):
                hv = kh * rep + e
                gcol = cum[:, vpg + hv:vpg + hv + 1]
                g_last.append([gcol[(s + 1) * seg - 1:(s + 1) * seg, :] for s in range(n_seq)])
                eg.append(jnp.exp(gcol))
                ks.append(kq_s[:chunk, e * LANES:(e + 1) * LANES] * eg[hv])
                qs.append(kq_s[chunk:, e * LANES:(e + 1) * LANES] * eg[hv])
        v_dec, o_intra = [], []
        for p in range(n_packs):
            heads = range(p * GDN_PACK, (p + 1) * GDN_PACK)
            rhs = jnp.concatenate([v_ref[0, rs, hv * LANES:(hv + 1) * LANES].astype(F32) - ks[hv]
                                   for hv in heads], axis=0).astype(BF16)
            vd = _dot(tdec_s[c, p], rhs)
            oi = _dot(mo_s[c, p], rhs)
            for j in range(GDN_PACK):
                v_dec.append(vd[j * chunk:(j + 1) * chunk])
                o_intra.append(oi[j * chunk:(j + 1) * chunk])
        for hv in range(vpg):
            vsl = slice(hv * LANES, (hv + 1) * LANES)
            o = qs[hv] + o_intra[hv]
            on = o * lax.rsqrt(jnp.mean(o * o, axis=-1, keepdims=True) + NORM_EPS) * onorm_ref[...]
            o_ref[0, rs, vsl] = (on * zg_ref[0, rs, vsl].astype(F32)).astype(o_ref.dtype)
        lane_v = lax.broadcasted_iota(jnp.int32, (1, rep * LANES), 1) // LANES
        row_seq = lax.broadcasted_iota(jnp.int32, (chunk, 1), 0) >> seg_shift
        for kh in range(n_kh):
            heads = range(kh * rep, (kh + 1) * rep)
            k_bf = k_ref[0, rs, kh * LANES:(kh + 1) * LANES]
            vd_pair = jnp.concatenate([v_dec[hv] for hv in heads], axis=1)
            for s in range(n_seq):
                vd_s = vd_pair if n_seq == 1 else jnp.where(row_seq == s, vd_pair, 0.0)
                ds = _dot_tn(k_bf, vd_s.astype(BF16))
                keep = jnp.exp(g_last[kh * rep][s])
                for e in range(1, rep):
                    keep = jnp.where(lane_v == e, jnp.exp(g_last[kh * rep + e][s]), keep)
                s_s[s * n_kh + kh] = s_s[s * n_kh + kh] * keep + ds
        return carry

    lax.fori_loop(0, n_chunks, chunk_body, 0)

    def store_states():
        for s in range(n_seq):
            for hv in range(vpg):
                s1_ref[s, hv] = s_s[s * n_kh + hv // rep][:, (hv % rep) * LANES:(hv % rep + 1) * LANES]

    if seq_rows is None:
        pl.when(r == n_r - 1)(store_states)
    else:
        store_states()


def gdn_core(qkvz, bg, s0, out_norm, rows, seq_rows=None):
    b, l, _ = qkvz.shape
    n_state = 1 if seq_rows is None else GDN_CHUNK // seq_rows
    if seq_rows is None:
        state_idx = lambda b_, g, r: (b_, g, 0, 0)
    else:
        assert b == 1 and rows == GDN_CHUNK
        state_idx = lambda b_, g, r: (r, g, 0, 0)
    hg = GDN_HEAD_GROUPS
    qw = GDN_KEY_DIM // hg
    vw = GDN_VAL_DIM // hg
    kq = GDN_KEY_DIM // qw
    kv = 2 * GDN_KEY_DIM // vw
    kz = GDN_CONV_DIM // vw
    vpg = GDN_VH_PER_GROUP
    blocks = (2 * _nbytes((rows, qw), BF16) + 3 * _nbytes((rows, vw), BF16) + _nbytes((rows, LANES), F32)
              + 2 * _nbytes((n_state, vpg, GDN_DK, GDN_DV), F32))
    n_chunks = rows // GDN_CHUNK
    n_packs = vpg // GDN_PACK
    pw = GDN_PACK * GDN_CHUNK
    scratch = (_nbytes((n_state, vpg, GDN_DK, GDN_DV), F32) + _nbytes((rows, LANES), F32)
               + 2 * _nbytes((n_chunks, n_packs, pw, pw), BF16) + 6 * _nbytes((rows, vw), F32))
    return pl.pallas_call(
        functools.partial(_gdn_core_kernel, seq_rows=seq_rows),
        grid=(b, hg, l // rows),
        in_specs=[pl.BlockSpec((1, rows, qw), lambda b_, g, r: (b_, r, g)),
                  pl.BlockSpec((1, rows, qw), lambda b_, g, r: (b_, r, kq + g)),
                  pl.BlockSpec((1, rows, vw), lambda b_, g, r: (b_, r, kv + g)),
                  pl.BlockSpec((1, rows, vw), lambda b_, g, r: (b_, r, kz + g)),
                  pl.BlockSpec((1, rows, LANES), lambda b_, g, r: (b_, r, g)),
                  pl.BlockSpec((1, GDN_DV), lambda b_, g, r: (0, 0)),
                  pl.BlockSpec((n_state, vpg, GDN_DK, GDN_DV), state_idx)],
        out_specs=[pl.BlockSpec((1, rows, vw), lambda b_, g, r: (b_, r, g)),
                   pl.BlockSpec((n_state, vpg, GDN_DK, GDN_DV), state_idx)],
        out_shape=[jax.ShapeDtypeStruct((b, l, GDN_VAL_DIM), BF16),
                   jax.ShapeDtypeStruct(s0.shape, F32)],
        scratch_shapes=[pltpu.VMEM((n_state * GDN_KH_PER_GROUP, GDN_DK,
                                    (GDN_V_HEADS // GDN_K_HEADS) * GDN_DV), F32),
                        pltpu.VMEM((rows, LANES), F32),
                        pltpu.VMEM((n_chunks, n_packs, pw, pw), BF16),
                        pltpu.VMEM((n_chunks, n_packs, pw, pw), BF16)],
        compiler_params=_params(3, blocks, scratch),
        name="gdn_core",
    )(qkvz, qkvz, qkvz, qkvz, bg, out_norm.reshape(1, GDN_DV).astype(F32), s0)


def _tile_rows(l, cap):
    t = min(l, cap)
    while l % t:
        t //= 2
    return t


def _to_time_major(a):
    return jnp.transpose(a, (1, 0, 2)).reshape(a.shape[0] * a.shape[1], a.shape[2])


def _from_time_major(a, bsz):
    return jnp.transpose(a.reshape(a.shape[0] // bsz, bsz, a.shape[1]), (1, 0, 2))


def _trunk(x, mod, states, P, sample):
    bsz, l, d = x.shape
    tm = _tile_rows(l, ROW_TILE_CAP)
    new = {}
    depth = mod.shape[0]
    for layer in range(depth):
        if sample:
            parts = [mod[layer][None, :, k * d:(k + 1) * d] for k in range(6)]
        else:
            parts = [mod[layer][:, None, k * d:(k + 1) * d] for k in range(6)]
        sh_m, sc_m, g_m, sh_f, sc_f, g_f = parts
        i = layer // 2
        if layer % 2 == 0:
            qkv = qkv_projection(x, P['norm_mix'][layer], sh_m, sc_m, P['w_attn_qkv'][i],
                                 P['attn_q_norm'][i], P['attn_k_norm'][i], tm)
            nq, nkv = N_HEADS * HEAD_DIM, N_KV_HEADS * HEAD_DIM
            if sample:
                o, k_win, v_win = attention_sample(qkv[0], states['win_k'][i], states['win_v'][i],
                                                   P['attn_sinks'][i], P['rel_bias_table'])
                o = o[None]
                w_len = k_win.shape[1]
                new.setdefault('win_k', []).append(k_win.reshape(-1, w_len, N_KV_HEADS, HEAD_DIM))
                new.setdefault('win_v', []).append(v_win.reshape(-1, w_len, N_KV_HEADS, HEAD_DIM))
            else:
                o = attention_prompt(qkv, P['attn_sinks'][i], P['rel_bias_table'])
                keep = min(WINDOW, PAST_LEN)
                new.setdefault('win_k', []).append(
                    qkv[:, l - keep:, nq:nq + nkv].reshape(bsz, keep, N_KV_HEADS, HEAD_DIM))
                new.setdefault('win_v', []).append(
                    qkv[:, l - keep:, nq + nkv:].reshape(bsz, keep, N_KV_HEADS, HEAD_DIM))
            x = matmul_gate_residual(o, P['w_attn_o'], i, x, g_m, tm, 512)
        else:
            keep = GDN_CONV - 1
            gdn_args = (x, P['norm_mix'][layer], sh_m, sc_m, P['w_gdn_in'][i], P['gdn_conv_w'][i],
                        P['gdn_a_log'][i], P['gdn_dt_bias'][i], tm)
            if sample:
                seqs = states['gdn'].shape[1]
                t = l // seqs
                qkvz, bg, hist1 = gdn_in_projection(
                    *gdn_args, conv_hist=_to_time_major(states['gdn_conv'][i].astype(F32)))
                new.setdefault('gdn_conv', []).append(_from_time_major(hist1, seqs))
                seq_rows = -(-t // SUBLANES) * SUBLANES
                per_seq = lambda a: jnp.pad(_from_time_major(a, seqs), ((0, 0), (0, seq_rows - t), (0, 0))
                                            ).reshape(1, seqs * seq_rows, a.shape[-1])
                o_seq, s1 = gdn_core(per_seq(qkvz[0]), per_seq(bg[0]), states['gdn'][i].astype(F32),
                                     P['gdn_out_norm'][i], GDN_CHUNK, seq_rows=seq_rows)
                o = _to_time_major(o_seq.reshape(seqs, seq_rows, GDN_VAL_DIM)[:, :t])[None]
            else:
                qkvz, bg, raw = gdn_in_projection(*gdn_args)
                s0 = jnp.zeros((bsz, GDN_V_HEADS, GDN_DK, GDN_DV), F32)
                o, s1 = gdn_core(qkvz, bg, s0, P['gdn_out_norm'][i], _tile_rows(l, GDN_ROW_TILE_CAP))
                new.setdefault('gdn_conv', []).append(raw[:, -1, SUBLANES - keep:, :])
            new.setdefault('gdn', []).append(s1)
            x = matmul_gate_residual(o, P['w_gdn_out'], i, x, g_m, tm, 512)
        if sample:
            seqs = states['ffn_conv'].shape[1]
            act, f1 = ffn_up_sample(x[0], P['norm_ffn'][layer], sh_f[0], sc_f[0], P['w_ffn_up'],
                                    P['ffn_conv_w'], P['ffn_conv_b'], layer,
                                    _to_time_major(states['ffn_conv'][layer].astype(F32)))
            act, f1 = act[None], _from_time_major(f1, seqs)
        else:
            act, f1 = ffn_up_prompt(x, P['norm_ffn'][layer], sh_f, sc_f, P['w_ffn_up'],
                                    P['ffn_conv_w'], P['ffn_conv_b'], layer, tm)
        new.setdefault('ffn_conv', []).append(f1)
        x = matmul_gate_residual(act, P['w_ffn_down'], layer, x, g_f, tm, 256)
    return x, {k: jnp.stack(v) for k, v in new.items()}


def kernel(x_prompt, x_sample, c_prompt, c_sample, cache_win_k, cache_win_v, state_gdn, state_gdn_conv,
           state_ffn_conv, rel_bias_table, w_ada, b_ada, norm_mix, norm_ffn, w_attn_qkv, attn_q_norm,
           attn_k_norm, attn_sinks, w_attn_o, w_gdn_in, gdn_conv_w, gdn_a_log, gdn_dt_bias, gdn_out_norm,
           w_gdn_out, w_ffn_up, ffn_conv_w, ffn_conv_b, w_ffn_down):
    P = dict(rel_bias_table=rel_bias_table, norm_mix=norm_mix, norm_ffn=norm_ffn, w_attn_qkv=w_attn_qkv,
             attn_q_norm=attn_q_norm, attn_k_norm=attn_k_norm, attn_sinks=attn_sinks, w_attn_o=w_attn_o,
             w_gdn_in=w_gdn_in, gdn_conv_w=gdn_conv_w, gdn_a_log=gdn_a_log, gdn_dt_bias=gdn_dt_bias,
             gdn_out_norm=gdn_out_norm, w_gdn_out=w_gdn_out, w_ffn_up=w_ffn_up, ffn_conv_w=ffn_conv_w,
             ffn_conv_b=ffn_conv_b, w_ffn_down=w_ffn_down)
    bp = x_prompt.shape[0]
    bs, ts, d = x_sample.shape
    n_c = bp + bs
    c_rows = -(-n_c // SUBLANES) * SUBLANES
    c_all = jnp.pad(jnp.concatenate([c_prompt, c_sample], axis=0), ((0, c_rows - n_c), (0, 0)))
    mod = ada_modulation(c_all, w_ada, b_ada)
    mod_prompt = mod[:, :bp]
    mod_sample = jnp.tile(mod[:, bp:n_c], (1, ts, 1))

    y_p, new_p = _trunk(x_prompt, mod_prompt, None, P, sample=False)
    states = dict(win_k=cache_win_k, win_v=cache_win_v, gdn=state_gdn, gdn_conv=state_gdn_conv,
                  ffn_conv=state_ffn_conv)
    y_s, new_s = _trunk(_to_time_major(x_sample)[None], mod_sample, states, P, sample=True)
    y_s = _from_time_major(y_s[0], bs)
    return (y_p, y_s, new_p['win_k'], new_p['win_v'], new_s['win_k'], new_s['win_v'],
            new_p['gdn'], new_s['gdn'], new_p['gdn_conv'], new_s['gdn_conv'],
            new_p['ffn_conv'], new_s['ffn_conv'])
```

```python
import functools
import math

import numpy as np
import jax
import jax.numpy as jnp
from jax import lax
from jax.experimental import pallas as pl
from jax.experimental.pallas import tpu as pltpu

HEAD_DIM = 64
N_HEADS = 32
N_KV_HEADS = 4
ATT_GROUP = N_HEADS // N_KV_HEADS
WINDOW = 128
ATT_BLOCK = 128
ATT_SAMPLE_SEQS = 8
N_BUCKETS = 32
MAX_DISTANCE = 128
NEG_INF = -1e30
PAST_LEN = 16384

GDN_K_HEADS = 16
GDN_V_HEADS = 32
GDN_DK = 128
GDN_DV = 128
GDN_KEY_DIM = GDN_K_HEADS * GDN_DK
GDN_VAL_DIM = GDN_V_HEADS * GDN_DV
GDN_CONV_DIM = 2 * GDN_KEY_DIM + GDN_VAL_DIM
GDN_CONV = 4
GDN_CHUNK = 64
GDN_HEAD_GROUPS = 4
GDN_VH_PER_GROUP = GDN_V_HEADS // GDN_HEAD_GROUPS
GDN_KH_PER_GROUP = GDN_K_HEADS // GDN_HEAD_GROUPS
GDN_PACK = 4
GDN_PHASE_A_CHUNKS = 4

FFN_CONV = 3
NORM_EPS = 1e-6

LANES = 128
SUBLANES = 8
VMEM_CAP_BYTES = 60 * 1024 * 1024
VMEM_SLACK_BYTES = 8 * 1024 * 1024

ROW_TILE_CAP = 1024
WEIGHT_CAST_K_PIECE = 512
GDN_IN_SUB_ROWS = 256
FFN_SUB_ROWS = 256
GDN_ROW_TILE_CAP = 512

BF16 = jnp.bfloat16
F32 = jnp.float32


def _vmem_limit(block_bytes, scratch_bytes=0):
    est = 2 * int(block_bytes) + int(scratch_bytes) + VMEM_SLACK_BYTES
    return int(min(max(est, 16 * 1024 * 1024), VMEM_CAP_BYTES))


def _params(n_grid, block_bytes, scratch_bytes=0):
    return pltpu.CompilerParams(
        dimension_semantics=("arbitrary",) * n_grid,
        vmem_limit_bytes=_vmem_limit(block_bytes, scratch_bytes))


def _nbytes(shape, dtype):
    return int(np.prod(shape)) * jnp.dtype(dtype).itemsize


def _silu(x):
    return x * (1.0 / (1.0 + jnp.exp2(x * (-math.log2(math.e)))))


def _dot(a, b):
    return jnp.dot(a, b, preferred_element_type=F32)


def _dot_nt(a, b):
    return lax.dot_general(a, b, (((1,), (1,)), ((), ())), preferred_element_type=F32)


def _dot_tn(a, b):
    return lax.dot_general(a, b, (((0,), (0,)), ((), ())), preferred_element_type=F32)


def _dot_f32_weight(lhs, w_ref):
    k = lhs.shape[1]
    step = WEIGHT_CAST_K_PIECE if k % WEIGHT_CAST_K_PIECE == 0 else k
    acc = None
    for k0 in range(0, k, step):
        part = _dot(lhs[:, k0:k0 + step], w_ref[k0:k0 + step, :].astype(BF16))
        acc = part if acc is None else acc + part
    return acc


def _norm_mod(x, gain, shift, scale):
    ms = jnp.mean(x * x, axis=-1, keepdims=True)
    y = x * lax.rsqrt(ms + NORM_EPS) * gain
    return y * (1.0 + scale) + shift


def _mod_specs(shift, tm):
    d = shift.shape[-1]
    if shift.shape[1] == 1:
        return pl.BlockSpec((1, 1, d), lambda b, i, j: (b, 0, 0))
    return pl.BlockSpec((1, tm, d), lambda b, i, j: (b, i, 0))


def _ada_kernel(c_ref, w_ref, b_ref, o_ref):
    a = _silu(c_ref[...]).astype(BF16)
    o_ref[0] = _dot(a, w_ref[0].astype(BF16)) + b_ref[0]


def ada_modulation(c_all, w_ada, b_ada, tn=1024):
    rows, d = c_all.shape
    depth, _, n = w_ada.shape
    blocks = _nbytes((rows, d), F32) + _nbytes((d, tn), F32) + _nbytes((rows, tn), F32)
    return pl.pallas_call(
        _ada_kernel,
        grid=(depth, n // tn),
        in_specs=[pl.BlockSpec((rows, d), lambda l, j: (0, 0)),
                  pl.BlockSpec((1, d, tn), lambda l, j: (l, 0, j)),
                  pl.BlockSpec((1, 1, tn), lambda l, j: (l, 0, j))],
        out_specs=pl.BlockSpec((1, rows, tn), lambda l, j: (l, 0, j)),
        out_shape=jax.ShapeDtypeStruct((depth, rows, n), F32),
        compiler_params=_params(2, blocks, _nbytes((d, tn), BF16)),
        name="ada_modulation",
    )(c_all, w_ada, b_ada.reshape(depth, 1, n))


def _qkv_kernel(x_ref, gain_ref, shift_ref, scale_ref, w_ref, hgain_ref, hflag_ref, gmat_ref,
                o_ref, h_s):
    @pl.when(pl.program_id(2) == 0)
    def _():
        h_s[...] = _norm_mod(x_ref[0], gain_ref[...], shift_ref[0], scale_ref[0]).astype(BF16)

    y = _dot(h_s[...], w_ref[...].astype(BF16))
    ms = _dot((y * y).astype(BF16), gmat_ref[...])
    yn = y * lax.rsqrt(ms + NORM_EPS) * hgain_ref[...]
    o_ref[0] = jnp.where(hflag_ref[...] > 0.0, yn, y)


def qkv_projection(x, gain, shift, scale, w, q_gain, k_gain, tm, tn=512):
    b, l, d = x.shape
    n = w.shape[1]
    nq, nk = N_HEADS * HEAD_DIM, N_KV_HEADS * HEAD_DIM
    hgain = jnp.concatenate([jnp.tile(q_gain, N_HEADS), jnp.tile(k_gain, N_KV_HEADS),
                             jnp.ones((nk,), F32)]).reshape(1, n)
    hflag = jnp.concatenate([jnp.ones((nq + nk,), F32), jnp.zeros((nk,), F32)]).reshape(1, n)
    gidx = np.arange(tn) // HEAD_DIM
    gmat = jnp.asarray((gidx[:, None] == gidx[None, :]).astype(np.float32) / HEAD_DIM, BF16)
    blocks = (_nbytes((tm, d), F32) + _nbytes((d, tn), F32) + _nbytes((tm, tn), F32)
              + 2 * _nbytes((shift.shape[1] == 1 and 1 or tm, d), F32) + _nbytes((tn, tn), BF16))
    scratch = _nbytes((tm, d), BF16) + _nbytes((d, tn), BF16) + 4 * _nbytes((tm, tn), F32)
    return pl.pallas_call(
        _qkv_kernel,
        grid=(b, l // tm, n // tn),
        in_specs=[pl.BlockSpec((1, tm, d), lambda b_, i, j: (b_, i, 0)),
                  pl.BlockSpec((1, d), lambda b_, i, j: (0, 0)),
                  _mod_specs(shift, tm), _mod_specs(scale, tm),
                  pl.BlockSpec((d, tn), lambda b_, i, j: (0, j)),
                  pl.BlockSpec((1, tn), lambda b_, i, j: (0, j)),
                  pl.BlockSpec((1, tn), lambda b_, i, j: (0, j)),
                  pl.BlockSpec((tn, tn), lambda b_, i, j: (0, 0))],
        out_specs=pl.BlockSpec((1, tm, tn), lambda b_, i, j: (b_, i, j)),
        out_shape=jax.ShapeDtypeStruct((b, l, n), F32),
        scratch_shapes=[pltpu.VMEM((tm, d), BF16)],
        compiler_params=_params(3, blocks, scratch),
        name="qkv_projection",
    )(x, gain.reshape(1, d), shift, scale, w, hgain, hflag, gmat)


def _attn_prompt_kernel(sink_ref, q_ref, cur_ref, prev_ref, bprev_ref, bcur_ref, o_ref, kz_s, vz_s):
    n = pl.program_id(1)
    blk = ATT_BLOCK
    lane = lax.broadcasted_iota(jnp.int32, (2 * blk, LANES), 1)
    lo_half = lane < HEAD_DIM
    neg_prev = jnp.where(n == 0, NEG_INF, 0.0).astype(F32)

    kv_width = N_KV_HEADS * HEAD_DIM
    for pair in range(N_KV_HEADS // 2):
        for part, dst in ((0, kz_s), (1, vz_s)):
            col = part * kv_width + pair * LANES
            both = jnp.concatenate([prev_ref[0, :, col:col + LANES], cur_ref[0, :, col:col + LANES]], axis=0)
            swapped = pltpu.roll(both, HEAD_DIM, axis=1)
            zero = jnp.zeros_like(both)
            c0, c1 = 2 * pair, 2 * pair + 1
            dst[2 * c0 + 0] = jnp.where(lo_half, both, zero).astype(BF16)
            dst[2 * c0 + 1] = jnp.where(lo_half, zero, swapped).astype(BF16)
            dst[2 * c1 + 0] = jnp.where(lo_half, swapped, zero).astype(BF16)
            dst[2 * c1 + 1] = jnp.where(lo_half, zero, both).astype(BF16)

    scale = HEAD_DIM ** -0.5 * math.log2(math.e)
    for p in range(N_HEADS // 2):
        c = (2 * p) // ATT_GROUP
        qp = (q_ref[0, :, p * LANES:(p + 1) * LANES] * scale).astype(BF16)
        o_pair = None
        for a in range(2):
            h = 2 * p + a
            sink = sink_ref[h]
            s = _dot_nt(qp, kz_s[2 * c + a])
            s_prev = s[:, :blk] + bprev_ref[h] + neg_prev
            s_cur = s[:, blk:] + bcur_ref[h]
            m = jnp.maximum(jnp.max(jnp.maximum(s_prev, s_cur), axis=-1, keepdims=True), sink)
            e_prev = jnp.exp2(s_prev - m)
            e_cur = jnp.exp2(s_cur - m)
            den = jnp.sum(e_prev + e_cur, axis=-1, keepdims=True) + jnp.exp2(sink - m)
            pm = jnp.concatenate([e_prev, e_cur], axis=1).astype(BF16)
            o_a = _dot(pm, vz_s[2 * c + a]) * (1.0 / den)
            o_pair = o_a if o_pair is None else o_pair + o_a
        o_ref[0, :, p * LANES:(p + 1) * LANES] = o_pair.astype(o_ref.dtype)


def _t5_bucket_np(dist):
    max_exact = N_BUCKETS // 2
    d = np.maximum(dist, 0)
    df = np.maximum(d, 1).astype(np.float32)
    large = max_exact + (np.log(df / np.float32(max_exact)) / np.float32(math.log(MAX_DISTANCE / max_exact))
                         * np.float32(N_BUCKETS - max_exact)).astype(np.int32)
    large = np.minimum(large, N_BUCKETS - 1)
    return np.where(d < max_exact, d, large)


def _bias_from_dist(dist, in_band, rel_table):
    onehot = (_t5_bucket_np(dist)[..., None] == np.arange(N_BUCKETS)).astype(np.float32)
    tab = jnp.einsum('qsb,bh->hqs', jnp.asarray(onehot), rel_table.astype(F32), precision=lax.Precision.HIGHEST)
    return jnp.where(jnp.asarray(in_band)[None], tab, NEG_INF)


def attention_prompt(qkv, sinks, rel_table):
    b, l, n = qkv.shape
    blk = ATT_BLOCK
    nq = N_HEADS * HEAD_DIM
    kvw = 2 * N_KV_HEADS * HEAD_DIM
    kv_blk = nq // kvw
    qi = np.arange(blk)[:, None]
    sj = np.arange(blk)[None, :]
    d_prev = qi + blk - sj
    d_cur = qi - sj
    log2e = math.log2(math.e)
    bias_prev = _bias_from_dist(d_prev, (d_prev >= 0) & (d_prev <= WINDOW), rel_table) * log2e
    bias_cur = _bias_from_dist(d_cur, (d_cur >= 0) & (d_cur <= WINDOW), rel_table) * log2e
    blocks = (_nbytes((blk, nq), F32) + 2 * _nbytes((blk, kvw), F32) + _nbytes((blk, nq), BF16))
    scratch = 2 * _nbytes((2 * N_KV_HEADS, 2 * blk, LANES), BF16) + 4 * _nbytes((N_HEADS, blk, blk), F32)
    return pl.pallas_call(
        _attn_prompt_kernel,
        grid=(b, l // blk),
        in_specs=[pl.BlockSpec(memory_space=pltpu.SMEM),
                  pl.BlockSpec((1, blk, nq), lambda b_, i: (b_, i, 0)),
                  pl.BlockSpec((1, blk, kvw), lambda b_, i: (b_, i, kv_blk)),
                  pl.BlockSpec((1, blk, kvw), lambda b_, i: (b_, jnp.maximum(i - 1, 0), kv_blk)),
                  pl.BlockSpec((N_HEADS, blk, blk), lambda b_, i: (0, 0, 0)),
                  pl.BlockSpec((N_HEADS, blk, blk), lambda b_, i: (0, 0, 0))],
        out_specs=pl.BlockSpec((1, blk, nq), lambda b_, i: (b_, i, 0)),
        out_shape=jax.ShapeDtypeStruct((b, l, nq), BF16),
        scratch_shapes=[pltpu.VMEM((2 * N_KV_HEADS, 2 * blk, LANES), BF16),
                        pltpu.VMEM((2 * N_KV_HEADS, 2 * blk, LANES), BF16)],
        compiler_params=_params(2, blocks, scratch),
        name="attention_prompt",
    )(sinks.astype(F32) * log2e, qkv, qkv, qkv, bias_prev, bias_cur)


def _attn_sample_kernel(q_ref, k_ref, v_ref, bias_ref, sink_ref, o_ref):
    pairs = [(b, c) for b in range(q_ref.shape[0]) for c in range(N_KV_HEADS)]
    k = [k_ref[b].astype(BF16) for b in range(q_ref.shape[0])]
    s = {bc: _dot_nt(q_ref[bc[0], bc[1]].astype(BF16), k[bc[0]]) + bias_ref[bc[1]] for bc in pairs}
    e, den = {}, {}
    for b, c in pairs:
        sink = sink_ref[c]
        m = jnp.maximum(jnp.max(s[(b, c)], axis=-1, keepdims=True), sink)
        e[(b, c)] = jnp.exp(s[(b, c)] - m)
        den[(b, c)] = jnp.sum(e[(b, c)], axis=-1, keepdims=True) + jnp.exp(sink - m)
    for b, c in pairs:
        o_ref[b, c] = _dot(e[(b, c)].astype(BF16), v_ref[b].astype(BF16)) * (1.0 / den[(b, c)])


def attention_sample(qkv, cache_k, cache_v, sinks, rel_table):
    bsz, w = cache_k.shape[0], cache_k.shape[1]
    t = qkv.shape[0] // bsz
    nq, nkv = N_HEADS * HEAD_DIM, N_KV_HEADS * HEAD_DIM
    keys = w + t
    keys_pad = -(-keys // 16) * 16
    rows = ATT_GROUP * t
    q = qkv[:, :nq].reshape(t, bsz, N_KV_HEADS, ATT_GROUP, HEAD_DIM) * HEAD_DIM ** -0.5
    q = jnp.transpose(q, (1, 2, 3, 0, 4)).reshape(bsz, N_KV_HEADS, rows, HEAD_DIM)
    qz = jnp.einsum('bcrd,ck->bcrkd', q, jnp.eye(N_KV_HEADS, dtype=F32)).reshape(bsz, N_KV_HEADS, rows, nkv)
    new_kv = jnp.transpose(qkv[:, nq:].reshape(t, bsz, 2 * nkv), (1, 0, 2))
    k_all = jnp.concatenate([cache_k.reshape(bsz, w, nkv), new_kv[:, :, :nkv]], axis=1)
    v_all = jnp.concatenate([cache_v.reshape(bsz, w, nkv), new_kv[:, :, nkv:]], axis=1)
    pad = ((0, 0), (0, keys_pad - keys), (0, 0))
    k_pad, v_pad = jnp.pad(k_all, pad), jnp.pad(v_all, pad)
    tq = np.arange(t)[:, None]
    sk = np.arange(keys_pad)[None, :]
    dist = tq + w - sk
    in_band = (dist >= 0) & (dist <= WINDOW) & (sk < keys)
    bias_t = _bias_from_dist(dist, in_band, rel_table)
    bias = bias_t.reshape(N_KV_HEADS, rows, keys_pad)
    sink_rows = jnp.repeat(sinks.astype(F32), t).reshape(N_KV_HEADS, rows, 1)
    sb = math.gcd(bsz, ATT_SAMPLE_SEQS)
    blocks = sb * (_nbytes((N_KV_HEADS, rows, nkv), F32) * 2 + 2 * _nbytes((keys_pad, nkv), F32))
    oz = pl.pallas_call(
        _attn_sample_kernel,
        grid=(bsz // sb,),
        in_specs=[pl.BlockSpec((sb, N_KV_HEADS, rows, nkv), lambda b_: (b_, 0, 0, 0)),
                  pl.BlockSpec((sb, keys_pad, nkv), lambda b_: (b_, 0, 0)),
                  pl.BlockSpec((sb, keys_pad, nkv), lambda b_: (b_, 0, 0)),
                  pl.BlockSpec((N_KV_HEADS, rows, keys_pad), lambda b_: (0, 0, 0)),
                  pl.BlockSpec((N_KV_HEADS, rows, 1), lambda b_: (0, 0, 0))],
        out_specs=pl.BlockSpec((sb, N_KV_HEADS, rows, nkv), lambda b_: (b_, 0, 0, 0)),
        out_shape=jax.ShapeDtypeStruct((bsz, N_KV_HEADS, rows, nkv), F32),
        compiler_params=_params(1, blocks),
        name="attention_sample",
    )(qz, k_pad, v_pad, bias, sink_rows)
    o = jnp.stack([oz[:, c, :, c * HEAD_DIM:(c + 1) * HEAD_DIM] for c in range(N_KV_HEADS)], axis=1)
    o = o.reshape(bsz, N_KV_HEADS, ATT_GROUP, t, HEAD_DIM)
    o = jnp.transpose(o, (3, 0, 1, 2, 4)).reshape(t * bsz, nq)
    return o.astype(BF16), k_all[:, t:], v_all[:, t:]


def _mm_res_kernel(a_ref, w_ref, x_ref, g_ref, o_ref):
    y = _dot_f32_weight(a_ref[0], w_ref)
    o_ref[0] = x_ref[0] + g_ref[0] * y


def matmul_gate_residual(a, w_stack, layer, x, gate, tm, tn):
    b, l, k = a.shape
    n = w_stack.shape[2]
    grows = 1 if gate.shape[1] == 1 else tm
    if gate.shape[1] == 1:
        g_spec = pl.BlockSpec((1, 1, tn), lambda b_, i, j: (b_, 0, j))
    else:
        g_spec = pl.BlockSpec((1, tm, tn), lambda b_, i, j: (b_, i, j))
    blocks = (_nbytes((tm, k), a.dtype) + _nbytes((k, tn), F32) + 2 * _nbytes((tm, tn), F32)
              + _nbytes((grows, tn), F32))
    scratch = _nbytes((k, tn), BF16) + _nbytes((tm, tn), F32)
    return pl.pallas_call(
        _mm_res_kernel,
        grid=(b, l // tm, n // tn),
        in_specs=[pl.BlockSpec((1, tm, k), lambda b_, i, j: (b_, i, 0)),
                  pl.BlockSpec((None, k, tn), lambda b_, i, j: (layer, 0, j)),
                  pl.BlockSpec((1, tm, tn), lambda b_, i, j: (b_, i, j)),
                  g_spec],
        out_specs=pl.BlockSpec((1, tm, tn), lambda b_, i, j: (b_, i, j)),
        out_shape=jax.ShapeDtypeStruct((b, l, n), F32),
        compiler_params=_params(3, blocks, scratch),
        name="matmul_gate_residual",
    )(a, w_stack, x, gate)


def _ffn_conv_rows(u, prev1, prev2, cw_ref, cb_ref):
    return u * cw_ref[2:3, :] + prev1 * cw_ref[1:2, :] + prev2 * cw_ref[0:1, :] + cb_ref[...]


def _ffn_up_prompt_kernel(x_ref, gain_ref, shift_ref, scale_ref, wg_ref, wv_ref, cwg_ref, cwv_ref,
                          cbg_ref, cbv_ref, o_ref, tail_ref, h_s, carry_s, ext_s):
    i = pl.program_id(1)
    j = pl.program_id(2)
    tm = h_s.shape[0]

    @pl.when(j == 0)
    def _():
        h_s[...] = _norm_mod(x_ref[0], gain_ref[...], shift_ref[0], scale_ref[0]).astype(BF16)

    @pl.when(i == 0)
    def _():
        carry_s[j] = jnp.zeros(carry_s.shape[1:], F32)

    halves = ((wg_ref, cwg_ref, cbg_ref), (wv_ref, cwv_ref, cbv_ref))
    w_bf = [w_ref[...].astype(BF16) for w_ref, _, _ in halves]
    for half in range(2):
        ext_s[half, 0:SUBLANES, :] = carry_s[j, half]
    sub = min(FFN_SUB_ROWS, tm)
    for m in range(tm // sub):
        hm = h_s[m * sub:(m + 1) * sub, :]
        r0 = SUBLANES + m * sub
        ys = []
        for half, (_, cw_ref, cb_ref) in enumerate(halves):
            u = _dot(hm, w_bf[half])
            ext_s[half, r0:r0 + sub, :] = u
            ys.append(_ffn_conv_rows(u, ext_s[half, r0 - 1:r0 - 1 + sub, :], ext_s[half, r0 - 2:r0 - 2 + sub, :],
                                     cw_ref, cb_ref))
        o_ref[0, m * sub:(m + 1) * sub, :] = (_silu(ys[0]) * ys[1]).astype(o_ref.dtype)
    for half in range(2):
        tail = ext_s[half, tm:tm + SUBLANES, :]
        carry_s[j, half] = tail
        tail_ref[0, 0, half] = tail


def ffn_up_prompt(x, gain, shift, scale, w_up, conv_w, conv_b, layer, tm, tn=512):
    b, l, d = x.shape
    dff = w_up.shape[2] // 2
    nj = dff // tn
    conv_b = conv_b.reshape(conv_b.shape[0], 1, 2 * dff)
    blocks = (_nbytes((tm, d), F32) + 2 * _nbytes((d, tn), F32) + _nbytes((tm, tn), BF16)
              + 2 * _nbytes((1, d), F32) + 8 * _nbytes((SUBLANES, tn), F32))
    scratch = (_nbytes((tm, d), BF16) + _nbytes((nj, 2, SUBLANES, tn), F32) + 2 * _nbytes((d, tn), BF16)
               + 8 * _nbytes((tm, tn), F32))
    act, tail = pl.pallas_call(
        _ffn_up_prompt_kernel,
        grid=(b, l // tm, nj),
        in_specs=[pl.BlockSpec((1, tm, d), lambda b_, i, j: (b_, i, 0)),
                  pl.BlockSpec((1, d), lambda b_, i, j: (0, 0)),
                  _mod_specs(shift, tm), _mod_specs(scale, tm),
                  pl.BlockSpec((None, d, tn), lambda b_, i, j: (layer, 0, j)),
                  pl.BlockSpec((None, d, tn), lambda b_, i, j: (layer, 0, j + nj)),
                  pl.BlockSpec((None, FFN_CONV, tn), lambda b_, i, j: (layer, 0, j)),
                  pl.BlockSpec((None, FFN_CONV, tn), lambda b_, i, j: (layer, 0, j + nj)),
                  pl.BlockSpec((None, 1, tn), lambda b_, i, j: (layer, 0, j)),
                  pl.BlockSpec((None, 1, tn), lambda b_, i, j: (layer, 0, j + nj))],
        out_specs=[pl.BlockSpec((1, tm, tn), lambda b_, i, j: (b_, i, j)),
                   pl.BlockSpec((1, 1, 2, SUBLANES, tn), lambda b_, i, j: (b_, i, 0, 0, j))],
        out_shape=[jax.ShapeDtypeStruct((b, l, dff), BF16),
                   jax.ShapeDtypeStruct((b, l // tm, 2, SUBLANES, dff), F32)],
        scratch_shapes=[pltpu.VMEM((tm, d), BF16), pltpu.VMEM((nj, 2, SUBLANES, tn), F32),
                        pltpu.VMEM((2, SUBLANES + tm, tn), F32)],
        compiler_params=_params(3, blocks, scratch),
        name="ffn_up_prompt",
    )(x, gain.reshape(1, d), shift, scale, w_up, w_up, conv_w, conv_w, conv_b, conv_b)
    keep = FFN_CONV - 1
    hist = jnp.transpose(tail[:, -1, :, SUBLANES - keep:, :], (0, 2, 1, 3)).reshape(b, keep, 2 * dff)
    return act, hist


def _ffn_up_sample_kernel(x_ref, gain_ref, shift_ref, scale_ref, wg_ref, wv_ref, cwg_ref, cwv_ref,
                          cbg_ref, cbv_ref, hg_ref, hv_ref, o_ref, ng_ref, nv_ref, h_s, *, bsz):
    @pl.when(pl.program_id(0) == 0)
    def _():
        h_s[...] = _norm_mod(x_ref[...], gain_ref[...], shift_ref[...], scale_ref[...]).astype(BF16)

    h = h_s[...]
    rows = h_s.shape[0]
    keep = FFN_CONV - 1
    ys = []
    for w_ref, cw_ref, cb_ref, hist_ref, new_ref in ((wg_ref, cwg_ref, cbg_ref, hg_ref, ng_ref),
                                                     (wv_ref, cwv_ref, cbv_ref, hv_ref, nv_ref)):
        u = _dot(h, w_ref[...].astype(BF16))
        ext = jnp.concatenate([hist_ref[...], u], axis=0)
        prev = [ext[(keep - k) * bsz:(keep - k) * bsz + rows] for k in (1, 2)]
        ys.append(_ffn_conv_rows(u, prev[0], prev[1], cw_ref, cb_ref))
        new_ref[...] = ext[rows:rows + keep * bsz]
    o_ref[...] = (_silu(ys[0]) * ys[1]).astype(o_ref.dtype)


def ffn_up_sample(x, gain, shift, scale, w_up, conv_w, conv_b, layer, hist, tn=512):
    rows, d = x.shape
    dff = w_up.shape[2] // 2
    nj = dff // tn
    keep = FFN_CONV - 1
    bsz = hist.shape[0] // keep
    conv_b = conv_b.reshape(conv_b.shape[0], 1, 2 * dff)
    col = lambda j: (0, j)
    col_hi = lambda j: (0, j + nj)
    lcol = lambda j: (layer, 0, j)
    lcol_hi = lambda j: (layer, 0, j + nj)
    blocks = (_nbytes((rows, d), F32) * 3 + 2 * _nbytes((d, tn), F32) + 5 * _nbytes((rows, tn), F32))
    scratch = _nbytes((rows, d), BF16) + 2 * _nbytes((d, tn), BF16) + 8 * _nbytes((rows, tn), F32)
    act, new_g, new_v = pl.pallas_call(
        functools.partial(_ffn_up_sample_kernel, bsz=bsz),
        grid=(nj,),
        in_specs=[pl.BlockSpec((rows, d), lambda j: (0, 0)),
                  pl.BlockSpec((1, d), lambda j: (0, 0)),
                  pl.BlockSpec((rows, d), lambda j: (0, 0)),
                  pl.BlockSpec((rows, d), lambda j: (0, 0)),
                  pl.BlockSpec((None, d, tn), lcol), pl.BlockSpec((None, d, tn), lcol_hi),
                  pl.BlockSpec((None, FFN_CONV, tn), lcol), pl.BlockSpec((None, FFN_CONV, tn), lcol_hi),
                  pl.BlockSpec((None, 1, tn), lcol), pl.BlockSpec((None, 1, tn), lcol_hi),
                  pl.BlockSpec((keep * bsz, tn), col), pl.BlockSpec((keep * bsz, tn), col_hi)],
        out_specs=[pl.BlockSpec((rows, tn), col), pl.BlockSpec((keep * bsz, tn), col),
                   pl.BlockSpec((keep * bsz, tn), col)],
        out_shape=[jax.ShapeDtypeStruct((rows, dff), BF16),
                   jax.ShapeDtypeStruct((keep * bsz, dff), F32),
                   jax.ShapeDtypeStruct((keep * bsz, dff), F32)],
        scratch_shapes=[pltpu.VMEM((rows, d), BF16)],
        compiler_params=_params(1, blocks, scratch),
        name="ffn_up_sample",
    )(x, gain.reshape(1, d), shift, scale, w_up, w_up, conv_w, conv_w, conv_b, conv_b, hist, hist)
    return act, jnp.concatenate([new_g, new_v], axis=-1)


def _gdn_in_kernel(*refs, n_qk_tiles, n_conv_tiles, bsz):
    if bsz is None:
        (x_ref, gain_ref, shift_ref, scale_ref, w_ref, wba_ref, alog_ref, dtb_ref, cw_ref,
         o_ref, bg_ref, raw_ref, h_s, carry_s) = refs
    else:
        (x_ref, gain_ref, shift_ref, scale_ref, w_ref, wba_ref, alog_ref, dtb_ref, cw_ref, hist_ref,
         o_ref, bg_ref, raw_ref, h_s) = refs
    i = pl.program_id(1)
    j = pl.program_id(2)
    tm = h_s.shape[0]
    width = GDN_CONV

    @pl.when(j == 0)
    def _():
        h = _norm_mod(x_ref[0], gain_ref[...], shift_ref[0], scale_ref[0]).astype(BF16)
        h_s[...] = h
        y = _dot_nt(h, wba_ref[...].astype(BF16))
        lane = lax.broadcasted_iota(jnp.int32, y.shape, 1) % LANES
        beta = 1.0 / (1.0 + jnp.exp(-y))
        a = y + dtb_ref[...]
        softplus = jnp.maximum(a, 0.0) + jnp.log1p(jnp.exp(-jnp.abs(a)))
        g = -jnp.exp(alog_ref[...]) * softplus
        bg_ref[0] = jnp.where(lane < GDN_VH_PER_GROUP, beta, jnp.where(lane < 2 * GDN_VH_PER_GROUP, g, 0.0))

    sub = min(GDN_IN_SUB_ROWS, tm)
    row8 = lax.broadcasted_iota(jnp.int32, (SUBLANES, 1), 0)

    def conv_tile(l2_scale):
        w_bf = w_ref[...].astype(BF16)
        jc = jnp.minimum(j, n_conv_tiles - 1)
        if bsz is None:
            @pl.when(i == 0)
            def _():
                carry_s[jc] = jnp.zeros(carry_s.shape[1:], F32)
            prev_tail = carry_s[jc]
        for m in range(tm // sub):
            rs = slice(m * sub, (m + 1) * sub)
            u = _dot_nt(h_s[rs, :], w_bf)
            y = u * cw_ref[width - 1:width, :]
            if bsz is None:
                top = u[0:SUBLANES]
                y_top = top * cw_ref[width - 1:width, :]
                for k in range(1, width):
                    wk = cw_ref[width - 1 - k:width - k, :]
                    y = y + pltpu.roll(u, k, axis=0) * wk
                    y_top = y_top + jnp.where(row8 < k, pltpu.roll(prev_tail, k, axis=0),
                                              pltpu.roll(top, k, axis=0)) * wk
                y = jnp.concatenate([y_top, y[SUBLANES:]], axis=0)
                prev_tail = u[sub - SUBLANES:sub]
            else:
                ext = jnp.concatenate([hist_ref[...], u], axis=0)
                for k in range(1, width):
                    wk = cw_ref[width - 1 - k:width - k, :]
                    y = y + ext[(width - 1 - k) * bsz:(width - 1 - k) * bsz + sub] * wk
                raw_ref[...] = ext[sub:sub + (width - 1) * bsz]
            y = _silu(y)
            if l2_scale is not None:
                heads = [y[:, hd * LANES:(hd + 1) * LANES] for hd in range(y.shape[1] // LANES)]
                y = jnp.concatenate(
                    [yh * (lax.rsqrt(jnp.sum(yh * yh, axis=-1, keepdims=True) + NORM_EPS) * l2_scale)
                     for yh in heads], axis=1)
            o_ref[0, rs, :] = y.astype(o_ref.dtype)
        if bsz is None:
            carry_s[jc] = prev_tail
            raw_ref[0, 0] = prev_tail

    @pl.when(j < n_qk_tiles)
    def _():
        conv_tile(jnp.where(j < n_qk_tiles // 2, GDN_DK ** -0.5, 1.0).astype(F32))

    @pl.when((j >= n_qk_tiles) & (j < n_conv_tiles))
    def _():
        conv_tile(None)

    @pl.when(j >= n_conv_tiles)
    def _():
        w_bf = w_ref[...].astype(BF16)
        for m in range(tm // sub):
            rs = slice(m * sub, (m + 1) * sub)
            o_ref[0, rs, :] = _silu(_dot_nt(h_s[rs, :], w_bf)).astype(o_ref.dtype)


def _group_lane_layout(vec_b, vec_a):
    lead = vec_b.shape[:-1]
    vb = vec_b.reshape(*lead, GDN_HEAD_GROUPS, GDN_VH_PER_GROUP)
    va = vec_a.reshape(*lead, GDN_HEAD_GROUPS, GDN_VH_PER_GROUP)
    pad = jnp.zeros((*lead, GDN_HEAD_GROUPS, LANES - 2 * GDN_VH_PER_GROUP), vec_b.dtype)
    return jnp.concatenate([vb, va, pad], axis=-1).reshape(*lead, GDN_HEAD_GROUPS * LANES)


def gdn_in_projection(x, gain, shift, scale, w_in, conv_w, a_log, dt_bias, tm, conv_hist=None, tn=512):
    b, l, d = x.shape
    n_main = GDN_CONV_DIM + GDN_VAL_DIM
    w_t = jnp.transpose(w_in)
    wba = _group_lane_layout(w_in[:, n_main:n_main + GDN_V_HEADS], w_in[:, n_main + GDN_V_HEADS:])
    wba_t = jnp.transpose(wba)
    zeros = jnp.zeros((1, GDN_V_HEADS), F32)
    alog = _group_lane_layout(zeros, a_log.reshape(1, -1).astype(F32))
    dtb = _group_lane_layout(zeros, dt_bias.reshape(1, -1).astype(F32))
    nbg = wba_t.shape[0]
    n_conv_tiles = GDN_CONV_DIM // tn
    n_qk_tiles = 2 * GDN_KEY_DIM // tn
    conv_col = lambda b_, i, j: (0, jnp.minimum(j, n_conv_tiles - 1))
    in_specs = [pl.BlockSpec((1, tm, d), lambda b_, i, j: (b_, i, 0)),
                pl.BlockSpec((1, d), lambda b_, i, j: (0, 0)),
                _mod_specs(shift, tm), _mod_specs(scale, tm),
                pl.BlockSpec((tn, d), lambda b_, i, j: (j, 0)),
                pl.BlockSpec((nbg, d), lambda b_, i, j: (0, 0)),
                pl.BlockSpec((1, nbg), lambda b_, i, j: (0, 0)),
                pl.BlockSpec((1, nbg), lambda b_, i, j: (0, 0)),
                pl.BlockSpec((GDN_CONV, tn), conv_col)]
    args = [x, gain.reshape(1, d), shift, scale, w_t, wba_t, alog, dtb, conv_w]
    scratch_shapes = [pltpu.VMEM((tm, d), BF16)]
    blocks = (_nbytes((tm, d), F32) + _nbytes((tn, d), F32) + _nbytes((tm, tn), BF16)
              + _nbytes((nbg, d), F32) + _nbytes((tm, nbg), F32) + 2 * _nbytes((1, d), F32))
    scratch = _nbytes((tm, d), BF16) + _nbytes((tn, d), BF16) + 6 * _nbytes((tm, nbg), F32)
    if conv_hist is None:
        bsz = None
        raw_spec = pl.BlockSpec((1, 1, SUBLANES, tn), lambda b_, i, j: (b_, i, 0, jnp.minimum(j, n_conv_tiles - 1)))
        raw_shape = jax.ShapeDtypeStruct((b, l // tm, SUBLANES, GDN_CONV_DIM), F32)
        scratch_shapes.append(pltpu.VMEM((n_conv_tiles, SUBLANES, tn), F32))
    else:
        assert b == 1 and l == tm <= GDN_IN_SUB_ROWS
        hist_rows = conv_hist.shape[0]
        bsz = hist_rows // (GDN_CONV - 1)
        args.append(conv_hist)
        in_specs.append(pl.BlockSpec((hist_rows, tn), conv_col))
        raw_spec = pl.BlockSpec((hist_rows, tn), conv_col)
        raw_shape = jax.ShapeDtypeStruct((hist_rows, GDN_CONV_DIM), F32)
        blocks += 2 * _nbytes((hist_rows, tn), F32)
    return pl.pallas_call(
        functools.partial(_gdn_in_kernel, n_qk_tiles=n_qk_tiles, n_conv_tiles=n_conv_tiles, bsz=bsz),
        grid=(b, l // tm, n_main // tn),
        in_specs=in_specs,
        out_specs=[pl.BlockSpec((1, tm, tn), lambda b_, i, j: (b_, i, j)),
                   pl.BlockSpec((1, tm, nbg), lambda b_, i, j: (b_, i, 0)),
                   raw_spec],
        out_shape=[jax.ShapeDtypeStruct((b, l, n_main), BF16),
                   jax.ShapeDtypeStruct((b, l, nbg), F32),
                   raw_shape],
        scratch_shapes=scratch_shapes,
        compiler_params=_params(3, blocks, scratch),
        name="gdn_in_projection",
    )(*args)


def _gdn_core_kernel(q_ref, k_ref, v_ref, zg_ref, bg_ref, onorm_ref, s0_ref, o_ref, s1_ref,
                     s_s, gc_s, tdec_s, mo_s, *, seq_rows):
    r = pl.program_id(2)
    n_r = pl.num_programs(2)
    rows = q_ref.shape[1]
    chunk = GDN_CHUNK
    vpg = GDN_VH_PER_GROUP
    rep = GDN_V_HEADS // GDN_K_HEADS
    n_kh = GDN_KH_PER_GROUP
    seg = chunk if seq_rows is None else seq_rows
    n_seq = chunk // seg

    def load_states():
        for s in range(n_seq):
            for kh in range(n_kh):
                s_s[s * n_kh + kh] = jnp.concatenate([s0_ref[s, kh * rep + e] for e in range(rep)], axis=1)

    if seq_rows is None:
        pl.when(r == 0)(load_states)
    else:
        load_states()

    def seg_last(x):
        if n_seq == 1:
            return x[chunk - 1:chunk, :]
        blocks = x.reshape(n_seq, seg, x.shape[1])
        return jnp.broadcast_to(blocks[:, seg - 1:seg, :], blocks.shape).reshape(x.shape)

    n_chunks = rows // chunk
    pw = GDN_PACK * chunk
    n_packs = vpg // GDN_PACK
    ri = lax.broadcasted_iota(jnp.int32, (chunk, pw), 0)
    li = lax.broadcasted_iota(jnp.int32, (chunk, pw), 1) % chunk
    seg_shift = int(math.log2(seg))
    same_seq = (ri >> seg_shift) == (li >> seg_shift)
    tril_p = (ri >= li) & same_seq
    strict_p = (ri > li) & same_seq
    eye_p = (ri == li).astype(F32)
    n_levels = seg_shift
    off_masks = [((ri >> (lvl + 1)) == (li >> (lvl + 1))) & ((ri >> lvl) != (li >> lvl))
                 for lvl in range(n_levels)]
    bd_rows = lax.broadcasted_iota(jnp.int32, (pw, pw), 0) // chunk
    bd_cols = lax.broadcasted_iota(jnp.int32, (pw, pw), 1) // chunk
    bd_mask = (bd_rows == bd_cols).astype(BF16)
    row_c = lax.broadcasted_iota(jnp.int32, (chunk, LANES), 0)
    lane_lo = lax.broadcasted_iota(jnp.int32, (chunk, LANES), 1) < chunk

    def block_diag(xp):
        return jnp.concatenate([xp.astype(BF16)] * GDN_PACK, axis=0) * bd_mask

    def pack_cols(arr, first_col):
        tiles = []
        for t in range(pw // LANES):
            even = jnp.broadcast_to(arr[:, first_col + 2 * t:first_col + 2 * t + 1], (chunk, LANES))
            odd = jnp.broadcast_to(arr[:, first_col + 2 * t + 1:first_col + 2 * t + 2], (chunk, LANES))
            tiles.append(jnp.where(lane_lo, even, odd))
        return jnp.concatenate(tiles, axis=1)

    for c0 in range(0, n_chunks, GDN_PHASE_A_CHUNKS):
        group = [(c, p) for c in range(c0, min(c0 + GDN_PHASE_A_CHUNKS, n_chunks)) for p in range(n_packs)]
        a_list, qkd, brow, dec = {}, {}, {}, {}
        for c in range(c0, min(c0 + GDN_PHASE_A_CHUNKS, n_chunks)):
            rs = slice(c * chunk, (c + 1) * chunk)
            bg = bg_ref[0, rs, :]
            cum = bg
            shift = 1
            while shift < seg:
                cum = cum + jnp.where((row_c & (seg - 1)) >= shift, pltpu.roll(cum, shift, axis=0), 0.0)
                shift *= 2
            gc_s[rs, :] = cum
            gq = []
            for kh in range(GDN_KH_PER_GROUP):
                ksl = slice(kh * LANES, (kh + 1) * LANES)
                k = k_ref[0, rs, ksl]
                kq = jnp.concatenate([k, q_ref[0, rs, ksl]], axis=0)
                kk = jnp.concatenate([k, k], axis=0)
                gq.append(_dot_nt(kq, kk))
            for p in range(n_packs):
                khs = [(p * GDN_PACK + 2 * t) // rep for t in range(pw // LANES)]
                gram = jnp.concatenate([gq[kh][:chunk] for kh in khs], axis=1)
                qk = jnp.concatenate([gq[kh][chunk:] for kh in khs], axis=1)
                beta_p = pack_cols(bg, p * GDN_PACK)
                gcol_p = pack_cols(cum, vpg + p * GDN_PACK)
                grow_p = jnp.sum(gcol_p * eye_p, axis=0, keepdims=True)
                decay = jnp.exp(jnp.where(tril_p, gcol_p - grow_p, NEG_INF))
                a_list[(c, p)] = jnp.where(strict_p, gram * beta_p * decay, 0.0)
                qkd[(c, p)] = (qk * decay).astype(BF16)
                brow[(c, p)] = jnp.sum(beta_p * eye_p, axis=0, keepdims=True)
                dec[(c, p)] = jnp.exp(seg_last(gcol_p) - gcol_p)
        inv = {cp: eye_p - jnp.where(off_masks[0], a_list[cp], 0.0) for cp in group}
        for lvl in range(1, n_levels):
            w = {cp: _dot(jnp.where(off_masks[lvl], a_list[cp], 0.0).astype(BF16), block_diag(inv[cp]))
                 for cp in group}
            inv = {cp: inv[cp] - _dot(inv[cp].astype(BF16), block_diag(w[cp])) for cp in group}
        for cp in group:
            t_beta = inv[cp] * brow[cp]
            tdec_s[cp[0], cp[1]] = block_diag(dec[cp] * t_beta)
            mo_s[cp[0], cp[1]] = block_diag(_dot(qkd[cp], block_diag(t_beta)))

    def chunk_body(c, carry):
        rs = pl.ds(pl.multiple_of(c * chunk, chunk), chunk)
        cum = gc_s[rs, :]
        eg, g_last, ks, qs = [], [], [], []
        for kh in range(n_kh):
            ksl = slice(kh * LANES, (kh + 1) * LANES)
            kq = jnp.concatenate([k_ref[0, rs, ksl], q_ref[0, rs, ksl]], axis=0)
            per_seq = [_dot(kq, s_s[s * n_kh + kh].astype(BF16)) for s in range(n_seq)]
            kq_s = jnp.concatenate([per_seq[s][half * chunk + s * seg:half * chunk + (s + 1) * seg]
                                    for half in range(2) for s in range(n_seq)], axis=0)
            for e in range(rep):
                hv = kh * rep + e
                gcol = cum[:, vpg + hv:vpg + hv + 1]
                g_last.append([gcol[(s + 1) * seg - 1:(s + 1) * seg, :] for s in range(n_seq)])
                eg.append(jnp.exp(gcol))
                ks.append(kq_s[:chunk, e * LANES:(e + 1) * LANES] * eg[hv])
                qs.append(kq_s[chunk:, e * LANES:(e + 1) * LANES] * eg[hv])
        v_dec, o_intra = [], []
        for p in range(n_packs):
            heads = range(p * GDN_PACK, (p + 1) * GDN_PACK)
            rhs = jnp.concatenate([v_ref[0, rs, hv * LANES:(hv + 1) * LANES].astype(F32) - ks[hv]
                                   for hv in heads], axis=0).astype(BF16)
            vd = _dot(tdec_s[c, p], rhs)
            oi = _dot(mo_s[c, p], rhs)
            for j in range(GDN_PACK):
                v_dec.append(vd[j * chunk:(j + 1) * chunk])
                o_intra.append(oi[j * chunk:(j + 1) * chunk])
        for hv in range(vpg):
            vsl = slice(hv * LANES, (hv + 1) * LANES)
            o = qs[hv] + o_intra[hv]
            on = o * lax.rsqrt(jnp.mean(o * o, axis=-1, keepdims=True) + NORM_EPS) * onorm_ref[...]
            o_ref[0, rs, vsl] = (on * zg_ref[0, rs, vsl].astype(F32)).astype(o_ref.dtype)
        lane_v = lax.broadcasted_iota(jnp.int32, (1, rep * LANES), 1) // LANES
        row_seq = lax.broadcasted_iota(jnp.int32, (chunk, 1), 0) >> seg_shift
        for kh in range(n_kh):
            heads = range(kh * rep, (kh + 1) * rep)
            k_bf = k_ref[0, rs, kh * LANES:(kh + 1) * LANES]
            vd_pair = jnp.concatenate([v_dec[hv] for hv in heads], axis=1)
            for s in range(n_seq):
                vd_s = vd_pair if n_seq == 1 else jnp.where(row_seq == s, vd_pair, 0.0)
                ds = _dot_tn(k_bf, vd_s.astype(BF16))
                keep = jnp.exp(g_last[kh * rep][s])
                for e in range(1, rep):
                    keep = jnp.where(lane_v == e, jnp.exp(g_last[kh * rep + e][s]), keep)
                s_s[s * n_kh + kh] = s_s[s * n_kh + kh] * keep + ds
        return carry

    lax.fori_loop(0, n_chunks, chunk_body, 0)

    def store_states():
        for s in range(n_seq):
            for hv in range(vpg):
                s1_ref[s, hv] = s_s[s * n_kh + hv // rep][:, (hv % rep) * LANES:(hv % rep + 1) * LANES]

    if seq_rows is None:
        pl.when(r == n_r - 1)(store_states)
    else:
        store_states()


def gdn_core(qkvz, bg, s0, out_norm, rows, seq_rows=None):
    b, l, _ = qkvz.shape
    n_state = 1 if seq_rows is None else GDN_CHUNK // seq_rows
    if seq_rows is None:
        state_idx = lambda b_, g, r: (b_, g, 0, 0)
    else:
        assert b == 1 and rows == GDN_CHUNK
        state_idx = lambda b_, g, r: (r, g, 0, 0)
    hg = GDN_HEAD_GROUPS
    qw = GDN_KEY_DIM // hg
    vw = GDN_VAL_DIM // hg
    kq = GDN_KEY_DIM // qw
    kv = 2 * GDN_KEY_DIM // vw
    kz = GDN_CONV_DIM // vw
    vpg = GDN_VH_PER_GROUP
    blocks = (2 * _nbytes((rows, qw), BF16) + 3 * _nbytes((rows, vw), BF16) + _nbytes((rows, LANES), F32)
              + 2 * _nbytes((n_state, vpg, GDN_DK, GDN_DV), F32))
    n_chunks = rows // GDN_CHUNK
    n_packs = vpg // GDN_PACK
    pw = GDN_PACK * GDN_CHUNK
    scratch = (_nbytes((n_state, vpg, GDN_DK, GDN_DV), F32) + _nbytes((rows, LANES), F32)
               + 2 * _nbytes((n_chunks, n_packs, pw, pw), BF16) + 6 * _nbytes((rows, vw), F32))
    return pl.pallas_call(
        functools.partial(_gdn_core_kernel, seq_rows=seq_rows),
        grid=(b, hg, l // rows),
        in_specs=[pl.BlockSpec((1, rows, qw), lambda b_, g, r: (b_, r, g)),
                  pl.BlockSpec((1, rows, qw), lambda b_, g, r: (b_, r, kq + g)),
                  pl.BlockSpec((1, rows, vw), lambda b_, g, r: (b_, r, kv + g)),
                  pl.BlockSpec((1, rows, vw), lambda b_, g, r: (b_, r, kz + g)),
                  pl.BlockSpec((1, rows, LANES), lambda b_, g, r: (b_, r, g)),
                  pl.BlockSpec((1, GDN_DV), lambda b_, g, r: (0, 0)),
                  pl.BlockSpec((n_state, vpg, GDN_DK, GDN_DV), state_idx)],
        out_specs=[pl.BlockSpec((1, rows, vw), lambda b_, g, r: (b_, r, g)),
                   pl.BlockSpec((n_state, vpg, GDN_DK, GDN_DV), state_idx)],
        out_shape=[jax.ShapeDtypeStruct((b, l, GDN_VAL_DIM), BF16),
                   jax.ShapeDtypeStruct(s0.shape, F32)],
        scratch_shapes=[pltpu.VMEM((n_state * GDN_KH_PER_GROUP, GDN_DK,
                                    (GDN_V_HEADS // GDN_K_HEADS) * GDN_DV), F32),
                        pltpu.VMEM((rows, LANES), F32),
                        pltpu.VMEM((n_chunks, n_packs, pw, pw), BF16),
                        pltpu.VMEM((n_chunks, n_packs, pw, pw), BF16)],
        compiler_params=_params(3, blocks, scratch),
        name="gdn_core",
    )(qkvz, qkvz, qkvz, qkvz, bg, out_norm.reshape(1, GDN_DV).astype(F32), s0)


def _tile_rows(l, cap):
    t = min(l, cap)
    while l % t:
        t //= 2
    return t


def _to_time_major(a):
    return jnp.transpose(a, (1, 0, 2)).reshape(a.shape[0] * a.shape[1], a.shape[2])


def _from_time_major(a, bsz):
    return jnp.transpose(a.reshape(a.shape[0] // bsz, bsz, a.shape[1]), (1, 0, 2))


def _trunk(x, mod, states, P, sample):
    bsz, l, d = x.shape
    tm = _tile_rows(l, ROW_TILE_CAP)
    new = {}
    depth = mod.shape[0]
    for layer in range(depth):
        if sample:
            parts = [mod[layer][None, :, k * d:(k + 1) * d] for k in range(6)]
        else:
            parts = [mod[layer][:, None, k * d:(k + 1) * d] for k in range(6)]
        sh_m, sc_m, g_m, sh_f, sc_f, g_f = parts
        i = layer // 2
        if layer % 2 == 0:
            qkv = qkv_projection(x, P['norm_mix'][layer], sh_m, sc_m, P['w_attn_qkv'][i],
                                 P['attn_q_norm'][i], P['attn_k_norm'][i], tm)
            nq, nkv = N_HEADS * HEAD_DIM, N_KV_HEADS * HEAD_DIM
            if sample:
                o, k_win, v_win = attention_sample(qkv[0], states['win_k'][i], states['win_v'][i],
                                                   P['attn_sinks'][i], P['rel_bias_table'])
                o = o[None]
                w_len = k_win.shape[1]
                new.setdefault('win_k', []).append(k_win.reshape(-1, w_len, N_KV_HEADS, HEAD_DIM))
                new.setdefault('win_v', []).append(v_win.reshape(-1, w_len, N_KV_HEADS, HEAD_DIM))
            else:
                o = attention_prompt(qkv, P['attn_sinks'][i], P['rel_bias_table'])
                keep = min(WINDOW, PAST_LEN)
                new.setdefault('win_k', []).append(
                    qkv[:, l - keep:, nq:nq + nkv].reshape(bsz, keep, N_KV_HEADS, HEAD_DIM))
                new.setdefault('win_v', []).append(
                    qkv[:, l - keep:, nq + nkv:].reshape(bsz, keep, N_KV_HEADS, HEAD_DIM))
            x = matmul_gate_residual(o, P['w_attn_o'], i, x, g_m, tm, 512)
        else:
            keep = GDN_CONV - 1
            gdn_args = (x, P['norm_mix'][layer], sh_m, sc_m, P['w_gdn_in'][i], P['gdn_conv_w'][i],
                        P['gdn_a_log'][i], P['gdn_dt_bias'][i], tm)
            if sample:
                seqs = states['gdn'].shape[1]
                t = l // seqs
                qkvz, bg, hist1 = gdn_in_projection(
                    *gdn_args, conv_hist=_to_time_major(states['gdn_conv'][i].astype(F32)))
                new.setdefault('gdn_conv', []).append(_from_time_major(hist1, seqs))
                seq_rows = -(-t // SUBLANES) * SUBLANES
                per_seq = lambda a: jnp.pad(_from_time_major(a, seqs), ((0, 0), (0, seq_rows - t), (0, 0))
                                            ).reshape(1, seqs * seq_rows, a.shape[-1])
                o_seq, s1 = gdn_core(per_seq(qkvz[0]), per_seq(bg[0]), states['gdn'][i].astype(F32),
                                     P['gdn_out_norm'][i], GDN_CHUNK, seq_rows=seq_rows)
                o = _to_time_major(o_seq.reshape(seqs, seq_rows, GDN_VAL_DIM)[:, :t])[None]
            else:
                qkvz, bg, raw = gdn_in_projection(*gdn_args)
                s0 = jnp.zeros((bsz, GDN_V_HEADS, GDN_DK, GDN_DV), F32)
                o, s1 = gdn_core(qkvz, bg, s0, P['gdn_out_norm'][i], _tile_rows(l, GDN_ROW_TILE_CAP))
                new.setdefault('gdn_conv', []).append(raw[:, -1, SUBLANES - keep:, :])
            new.setdefault('gdn', []).append(s1)
            x = matmul_gate_residual(o, P['w_gdn_out'], i, x, g_m, tm, 512)
        if sample:
            seqs = states['ffn_conv'].shape[1]
            act, f1 = ffn_up_sample(x[0], P['norm_ffn'][layer], sh_f[0], sc_f[0], P['w_ffn_up'],
                                    P['ffn_conv_w'], P['ffn_conv_b'], layer,
                                    _to_time_major(states['ffn_conv'][layer].astype(F32)))
            act, f1 = act[None], _from_time_major(f1, seqs)
        else:
            act, f1 = ffn_up_prompt(x, P['norm_ffn'][layer], sh_f, sc_f, P['w_ffn_up'],
                                    P['ffn_conv_w'], P['ffn_conv_b'], layer, tm)
        new.setdefault('ffn_conv', []).append(f1)
        x = matmul_gate_residual(act, P['w_ffn_down'], layer, x, g_f, tm, 256)
    return x, {k: jnp.stack(v) for k, v in new.items()}


def kernel(x_prompt, x_sample, c_prompt, c_sample, cache_win_k, cache_win_v, state_gdn, state_gdn_conv,
           state_ffn_conv, rel_bias_table, w_ada, b_ada, norm_mix, norm_ffn, w_attn_qkv, attn_q_norm,
           attn_k_norm, attn_sinks, w_attn_o, w_gdn_in, gdn_conv_w, gdn_a_log, gdn_dt_bias, gdn_out_norm,
           w_gdn_out, w_ffn_up, ffn_conv_w, ffn_conv_b, w_ffn_down):
    P = dict(rel_bias_table=rel_bias_table, norm_mix=norm_mix, norm_ffn=norm_ffn, w_attn_qkv=w_attn_qkv,
             attn_q_norm=attn_q_norm, attn_k_norm=attn_k_norm, attn_sinks=attn_sinks, w_attn_o=w_attn_o,
             w_gdn_in=w_gdn_in, gdn_conv_w=gdn_conv_w, gdn_a_log=gdn_a_log, gdn_dt_bias=gdn_dt_bias,
             gdn_out_norm=gdn_out_norm, w_gdn_out=w_gdn_out, w_ffn_up=w_ffn_up, ffn_conv_w=ffn_conv_w,
             ffn_conv_b=ffn_conv_b, w_ffn_down=w_ffn_down)
    bp = x_prompt.shape[0]
    bs, ts, d = x_sample.shape
    n_c = bp + bs
    c_rows = -(-n_c // SUBLANES) * SUBLANES
    c_all = jnp.pad(jnp.concatenate([c_prompt, c_sample], axis=0), ((0, c_rows - n_c), (0, 0)))
    mod = ada_modulation(c_all, w_ada, b_ada)
    mod_prompt = mod[:, :bp]
    mod_sample = jnp.tile(mod[:, bp:n_c], (1, ts, 1))

    y_p, new_p = _trunk(x_prompt, mod_prompt, None, P, sample=False)
    states = dict(win_k=cache_win_k, win_v=cache_win_v, gdn=state_gdn, gdn_conv=state_gdn_conv,
                  ffn_conv=state_ffn_conv)
    y_s, new_s = _trunk(_to_time_major(x_sample)[None], mod_sample, states, P, sample=True)
    y_s = _from_time_major(y_s[0], bs)
    return (y_p, y_s, new_p['win_k'], new_p['win_v'], new_s['win_k'], new_s['win_v'],
            new_p['gdn'], new_s['gdn'], new_p['gdn_conv'], new_s['gdn_conv'],
            new_p['ffn_conv'], new_s['ffn_conv'])
```

```python
import functools
import math

import numpy as np
import jax
import jax.numpy as jnp
from jax import lax
from jax.experimental import pallas as pl
from jax.experimental.pallas import tpu as pltpu

HEAD_DIM = 64
N_HEADS = 32
N_KV_HEADS = 4
ATT_GROUP = N_HEADS // N_KV_HEADS
WINDOW = 128
ATT_BLOCK = 128
ATT_SAMPLE_SEQS = 8
N_BUCKETS = 32
MAX_DISTANCE = 128
NEG_INF = -1e30
PAST_LEN = 16384

GDN_K_HEADS = 16
GDN_V_HEADS = 32
GDN_DK = 128
GDN_DV = 128
GDN_KEY_DIM = GDN_K_HEADS * GDN_DK
GDN_VAL_DIM = GDN_V_HEADS * GDN_DV
GDN_CONV_DIM = 2 * GDN_KEY_DIM + GDN_VAL_DIM
GDN_CONV = 4
GDN_CHUNK = 64
GDN_HEAD_GROUPS = 4
GDN_VH_PER_GROUP = GDN_V_HEADS // GDN_HEAD_GROUPS
GDN_KH_PER_GROUP = GDN_K_HEADS // GDN_HEAD_GROUPS
GDN_PACK = 4
GDN_PHASE_A_CHUNKS = 4

FFN_CONV = 3
NORM_EPS = 1e-6

LANES = 128
SUBLANES = 8
VMEM_CAP_BYTES = 60 * 1024 * 1024
VMEM_SLACK_BYTES = 8 * 1024 * 1024

ROW_TILE_CAP = 1024
WEIGHT_CAST_K_PIECE = 512
GDN_IN_SUB_ROWS = 256
FFN_SUB_ROWS = 256
GDN_ROW_TILE_CAP = 1024

BF16 = jnp.bfloat16
F32 = jnp.float32


def _vmem_limit(block_bytes, scratch_bytes=0):
    est = 2 * int(block_bytes) + int(scratch_bytes) + VMEM_SLACK_BYTES
    return int(min(max(est, 16 * 1024 * 1024), VMEM_CAP_BYTES))


def _params(n_grid, block_bytes, scratch_bytes=0):
    return pltpu.CompilerParams(
        dimension_semantics=("arbitrary",) * n_grid,
        vmem_limit_bytes=_vmem_limit(block_bytes, scratch_bytes))


def _nbytes(shape, dtype):
    return int(np.prod(shape)) * jnp.dtype(dtype).itemsize


def _silu(x):
    return x * (1.0 / (1.0 + jnp.exp2(x * (-math.log2(math.e)))))


def _dot(a, b):
    return jnp.dot(a, b, preferred_element_type=F32)


def _dot_nt(a, b):
    return lax.dot_general(a, b, (((1,), (1,)), ((), ())), preferred_element_type=F32)


def _dot_tn(a, b):
    return lax.dot_general(a, b, (((0,), (0,)), ((), ())), preferred_element_type=F32)


def _dot_f32_weight(lhs, w_ref):
    k = lhs.shape[1]
    step = WEIGHT_CAST_K_PIECE if k % WEIGHT_CAST_K_PIECE == 0 else k
    acc = None
    for k0 in range(0, k, step):
        part = _dot(lhs[:, k0:k0 + step], w_ref[k0:k0 + step, :].astype(BF16))
        acc = part if acc is None else acc + part
    return acc


def _norm_mod(x, gain, shift, scale):
    ms = jnp.mean(x * x, axis=-1, keepdims=True)
    y = x * lax.rsqrt(ms + NORM_EPS) * gain
    return y * (1.0 + scale) + shift


def _mod_specs(shift, tm):
    d = shift.shape[-1]
    if shift.shape[1] == 1:
        return pl.BlockSpec((1, 1, d), lambda b, i, j: (b, 0, 0))
    return pl.BlockSpec((1, tm, d), lambda b, i, j: (b, i, 0))


def _ada_kernel(c_ref, w_ref, b_ref, o_ref):
    a = _silu(c_ref[...]).astype(BF16)
    o_ref[0] = _dot(a, w_ref[0].astype(BF16)) + b_ref[0]


def ada_modulation(c_all, w_ada, b_ada, tn=1024):
    rows, d = c_all.shape
    depth, _, n = w_ada.shape
    blocks = _nbytes((rows, d), F32) + _nbytes((d, tn), F32) + _nbytes((rows, tn), F32)
    return pl.pallas_call(
        _ada_kernel,
        grid=(depth, n // tn),
        in_specs=[pl.BlockSpec((rows, d), lambda l, j: (0, 0)),
                  pl.BlockSpec((1, d, tn), lambda l, j: (l, 0, j)),
                  pl.BlockSpec((1, 1, tn), lambda l, j: (l, 0, j))],
        out_specs=pl.BlockSpec((1, rows, tn), lambda l, j: (l, 0, j)),
        out_shape=jax.ShapeDtypeStruct((depth, rows, n), F32),
        compiler_params=_params(2, blocks, _nbytes((d, tn), BF16)),
        name="ada_modulation",
    )(c_all, w_ada, b_ada.reshape(depth, 1, n))


def _qkv_kernel(x_ref, gain_ref, shift_ref, scale_ref, w_ref, hgain_ref, hflag_ref, gmat_ref,
                o_ref, h_s):
    @pl.when(pl.program_id(2) == 0)
    def _():
        h_s[...] = _norm_mod(x_ref[0], gain_ref[...], shift_ref[0], scale_ref[0]).astype(BF16)

    y = _dot(h_s[...], w_ref[...].astype(BF16))
    ms = _dot((y * y).astype(BF16), gmat_ref[...])
    yn = y * lax.rsqrt(ms + NORM_EPS) * hgain_ref[...]
    o_ref[0] = jnp.where(hflag_ref[...] > 0.0, yn, y)


def qkv_projection(x, gain, shift, scale, w, q_gain, k_gain, tm, tn=512):
    b, l, d = x.shape
    n = w.shape[1]
    nq, nk = N_HEADS * HEAD_DIM, N_KV_HEADS * HEAD_DIM
    hgain = jnp.concatenate([jnp.tile(q_gain, N_HEADS), jnp.tile(k_gain, N_KV_HEADS),
                             jnp.ones((nk,), F32)]).reshape(1, n)
    hflag = jnp.concatenate([jnp.ones((nq + nk,), F32), jnp.zeros((nk,), F32)]).reshape(1, n)
    gidx = np.arange(tn) // HEAD_DIM
    gmat = jnp.asarray((gidx[:, None] == gidx[None, :]).astype(np.float32) / HEAD_DIM, BF16)
    blocks = (_nbytes((tm, d), F32) + _nbytes((d, tn), F32) + _nbytes((tm, tn), F32)
              + 2 * _nbytes((shift.shape[1] == 1 and 1 or tm, d), F32) + _nbytes((tn, tn), BF16))
    scratch = _nbytes((tm, d), BF16) + _nbytes((d, tn), BF16) + 4 * _nbytes((tm, tn), F32)
    return pl.pallas_call(
        _qkv_kernel,
        grid=(b, l // tm, n // tn),
        in_specs=[pl.BlockSpec((1, tm, d), lambda b_, i, j: (b_, i, 0)),
                  pl.BlockSpec((1, d), lambda b_, i, j: (0, 0)),
                  _mod_specs(shift, tm), _mod_specs(scale, tm),
                  pl.BlockSpec((d, tn), lambda b_, i, j: (0, j)),
                  pl.BlockSpec((1, tn), lambda b_, i, j: (0, j)),
                  pl.BlockSpec((1, tn), lambda b_, i, j: (0, j)),
                  pl.BlockSpec((tn, tn), lambda b_, i, j: (0, 0))],
        out_specs=pl.BlockSpec((1, tm, tn), lambda b_, i, j: (b_, i, j)),
        out_shape=jax.ShapeDtypeStruct((b, l, n), F32),
        scratch_shapes=[pltpu.VMEM((tm, d), BF16)],
        compiler_params=_params(3, blocks, scratch),
        name="qkv_projection",
    )(x, gain.reshape(1, d), shift, scale, w, hgain, hflag, gmat)


def _attn_prompt_kernel(sink_ref, q_ref, cur_ref, prev_ref, bprev_ref, bcur_ref, o_ref, kz_s, vz_s):
    n = pl.program_id(1)
    blk = ATT_BLOCK
    lane = lax.broadcasted_iota(jnp.int32, (2 * blk, LANES), 1)
    lo_half = lane < HEAD_DIM
    neg_prev = jnp.where(n == 0, NEG_INF, 0.0).astype(F32)

    kv_width = N_KV_HEADS * HEAD_DIM
    for pair in range(N_KV_HEADS // 2):
        for part, dst in ((0, kz_s), (1, vz_s)):
            col = part * kv_width + pair * LANES
            both = jnp.concatenate([prev_ref[0, :, col:col + LANES], cur_ref[0, :, col:col + LANES]], axis=0)
            swapped = pltpu.roll(both, HEAD_DIM, axis=1)
            zero = jnp.zeros_like(both)
            c0, c1 = 2 * pair, 2 * pair + 1
            dst[2 * c0 + 0] = jnp.where(lo_half, both, zero).astype(BF16)
            dst[2 * c0 + 1] = jnp.where(lo_half, zero, swapped).astype(BF16)
            dst[2 * c1 + 0] = jnp.where(lo_half, swapped, zero).astype(BF16)
            dst[2 * c1 + 1] = jnp.where(lo_half, zero, both).astype(BF16)

    scale = HEAD_DIM ** -0.5 * math.log2(math.e)
    for p in range(N_HEADS // 2):
        c = (2 * p) // ATT_GROUP
        qp = (q_ref[0, :, p * LANES:(p + 1) * LANES] * scale).astype(BF16)
        o_pair = None
        for a in range(2):
            h = 2 * p + a
            sink = sink_ref[h]
            s = _dot_nt(qp, kz_s[2 * c + a])
            s_prev = s[:, :blk] + bprev_ref[h] + neg_prev
            s_cur = s[:, blk:] + bcur_ref[h]
            m = jnp.maximum(jnp.max(jnp.maximum(s_prev, s_cur), axis=-1, keepdims=True), sink)
            e_prev = jnp.exp2(s_prev - m)
            e_cur = jnp.exp2(s_cur - m)
            den = jnp.sum(e_prev + e_cur, axis=-1, keepdims=True) + jnp.exp2(sink - m)
            pm = jnp.concatenate([e_prev, e_cur], axis=1).astype(BF16)
            o_a = _dot(pm, vz_s[2 * c + a]) * (1.0 / den)
            o_pair = o_a if o_pair is None else o_pair + o_a
        o_ref[0, :, p * LANES:(p + 1) * LANES] = o_pair.astype(o_ref.dtype)


def _t5_bucket_np(dist):
    max_exact = N_BUCKETS // 2
    d = np.maximum(dist, 0)
    df = np.maximum(d, 1).astype(np.float32)
    large = max_exact + (np.log(df / np.float32(max_exact)) / np.float32(math.log(MAX_DISTANCE / max_exact))
                         * np.float32(N_BUCKETS - max_exact)).astype(np.int32)
    large = np.minimum(large, N_BUCKETS - 1)
    return np.where(d < max_exact, d, large)


def _bias_from_dist(dist, in_band, rel_table):
    onehot = (_t5_bucket_np(dist)[..., None] == np.arange(N_BUCKETS)).astype(np.float32)
    tab = jnp.einsum('qsb,bh->hqs', jnp.asarray(onehot), rel_table.astype(F32), precision=lax.Precision.HIGHEST)
    return jnp.where(jnp.asarray(in_band)[None], tab, NEG_INF)


def attention_prompt(qkv, sinks, rel_table):
    b, l, n = qkv.shape
    blk = ATT_BLOCK
    nq = N_HEADS * HEAD_DIM
    kvw = 2 * N_KV_HEADS * HEAD_DIM
    kv_blk = nq // kvw
    qi = np.arange(blk)[:, None]
    sj = np.arange(blk)[None, :]
    d_prev = qi + blk - sj
    d_cur = qi - sj
    log2e = math.log2(math.e)
    bias_prev = _bias_from_dist(d_prev, (d_prev >= 0) & (d_prev <= WINDOW), rel_table) * log2e
    bias_cur = _bias_from_dist(d_cur, (d_cur >= 0) & (d_cur <= WINDOW), rel_table) * log2e
    blocks = (_nbytes((blk, nq), F32) + 2 * _nbytes((blk, kvw), F32) + _nbytes((blk, nq), BF16))
    scratch = 2 * _nbytes((2 * N_KV_HEADS, 2 * blk, LANES), BF16) + 4 * _nbytes((N_HEADS, blk, blk), F32)
    return pl.pallas_call(
        _attn_prompt_kernel,
        grid=(b, l // blk),
        in_specs=[pl.BlockSpec(memory_space=pltpu.SMEM),
                  pl.BlockSpec((1, blk, nq), lambda b_, i: (b_, i, 0)),
                  pl.BlockSpec((1, blk, kvw), lambda b_, i: (b_, i, kv_blk)),
                  pl.BlockSpec((1, blk, kvw), lambda b_, i: (b_, jnp.maximum(i - 1, 0), kv_blk)),
                  pl.BlockSpec((N_HEADS, blk, blk), lambda b_, i: (0, 0, 0)),
                  pl.BlockSpec((N_HEADS, blk, blk), lambda b_, i: (0, 0, 0))],
        out_specs=pl.BlockSpec((1, blk, nq), lambda b_, i: (b_, i, 0)),
        out_shape=jax.ShapeDtypeStruct((b, l, nq), BF16),
        scratch_shapes=[pltpu.VMEM((2 * N_KV_HEADS, 2 * blk, LANES), BF16),
                        pltpu.VMEM((2 * N_KV_HEADS, 2 * blk, LANES), BF16)],
        compiler_params=_params(2, blocks, scratch),
        name="attention_prompt",
    )(sinks.astype(F32) * log2e, qkv, qkv, qkv, bias_prev, bias_cur)


def _attn_sample_kernel(q_ref, k_ref, v_ref, bias_ref, sink_ref, o_ref):
    pairs = [(b, c) for b in range(q_ref.shape[0]) for c in range(N_KV_HEADS)]
    k = [k_ref[b].astype(BF16) for b in range(q_ref.shape[0])]
    s = {bc: _dot_nt(q_ref[bc[0], bc[1]].astype(BF16), k[bc[0]]) + bias_ref[bc[1]] for bc in pairs}
    e, den = {}, {}
    for b, c in pairs:
        sink = sink_ref[c]
        m = jnp.maximum(jnp.max(s[(b, c)], axis=-1, keepdims=True), sink)
        e[(b, c)] = jnp.exp(s[(b, c)] - m)
        den[(b, c)] = jnp.sum(e[(b, c)], axis=-1, keepdims=True) + jnp.exp(sink - m)
    for b, c in pairs:
        o_ref[b, c] = _dot(e[(b, c)].astype(BF16), v_ref[b].astype(BF16)) * (1.0 / den[(b, c)])


def attention_sample(qkv, cache_k, cache_v, sinks, rel_table):
    bsz, w = cache_k.shape[0], cache_k.shape[1]
    t = qkv.shape[0] // bsz
    nq, nkv = N_HEADS * HEAD_DIM, N_KV_HEADS * HEAD_DIM
    keys = w + t
    keys_pad = -(-keys // 16) * 16
    rows = ATT_GROUP * t
    q = qkv[:, :nq].reshape(t, bsz, N_KV_HEADS, ATT_GROUP, HEAD_DIM) * HEAD_DIM ** -0.5
    q = jnp.transpose(q, (1, 2, 3, 0, 4)).reshape(bsz, N_KV_HEADS, rows, HEAD_DIM)
    qz = jnp.einsum('bcrd,ck->bcrkd', q, jnp.eye(N_KV_HEADS, dtype=F32)).reshape(bsz, N_KV_HEADS, rows, nkv)
    new_kv = jnp.transpose(qkv[:, nq:].reshape(t, bsz, 2 * nkv), (1, 0, 2))
    k_all = jnp.concatenate([cache_k.reshape(bsz, w, nkv), new_kv[:, :, :nkv]], axis=1)
    v_all = jnp.concatenate([cache_v.reshape(bsz, w, nkv), new_kv[:, :, nkv:]], axis=1)
    pad = ((0, 0), (0, keys_pad - keys), (0, 0))
    k_pad, v_pad = jnp.pad(k_all, pad), jnp.pad(v_all, pad)
    tq = np.arange(t)[:, None]
    sk = np.arange(keys_pad)[None, :]
    dist = tq + w - sk
    in_band = (dist >= 0) & (dist <= WINDOW) & (sk < keys)
    bias_t = _bias_from_dist(dist, in_band, rel_table)
    bias = bias_t.reshape(N_KV_HEADS, rows, keys_pad)
    sink_rows = jnp.repeat(sinks.astype(F32), t).reshape(N_KV_HEADS, rows, 1)
    sb = math.gcd(bsz, ATT_SAMPLE_SEQS)
    blocks = sb * (_nbytes((N_KV_HEADS, rows, nkv), F32) * 2 + 2 * _nbytes((keys_pad, nkv), F32))
    oz = pl.pallas_call(
        _attn_sample_kernel,
        grid=(bsz // sb,),
        in_specs=[pl.BlockSpec((sb, N_KV_HEADS, rows, nkv), lambda b_: (b_, 0, 0, 0)),
                  pl.BlockSpec((sb, keys_pad, nkv), lambda b_: (b_, 0, 0)),
                  pl.BlockSpec((sb, keys_pad, nkv), lambda b_: (b_, 0, 0)),
                  pl.BlockSpec((N_KV_HEADS, rows, keys_pad), lambda b_: (0, 0, 0)),
                  pl.BlockSpec((N_KV_HEADS, rows, 1), lambda b_: (0, 0, 0))],
        out_specs=pl.BlockSpec((sb, N_KV_HEADS, rows, nkv), lambda b_: (b_, 0, 0, 0)),
        out_shape=jax.ShapeDtypeStruct((bsz, N_KV_HEADS, rows, nkv), F32),
        compiler_params=_params(1, blocks),
        name="attention_sample",
    )(qz, k_pad, v_pad, bias, sink_rows)
    o = jnp.stack([oz[:, c, :, c * HEAD_DIM:(c + 1) * HEAD_DIM] for c in range(N_KV_HEADS)], axis=1)
    o = o.reshape(bsz, N_KV_HEADS, ATT_GROUP, t, HEAD_DIM)
    o = jnp.transpose(o, (3, 0, 1, 2, 4)).reshape(t * bsz, nq)
    return o.astype(BF16), k_all[:, t:], v_all[:, t:]


def _mm_res_kernel(a_ref, w_ref, x_ref, g_ref, o_ref):
    y = _dot_f32_weight(a_ref[0], w_ref)
    o_ref[0] = x_ref[0] + g_ref[0] * y


def _mm_res_wres_kernel(a_ref, w_ref, x_ref, g_ref, o_ref, wbf_s):
    @pl.when((pl.program_id(1) == 0) & (pl.program_id(2) == 0))
    def _():
        wbf_s[...] = w_ref[...].astype(BF16)

    o_ref[0] = x_ref[0] + g_ref[0] * _dot(a_ref[0], wbf_s[...])


def matmul_gate_residual_wres(a, w_stack, layer, x, gate, tm, tn):
    b, l, k = a.shape
    n = w_stack.shape[2]
    if gate.shape[1] == 1:
        g_spec = pl.BlockSpec((1, 1, tn), lambda j, b_, i: (b_, 0, j))
    else:
        g_spec = pl.BlockSpec((1, tm, tn), lambda j, b_, i: (b_, i, j))
    blocks = _nbytes((tm, k), a.dtype) + _nbytes((k, tn), F32) + 3 * _nbytes((tm, tn), F32)
    scratch = _nbytes((k, tn), BF16) + _nbytes((tm, tn), F32)
    return pl.pallas_call(
        _mm_res_wres_kernel,
        grid=(n // tn, b, l // tm),
        in_specs=[pl.BlockSpec((1, tm, k), lambda j, b_, i: (b_, i, 0)),
                  pl.BlockSpec((None, k, tn), lambda j, b_, i: (layer, 0, j)),
                  pl.BlockSpec((1, tm, tn), lambda j, b_, i: (b_, i, j)),
                  g_spec],
        out_specs=pl.BlockSpec((1, tm, tn), lambda j, b_, i: (b_, i, j)),
        out_shape=jax.ShapeDtypeStruct((b, l, n), F32),
        scratch_shapes=[pltpu.VMEM((k, tn), BF16)],
        compiler_params=_params(3, blocks, scratch),
        name="matmul_gate_residual_wres",
    )(a, w_stack, x, gate)


def matmul_gate_residual(a, w_stack, layer, x, gate, tm, tn):
    b, l, k = a.shape
    n = w_stack.shape[2]
    grows = 1 if gate.shape[1] == 1 else tm
    if gate.shape[1] == 1:
        g_spec = pl.BlockSpec((1, 1, tn), lambda b_, i, j: (b_, 0, j))
    else:
        g_spec = pl.BlockSpec((1, tm, tn), lambda b_, i, j: (b_, i, j))
    blocks = (_nbytes((tm, k), a.dtype) + _nbytes((k, tn), F32) + 2 * _nbytes((tm, tn), F32)
              + _nbytes((grows, tn), F32))
    scratch = _nbytes((k, tn), BF16) + _nbytes((tm, tn), F32)
    return pl.pallas_call(
        _mm_res_kernel,
        grid=(b, l // tm, n // tn),
        in_specs=[pl.BlockSpec((1, tm, k), lambda b_, i, j: (b_, i, 0)),
                  pl.BlockSpec((None, k, tn), lambda b_, i, j: (layer, 0, j)),
                  pl.BlockSpec((1, tm, tn), lambda b_, i, j: (b_, i, j)),
                  g_spec],
        out_specs=pl.BlockSpec((1, tm, tn), lambda b_, i, j: (b_, i, j)),
        out_shape=jax.ShapeDtypeStruct((b, l, n), F32),
        compiler_params=_params(3, blocks, scratch),
        name="matmul_gate_residual",
    )(a, w_stack, x, gate)


def _ffn_conv_rows(u, prev1, prev2, cw_ref, cb_ref):
    return u * cw_ref[2:3, :] + prev1 * cw_ref[1:2, :] + prev2 * cw_ref[0:1, :] + cb_ref[...]


def _ffn_up_prompt_kernel(x_ref, gain_ref, shift_ref, scale_ref, wg_ref, wv_ref, cwg_ref, cwv_ref,
                          cbg_ref, cbv_ref, o_ref, tail_ref, h_s, carry_s, ext_s):
    i = pl.program_id(1)
    j = pl.program_id(2)
    tm = h_s.shape[0]

    @pl.when(j == 0)
    def _():
        h_s[...] = _norm_mod(x_ref[0], gain_ref[...], shift_ref[0], scale_ref[0]).astype(BF16)

    @pl.when(i == 0)
    def _():
        carry_s[j] = jnp.zeros(carry_s.shape[1:], F32)

    halves = ((wg_ref, cwg_ref, cbg_ref), (wv_ref, cwv_ref, cbv_ref))
    w_bf = [w_ref[...].astype(BF16) for w_ref, _, _ in halves]
    for half in range(2):
        ext_s[half, 0:SUBLANES, :] = carry_s[j, half]
    sub = min(FFN_SUB_ROWS, tm)
    for m in range(tm // sub):
        hm = h_s[m * sub:(m + 1) * sub, :]
        r0 = SUBLANES + m * sub
        ys = []
        for half, (_, cw_ref, cb_ref) in enumerate(halves):
            u = _dot(hm, w_bf[half])
            ext_s[half, r0:r0 + sub, :] = u
            ys.append(_ffn_conv_rows(u, ext_s[half, r0 - 1:r0 - 1 + sub, :], ext_s[half, r0 - 2:r0 - 2 + sub, :],
                                     cw_ref, cb_ref))
        o_ref[0, m * sub:(m + 1) * sub, :] = (_silu(ys[0]) * ys[1]).astype(o_ref.dtype)
    for half in range(2):
        tail = ext_s[half, tm:tm + SUBLANES, :]
        carry_s[j, half] = tail
        tail_ref[0, 0, half] = tail


def ffn_up_prompt(x, gain, shift, scale, w_up, conv_w, conv_b, layer, tm, tn=512):
    b, l, d = x.shape
    dff = w_up.shape[2] // 2
    nj = dff // tn
    conv_b = conv_b.reshape(conv_b.shape[0], 1, 2 * dff)
    blocks = (_nbytes((tm, d), F32) + 2 * _nbytes((d, tn), F32) + _nbytes((tm, tn), BF16)
              + 2 * _nbytes((1, d), F32) + 8 * _nbytes((SUBLANES, tn), F32))
    scratch = (_nbytes((tm, d), BF16) + _nbytes((nj, 2, SUBLANES, tn), F32) + 2 * _nbytes((d, tn), BF16)
               + 8 * _nbytes((tm, tn), F32))
    act, tail = pl.pallas_call(
        _ffn_up_prompt_kernel,
        grid=(b, l // tm, nj),
        in_specs=[pl.BlockSpec((1, tm, d), lambda b_, i, j: (b_, i, 0)),
                  pl.BlockSpec((1, d), lambda b_, i, j: (0, 0)),
                  _mod_specs(shift, tm), _mod_specs(scale, tm),
                  pl.BlockSpec((None, d, tn), lambda b_, i, j: (layer, 0, j)),
                  pl.BlockSpec((None, d, tn), lambda b_, i, j: (layer, 0, j + nj)),
                  pl.BlockSpec((None, FFN_CONV, tn), lambda b_, i, j: (layer, 0, j)),
                  pl.BlockSpec((None, FFN_CONV, tn), lambda b_, i, j: (layer, 0, j + nj)),
                  pl.BlockSpec((None, 1, tn), lambda b_, i, j: (layer, 0, j)),
                  pl.BlockSpec((None, 1, tn), lambda b_, i, j: (layer, 0, j + nj))],
        out_specs=[pl.BlockSpec((1, tm, tn), lambda b_, i, j: (b_, i, j)),
                   pl.BlockSpec((1, 1, 2, SUBLANES, tn), lambda b_, i, j: (b_, i, 0, 0, j))],
        out_shape=[jax.ShapeDtypeStruct((b, l, dff), BF16),
                   jax.ShapeDtypeStruct((b, l // tm, 2, SUBLANES, dff), F32)],
        scratch_shapes=[pltpu.VMEM((tm, d), BF16), pltpu.VMEM((nj, 2, SUBLANES, tn), F32),
                        pltpu.VMEM((2, SUBLANES + tm, tn), F32)],
        compiler_params=_params(3, blocks, scratch),
        name="ffn_up_prompt",
    )(x, gain.reshape(1, d), shift, scale, w_up, w_up, conv_w, conv_w, conv_b, conv_b)
    keep = FFN_CONV - 1
    hist = jnp.transpose(tail[:, -1, :, SUBLANES - keep:, :], (0, 2, 1, 3)).reshape(b, keep, 2 * dff)
    return act, hist


def _ffn_up_sample_kernel(x_ref, gain_ref, shift_ref, scale_ref, wg_ref, wv_ref, cwg_ref, cwv_ref,
                          cbg_ref, cbv_ref, hg_ref, hv_ref, o_ref, ng_ref, nv_ref, h_s, *, bsz):
    @pl.when(pl.program_id(0) == 0)
    def _():
        h_s[...] = _norm_mod(x_ref[...], gain_ref[...], shift_ref[...], scale_ref[...]).astype(BF16)

    h = h_s[...]
    rows = h_s.shape[0]
    keep = FFN_CONV - 1
    ys = []
    for w_ref, cw_ref, cb_ref, hist_ref, new_ref in ((wg_ref, cwg_ref, cbg_ref, hg_ref, ng_ref),
                                                     (wv_ref, cwv_ref, cbv_ref, hv_ref, nv_ref)):
        u = _dot(h, w_ref[...].astype(BF16))
        ext = jnp.concatenate([hist_ref[...], u], axis=0)
        prev = [ext[(keep - k) * bsz:(keep - k) * bsz + rows] for k in (1, 2)]
        ys.append(_ffn_conv_rows(u, prev[0], prev[1], cw_ref, cb_ref))
        new_ref[...] = ext[rows:rows + keep * bsz]
    o_ref[...] = (_silu(ys[0]) * ys[1]).astype(o_ref.dtype)


def ffn_up_sample(x, gain, shift, scale, w_up, conv_w, conv_b, layer, hist, tn=512):
    rows, d = x.shape
    dff = w_up.shape[2] // 2
    nj = dff // tn
    keep = FFN_CONV - 1
    bsz = hist.shape[0] // keep
    conv_b = conv_b.reshape(conv_b.shape[0], 1, 2 * dff)
    col = lambda j: (0, j)
    col_hi = lambda j: (0, j + nj)
    lcol = lambda j: (layer, 0, j)
    lcol_hi = lambda j: (layer, 0, j + nj)
    blocks = (_nbytes((rows, d), F32) * 3 + 2 * _nbytes((d, tn), F32) + 5 * _nbytes((rows, tn), F32))
    scratch = _nbytes((rows, d), BF16) + 2 * _nbytes((d, tn), BF16) + 8 * _nbytes((rows, tn), F32)
    act, new_g, new_v = pl.pallas_call(
        functools.partial(_ffn_up_sample_kernel, bsz=bsz),
        grid=(nj,),
        in_specs=[pl.BlockSpec((rows, d), lambda j: (0, 0)),
                  pl.BlockSpec((1, d), lambda j: (0, 0)),
                  pl.BlockSpec((rows, d), lambda j: (0, 0)),
                  pl.BlockSpec((rows, d), lambda j: (0, 0)),
                  pl.BlockSpec((None, d, tn), lcol), pl.BlockSpec((None, d, tn), lcol_hi),
                  pl.BlockSpec((None, FFN_CONV, tn), lcol), pl.BlockSpec((None, FFN_CONV, tn), lcol_hi),
                  pl.BlockSpec((None, 1, tn), lcol), pl.BlockSpec((None, 1, tn), lcol_hi),
                  pl.BlockSpec((keep * bsz, tn), col), pl.BlockSpec((keep * bsz, tn), col_hi)],
        out_specs=[pl.BlockSpec((rows, tn), col), pl.BlockSpec((keep * bsz, tn), col),
                   pl.BlockSpec((keep * bsz, tn), col)],
        out_shape=[jax.ShapeDtypeStruct((rows, dff), BF16),
                   jax.ShapeDtypeStruct((keep * bsz, dff), F32),
                   jax.ShapeDtypeStruct((keep * bsz, dff), F32)],
        scratch_shapes=[pltpu.VMEM((rows, d), BF16)],
        compiler_params=_params(1, blocks, scratch),
        name="ffn_up_sample",
    )(x, gain.reshape(1, d), shift, scale, w_up, w_up, conv_w, conv_w, conv_b, conv_b, hist, hist)
    return act, jnp.concatenate([new_g, new_v], axis=-1)


def _gdn_in_kernel(*refs, n_qk_tiles, n_conv_tiles, bsz):
    if bsz is None:
        (x_ref, gain_ref, shift_ref, scale_ref, w_ref, wba_ref, alog_ref, dtb_ref, cw_ref,
         o_ref, bg_ref, raw_ref, h_s, carry_s) = refs
    else:
        (x_ref, gain_ref, shift_ref, scale_ref, w_ref, wba_ref, alog_ref, dtb_ref, cw_ref, hist_ref,
         o_ref, bg_ref, raw_ref, h_s) = refs
    i = pl.program_id(1)
    j = pl.program_id(2)
    tm = h_s.shape[0]
    width = GDN_CONV

    @pl.when(j == 0)
    def _():
        h = _norm_mod(x_ref[0], gain_ref[...], shift_ref[0], scale_ref[0]).astype(BF16)
        h_s[...] = h
        y = _dot_nt(h, wba_ref[...].astype(BF16))
        lane = lax.broadcasted_iota(jnp.int32, y.shape, 1) % LANES
        beta = 1.0 / (1.0 + jnp.exp(-y))
        a = y + dtb_ref[...]
        softplus = jnp.maximum(a, 0.0) + jnp.log1p(jnp.exp(-jnp.abs(a)))
        g = -jnp.exp(alog_ref[...]) * softplus
        bg_ref[0] = jnp.where(lane < GDN_VH_PER_GROUP, beta, jnp.where(lane < 2 * GDN_VH_PER_GROUP, g, 0.0))

    sub = min(GDN_IN_SUB_ROWS, tm)
    row8 = lax.broadcasted_iota(jnp.int32, (SUBLANES, 1), 0)

    def conv_tile(l2_scale):
        w_bf = w_ref[...].astype(BF16)
        jc = jnp.minimum(j, n_conv_tiles - 1)
        if bsz is None:
            @pl.when(i == 0)
            def _():
                carry_s[jc] = jnp.zeros(carry_s.shape[1:], F32)
            prev_tail = carry_s[jc]
        for m in range(tm // sub):
            rs = slice(m * sub, (m + 1) * sub)
            u = _dot_nt(h_s[rs, :], w_bf)
            y = u * cw_ref[width - 1:width, :]
            if bsz is None:
                top = u[0:SUBLANES]
                y_top = top * cw_ref[width - 1:width, :]
                for k in range(1, width):
                    wk = cw_ref[width - 1 - k:width - k, :]
                    y = y + pltpu.roll(u, k, axis=0) * wk
                    y_top = y_top + jnp.where(row8 < k, pltpu.roll(prev_tail, k, axis=0),
                                              pltpu.roll(top, k, axis=0)) * wk
                y = jnp.concatenate([y_top, y[SUBLANES:]], axis=0)
                prev_tail = u[sub - SUBLANES:sub]
            else:
                ext = jnp.concatenate([hist_ref[...], u], axis=0)
                for k in range(1, width):
                    wk = cw_ref[width - 1 - k:width - k, :]
                    y = y + ext[(width - 1 - k) * bsz:(width - 1 - k) * bsz + sub] * wk
                raw_ref[...] = ext[sub:sub + (width - 1) * bsz]
            y = _silu(y)
            if l2_scale is not None:
                heads = [y[:, hd * LANES:(hd + 1) * LANES] for hd in range(y.shape[1] // LANES)]
                y = jnp.concatenate(
                    [yh * (lax.rsqrt(jnp.sum(yh * yh, axis=-1, keepdims=True) + NORM_EPS) * l2_scale)
                     for yh in heads], axis=1)
            o_ref[0, rs, :] = y.astype(o_ref.dtype)
        if bsz is None:
            carry_s[jc] = prev_tail
            raw_ref[0, 0] = prev_tail

    @pl.when(j < n_qk_tiles)
    def _():
        conv_tile(jnp.where(j < n_qk_tiles // 2, GDN_DK ** -0.5, 1.0).astype(F32))

    @pl.when((j >= n_qk_tiles) & (j < n_conv_tiles))
    def _():
        conv_tile(None)

    @pl.when(j >= n_conv_tiles)
    def _():
        w_bf = w_ref[...].astype(BF16)
        for m in range(tm // sub):
            rs = slice(m * sub, (m + 1) * sub)
            o_ref[0, rs, :] = _silu(_dot_nt(h_s[rs, :], w_bf)).astype(o_ref.dtype)


def _group_lane_layout(vec_b, vec_a):
    lead = vec_b.shape[:-1]
    vb = vec_b.reshape(*lead, GDN_HEAD_GROUPS, GDN_VH_PER_GROUP)
    va = vec_a.reshape(*lead, GDN_HEAD_GROUPS, GDN_VH_PER_GROUP)
    pad = jnp.zeros((*lead, GDN_HEAD_GROUPS, LANES - 2 * GDN_VH_PER_GROUP), vec_b.dtype)
    return jnp.concatenate([vb, va, pad], axis=-1).reshape(*lead, GDN_HEAD_GROUPS * LANES)


def gdn_in_projection(x, gain, shift, scale, w_in, conv_w, a_log, dt_bias, tm, conv_hist=None, tn=512):
    b, l, d = x.shape
    n_main = GDN_CONV_DIM + GDN_VAL_DIM
    w_t = jnp.transpose(w_in)
    wba = _group_lane_layout(w_in[:, n_main:n_main + GDN_V_HEADS], w_in[:, n_main + GDN_V_HEADS:])
    wba_t = jnp.transpose(wba)
    zeros = jnp.zeros((1, GDN_V_HEADS), F32)
    alog = _group_lane_layout(zeros, a_log.reshape(1, -1).astype(F32))
    dtb = _group_lane_layout(zeros, dt_bias.reshape(1, -1).astype(F32))
    nbg = wba_t.shape[0]
    n_conv_tiles = GDN_CONV_DIM // tn
    n_qk_tiles = 2 * GDN_KEY_DIM // tn
    conv_col = lambda b_, i, j: (0, jnp.minimum(j, n_conv_tiles - 1))
    in_specs = [pl.BlockSpec((1, tm, d), lambda b_, i, j: (b_, i, 0)),
                pl.BlockSpec((1, d), lambda b_, i, j: (0, 0)),
                _mod_specs(shift, tm), _mod_specs(scale, tm),
                pl.BlockSpec((tn, d), lambda b_, i, j: (j, 0)),
                pl.BlockSpec((nbg, d), lambda b_, i, j: (0, 0)),
                pl.BlockSpec((1, nbg), lambda b_, i, j: (0, 0)),
                pl.BlockSpec((1, nbg), lambda b_, i, j: (0, 0)),
                pl.BlockSpec((GDN_CONV, tn), conv_col)]
    args = [x, gain.reshape(1, d), shift, scale, w_t, wba_t, alog, dtb, conv_w]
    scratch_shapes = [pltpu.VMEM((tm, d), BF16)]
    blocks = (_nbytes((tm, d), F32) + _nbytes((tn, d), F32) + _nbytes((tm, tn), BF16)
              + _nbytes((nbg, d), F32) + _nbytes((tm, nbg), F32) + 2 * _nbytes((1, d), F32))
    scratch = _nbytes((tm, d), BF16) + _nbytes((tn, d), BF16) + 6 * _nbytes((tm, nbg), F32)
    if conv_hist is None:
        bsz = None
        raw_spec = pl.BlockSpec((1, 1, SUBLANES, tn), lambda b_, i, j: (b_, i, 0, jnp.minimum(j, n_conv_tiles - 1)))
        raw_shape = jax.ShapeDtypeStruct((b, l // tm, SUBLANES, GDN_CONV_DIM), F32)
        scratch_shapes.append(pltpu.VMEM((n_conv_tiles, SUBLANES, tn), F32))
    else:
        assert b == 1 and l == tm <= GDN_IN_SUB_ROWS
        hist_rows = conv_hist.shape[0]
        bsz = hist_rows // (GDN_CONV - 1)
        args.append(conv_hist)
        in_specs.append(pl.BlockSpec((hist_rows, tn), conv_col))
        raw_spec = pl.BlockSpec((hist_rows, tn), conv_col)
        raw_shape = jax.ShapeDtypeStruct((hist_rows, GDN_CONV_DIM), F32)
        blocks += 2 * _nbytes((hist_rows, tn), F32)
    return pl.pallas_call(
        functools.partial(_gdn_in_kernel, n_qk_tiles=n_qk_tiles, n_conv_tiles=n_conv_tiles, bsz=bsz),
        grid=(b, l // tm, n_main // tn),
        in_specs=in_specs,
        out_specs=[pl.BlockSpec((1, tm, tn), lambda b_, i, j: (b_, i, j)),
                   pl.BlockSpec((1, tm, nbg), lambda b_, i, j: (b_, i, 0)),
                   raw_spec],
        out_shape=[jax.ShapeDtypeStruct((b, l, n_main), BF16),
                   jax.ShapeDtypeStruct((b, l, nbg), F32),
                   raw_shape],
        scratch_shapes=scratch_shapes,
        compiler_params=_params(3, blocks, scratch),
        name="gdn_in_projection",
    )(*args)


def _gdn_core_kernel(q_ref, k_ref, v_ref, zg_ref, bg_ref, onorm_ref, s0_ref, o_ref, s1_ref,
                     s_s, gc_s, tdec_s, mo_s, *, seq_rows):
    r = pl.program_id(2)
    n_r = pl.num_programs(2)
    rows = q_ref.shape[1]
    chunk = GDN_CHUNK
    vpg = GDN_VH_PER_GROUP
    rep = GDN_V_HEADS // GDN_K_HEADS
    n_kh = GDN_KH_PER_GROUP
    seg = chunk if seq_rows is None else seq_rows
    n_seq = chunk // seg

    def load_states():
        for s in range(n_seq):
            for kh in range(n_kh):
                s_s[s * n_kh + kh] = jnp.concatenate([s0_ref[s, kh * rep + e] for e in range(rep)], axis=1)

    if seq_rows is None:
        pl.when(r == 0)(load_states)
    else:
        load_states()

    def seg_last(x):
        if n_seq == 1:
            return x[chunk - 1:chunk, :]
        blocks = x.reshape(n_seq, seg, x.shape[1])
        return jnp.broadcast_to(blocks[:, seg - 1:seg, :], blocks.shape).reshape(x.shape)

    n_chunks = rows // chunk
    pw = GDN_PACK * chunk
    n_packs = vpg // GDN_PACK
    ri = lax.broadcasted_iota(jnp.int32, (chunk, pw), 0)
    li = lax.broadcasted_iota(jnp.int32, (chunk, pw), 1) % chunk
    seg_shift = int(math.log2(seg))
    same_seq = (ri >> seg_shift) == (li >> seg_shift)
    tril_p = (ri >= li) & same_seq
    strict_p = (ri > li) & same_seq
    eye_p = (ri == li).astype(F32)
    n_levels = seg_shift
    off_masks = [((ri >> (lvl + 1)) == (li >> (lvl + 1))) & ((ri >> lvl) != (li >> lvl))
                 for lvl in range(n_levels)]
    bd_rows = lax.broadcasted_iota(jnp.int32, (pw, pw), 0) // chunk
    bd_cols = lax.broadcasted_iota(jnp.int32, (pw, pw), 1) // chunk
    bd_mask = (bd_rows == bd_cols).astype(BF16)
    row_c = lax.broadcasted_iota(jnp.int32, (chunk, LANES), 0)
    lane_lo = lax.broadcasted_iota(jnp.int32, (chunk, LANES), 1) < chunk

    def block_diag(xp):
        return jnp.concatenate([xp.astype(BF16)] * GDN_PACK, axis=0) * bd_mask

    def pack_cols(arr, first_col):
        tiles = []
        for t in range(pw // LANES):
            even = jnp.broadcast_to(arr[:, first_col + 2 * t:first_col + 2 * t + 1], (chunk, LANES))
            odd = jnp.broadcast_to(arr[:, first_col + 2 * t + 1:first_col + 2 * t + 2], (chunk, LANES))
            tiles.append(jnp.where(lane_lo, even, odd))
        return jnp.concatenate(tiles, axis=1)

    for c0 in range(0, n_chunks, GDN_PHASE_A_CHUNKS):
        group = [(c, p) for c in range(c0, min(c0 + GDN_PHASE_A_CHUNKS, n_chunks)) for p in range(n_packs)]
        a_list, qkd, brow, dec = {}, {}, {}, {}
        for c in range(c0, min(c0 + GDN_PHASE_A_CHUNKS, n_chunks)):
            rs = slice(c * chunk, (c + 1) * chunk)
            bg = bg_ref[0, rs, :]
            cum = bg
            shift = 1
            while shift < seg:
                cum = cum + jnp.where((row_c & (seg - 1)) >= shift, pltpu.roll(cum, shift, axis=0), 0.0)
                shift *= 2
            gc_s[rs, :] = cum
            gq = []
            for kh in range(GDN_KH_PER_GROUP):
                ksl = slice(kh * LANES, (kh + 1) * LANES)
                k = k_ref[0, rs, ksl]
                kq = jnp.concatenate([k, q_ref[0, rs, ksl]], axis=0)
                kk = jnp.concatenate([k, k], axis=0)
                gq.append(_dot_nt(kq, kk))
            for p in range(n_packs):
                khs = [(p * GDN_PACK + 2 * t) // rep for t in range(pw // LANES)]
                gram = jnp.concatenate([gq[kh][:chunk] for kh in khs], axis=1)
                qk = jnp.concatenate([gq[kh][chunk:] for kh in khs], axis=1)
                beta_p = pack_cols(bg, p * GDN_PACK)
                gcol_p = pack_cols(cum, vpg + p * GDN_PACK)
                grow_p = jnp.sum(gcol_p * eye_p, axis=0, keepdims=True)
                decay = jnp.exp(jnp.where(tril_p, gcol_p - grow_p, NEG_INF))
                a_list[(c, p)] = jnp.where(strict_p, gram * beta_p * decay, 0.0)
                qkd[(c, p)] = (qk * decay).astype(BF16)
                brow[(c, p)] = jnp.sum(beta_p * eye_p, axis=0, keepdims=True)
                dec[(c, p)] = jnp.exp(seg_last(gcol_p) - gcol_p)
        inv = {cp: eye_p - jnp.where(off_masks[0], a_list[cp], 0.0) for cp in group}
        for lvl in range(1, n_levels):
            w = {cp: _dot(jnp.where(off_masks[lvl], a_list[cp], 0.0).astype(BF16), block_diag(inv[cp]))
                 for cp in group}
            inv = {cp: inv[cp] - _dot(inv[cp].astype(BF16), block_diag(w[cp])) for cp in group}
        for cp in group:
            t_beta = inv[cp] * brow[cp]
            tdec_s[cp[0], cp[1]] = block_diag(dec[cp] * t_beta)
            mo_s[cp[0], cp[1]] = block_diag(_dot(qkd[cp], block_diag(t_beta)))

    def chunk_body(c, carry):
        rs = pl.ds(pl.multiple_of(c * chunk, chunk), chunk)
        cum = gc_s[rs, :]
        eg, g_last, ks, qs = [], [], [], []
        for kh in range(n_kh):
            ksl = slice(kh * LANES, (kh + 1) * LANES)
            kq = jnp.concatenate([k_ref[0, rs, ksl], q_ref[0, rs, ksl]], axis=0)
            per_seq = [_dot(kq, s_s[s * n_kh + kh].astype(BF16)) for s in range(n_seq)]
            kq_s = jnp.concatenate([per_seq[s][half * chunk + s * seg:half * chunk + (s + 1) * seg]
                                    for half in range(2) for s in range(n_seq)], axis=0)
            for e in range(rep):
                hv = kh * rep + e
                gcol = cum[:, vpg + hv:vpg + hv + 1]
                g_last.append([gcol[(s + 1) * seg - 1:(s + 1) * seg, :] for s in range(n_seq)])
                eg.append(jnp.exp(gcol))
                ks.append(kq_s[:chunk, e * LANES:(e + 1) * LANES] * eg[hv])
                qs.append(kq_s[chunk:, e * LANES:(e + 1) * LANES] * eg[hv])
        v_dec, o_intra = [], []
        for p in range(n_packs):
            heads = range(p * GDN_PACK, (p + 1) * GDN_PACK)
            rhs = jnp.concatenate([v_ref[0, rs, hv * LANES:(hv + 1) * LANES].astype(F32) - ks[hv]
                                   for hv in heads], axis=0).astype(BF16)
            vd = _dot(tdec_s[c, p], rhs)
            oi = _dot(mo_s[c, p], rhs)
            for j in range(GDN_PACK):
                v_dec.append(vd[j * chunk:(j + 1) * chunk])
                o_intra.append(oi[j * chunk:(j + 1) * chunk])
        for hv in range(vpg):
            vsl = slice(hv * LANES, (hv + 1) * LANES)
            o = qs[hv] + o_intra[hv]
            on = o * lax.rsqrt(jnp.mean(o * o, axis=-1, keepdims=True) + NORM_EPS) * onorm_ref[...]
            o_ref[0, rs, vsl] = (on * zg_ref[0, rs, vsl].astype(F32)).astype(o_ref.dtype)
        lane_v = lax.broadcasted_iota(jnp.int32, (1, rep * LANES), 1) // LANES
        row_seq = lax.broadcasted_iota(jnp.int32, (chunk, 1), 0) >> seg_shift
        for kh in range(n_kh):
            heads = range(kh * rep, (kh + 1) * rep)
            k_bf = k_ref[0, rs, kh * LANES:(kh + 1) * LANES]
            vd_pair = jnp.concatenate([v_dec[hv] for hv in heads], axis=1)
            for s in range(n_seq):
                vd_s = vd_pair if n_seq == 1 else jnp.where(row_seq == s, vd_pair, 0.0)
                ds = _dot_tn(k_bf, vd_s.astype(BF16))
                keep = jnp.exp(g_last[kh * rep][s])
                for e in range(1, rep):
                    keep = jnp.where(lane_v == e, jnp.exp(g_last[kh * rep + e][s]), keep)
                s_s[s * n_kh + kh] = s_s[s * n_kh + kh] * keep + ds
        return carry

    lax.fori_loop(0, n_chunks, chunk_body, 0)

    def store_states():
        for s in range(n_seq):
            for hv in range(vpg):
                s1_ref[s, hv] = s_s[s * n_kh + hv // rep][:, (hv % rep) * LANES:(hv % rep + 1) * LANES]

    if seq_rows is None:
        pl.when(r == n_r - 1)(store_states)
    else:
        store_states()


def gdn_core(qkvz, bg, s0, out_norm, rows, seq_rows=None):
    b, l, _ = qkvz.shape
    n_state = 1 if seq_rows is None else GDN_CHUNK // seq_rows
    if seq_rows is None:
        state_idx = lambda b_, g, r: (b_, g, 0, 0)
    else:
        assert b == 1 and rows == GDN_CHUNK
        state_idx = lambda b_, g, r: (r, g, 0, 0)
    hg = GDN_HEAD_GROUPS
    qw = GDN_KEY_DIM // hg
    vw = GDN_VAL_DIM // hg
    kq = GDN_KEY_DIM // qw
    kv = 2 * GDN_KEY_DIM // vw
    kz = GDN_CONV_DIM // vw
    vpg = GDN_VH_PER_GROUP
    blocks = (2 * _nbytes((rows, qw), BF16) + 3 * _nbytes((rows, vw), BF16) + _nbytes((rows, LANES), F32)
              + 2 * _nbytes((n_state, vpg, GDN_DK, GDN_DV), F32))
    n_chunks = rows // GDN_CHUNK
    n_packs = vpg // GDN_PACK
    pw = GDN_PACK * GDN_CHUNK
    scratch = (_nbytes((n_state, vpg, GDN_DK, GDN_DV), F32) + _nbytes((rows, LANES), F32)
               + 2 * _nbytes((n_chunks, n_packs, pw, pw), BF16) + 6 * _nbytes((rows, vw), F32))
    return pl.pallas_call(
        functools.partial(_gdn_core_kernel, seq_rows=seq_rows),
        grid=(b, hg, l // rows),
        in_specs=[pl.BlockSpec((1, rows, qw), lambda b_, g, r: (b_, r, g)),
                  pl.BlockSpec((1, rows, qw), lambda b_, g, r: (b_, r, kq + g)),
                  pl.BlockSpec((1, rows, vw), lambda b_, g, r: (b_, r, kv + g)),
                  pl.BlockSpec((1, rows, vw), lambda b_, g, r: (b_, r, kz + g)),
                  pl.BlockSpec((1, rows, LANES), lambda b_, g, r: (b_, r, g)),
                  pl.BlockSpec((1, GDN_DV), lambda b_, g, r: (0, 0)),
                  pl.BlockSpec((n_state, vpg, GDN_DK, GDN_DV), state_idx)],
        out_specs=[pl.BlockSpec((1, rows, vw), lambda b_, g, r: (b_, r, g)),
                   pl.BlockSpec((n_state, vpg, GDN_DK, GDN_DV), state_idx)],
        out_shape=[jax.ShapeDtypeStruct((b, l, GDN_VAL_DIM), BF16),
                   jax.ShapeDtypeStruct(s0.shape, F32)],
        scratch_shapes=[pltpu.VMEM((n_state * GDN_KH_PER_GROUP, GDN_DK,
                                    (GDN_V_HEADS // GDN_K_HEADS) * GDN_DV), F32),
                        pltpu.VMEM((rows, LANES), F32),
                        pltpu.VMEM((n_chunks, n_packs, pw, pw), BF16),
                        pltpu.VMEM((n_chunks, n_packs, pw, pw), BF16)],
        compiler_params=_params(3, blocks, scratch),
        name="gdn_core",
    )(qkvz, qkvz, qkvz, qkvz, bg, out_norm.reshape(1, GDN_DV).astype(F32), s0)


def _tile_rows(l, cap):
    t = min(l, cap)
    while l % t:
        t //= 2
    return t


def _to_time_major(a):
    return jnp.transpose(a, (1, 0, 2)).reshape(a.shape[0] * a.shape[1], a.shape[2])


def _from_time_major(a, bsz):
    return jnp.transpose(a.reshape(a.shape[0] // bsz, bsz, a.shape[1]), (1, 0, 2))


def _trunk(x, mod, states, P, sample):
    bsz, l, d = x.shape
    tm = _tile_rows(l, ROW_TILE_CAP)
    out_proj = matmul_gate_residual if sample else matmul_gate_residual_wres
    new = {}
    depth = mod.shape[0]
    for layer in range(depth):
        if sample:
            parts = [mod[layer][None, :, k * d:(k + 1) * d] for k in range(6)]
        else:
            parts = [mod[layer][:, None, k * d:(k + 1) * d] for k in range(6)]
        sh_m, sc_m, g_m, sh_f, sc_f, g_f = parts
        i = layer // 2
        if layer % 2 == 0:
            qkv = qkv_projection(x, P['norm_mix'][layer], sh_m, sc_m, P['w_attn_qkv'][i],
                                 P['attn_q_norm'][i], P['attn_k_norm'][i], tm)
            nq, nkv = N_HEADS * HEAD_DIM, N_KV_HEADS * HEAD_DIM
            if sample:
                o, k_win, v_win = attention_sample(qkv[0], states['win_k'][i], states['win_v'][i],
                                                   P['attn_sinks'][i], P['rel_bias_table'])
                o = o[None]
                w_len = k_win.shape[1]
                new.setdefault('win_k', []).append(k_win.reshape(-1, w_len, N_KV_HEADS, HEAD_DIM))
                new.setdefault('win_v', []).append(v_win.reshape(-1, w_len, N_KV_HEADS, HEAD_DIM))
            else:
                o = attention_prompt(qkv, P['attn_sinks'][i], P['rel_bias_table'])
                keep = min(WINDOW, PAST_LEN)
                new.setdefault('win_k', []).append(
                    qkv[:, l - keep:, nq:nq + nkv].reshape(bsz, keep, N_KV_HEADS, HEAD_DIM))
                new.setdefault('win_v', []).append(
                    qkv[:, l - keep:, nq + nkv:].reshape(bsz, keep, N_KV_HEADS, HEAD_DIM))
            x = matmul_gate_residual(o, P['w_attn_o'], i, x, g_m, tm, 512)
        else:
            keep = GDN_CONV - 1
            gdn_args = (x, P['norm_mix'][layer], sh_m, sc_m, P['w_gdn_in'][i], P['gdn_conv_w'][i],
                        P['gdn_a_log'][i], P['gdn_dt_bias'][i], tm)
            if sample:
                seqs = states['gdn'].shape[1]
                t = l // seqs
                qkvz, bg, hist1 = gdn_in_projection(
                    *gdn_args, conv_hist=_to_time_major(states['gdn_conv'][i].astype(F32)))
                new.setdefault('gdn_conv', []).append(_from_time_major(hist1, seqs))
                seq_rows = -(-t // SUBLANES) * SUBLANES
                per_seq = lambda a: jnp.pad(_from_time_major(a, seqs), ((0, 0), (0, seq_rows - t), (0, 0))
                                            ).reshape(1, seqs * seq_rows, a.shape[-1])
                o_seq, s1 = gdn_core(per_seq(qkvz[0]), per_seq(bg[0]), states['gdn'][i].astype(F32),
                                     P['gdn_out_norm'][i], GDN_CHUNK, seq_rows=seq_rows)
                o = _to_time_major(o_seq.reshape(seqs, seq_rows, GDN_VAL_DIM)[:, :t])[None]
            else:
                qkvz, bg, raw = gdn_in_projection(*gdn_args)
                s0 = jnp.zeros((bsz, GDN_V_HEADS, GDN_DK, GDN_DV), F32)
                o, s1 = gdn_core(qkvz, bg, s0, P['gdn_out_norm'][i], _tile_rows(l, GDN_ROW_TILE_CAP))
                new.setdefault('gdn_conv', []).append(raw[:, -1, SUBLANES - keep:, :])
            new.setdefault('gdn', []).append(s1)
            x = out_proj(o, P['w_gdn_out'], i, x, g_m, tm, 512)
        if sample:
            seqs = states['ffn_conv'].shape[1]
            act, f1 = ffn_up_sample(x[0], P['norm_ffn'][layer], sh_f[0], sc_f[0], P['w_ffn_up'],
                                    P['ffn_conv_w'], P['ffn_conv_b'], layer,
                                    _to_time_major(states['ffn_conv'][layer].astype(F32)))
            act, f1 = act[None], _from_time_major(f1, seqs)
        else:
            act, f1 = ffn_up_prompt(x, P['norm_ffn'][layer], sh_f, sc_f, P['w_ffn_up'],
                                    P['ffn_conv_w'], P['ffn_conv_b'], layer, tm)
        new.setdefault('ffn_conv', []).append(f1)
        if sample:
            x = matmul_gate_residual(act, P['w_ffn_down'], layer, x, g_f, tm, 256)
        else:
            x = matmul_gate_residual_wres(act, P['w_ffn_down'], layer, x, g_f, _tile_rows(l, 512), 512)
    return x, {k: jnp.stack(v) for k, v in new.items()}


def kernel(x_prompt, x_sample, c_prompt, c_sample, cache_win_k, cache_win_v, state_gdn, state_gdn_conv,
           state_ffn_conv, rel_bias_table, w_ada, b_ada, norm_mix, norm_ffn, w_attn_qkv, attn_q_norm,
           attn_k_norm, attn_sinks, w_attn_o, w_gdn_in, gdn_conv_w, gdn_a_log, gdn_dt_bias, gdn_out_norm,
           w_gdn_out, w_ffn_up, ffn_conv_w, ffn_conv_b, w_ffn_down):
    P = dict(rel_bias_table=rel_bias_table, norm_mix=norm_mix, norm_ffn=norm_ffn, w_attn_qkv=w_attn_qkv,
             attn_q_norm=attn_q_norm, attn_k_norm=attn_k_norm, attn_sinks=attn_sinks, w_attn_o=w_attn_o,
             w_gdn_in=w_gdn_in, gdn_conv_w=gdn_conv_w, gdn_a_log=gdn_a_log, gdn_dt_bias=gdn_dt_bias,
             gdn_out_norm=gdn_out_norm, w_gdn_out=w_gdn_out, w_ffn_up=w_ffn_up, ffn_conv_w=ffn_conv_w,
             ffn_conv_b=ffn_conv_b, w_ffn_down=w_ffn_down)
    bp = x_prompt.shape[0]
    bs, ts, d = x_sample.shape
    n_c = bp + bs
    c_rows = -(-n_c // SUBLANES) * SUBLANES
    c_all = jnp.pad(jnp.concatenate([c_prompt, c_sample], axis=0), ((0, c_rows - n_c), (0, 0)))
    mod = ada_modulation(c_all, w_ada, b_ada)
    mod_prompt = mod[:, :bp]
    mod_sample = jnp.tile(mod[:, bp:n_c], (1, ts, 1))

    y_p, new_p = _trunk(x_prompt, mod_prompt, None, P, sample=False)
    states = dict(win_k=cache_win_k, win_v=cache_win_v, gdn=state_gdn, gdn_conv=state_gdn_conv,
                  ffn_conv=state_ffn_conv)
    y_s, new_s = _trunk(_to_time_major(x_sample)[None], mod_sample, states, P, sample=True)
    y_s = _from_time_major(y_s[0], bs)
    return (y_p, y_s, new_p['win_k'], new_p['win_v'], new_s['win_k'], new_s['win_v'],
            new_p['gdn'], new_s['gdn'], new_p['gdn_conv'], new_s['gdn_conv'],
            new_p['ffn_conv'], new_s['ffn_conv'])
```

```python
import functools
import math

import numpy as np
import jax
import jax.numpy as jnp
from jax import lax
from jax.experimental import pallas as pl
from jax.experimental.pallas import tpu as pltpu

HEAD_DIM = 64
N_HEADS = 32
N_KV_HEADS = 4
ATT_GROUP = N_HEADS // N_KV_HEADS
WINDOW = 128
ATT_BLOCK = 128
ATT_SAMPLE_SEQS = 8
N_BUCKETS = 32
MAX_DISTANCE = 128
NEG_INF = -1e30
PAST_LEN = 16384

GDN_K_HEADS = 16
GDN_V_HEADS = 32
GDN_DK = 128
GDN_DV = 128
GDN_KEY_DIM = GDN_K_HEADS * GDN_DK
GDN_VAL_DIM = GDN_V_HEADS * GDN_DV
GDN_CONV_DIM = 2 * GDN_KEY_DIM + GDN_VAL_DIM
GDN_CONV = 4
GDN_CHUNK = 64
GDN_HEAD_GROUPS = 4
GDN_VH_PER_GROUP = GDN_V_HEADS // GDN_HEAD_GROUPS
GDN_KH_PER_GROUP = GDN_K_HEADS // GDN_HEAD_GROUPS
GDN_PACK = 4
GDN_PHASE_A_CHUNKS = 4

FFN_CONV = 3
NORM_EPS = 1e-6

LANES = 128
SUBLANES = 8
VMEM_CAP_BYTES = 60 * 1024 * 1024
VMEM_SLACK_BYTES = 8 * 1024 * 1024

ROW_TILE_CAP = 1024
WEIGHT_CAST_K_PIECE = 512
GDN_IN_SUB_ROWS = 256
FFN_SUB_ROWS = 256
GDN_ROW_TILE_CAP = 1024

BF16 = jnp.bfloat16
F32 = jnp.float32


def _vmem_limit(block_bytes, scratch_bytes=0):
    est = 2 * int(block_bytes) + int(scratch_bytes) + VMEM_SLACK_BYTES
    return int(min(max(est, 16 * 1024 * 1024), VMEM_CAP_BYTES))


def _params(n_grid, block_bytes, scratch_bytes=0):
    return pltpu.CompilerParams(
        dimension_semantics=("arbitrary",) * n_grid,
        vmem_limit_bytes=_vmem_limit(block_bytes, scratch_bytes))


def _nbytes(shape, dtype):
    return int(np.prod(shape)) * jnp.dtype(dtype).itemsize


def _silu(x):
    return x * (1.0 / (1.0 + jnp.exp2(x * (-math.log2(math.e)))))


def _dot(a, b):
    return jnp.dot(a, b, preferred_element_type=F32)


def _dot_nt(a, b):
    return lax.dot_general(a, b, (((1,), (1,)), ((), ())), preferred_element_type=F32)


def _dot_tn(a, b):
    return lax.dot_general(a, b, (((0,), (0,)), ((), ())), preferred_element_type=F32)


def _dot_f32_weight(lhs, w_ref):
    k = lhs.shape[1]
    step = WEIGHT_CAST_K_PIECE if k % WEIGHT_CAST_K_PIECE == 0 else k
    acc = None
    for k0 in range(0, k, step):
        part = _dot(lhs[:, k0:k0 + step], w_ref[k0:k0 + step, :].astype(BF16))
        acc = part if acc is None else acc + part
    return acc


def _norm_mod(x, gain, shift, scale):
    ms = jnp.mean(x * x, axis=-1, keepdims=True)
    y = x * lax.rsqrt(ms + NORM_EPS) * gain
    return y * (1.0 + scale) + shift


def _mod_specs(shift, tm):
    d = shift.shape[-1]
    if shift.shape[1] == 1:
        return pl.BlockSpec((1, 1, d), lambda b, i, j: (b, 0, 0))
    return pl.BlockSpec((1, tm, d), lambda b, i, j: (b, i, 0))


def _ada_kernel(c_ref, w_ref, b_ref, o_ref):
    a = _silu(c_ref[...]).astype(BF16)
    o_ref[0] = _dot(a, w_ref[0].astype(BF16)) + b_ref[0]


def ada_modulation(c_all, w_ada, b_ada, tn=1024):
    rows, d = c_all.shape
    depth, _, n = w_ada.shape
    blocks = _nbytes((rows, d), F32) + _nbytes((d, tn), F32) + _nbytes((rows, tn), F32)
    return pl.pallas_call(
        _ada_kernel,
        grid=(depth, n // tn),
        in_specs=[pl.BlockSpec((rows, d), lambda l, j: (0, 0)),
                  pl.BlockSpec((1, d, tn), lambda l, j: (l, 0, j)),
                  pl.BlockSpec((1, 1, tn), lambda l, j: (l, 0, j))],
        out_specs=pl.BlockSpec((1, rows, tn), lambda l, j: (l, 0, j)),
        out_shape=jax.ShapeDtypeStruct((depth, rows, n), F32),
        compiler_params=_params(2, blocks, _nbytes((d, tn), BF16)),
        name="ada_modulation",
    )(c_all, w_ada, b_ada.reshape(depth, 1, n))


def _qkv_kernel(x_ref, gain_ref, shift_ref, scale_ref, w_ref, hgain_ref, hflag_ref, gmat_ref,
                o_ref, h_s):
    @pl.when(pl.program_id(2) == 0)
    def _():
        h_s[...] = _norm_mod(x_ref[0], gain_ref[...], shift_ref[0], scale_ref[0]).astype(BF16)

    y = _dot(h_s[...], w_ref[...].astype(BF16))
    ms = _dot((y * y).astype(BF16), gmat_ref[...])
    yn = y * lax.rsqrt(ms + NORM_EPS) * hgain_ref[...]
    o_ref[0] = jnp.where(hflag_ref[...] > 0.0, yn, y)


def qkv_projection(x, gain, shift, scale, w, q_gain, k_gain, tm, tn=512):
    b, l, d = x.shape
    n = w.shape[1]
    nq, nk = N_HEADS * HEAD_DIM, N_KV_HEADS * HEAD_DIM
    hgain = jnp.concatenate([jnp.tile(q_gain, N_HEADS), jnp.tile(k_gain, N_KV_HEADS),
                             jnp.ones((nk,), F32)]).reshape(1, n)
    hflag = jnp.concatenate([jnp.ones((nq + nk,), F32), jnp.zeros((nk,), F32)]).reshape(1, n)
    gidx = np.arange(tn) // HEAD_DIM
    gmat = jnp.asarray((gidx[:, None] == gidx[None, :]).astype(np.float32) / HEAD_DIM, BF16)
    blocks = (_nbytes((tm, d), F32) + _nbytes((d, tn), F32) + _nbytes((tm, tn), F32)
              + 2 * _nbytes((shift.shape[1] == 1 and 1 or tm, d), F32) + _nbytes((tn, tn), BF16))
    scratch = _nbytes((tm, d), BF16) + _nbytes((d, tn), BF16) + 4 * _nbytes((tm, tn), F32)
    return pl.pallas_call(
        _qkv_kernel,
        grid=(b, l // tm, n // tn),
        in_specs=[pl.BlockSpec((1, tm, d), lambda b_, i, j: (b_, i, 0)),
                  pl.BlockSpec((1, d), lambda b_, i, j: (0, 0)),
                  _mod_specs(shift, tm), _mod_specs(scale, tm),
                  pl.BlockSpec((d, tn), lambda b_, i, j: (0, j)),
                  pl.BlockSpec((1, tn), lambda b_, i, j: (0, j)),
                  pl.BlockSpec((1, tn), lambda b_, i, j: (0, j)),
                  pl.BlockSpec((tn, tn), lambda b_, i, j: (0, 0))],
        out_specs=pl.BlockSpec((1, tm, tn), lambda b_, i, j: (b_, i, j)),
        out_shape=jax.ShapeDtypeStruct((b, l, n), F32),
        scratch_shapes=[pltpu.VMEM((tm, d), BF16)],
        compiler_params=_params(3, blocks, scratch),
        name="qkv_projection",
    )(x, gain.reshape(1, d), shift, scale, w, hgain, hflag, gmat)


def _attn_prompt_kernel(sink_ref, q_ref, cur_ref, prev_ref, bprev_ref, bcur_ref, o_ref, kz_s, vz_s):
    n = pl.program_id(1)
    blk = ATT_BLOCK
    lane = lax.broadcasted_iota(jnp.int32, (2 * blk, LANES), 1)
    lo_half = lane < HEAD_DIM
    neg_prev = jnp.where(n == 0, NEG_INF, 0.0).astype(F32)

    kv_width = N_KV_HEADS * HEAD_DIM
    for pair in range(N_KV_HEADS // 2):
        for part, dst in ((0, kz_s), (1, vz_s)):
            col = part * kv_width + pair * LANES
            both = jnp.concatenate([prev_ref[0, :, col:col + LANES], cur_ref[0, :, col:col + LANES]], axis=0)
            swapped = pltpu.roll(both, HEAD_DIM, axis=1)
            zero = jnp.zeros_like(both)
            c0, c1 = 2 * pair, 2 * pair + 1
            dst[2 * c0 + 0] = jnp.where(lo_half, both, zero).astype(BF16)
            dst[2 * c0 + 1] = jnp.where(lo_half, zero, swapped).astype(BF16)
            dst[2 * c1 + 0] = jnp.where(lo_half, swapped, zero).astype(BF16)
            dst[2 * c1 + 1] = jnp.where(lo_half, zero, both).astype(BF16)

    scale = HEAD_DIM ** -0.5 * math.log2(math.e)
    for p in range(N_HEADS // 2):
        c = (2 * p) // ATT_GROUP
        qp = (q_ref[0, :, p * LANES:(p + 1) * LANES] * scale).astype(BF16)
        o_pair = None
        for a in range(2):
            h = 2 * p + a
            sink = sink_ref[h]
            s = _dot_nt(qp, kz_s[2 * c + a])
            s_prev = s[:, :blk] + bprev_ref[h] + neg_prev
            s_cur = s[:, blk:] + bcur_ref[h]
            m = jnp.maximum(jnp.max(jnp.maximum(s_prev, s_cur), axis=-1, keepdims=True), sink)
            e_prev = jnp.exp2(s_prev - m)
            e_cur = jnp.exp2(s_cur - m)
            den = jnp.sum(e_prev + e_cur, axis=-1, keepdims=True) + jnp.exp2(sink - m)
            pm = jnp.concatenate([e_prev, e_cur], axis=1).astype(BF16)
            o_a = _dot(pm, vz_s[2 * c + a]) * (1.0 / den)
            o_pair = o_a if o_pair is None else o_pair + o_a
        o_ref[0, :, p * LANES:(p + 1) * LANES] = o_pair.astype(o_ref.dtype)


def _t5_bucket_np(dist):
    max_exact = N_BUCKETS // 2
    d = np.maximum(dist, 0)
    df = np.maximum(d, 1).astype(np.float32)
    large = max_exact + (np.log(df / np.float32(max_exact)) / np.float32(math.log(MAX_DISTANCE / max_exact))
                         * np.float32(N_BUCKETS - max_exact)).astype(np.int32)
    large = np.minimum(large, N_BUCKETS - 1)
    return np.where(d < max_exact, d, large)


def _bias_from_dist(dist, in_band, rel_table):
    onehot = (_t5_bucket_np(dist)[..., None] == np.arange(N_BUCKETS)).astype(np.float32)
    tab = jnp.einsum('qsb,bh->hqs', jnp.asarray(onehot), rel_table.astype(F32), precision=lax.Precision.HIGHEST)
    return jnp.where(jnp.asarray(in_band)[None], tab, NEG_INF)


def attention_prompt(qkv, sinks, rel_table):
    b, l, n = qkv.shape
    blk = ATT_BLOCK
    nq = N_HEADS * HEAD_DIM
    kvw = 2 * N_KV_HEADS * HEAD_DIM
    kv_blk = nq // kvw
    qi = np.arange(blk)[:, None]
    sj = np.arange(blk)[None, :]
    d_prev = qi + blk - sj
    d_cur = qi - sj
    log2e = math.log2(math.e)
    bias_prev = _bias_from_dist(d_prev, (d_prev >= 0) & (d_prev <= WINDOW), rel_table) * log2e
    bias_cur = _bias_from_dist(d_cur, (d_cur >= 0) & (d_cur <= WINDOW), rel_table) * log2e
    blocks = (_nbytes((blk, nq), F32) + 2 * _nbytes((blk, kvw), F32) + _nbytes((blk, nq), BF16))
    scratch = 2 * _nbytes((2 * N_KV_HEADS, 2 * blk, LANES), BF16) + 4 * _nbytes((N_HEADS, blk, blk), F32)
    return pl.pallas_call(
        _attn_prompt_kernel,
        grid=(b, l // blk),
        in_specs=[pl.BlockSpec(memory_space=pltpu.SMEM),
                  pl.BlockSpec((1, blk, nq), lambda b_, i: (b_, i, 0)),
                  pl.BlockSpec((1, blk, kvw), lambda b_, i: (b_, i, kv_blk)),
                  pl.BlockSpec((1, blk, kvw), lambda b_, i: (b_, jnp.maximum(i - 1, 0), kv_blk)),
                  pl.BlockSpec((N_HEADS, blk, blk), lambda b_, i: (0, 0, 0)),
                  pl.BlockSpec((N_HEADS, blk, blk), lambda b_, i: (0, 0, 0))],
        out_specs=pl.BlockSpec((1, blk, nq), lambda b_, i: (b_, i, 0)),
        out_shape=jax.ShapeDtypeStruct((b, l, nq), BF16),
        scratch_shapes=[pltpu.VMEM((2 * N_KV_HEADS, 2 * blk, LANES), BF16),
                        pltpu.VMEM((2 * N_KV_HEADS, 2 * blk, LANES), BF16)],
        compiler_params=_params(2, blocks, scratch),
        name="attention_prompt",
    )(sinks.astype(F32) * log2e, qkv, qkv, qkv, bias_prev, bias_cur)


def _attn_sample_kernel(q_ref, k_ref, v_ref, bias_ref, sink_ref, o_ref):
    pairs = [(b, c) for b in range(q_ref.shape[0]) for c in range(N_KV_HEADS)]
    k = [k_ref[b].astype(BF16) for b in range(q_ref.shape[0])]
    s = {bc: _dot_nt(q_ref[bc[0], bc[1]].astype(BF16), k[bc[0]]) + bias_ref[bc[1]] for bc in pairs}
    e, den = {}, {}
    for b, c in pairs:
        sink = sink_ref[c]
        m = jnp.maximum(jnp.max(s[(b, c)], axis=-1, keepdims=True), sink)
        e[(b, c)] = jnp.exp(s[(b, c)] - m)
        den[(b, c)] = jnp.sum(e[(b, c)], axis=-1, keepdims=True) + jnp.exp(sink - m)
    for b, c in pairs:
        o_ref[b, c] = _dot(e[(b, c)].astype(BF16), v_ref[b].astype(BF16)) * (1.0 / den[(b, c)])


def attention_sample(qkv, cache_k, cache_v, sinks, rel_table):
    bsz, w = cache_k.shape[0], cache_k.shape[1]
    t = qkv.shape[0] // bsz
    nq, nkv = N_HEADS * HEAD_DIM, N_KV_HEADS * HEAD_DIM
    keys = w + t
    keys_pad = -(-keys // 16) * 16
    rows = ATT_GROUP * t
    q = qkv[:, :nq].reshape(t, bsz, N_KV_HEADS, ATT_GROUP, HEAD_DIM) * HEAD_DIM ** -0.5
    q = jnp.transpose(q, (1, 2, 3, 0, 4)).reshape(bsz, N_KV_HEADS, rows, HEAD_DIM)
    qz = jnp.einsum('bcrd,ck->bcrkd', q, jnp.eye(N_KV_HEADS, dtype=F32)).reshape(bsz, N_KV_HEADS, rows, nkv)
    new_kv = jnp.transpose(qkv[:, nq:].reshape(t, bsz, 2 * nkv), (1, 0, 2))
    k_all = jnp.concatenate([cache_k.reshape(bsz, w, nkv), new_kv[:, :, :nkv]], axis=1)
    v_all = jnp.concatenate([cache_v.reshape(bsz, w, nkv), new_kv[:, :, nkv:]], axis=1)
    pad = ((0, 0), (0, keys_pad - keys), (0, 0))
    k_pad, v_pad = jnp.pad(k_all, pad), jnp.pad(v_all, pad)
    tq = np.arange(t)[:, None]
    sk = np.arange(keys_pad)[None, :]
    dist = tq + w - sk
    in_band = (dist >= 0) & (dist <= WINDOW) & (sk < keys)
    bias_t = _bias_from_dist(dist, in_band, rel_table)
    bias = bias_t.reshape(N_KV_HEADS, rows, keys_pad)
    sink_rows = jnp.repeat(sinks.astype(F32), t).reshape(N_KV_HEADS, rows, 1)
    sb = math.gcd(bsz, ATT_SAMPLE_SEQS)
    blocks = sb * (_nbytes((N_KV_HEADS, rows, nkv), F32) * 2 + 2 * _nbytes((keys_pad, nkv), F32))
    oz = pl.pallas_call(
        _attn_sample_kernel,
        grid=(bsz // sb,),
        in_specs=[pl.BlockSpec((sb, N_KV_HEADS, rows, nkv), lambda b_: (b_, 0, 0, 0)),
                  pl.BlockSpec((sb, keys_pad, nkv), lambda b_: (b_, 0, 0)),
                  pl.BlockSpec((sb, keys_pad, nkv), lambda b_: (b_, 0, 0)),
                  pl.BlockSpec((N_KV_HEADS, rows, keys_pad), lambda b_: (0, 0, 0)),
                  pl.BlockSpec((N_KV_HEADS, rows, 1), lambda b_: (0, 0, 0))],
        out_specs=pl.BlockSpec((sb, N_KV_HEADS, rows, nkv), lambda b_: (b_, 0, 0, 0)),
        out_shape=jax.ShapeDtypeStruct((bsz, N_KV_HEADS, rows, nkv), F32),
        compiler_params=_params(1, blocks),
        name="attention_sample",
    )(qz, k_pad, v_pad, bias, sink_rows)
    o = jnp.stack([oz[:, c, :, c * HEAD_DIM:(c + 1) * HEAD_DIM] for c in range(N_KV_HEADS)], axis=1)
    o = o.reshape(bsz, N_KV_HEADS, ATT_GROUP, t, HEAD_DIM)
    o = jnp.transpose(o, (3, 0, 1, 2, 4)).reshape(t * bsz, nq)
    return o.astype(BF16), k_all[:, t:], v_all[:, t:]


def _mm_res_kernel(a_ref, w_ref, x_ref, g_ref, o_ref):
    y = _dot_f32_weight(a_ref[0], w_ref)
    o_ref[0] = x_ref[0] + g_ref[0] * y


def _mm_res_wres_kernel(a_ref, w_ref, x_ref, g_ref, o_ref, wbf_s):
    @pl.when((pl.program_id(1) == 0) & (pl.program_id(2) == 0))
    def _():
        wbf_s[...] = w_ref[...].astype(BF16)

    o_ref[0] = x_ref[0] + g_ref[0] * _dot(a_ref[0], wbf_s[...])


def matmul_gate_residual_wres(a, w_stack, layer, x, gate, tm, tn):
    b, l, k = a.shape
    n = w_stack.shape[2]
    if gate.shape[1] == 1:
        g_spec = pl.BlockSpec((1, 1, tn), lambda j, b_, i: (b_, 0, j))
    else:
        g_spec = pl.BlockSpec((1, tm, tn), lambda j, b_, i: (b_, i, j))
    blocks = _nbytes((tm, k), a.dtype) + _nbytes((k, tn), F32) + 3 * _nbytes((tm, tn), F32)
    scratch = _nbytes((k, tn), BF16) + _nbytes((tm, tn), F32)
    return pl.pallas_call(
        _mm_res_wres_kernel,
        grid=(n // tn, b, l // tm),
        in_specs=[pl.BlockSpec((1, tm, k), lambda j, b_, i: (b_, i, 0)),
                  pl.BlockSpec((None, k, tn), lambda j, b_, i: (layer, 0, j)),
                  pl.BlockSpec((1, tm, tn), lambda j, b_, i: (b_, i, j)),
                  g_spec],
        out_specs=pl.BlockSpec((1, tm, tn), lambda j, b_, i: (b_, i, j)),
        out_shape=jax.ShapeDtypeStruct((b, l, n), F32),
        scratch_shapes=[pltpu.VMEM((k, tn), BF16)],
        compiler_params=_params(3, blocks, scratch),
        name="matmul_gate_residual_wres",
    )(a, w_stack, x, gate)


def matmul_gate_residual(a, w_stack, layer, x, gate, tm, tn):
    b, l, k = a.shape
    n = w_stack.shape[2]
    grows = 1 if gate.shape[1] == 1 else tm
    if gate.shape[1] == 1:
        g_spec = pl.BlockSpec((1, 1, tn), lambda b_, i, j: (b_, 0, j))
    else:
        g_spec = pl.BlockSpec((1, tm, tn), lambda b_, i, j: (b_, i, j))
    blocks = (_nbytes((tm, k), a.dtype) + _nbytes((k, tn), F32) + 2 * _nbytes((tm, tn), F32)
              + _nbytes((grows, tn), F32))
    scratch = _nbytes((k, tn), BF16) + _nbytes((tm, tn), F32)
    return pl.pallas_call(
        _mm_res_kernel,
        grid=(b, l // tm, n // tn),
        in_specs=[pl.BlockSpec((1, tm, k), lambda b_, i, j: (b_, i, 0)),
                  pl.BlockSpec((None, k, tn), lambda b_, i, j: (layer, 0, j)),
                  pl.BlockSpec((1, tm, tn), lambda b_, i, j: (b_, i, j)),
                  g_spec],
        out_specs=pl.BlockSpec((1, tm, tn), lambda b_, i, j: (b_, i, j)),
        out_shape=jax.ShapeDtypeStruct((b, l, n), F32),
        compiler_params=_params(3, blocks, scratch),
        name="matmul_gate_residual",
    )(a, w_stack, x, gate)


def _ffn_conv_rows(u, prev1, prev2, cw_ref, cb_ref):
    return u * cw_ref[2:3, :] + prev1 * cw_ref[1:2, :] + prev2 * cw_ref[0:1, :] + cb_ref[...]


def _ffn_up_prompt_kernel(x_ref, gain_ref, shift_ref, scale_ref, wg_ref, wv_ref, cwg_ref, cwv_ref,
                          cbg_ref, cbv_ref, o_ref, tail_ref, h_s, carry_s, ext_s):
    i = pl.program_id(1)
    j = pl.program_id(2)
    tm = h_s.shape[0]

    @pl.when(j == 0)
    def _():
        h_s[...] = _norm_mod(x_ref[0], gain_ref[...], shift_ref[0], scale_ref[0]).astype(BF16)

    @pl.when(i == 0)
    def _():
        carry_s[j] = jnp.zeros(carry_s.shape[1:], F32)

    halves = ((wg_ref, cwg_ref, cbg_ref), (wv_ref, cwv_ref, cbv_ref))
    w_bf = [w_ref[...].astype(BF16) for w_ref, _, _ in halves]
    for half in range(2):
        ext_s[half, 0:SUBLANES, :] = carry_s[j, half]
    sub = min(FFN_SUB_ROWS, tm)
    for m in range(tm // sub):
        hm = h_s[m * sub:(m + 1) * sub, :]
        r0 = SUBLANES + m * sub
        ys = []
        for half, (_, cw_ref, cb_ref) in enumerate(halves):
            u = _dot(hm, w_bf[half])
            ext_s[half, r0:r0 + sub, :] = u
            ys.append(_ffn_conv_rows(u, ext_s[half, r0 - 1:r0 - 1 + sub, :], ext_s[half, r0 - 2:r0 - 2 + sub, :],
                                     cw_ref, cb_ref))
        o_ref[0, m * sub:(m + 1) * sub, :] = (_silu(ys[0]) * ys[1]).astype(o_ref.dtype)
    for half in range(2):
        tail = ext_s[half, tm:tm + SUBLANES, :]
        carry_s[j, half] = tail
        tail_ref[0, 0, half] = tail


def ffn_up_prompt(x, gain, shift, scale, w_up, conv_w, conv_b, layer, tm, tn=512):
    b, l, d = x.shape
    dff = w_up.shape[2] // 2
    nj = dff // tn
    conv_b = conv_b.reshape(conv_b.shape[0], 1, 2 * dff)
    blocks = (_nbytes((tm, d), F32) + 2 * _nbytes((d, tn), F32) + _nbytes((tm, tn), BF16)
              + 2 * _nbytes((1, d), F32) + 8 * _nbytes((SUBLANES, tn), F32))
    scratch = (_nbytes((tm, d), BF16) + _nbytes((nj, 2, SUBLANES, tn), F32) + 2 * _nbytes((d, tn), BF16)
               + 8 * _nbytes((tm, tn), F32))
    act, tail = pl.pallas_call(
        _ffn_up_prompt_kernel,
        grid=(b, l // tm, nj),
        in_specs=[pl.BlockSpec((1, tm, d), lambda b_, i, j: (b_, i, 0)),
                  pl.BlockSpec((1, d), lambda b_, i, j: (0, 0)),
                  _mod_specs(shift, tm), _mod_specs(scale, tm),
                  pl.BlockSpec((None, d, tn), lambda b_, i, j: (layer, 0, j)),
                  pl.BlockSpec((None, d, tn), lambda b_, i, j: (layer, 0, j + nj)),
                  pl.BlockSpec((None, FFN_CONV, tn), lambda b_, i, j: (layer, 0, j)),
                  pl.BlockSpec((None, FFN_CONV, tn), lambda b_, i, j: (layer, 0, j + nj)),
                  pl.BlockSpec((None, 1, tn), lambda b_, i, j: (layer, 0, j)),
                  pl.BlockSpec((None, 1, tn), lambda b_, i, j: (layer, 0, j + nj))],
        out_specs=[pl.BlockSpec((1, tm, tn), lambda b_, i, j: (b_, i, j)),
                   pl.BlockSpec((1, 1, 2, SUBLANES, tn), lambda b_, i, j: (b_, i, 0, 0, j))],
        out_shape=[jax.ShapeDtypeStruct((b, l, dff), BF16),
                   jax.ShapeDtypeStruct((b, l // tm, 2, SUBLANES, dff), F32)],
        scratch_shapes=[pltpu.VMEM((tm, d), BF16), pltpu.VMEM((nj, 2, SUBLANES, tn), F32),
                        pltpu.VMEM((2, SUBLANES + tm, tn), F32)],
        compiler_params=_params(3, blocks, scratch),
        name="ffn_up_prompt",
    )(x, gain.reshape(1, d), shift, scale, w_up, w_up, conv_w, conv_w, conv_b, conv_b)
    keep = FFN_CONV - 1
    hist = jnp.transpose(tail[:, -1, :, SUBLANES - keep:, :], (0, 2, 1, 3)).reshape(b, keep, 2 * dff)
    return act, hist


def _ffn_up_sample_kernel(x_ref, gain_ref, shift_ref, scale_ref, wg_ref, wv_ref, cwg_ref, cwv_ref,
                          cbg_ref, cbv_ref, hg_ref, hv_ref, o_ref, ng_ref, nv_ref, h_s, *, bsz):
    @pl.when(pl.program_id(0) == 0)
    def _():
        h_s[...] = _norm_mod(x_ref[...], gain_ref[...], shift_ref[...], scale_ref[...]).astype(BF16)

    h = h_s[...]
    rows = h_s.shape[0]
    keep = FFN_CONV - 1
    ys = []
    for w_ref, cw_ref, cb_ref, hist_ref, new_ref in ((wg_ref, cwg_ref, cbg_ref, hg_ref, ng_ref),
                                                     (wv_ref, cwv_ref, cbv_ref, hv_ref, nv_ref)):
        u = _dot(h, w_ref[...].astype(BF16))
        ext = jnp.concatenate([hist_ref[...], u], axis=0)
        prev = [ext[(keep - k) * bsz:(keep - k) * bsz + rows] for k in (1, 2)]
        ys.append(_ffn_conv_rows(u, prev[0], prev[1], cw_ref, cb_ref))
        new_ref[...] = ext[rows:rows + keep * bsz]
    o_ref[...] = (_silu(ys[0]) * ys[1]).astype(o_ref.dtype)


def ffn_up_sample(x, gain, shift, scale, w_up, conv_w, conv_b, layer, hist, tn=512):
    rows, d = x.shape
    dff = w_up.shape[2] // 2
    nj = dff // tn
    keep = FFN_CONV - 1
    bsz = hist.shape[0] // keep
    conv_b = conv_b.reshape(conv_b.shape[0], 1, 2 * dff)
    col = lambda j: (0, j)
    col_hi = lambda j: (0, j + nj)
    lcol = lambda j: (layer, 0, j)
    lcol_hi = lambda j: (layer, 0, j + nj)
    blocks = (_nbytes((rows, d), F32) * 3 + 2 * _nbytes((d, tn), F32) + 5 * _nbytes((rows, tn), F32))
    scratch = _nbytes((rows, d), BF16) + 2 * _nbytes((d, tn), BF16) + 8 * _nbytes((rows, tn), F32)
    act, new_g, new_v = pl.pallas_call(
        functools.partial(_ffn_up_sample_kernel, bsz=bsz),
        grid=(nj,),
        in_specs=[pl.BlockSpec((rows, d), lambda j: (0, 0)),
                  pl.BlockSpec((1, d), lambda j: (0, 0)),
                  pl.BlockSpec((rows, d), lambda j: (0, 0)),
                  pl.BlockSpec((rows, d), lambda j: (0, 0)),
                  pl.BlockSpec((None, d, tn), lcol), pl.BlockSpec((None, d, tn), lcol_hi),
                  pl.BlockSpec((None, FFN_CONV, tn), lcol), pl.BlockSpec((None, FFN_CONV, tn), lcol_hi),
                  pl.BlockSpec((None, 1, tn), lcol), pl.BlockSpec((None, 1, tn), lcol_hi),
                  pl.BlockSpec((keep * bsz, tn), col), pl.BlockSpec((keep * bsz, tn), col_hi)],
        out_specs=[pl.BlockSpec((rows, tn), col), pl.BlockSpec((keep * bsz, tn), col),
                   pl.BlockSpec((keep * bsz, tn), col)],
        out_shape=[jax.ShapeDtypeStruct((rows, dff), BF16),
                   jax.ShapeDtypeStruct((keep * bsz, dff), F32),
                   jax.ShapeDtypeStruct((keep * bsz, dff), F32)],
        scratch_shapes=[pltpu.VMEM((rows, d), BF16)],
        compiler_params=_params(1, blocks, scratch),
        name="ffn_up_sample",
    )(x, gain.reshape(1, d), shift, scale, w_up, w_up, conv_w, conv_w, conv_b, conv_b, hist, hist)
    return act, jnp.concatenate([new_g, new_v], axis=-1)


def _gdn_in_kernel(*refs, n_qk_tiles, n_conv_tiles, bsz):
    if bsz is None:
        (x_ref, gain_ref, shift_ref, scale_ref, w_ref, wba_ref, alog_ref, dtb_ref, cw_ref,
         o_ref, bg_ref, raw_ref, h_s, carry_s) = refs
    else:
        (x_ref, gain_ref, shift_ref, scale_ref, w_ref, wba_ref, alog_ref, dtb_ref, cw_ref, hist_ref,
         o_ref, bg_ref, raw_ref, h_s) = refs
    i = pl.program_id(1)
    j = pl.program_id(2)
    tm = h_s.shape[0]
    width = GDN_CONV

    @pl.when(j == 0)
    def _():
        h = _norm_mod(x_ref[0], gain_ref[...], shift_ref[0], scale_ref[0]).astype(BF16)
        h_s[...] = h
        y = _dot_nt(h, wba_ref[...].astype(BF16))
        lane = lax.broadcasted_iota(jnp.int32, y.shape, 1) % LANES
        beta = 1.0 / (1.0 + jnp.exp(-y))
        a = y + dtb_ref[...]
        softplus = jnp.maximum(a, 0.0) + jnp.log1p(jnp.exp(-jnp.abs(a)))
        g = -jnp.exp(alog_ref[...]) * softplus
        bg_ref[0] = jnp.where(lane < GDN_VH_PER_GROUP, beta, jnp.where(lane < 2 * GDN_VH_PER_GROUP, g, 0.0))

    sub = min(GDN_IN_SUB_ROWS, tm)
    row8 = lax.broadcasted_iota(jnp.int32, (SUBLANES, 1), 0)

    def conv_tile(l2_scale):
        w_bf = w_ref[...].astype(BF16)
        jc = jnp.minimum(j, n_conv_tiles - 1)
        if bsz is None:
            @pl.when(i == 0)
            def _():
                carry_s[jc] = jnp.zeros(carry_s.shape[1:], F32)
            prev_tail = carry_s[jc]
        for m in range(tm // sub):
            rs = slice(m * sub, (m + 1) * sub)
            u = _dot_nt(h_s[rs, :], w_bf)
            y = u * cw_ref[width - 1:width, :]
            if bsz is None:
                top = u[0:SUBLANES]
                y_top = top * cw_ref[width - 1:width, :]
                for k in range(1, width):
                    wk = cw_ref[width - 1 - k:width - k, :]
                    y = y + pltpu.roll(u, k, axis=0) * wk
                    y_top = y_top + jnp.where(row8 < k, pltpu.roll(prev_tail, k, axis=0),
                                              pltpu.roll(top, k, axis=0)) * wk
                y = jnp.concatenate([y_top, y[SUBLANES:]], axis=0)
                prev_tail = u[sub - SUBLANES:sub]
            else:
                ext = jnp.concatenate([hist_ref[...], u], axis=0)
                for k in range(1, width):
                    wk = cw_ref[width - 1 - k:width - k, :]
                    y = y + ext[(width - 1 - k) * bsz:(width - 1 - k) * bsz + sub] * wk
                raw_ref[...] = ext[sub:sub + (width - 1) * bsz]
            y = _silu(y)
            if l2_scale is not None:
                heads = [y[:, hd * LANES:(hd + 1) * LANES] for hd in range(y.shape[1] // LANES)]
                y = jnp.concatenate(
                    [yh * (lax.rsqrt(jnp.sum(yh * yh, axis=-1, keepdims=True) + NORM_EPS) * l2_scale)
                     for yh in heads], axis=1)
            o_ref[0, rs, :] = y.astype(o_ref.dtype)
        if bsz is None:
            carry_s[jc] = prev_tail
            raw_ref[0, 0] = prev_tail

    @pl.when(j < n_qk_tiles)
    def _():
        conv_tile(jnp.where(j < n_qk_tiles // 2, GDN_DK ** -0.5, 1.0).astype(F32))

    @pl.when((j >= n_qk_tiles) & (j < n_conv_tiles))
    def _():
        conv_tile(None)

    @pl.when(j >= n_conv_tiles)
    def _():
        w_bf = w_ref[...].astype(BF16)
        for m in range(tm // sub):
            rs = slice(m * sub, (m + 1) * sub)
            o_ref[0, rs, :] = _silu(_dot_nt(h_s[rs, :], w_bf)).astype(o_ref.dtype)


def _group_lane_layout(vec_b, vec_a):
    lead = vec_b.shape[:-1]
    vb = vec_b.reshape(*lead, GDN_HEAD_GROUPS, GDN_VH_PER_GROUP)
    va = vec_a.reshape(*lead, GDN_HEAD_GROUPS, GDN_VH_PER_GROUP)
    pad = jnp.zeros((*lead, GDN_HEAD_GROUPS, LANES - 2 * GDN_VH_PER_GROUP), vec_b.dtype)
    return jnp.concatenate([vb, va, pad], axis=-1).reshape(*lead, GDN_HEAD_GROUPS * LANES)


def gdn_in_projection(x, gain, shift, scale, w_in, conv_w, a_log, dt_bias, tm, conv_hist=None, tn=512):
    b, l, d = x.shape
    n_main = GDN_CONV_DIM + GDN_VAL_DIM
    w_t = jnp.transpose(w_in)
    wba = _group_lane_layout(w_in[:, n_main:n_main + GDN_V_HEADS], w_in[:, n_main + GDN_V_HEADS:])
    wba_t = jnp.transpose(wba)
    zeros = jnp.zeros((1, GDN_V_HEADS), F32)
    alog = _group_lane_layout(zeros, a_log.reshape(1, -1).astype(F32))
    dtb = _group_lane_layout(zeros, dt_bias.reshape(1, -1).astype(F32))
    nbg = wba_t.shape[0]
    n_conv_tiles = GDN_CONV_DIM // tn
    n_qk_tiles = 2 * GDN_KEY_DIM // tn
    conv_col = lambda b_, i, j: (0, jnp.minimum(j, n_conv_tiles - 1))
    in_specs = [pl.BlockSpec((1, tm, d), lambda b_, i, j: (b_, i, 0)),
                pl.BlockSpec((1, d), lambda b_, i, j: (0, 0)),
                _mod_specs(shift, tm), _mod_specs(scale, tm),
                pl.BlockSpec((tn, d), lambda b_, i, j: (j, 0)),
                pl.BlockSpec((nbg, d), lambda b_, i, j: (0, 0)),
                pl.BlockSpec((1, nbg), lambda b_, i, j: (0, 0)),
                pl.BlockSpec((1, nbg), lambda b_, i, j: (0, 0)),
                pl.BlockSpec((GDN_CONV, tn), conv_col)]
    args = [x, gain.reshape(1, d), shift, scale, w_t, wba_t, alog, dtb, conv_w]
    scratch_shapes = [pltpu.VMEM((tm, d), BF16)]
    blocks = (_nbytes((tm, d), F32) + _nbytes((tn, d), F32) + _nbytes((tm, tn), BF16)
              + _nbytes((nbg, d), F32) + _nbytes((tm, nbg), F32) + 2 * _nbytes((1, d), F32))
    scratch = _nbytes((tm, d), BF16) + _nbytes((tn, d), BF16) + 6 * _nbytes((tm, nbg), F32)
    if conv_hist is None:
        bsz = None
        raw_spec = pl.BlockSpec((1, 1, SUBLANES, tn), lambda b_, i, j: (b_, i, 0, jnp.minimum(j, n_conv_tiles - 1)))
        raw_shape = jax.ShapeDtypeStruct((b, l // tm, SUBLANES, GDN_CONV_DIM), F32)
        scratch_shapes.append(pltpu.VMEM((n_conv_tiles, SUBLANES, tn), F32))
    else:
        assert b == 1 and l == tm <= GDN_IN_SUB_ROWS
        hist_rows = conv_hist.shape[0]
        bsz = hist_rows // (GDN_CONV - 1)
        args.append(conv_hist)
        in_specs.append(pl.BlockSpec((hist_rows, tn), conv_col))
        raw_spec = pl.BlockSpec((hist_rows, tn), conv_col)
        raw_shape = jax.ShapeDtypeStruct((hist_rows, GDN_CONV_DIM), F32)
        blocks += 2 * _nbytes((hist_rows, tn), F32)
    return pl.pallas_call(
        functools.partial(_gdn_in_kernel, n_qk_tiles=n_qk_tiles, n_conv_tiles=n_conv_tiles, bsz=bsz),
        grid=(b, l // tm, n_main // tn),
        in_specs=in_specs,
        out_specs=[pl.BlockSpec((1, tm, tn), lambda b_, i, j: (b_, i, j)),
                   pl.BlockSpec((1, tm, nbg), lambda b_, i, j: (b_, i, 0)),
                   raw_spec],
        out_shape=[jax.ShapeDtypeStruct((b, l, n_main), BF16),
                   jax.ShapeDtypeStruct((b, l, nbg), F32),
                   raw_shape],
        scratch_shapes=scratch_shapes,
        compiler_params=_params(3, blocks, scratch),
        name="gdn_in_projection",
    )(*args)


def _gdn_core_kernel(q_ref, k_ref, v_ref, zg_ref, bg_ref, onorm_ref, s0_ref, o_ref, s1_ref,
                     s_s, gc_s, tdec_s, mo_s, *, seq_rows):
    r = pl.program_id(2)
    n_r = pl.num_programs(2)
    rows = q_ref.shape[1]
    chunk = GDN_CHUNK
    vpg = GDN_VH_PER_GROUP
    rep = GDN_V_HEADS // GDN_K_HEADS
    n_kh = GDN_KH_PER_GROUP
    seg = chunk if seq_rows is None else seq_rows
    n_seq = chunk // seg

    def load_states():
        for s in range(n_seq):
            for kh in range(n_kh):
                s_s[s * n_kh + kh] = jnp.concatenate([s0_ref[s, kh * rep + e] for e in range(rep)], axis=1)

    if seq_rows is None:
        pl.when(r == 0)(load_states)
    else:
        load_states()

    def seg_last(x):
        if n_seq == 1:
            return x[chunk - 1:chunk, :]
        blocks = x.reshape(n_seq, seg, x.shape[1])
        return jnp.broadcast_to(blocks[:, seg - 1:seg, :], blocks.shape).reshape(x.shape)

    n_chunks = rows // chunk
    pw = GDN_PACK * chunk
    n_packs = vpg // GDN_PACK
    ri = lax.broadcasted_iota(jnp.int32, (chunk, pw), 0)
    li = lax.broadcasted_iota(jnp.int32, (chunk, pw), 1) % chunk
    seg_shift = int(math.log2(seg))
    same_seq = (ri >> seg_shift) == (li >> seg_shift)
    tril_p = (ri >= li) & same_seq
    strict_p = (ri > li) & same_seq
    eye_p = (ri == li).astype(F32)
    n_levels = seg_shift
    off_masks = [((ri >> (lvl + 1)) == (li >> (lvl + 1))) & ((ri >> lvl) != (li >> lvl))
                 for lvl in range(n_levels)]
    bd_rows = lax.broadcasted_iota(jnp.int32, (pw, pw), 0) // chunk
    bd_cols = lax.broadcasted_iota(jnp.int32, (pw, pw), 1) // chunk
    bd_mask = (bd_rows == bd_cols).astype(BF16)
    row_c = lax.broadcasted_iota(jnp.int32, (chunk, LANES), 0)
    lane_lo = lax.broadcasted_iota(jnp.int32, (chunk, LANES), 1) < chunk

    def block_diag(xp):
        return jnp.concatenate([xp.astype(BF16)] * GDN_PACK, axis=0) * bd_mask

    def pack_cols(arr, first_col):
        tiles = []
        for t in range(pw // LANES):
            even = jnp.broadcast_to(arr[:, first_col + 2 * t:first_col + 2 * t + 1], (chunk, LANES))
            odd = jnp.broadcast_to(arr[:, first_col + 2 * t + 1:first_col + 2 * t + 2], (chunk, LANES))
            tiles.append(jnp.where(lane_lo, even, odd))
        return jnp.concatenate(tiles, axis=1)

    for c0 in range(0, n_chunks, GDN_PHASE_A_CHUNKS):
        group = [(c, p) for c in range(c0, min(c0 + GDN_PHASE_A_CHUNKS, n_chunks)) for p in range(n_packs)]
        a_list, qkd, brow, dec = {}, {}, {}, {}
        for c in range(c0, min(c0 + GDN_PHASE_A_CHUNKS, n_chunks)):
            rs = slice(c * chunk, (c + 1) * chunk)
            bg = bg_ref[0, rs, :]
            cum = bg
            shift = 1
            while shift < seg:
                cum = cum + jnp.where((row_c & (seg - 1)) >= shift, pltpu.roll(cum, shift, axis=0), 0.0)
                shift *= 2
            gc_s[rs, :] = cum
            gq = []
            for kh in range(GDN_KH_PER_GROUP):
                ksl = slice(kh * LANES, (kh + 1) * LANES)
                k = k_ref[0, rs, ksl]
                kq = jnp.concatenate([k, q_ref[0, rs, ksl]], axis=0)
                kk = jnp.concatenate([k, k], axis=0)
                gq.append(_dot_nt(kq, kk))
            for p in range(n_packs):
                khs = [(p * GDN_PACK + 2 * t) // rep for t in range(pw // LANES)]
                gram = jnp.concatenate([gq[kh][:chunk] for kh in khs], axis=1)
                qk = jnp.concatenate([gq[kh][chunk:] for kh in khs], axis=1)
                beta_p = pack_cols(bg, p * GDN_PACK)
                gcol_p = pack_cols(cum, vpg + p * GDN_PACK)
                grow_p = jnp.sum(gcol_p * eye_p, axis=0, keepdims=True)
                decay = jnp.exp(jnp.where(tril_p, gcol_p - grow_p, NEG_INF))
                a_list[(c, p)] = jnp.where(strict_p, gram * beta_p * decay, 0.0)
                qkd[(c, p)] = (qk * decay).astype(BF16)
                brow[(c, p)] = jnp.sum(beta_p * eye_p, axis=0, keepdims=True)
                dec[(c, p)] = jnp.exp(seg_last(gcol_p) - gcol_p)
        inv = {cp: eye_p - jnp.where(off_masks[0], a_list[cp], 0.0) for cp in group}
        for lvl in range(1, n_levels):
            w = {cp: _dot(jnp.where(off_masks[lvl], a_list[cp], 0.0).astype(BF16), block_diag(inv[cp]))
                 for cp in group}
            inv = {cp: inv[cp] - _dot(inv[cp].astype(BF16), block_diag(w[cp])) for cp in group}
        for cp in group:
            t_beta = inv[cp] * brow[cp]
            tdec_s[cp[0], cp[1]] = block_diag(dec[cp] * t_beta)
            mo_s[cp[0], cp[1]] = block_diag(_dot(qkd[cp], block_diag(t_beta)))

    def chunk_body(c, carry):
        rs = pl.ds(pl.multiple_of(c * chunk, chunk), chunk)
        cum = gc_s[rs, :]
        eg, g_last, ks, qs = [], [], [], []
        for kh in range(n_kh):
            ksl = slice(kh * LANES, (kh + 1) * LANES)
            kq = jnp.concatenate([k_ref[0, rs, ksl], q_ref[0, rs, ksl]], axis=0)
            per_seq = [_dot(kq, s_s[s * n_kh + kh].astype(BF16)) for s in range(n_seq)]
            kq_s = jnp.concatenate([per_seq[s][half * chunk + s * seg:half * chunk + (s + 1) * seg]
                                    for half in range(2) for s in range(n_seq)], axis=0)
            for e in range(rep):
                hv = kh * rep + e
                gcol = cum[:, vpg + hv:vpg + hv + 1]
                g_last.append([gcol[(s + 1) * seg - 1:(s + 1) * seg, :] for s in range(n_seq)])
                eg.append(jnp.exp(gcol))
                ks.append(kq_s[:chunk, e * LANES:(e + 1) * LANES] * eg[hv])
                qs.append(kq_s[chunk:, e * LANES:(e + 1) * LANES] * eg[hv])
        v_dec, o_intra = [], []
        for p in range(n_packs):
            heads = range(p * GDN_PACK, (p + 1) * GDN_PACK)
            rhs = jnp.concatenate([v_ref[0, rs, hv * LANES:(hv + 1) * LANES].astype(F32) - ks[hv]
                                   for hv in heads], axis=0).astype(BF16)
            vd = _dot(tdec_s[c, p], rhs)
            oi = _dot(mo_s[c, p], rhs)
            for j in range(GDN_PACK):
                v_dec.append(vd[j * chunk:(j + 1) * chunk])
                o_intra.append(oi[j * chunk:(j + 1) * chunk])
        for hv in range(vpg):
            vsl = slice(hv * LANES, (hv + 1) * LANES)
            o = qs[hv] + o_intra[hv]
            on = o * lax.rsqrt(jnp.mean(o * o, axis=-1, keepdims=True) + NORM_EPS) * onorm_ref[...]
            o_ref[0, rs, vsl] = (on * zg_ref[0, rs, vsl].astype(F32)).astype(o_ref.dtype)
        lane_v = lax.broadcasted_iota(jnp.int32, (1, rep * LANES), 1) // LANES
        row_seq = lax.broadcasted_iota(jnp.int32, (chunk, 1), 0) >> seg_shift
        for kh in range(n_kh):
            heads = range(kh * rep, (kh + 1) * rep)
            k_bf = k_ref[0, rs, kh * LANES:(kh + 1) * LANES]
            vd_pair = jnp.concatenate([v_dec[hv] for hv in heads], axis=1)
            for s in range(n_seq):
                vd_s = vd_pair if n_seq == 1 else jnp.where(row_seq == s, vd_pair, 0.0)
                ds = _dot_tn(k_bf, vd_s.astype(BF16))
                keep = jnp.exp(g_last[kh * rep][s])
                for e in range(1, rep):
                    keep = jnp.where(lane_v == e, jnp.exp(g_last[kh * rep + e][s]), keep)
                s_s[s * n_kh + kh] = s_s[s * n_kh + kh] * keep + ds
        return carry

    lax.fori_loop(0, n_chunks, chunk_body, 0)

    def store_states():
        for s in range(n_seq):
            for hv in range(vpg):
                s1_ref[s, hv] = s_s[s * n_kh + hv // rep][:, (hv % rep) * LANES:(hv % rep + 1) * LANES]

    if seq_rows is None:
        pl.when(r == n_r - 1)(store_states)
    else:
        store_states()


def gdn_core(qkvz, bg, s0, out_norm, rows, seq_rows=None):
    b, l, _ = qkvz.shape
    n_state = 1 if seq_rows is None else GDN_CHUNK // seq_rows
    if seq_rows is None:
        state_idx = lambda b_, g, r: (b_, g, 0, 0)
    else:
        assert b == 1 and rows == GDN_CHUNK
        state_idx = lambda b_, g, r: (r, g, 0, 0)
    hg = GDN_HEAD_GROUPS
    qw = GDN_KEY_DIM // hg
    vw = GDN_VAL_DIM // hg
    kq = GDN_KEY_DIM // qw
    kv = 2 * GDN_KEY_DIM // vw
    kz = GDN_CONV_DIM // vw
    vpg = GDN_VH_PER_GROUP
    blocks = (2 * _nbytes((rows, qw), BF16) + 3 * _nbytes((rows, vw), BF16) + _nbytes((rows, LANES), F32)
              + 2 * _nbytes((n_state, vpg, GDN_DK, GDN_DV), F32))
    n_chunks = rows // GDN_CHUNK
    n_packs = vpg // GDN_PACK
    pw = GDN_PACK * GDN_CHUNK
    scratch = (_nbytes((n_state, vpg, GDN_DK, GDN_DV), F32) + _nbytes((rows, LANES), F32)
               + 2 * _nbytes((n_chunks, n_packs, pw, pw), BF16) + 6 * _nbytes((rows, vw), F32))
    return pl.pallas_call(
        functools.partial(_gdn_core_kernel, seq_rows=seq_rows),
        grid=(b, hg, l // rows),
        in_specs=[pl.BlockSpec((1, rows, qw), lambda b_, g, r: (b_, r, g)),
                  pl.BlockSpec((1, rows, qw), lambda b_, g, r: (b_, r, kq + g)),
                  pl.BlockSpec((1, rows, vw), lambda b_, g, r: (b_, r, kv + g)),
                  pl.BlockSpec((1, rows, vw), lambda b_, g, r: (b_, r, kz + g)),
                  pl.BlockSpec((1, rows, LANES), lambda b_, g, r: (b_, r, g)),
                  pl.BlockSpec((1, GDN_DV), lambda b_, g, r: (0, 0)),
                  pl.BlockSpec((n_state, vpg, GDN_DK, GDN_DV), state_idx)],
        out_specs=[pl.BlockSpec((1, rows, vw), lambda b_, g, r: (b_, r, g)),
                   pl.BlockSpec((n_state, vpg, GDN_DK, GDN_DV), state_idx)],
        out_shape=[jax.ShapeDtypeStruct((b, l, GDN_VAL_DIM), BF16),
                   jax.ShapeDtypeStruct(s0.shape, F32)],
        scratch_shapes=[pltpu.VMEM((n_state * GDN_KH_PER_GROUP, GDN_DK,
                                    (GDN_V_HEADS // GDN_K_HEADS) * GDN_DV), F32),
                        pltpu.VMEM((rows, LANES), F32),
                        pltpu.VMEM((n_chunks, n_packs, pw, pw), BF16),
                        pltpu.VMEM((n_chunks, n_packs, pw, pw), BF16)],
        compiler_params=_params(3, blocks, scratch),
        name="gdn_core",
    )(qkvz, qkvz, qkvz, qkvz, bg, out_norm.reshape(1, GDN_DV).astype(F32), s0)


def _tile_rows(l, cap):
    t = min(l, cap)
    while l % t:
        t //= 2
    return t


def _to_time_major(a):
    return jnp.transpose(a, (1, 0, 2)).reshape(a.shape[0] * a.shape[1], a.shape[2])


def _from_time_major(a, bsz):
    return jnp.transpose(a.reshape(a.shape[0] // bsz, bsz, a.shape[1]), (1, 0, 2))


def _trunk(x, mod, states, P, sample):
    bsz, l, d = x.shape
    tm = _tile_rows(l, ROW_TILE_CAP)
    out_proj = matmul_gate_residual if sample else matmul_gate_residual_wres
    new = {}
    depth = mod.shape[0]
    for layer in range(depth):
        if sample:
            parts = [mod[layer][None, :, k * d:(k + 1) * d] for k in range(6)]
        else:
            parts = [mod[layer][:, None, k * d:(k + 1) * d] for k in range(6)]
        sh_m, sc_m, g_m, sh_f, sc_f, g_f = parts
        i = layer // 2
        if layer % 2 == 0:
            qkv = qkv_projection(x, P['norm_mix'][layer], sh_m, sc_m, P['w_attn_qkv'][i],
                                 P['attn_q_norm'][i], P['attn_k_norm'][i], tm)
            nq, nkv = N_HEADS * HEAD_DIM, N_KV_HEADS * HEAD_DIM
            if sample:
                o, k_win, v_win = attention_sample(qkv[0], states['win_k'][i], states['win_v'][i],
                                                   P['attn_sinks'][i], P['rel_bias_table'])
                o = o[None]
                w_len = k_win.shape[1]
                new.setdefault('win_k', []).append(k_win.reshape(-1, w_len, N_KV_HEADS, HEAD_DIM))
                new.setdefault('win_v', []).append(v_win.reshape(-1, w_len, N_KV_HEADS, HEAD_DIM))
            else:
                o = attention_prompt(qkv, P['attn_sinks'][i], P['rel_bias_table'])
                keep = min(WINDOW, PAST_LEN)
                new.setdefault('win_k', []).append(
                    qkv[:, l - keep:, nq:nq + nkv].reshape(bsz, keep, N_KV_HEADS, HEAD_DIM))
                new.setdefault('win_v', []).append(
                    qkv[:, l - keep:, nq + nkv:].reshape(bsz, keep, N_KV_HEADS, HEAD_DIM))
            x = matmul_gate_residual(o, P['w_attn_o'], i, x, g_m, _tile_rows(l, 2 * ROW_TILE_CAP), 512)
        else:
            keep = GDN_CONV - 1
            gdn_args = (x, P['norm_mix'][layer], sh_m, sc_m, P['w_gdn_in'][i], P['gdn_conv_w'][i],
                        P['gdn_a_log'][i], P['gdn_dt_bias'][i], tm)
            if sample:
                seqs = states['gdn'].shape[1]
                t = l // seqs
                qkvz, bg, hist1 = gdn_in_projection(
                    *gdn_args, conv_hist=_to_time_major(states['gdn_conv'][i].astype(F32)), tn=2 * 512)
                new.setdefault('gdn_conv', []).append(_from_time_major(hist1, seqs))
                seq_rows = -(-t // SUBLANES) * SUBLANES
                per_seq = lambda a: jnp.pad(_from_time_major(a, seqs), ((0, 0), (0, seq_rows - t), (0, 0))
                                            ).reshape(1, seqs * seq_rows, a.shape[-1])
                o_seq, s1 = gdn_core(per_seq(qkvz[0]), per_seq(bg[0]), states['gdn'][i].astype(F32),
                                     P['gdn_out_norm'][i], GDN_CHUNK, seq_rows=seq_rows)
                o = _to_time_major(o_seq.reshape(seqs, seq_rows, GDN_VAL_DIM)[:, :t])[None]
            else:
                qkvz, bg, raw = gdn_in_projection(*gdn_args)
                s0 = jnp.zeros((bsz, GDN_V_HEADS, GDN_DK, GDN_DV), F32)
                o, s1 = gdn_core(qkvz, bg, s0, P['gdn_out_norm'][i], _tile_rows(l, GDN_ROW_TILE_CAP))
                new.setdefault('gdn_conv', []).append(raw[:, -1, SUBLANES - keep:, :])
            new.setdefault('gdn', []).append(s1)
            x = out_proj(o, P['w_gdn_out'], i, x, g_m, tm, 512)
        if sample:
            seqs = states['ffn_conv'].shape[1]
            act, f1 = ffn_up_sample(x[0], P['norm_ffn'][layer], sh_f[0], sc_f[0], P['w_ffn_up'],
                                    P['ffn_conv_w'], P['ffn_conv_b'], layer,
                                    _to_time_major(states['ffn_conv'][layer].astype(F32)))
            act, f1 = act[None], _from_time_major(f1, seqs)
        else:
            act, f1 = ffn_up_prompt(x, P['norm_ffn'][layer], sh_f, sc_f, P['w_ffn_up'],
                                    P['ffn_conv_w'], P['ffn_conv_b'], layer, tm)
        new.setdefault('ffn_conv', []).append(f1)
        if sample:
            x = matmul_gate_residual(act, P['w_ffn_down'], layer, x, g_f, tm, 256)
        else:
            x = matmul_gate_residual_wres(act, P['w_ffn_down'], layer, x, g_f, _tile_rows(l, 512), 512)
    return x, {k: jnp.stack(v) for k, v in new.items()}


def kernel(x_prompt, x_sample, c_prompt, c_sample, cache_win_k, cache_win_v, state_gdn, state_gdn_conv,
           state_ffn_conv, rel_bias_table, w_ada, b_ada, norm_mix, norm_ffn, w_attn_qkv, attn_q_norm,
           attn_k_norm, attn_sinks, w_attn_o, w_gdn_in, gdn_conv_w, gdn_a_log, gdn_dt_bias, gdn_out_norm,
           w_gdn_out, w_ffn_up, ffn_conv_w, ffn_conv_b, w_ffn_down):
    P = dict(rel_bias_table=rel_bias_table, norm_mix=norm_mix, norm_ffn=norm_ffn, w_attn_qkv=w_attn_qkv,
             attn_q_norm=attn_q_norm, attn_k_norm=attn_k_norm, attn_sinks=attn_sinks, w_attn_o=w_attn_o,
             w_gdn_in=w_gdn_in, gdn_conv_w=gdn_conv_w, gdn_a_log=gdn_a_log, gdn_dt_bias=gdn_dt_bias,
             gdn_out_norm=gdn_out_norm, w_gdn_out=w_gdn_out, w_ffn_up=w_ffn_up, ffn_conv_w=ffn_conv_w,
             ffn_conv_b=ffn_conv_b, w_ffn_down=w_ffn_down)
    bp = x_prompt.shape[0]
    bs, ts, d = x_sample.shape
    n_c = bp + bs
    c_rows = -(-n_c // SUBLANES) * SUBLANES
    c_all = jnp.pad(jnp.concatenate([c_prompt, c_sample], axis=0), ((0, c_rows - n_c), (0, 0)))
    mod = ada_modulation(c_all, w_ada, b_ada)
    mod_prompt = mod[:, :bp]
    mod_sample = jnp.tile(mod[:, bp:n_c], (1, ts, 1))

    y_p, new_p = _trunk(x_prompt, mod_prompt, None, P, sample=False)
    states = dict(win_k=cache_win_k, win_v=cache_win_v, gdn=state_gdn, gdn_conv=state_gdn_conv,
                  ffn_conv=state_ffn_conv)
    y_s, new_s = _trunk(_to_time_major(x_sample)[None], mod_sample, states, P, sample=True)
    y_s = _from_time_major(y_s[0], bs)
    return (y_p, y_s, new_p['win_k'], new_p['win_v'], new_s['win_k'], new_s['win_v'],
            new_p['gdn'], new_s['gdn'], new_p['gdn_conv'], new_s['gdn_conv'],
            new_p['ffn_conv'], new_s['ffn_conv'])
```

```python
import functools
import math

import numpy as np
import jax
import jax.numpy as jnp
from jax import lax
from jax.experimental import pallas as pl
from jax.experimental.pallas import tpu as pltpu

HEAD_DIM = 64
N_HEADS = 32
N_KV_HEADS = 4
ATT_GROUP = N_HEADS // N_KV_HEADS
WINDOW = 128
ATT_BLOCK = 128
ATT_SAMPLE_SEQS = 8
N_BUCKETS = 32
MAX_DISTANCE = 128
NEG_INF = -1e30
PAST_LEN = 16384

GDN_K_HEADS = 16
GDN_V_HEADS = 32
GDN_DK = 128
GDN_DV = 128
GDN_KEY_DIM = GDN_K_HEADS * GDN_DK
GDN_VAL_DIM = GDN_V_HEADS * GDN_DV
GDN_CONV_DIM = 2 * GDN_KEY_DIM + GDN_VAL_DIM
GDN_CONV = 4
GDN_CHUNK = 64
GDN_HEAD_GROUPS = 4
GDN_VH_PER_GROUP = GDN_V_HEADS // GDN_HEAD_GROUPS
GDN_KH_PER_GROUP = GDN_K_HEADS // GDN_HEAD_GROUPS
GDN_PACK = 4
GDN_PHASE_A_CHUNKS = 4

FFN_CONV = 3
NORM_EPS = 1e-6

LANES = 128
SUBLANES = 8
VMEM_CAP_BYTES = 60 * 1024 * 1024
VMEM_SLACK_BYTES = 8 * 1024 * 1024

ROW_TILE_CAP = 1024
GDN_IN_SUB_ROWS = 256
FFN_SUB_ROWS = 256
GDN_ROW_TILE_CAP = 1024

BF16 = jnp.bfloat16
F32 = jnp.float32


def _vmem_limit(block_bytes, scratch_bytes=0):
    est = 2 * int(block_bytes) + int(scratch_bytes) + VMEM_SLACK_BYTES
    return int(min(max(est, 16 * 1024 * 1024), VMEM_CAP_BYTES))


def _params(n_grid, block_bytes, scratch_bytes=0):
    return pltpu.CompilerParams(
        dimension_semantics=("arbitrary",) * n_grid,
        vmem_limit_bytes=_vmem_limit(block_bytes, scratch_bytes))


def _nbytes(shape, dtype):
    return int(np.prod(shape)) * jnp.dtype(dtype).itemsize


def _silu(x):
    return x * (1.0 / (1.0 + jnp.exp2(x * (-math.log2(math.e)))))


def _dot(a, b):
    return jnp.dot(a, b, preferred_element_type=F32)


def _dot_nt(a, b):
    return lax.dot_general(a, b, (((1,), (1,)), ((), ())), preferred_element_type=F32)


def _dot_tn(a, b):
    return lax.dot_general(a, b, (((0,), (0,)), ((), ())), preferred_element_type=F32)


def _norm_mod(x, gain, shift, scale):
    ms = jnp.mean(x * x, axis=-1, keepdims=True)
    y = x * lax.rsqrt(ms + NORM_EPS) * gain
    return y * (1.0 + scale) + shift


def _mod_specs(shift, tm):
    d = shift.shape[-1]
    if shift.shape[1] == 1:
        return pl.BlockSpec((1, 1, d), lambda b, i, j: (b, 0, 0))
    return pl.BlockSpec((1, tm, d), lambda b, i, j: (b, i, 0))


def _ada_kernel(c_ref, w_ref, b_ref, o_ref):
    a = _silu(c_ref[...]).astype(BF16)
    o_ref[0] = _dot(a, w_ref[0].astype(BF16)) + b_ref[0]


def ada_modulation(c_all, w_ada, b_ada, tn=1024):
    rows, d = c_all.shape
    depth, _, n = w_ada.shape
    blocks = _nbytes((rows, d), F32) + _nbytes((d, tn), F32) + _nbytes((rows, tn), F32)
    return pl.pallas_call(
        _ada_kernel,
        grid=(depth, n // tn),
        in_specs=[pl.BlockSpec((rows, d), lambda l, j: (0, 0)),
                  pl.BlockSpec((1, d, tn), lambda l, j: (l, 0, j)),
                  pl.BlockSpec((1, 1, tn), lambda l, j: (l, 0, j))],
        out_specs=pl.BlockSpec((1, rows, tn), lambda l, j: (l, 0, j)),
        out_shape=jax.ShapeDtypeStruct((depth, rows, n), F32),
        compiler_params=_params(2, blocks, _nbytes((d, tn), BF16)),
        name="ada_modulation",
    )(c_all, w_ada, b_ada.reshape(depth, 1, n))


def _qkv_kernel(x_ref, gain_ref, shift_ref, scale_ref, w_ref, hgain_ref, hflag_ref, gmat_ref,
                o_ref, h_s):
    @pl.when(pl.program_id(2) == 0)
    def _():
        h_s[...] = _norm_mod(x_ref[0], gain_ref[...], shift_ref[0], scale_ref[0]).astype(BF16)

    y = _dot(h_s[...], w_ref[...].astype(BF16))
    ms = _dot((y * y).astype(BF16), gmat_ref[...])
    yn = y * lax.rsqrt(ms + NORM_EPS) * hgain_ref[...]
    o_ref[0] = jnp.where(hflag_ref[...] > 0.0, yn, y)


def qkv_projection(x, gain, shift, scale, w, q_gain, k_gain, tm, tn=512):
    b, l, d = x.shape
    n = w.shape[1]
    nq, nk = N_HEADS * HEAD_DIM, N_KV_HEADS * HEAD_DIM
    hgain = jnp.concatenate([jnp.tile(q_gain, N_HEADS), jnp.tile(k_gain, N_KV_HEADS),
                             jnp.ones((nk,), F32)]).reshape(1, n)
    hflag = jnp.concatenate([jnp.ones((nq + nk,), F32), jnp.zeros((nk,), F32)]).reshape(1, n)
    gidx = np.arange(tn) // HEAD_DIM
    gmat = jnp.asarray((gidx[:, None] == gidx[None, :]).astype(np.float32) / HEAD_DIM, BF16)
    blocks = (_nbytes((tm, d), F32) + _nbytes((d, tn), F32) + _nbytes((tm, tn), F32)
              + 2 * _nbytes((shift.shape[1] == 1 and 1 or tm, d), F32) + _nbytes((tn, tn), BF16))
    scratch = _nbytes((tm, d), BF16) + _nbytes((d, tn), BF16) + 4 * _nbytes((tm, tn), F32)
    return pl.pallas_call(
        _qkv_kernel,
        grid=(b, l // tm, n // tn),
        in_specs=[pl.BlockSpec((1, tm, d), lambda b_, i, j: (b_, i, 0)),
                  pl.BlockSpec((1, d), lambda b_, i, j: (0, 0)),
                  _mod_specs(shift, tm), _mod_specs(scale, tm),
                  pl.BlockSpec((d, tn), lambda b_, i, j: (0, j)),
                  pl.BlockSpec((1, tn), lambda b_, i, j: (0, j)),
                  pl.BlockSpec((1, tn), lambda b_, i, j: (0, j)),
                  pl.BlockSpec((tn, tn), lambda b_, i, j: (0, 0))],
        out_specs=pl.BlockSpec((1, tm, tn), lambda b_, i, j: (b_, i, j)),
        out_shape=jax.ShapeDtypeStruct((b, l, n), F32),
        scratch_shapes=[pltpu.VMEM((tm, d), BF16)],
        compiler_params=_params(3, blocks, scratch),
        name="qkv_projection",
    )(x, gain.reshape(1, d), shift, scale, w, hgain, hflag, gmat)


def _attn_prompt_kernel(sink_ref, q_ref, cur_ref, prev_ref, bprev_ref, bcur_ref, o_ref, kz_s, vz_s):
    n = pl.program_id(1)
    blk = ATT_BLOCK
    lane = lax.broadcasted_iota(jnp.int32, (2 * blk, LANES), 1)
    lo_half = lane < HEAD_DIM
    neg_prev = jnp.where(n == 0, NEG_INF, 0.0).astype(F32)

    kv_width = N_KV_HEADS * HEAD_DIM
    for pair in range(N_KV_HEADS // 2):
        for part, dst in ((0, kz_s), (1, vz_s)):
            col = part * kv_width + pair * LANES
            both = jnp.concatenate([prev_ref[0, :, col:col + LANES], cur_ref[0, :, col:col + LANES]], axis=0)
            swapped = pltpu.roll(both, HEAD_DIM, axis=1)
            zero = jnp.zeros_like(both)
            c0, c1 = 2 * pair, 2 * pair + 1
            dst[2 * c0 + 0] = jnp.where(lo_half, both, zero).astype(BF16)
            dst[2 * c0 + 1] = jnp.where(lo_half, zero, swapped).astype(BF16)
            dst[2 * c1 + 0] = jnp.where(lo_half, swapped, zero).astype(BF16)
            dst[2 * c1 + 1] = jnp.where(lo_half, zero, both).astype(BF16)

    scale = HEAD_DIM ** -0.5 * math.log2(math.e)
    for p in range(N_HEADS // 2):
        c = (2 * p) // ATT_GROUP
        qp = (q_ref[0, :, p * LANES:(p + 1) * LANES] * scale).astype(BF16)
        o_pair = None
        for a in range(2):
            h = 2 * p + a
            sink = sink_ref[h]
            s = _dot_nt(qp, kz_s[2 * c + a])
            s_prev = s[:, :blk] + bprev_ref[h] + neg_prev
            s_cur = s[:, blk:] + bcur_ref[h]
            m = jnp.maximum(jnp.max(jnp.maximum(s_prev, s_cur), axis=-1, keepdims=True), sink)
            e_prev = jnp.exp2(s_prev - m)
            e_cur = jnp.exp2(s_cur - m)
            den = jnp.sum(e_prev + e_cur, axis=-1, keepdims=True) + jnp.exp2(sink - m)
            pm = jnp.concatenate([e_prev, e_cur], axis=1).astype(BF16)
            o_a = _dot(pm, vz_s[2 * c + a]) * (1.0 / den)
            o_pair = o_a if o_pair is None else o_pair + o_a
        o_ref[0, :, p * LANES:(p + 1) * LANES] = o_pair.astype(o_ref.dtype)


def _t5_bucket_np(dist):
    max_exact = N_BUCKETS // 2
    d = np.maximum(dist, 0)
    df = np.maximum(d, 1).astype(np.float32)
    large = max_exact + (np.log(df / np.float32(max_exact)) / np.float32(math.log(MAX_DISTANCE / max_exact))
                         * np.float32(N_BUCKETS - max_exact)).astype(np.int32)
    large = np.minimum(large, N_BUCKETS - 1)
    return np.where(d < max_exact, d, large)


def _bias_from_dist(dist, in_band, rel_table):
    onehot = (_t5_bucket_np(dist)[..., None] == np.arange(N_BUCKETS)).astype(np.float32)
    tab = jnp.einsum('qsb,bh->hqs', jnp.asarray(onehot), rel_table.astype(F32), precision=lax.Precision.HIGHEST)
    return jnp.where(jnp.asarray(in_band)[None], tab, NEG_INF)


def attention_prompt(qkv, sinks, rel_table):
    b, l, n = qkv.shape
    blk = ATT_BLOCK
    nq = N_HEADS * HEAD_DIM
    kvw = 2 * N_KV_HEADS * HEAD_DIM
    kv_blk = nq // kvw
    qi = np.arange(blk)[:, None]
    sj = np.arange(blk)[None, :]
    d_prev = qi + blk - sj
    d_cur = qi - sj
    log2e = math.log2(math.e)
    bias_prev = _bias_from_dist(d_prev, (d_prev >= 0) & (d_prev <= WINDOW), rel_table) * log2e
    bias_cur = _bias_from_dist(d_cur, (d_cur >= 0) & (d_cur <= WINDOW), rel_table) * log2e
    blocks = (_nbytes((blk, nq), F32) + 2 * _nbytes((blk, kvw), F32) + _nbytes((blk, nq), BF16))
    scratch = 2 * _nbytes((2 * N_KV_HEADS, 2 * blk, LANES), BF16) + 4 * _nbytes((N_HEADS, blk, blk), F32)
    return pl.pallas_call(
        _attn_prompt_kernel,
        grid=(b, l // blk),
        in_specs=[pl.BlockSpec(memory_space=pltpu.SMEM),
                  pl.BlockSpec((1, blk, nq), lambda b_, i: (b_, i, 0)),
                  pl.BlockSpec((1, blk, kvw), lambda b_, i: (b_, i, kv_blk)),
                  pl.BlockSpec((1, blk, kvw), lambda b_, i: (b_, jnp.maximum(i - 1, 0), kv_blk)),
                  pl.BlockSpec((N_HEADS, blk, blk), lambda b_, i: (0, 0, 0)),
                  pl.BlockSpec((N_HEADS, blk, blk), lambda b_, i: (0, 0, 0))],
        out_specs=pl.BlockSpec((1, blk, nq), lambda b_, i: (b_, i, 0)),
        out_shape=jax.ShapeDtypeStruct((b, l, nq), BF16),
        scratch_shapes=[pltpu.VMEM((2 * N_KV_HEADS, 2 * blk, LANES), BF16),
                        pltpu.VMEM((2 * N_KV_HEADS, 2 * blk, LANES), BF16)],
        compiler_params=_params(2, blocks, scratch),
        name="attention_prompt",
    )(sinks.astype(F32) * log2e, qkv, qkv, qkv, bias_prev, bias_cur)


def _attn_sample_kernel(q_ref, k_ref, v_ref, bias_ref, sink_ref, o_ref):
    pairs = [(b, c) for b in range(q_ref.shape[0]) for c in range(N_KV_HEADS)]
    k = [k_ref[b].astype(BF16) for b in range(q_ref.shape[0])]
    s = {bc: _dot_nt(q_ref[bc[0], bc[1]].astype(BF16), k[bc[0]]) + bias_ref[bc[1]] for bc in pairs}
    e, den = {}, {}
    for b, c in pairs:
        sink = sink_ref[c]
        m = jnp.maximum(jnp.max(s[(b, c)], axis=-1, keepdims=True), sink)
        e[(b, c)] = jnp.exp(s[(b, c)] - m)
        den[(b, c)] = jnp.sum(e[(b, c)], axis=-1, keepdims=True) + jnp.exp(sink - m)
    for b, c in pairs:
        o_ref[b, c] = _dot(e[(b, c)].astype(BF16), v_ref[b].astype(BF16)) * (1.0 / den[(b, c)])


def attention_sample(qkv, cache_k, cache_v, sinks, rel_table):
    bsz, w = cache_k.shape[0], cache_k.shape[1]
    t = qkv.shape[0] // bsz
    nq, nkv = N_HEADS * HEAD_DIM, N_KV_HEADS * HEAD_DIM
    keys = w + t
    keys_pad = -(-keys // 16) * 16
    rows = ATT_GROUP * t
    q = qkv[:, :nq].reshape(t, bsz, N_KV_HEADS, ATT_GROUP, HEAD_DIM) * HEAD_DIM ** -0.5
    q = jnp.transpose(q, (1, 2, 3, 0, 4)).reshape(bsz, N_KV_HEADS, rows, HEAD_DIM)
    qz = jnp.einsum('bcrd,ck->bcrkd', q, jnp.eye(N_KV_HEADS, dtype=F32)).reshape(bsz, N_KV_HEADS, rows, nkv)
    new_kv = jnp.transpose(qkv[:, nq:].reshape(t, bsz, 2 * nkv), (1, 0, 2))
    k_all = jnp.concatenate([cache_k.reshape(bsz, w, nkv), new_kv[:, :, :nkv]], axis=1)
    v_all = jnp.concatenate([cache_v.reshape(bsz, w, nkv), new_kv[:, :, nkv:]], axis=1)
    pad = ((0, 0), (0, keys_pad - keys), (0, 0))
    k_pad, v_pad = jnp.pad(k_all, pad), jnp.pad(v_all, pad)
    tq = np.arange(t)[:, None]
    sk = np.arange(keys_pad)[None, :]
    dist = tq + w - sk
    in_band = (dist >= 0) & (dist <= WINDOW) & (sk < keys)
    bias_t = _bias_from_dist(dist, in_band, rel_table)
    bias = bias_t.reshape(N_KV_HEADS, rows, keys_pad)
    sink_rows = jnp.repeat(sinks.astype(F32), t).reshape(N_KV_HEADS, rows, 1)
    sb = math.gcd(bsz, ATT_SAMPLE_SEQS)
    blocks = sb * (_nbytes((N_KV_HEADS, rows, nkv), F32) * 2 + 2 * _nbytes((keys_pad, nkv), F32))
    oz = pl.pallas_call(
        _attn_sample_kernel,
        grid=(bsz // sb,),
        in_specs=[pl.BlockSpec((sb, N_KV_HEADS, rows, nkv), lambda b_: (b_, 0, 0, 0)),
                  pl.BlockSpec((sb, keys_pad, nkv), lambda b_: (b_, 0, 0)),
                  pl.BlockSpec((sb, keys_pad, nkv), lambda b_: (b_, 0, 0)),
                  pl.BlockSpec((N_KV_HEADS, rows, keys_pad), lambda b_: (0, 0, 0)),
                  pl.BlockSpec((N_KV_HEADS, rows, 1), lambda b_: (0, 0, 0))],
        out_specs=pl.BlockSpec((sb, N_KV_HEADS, rows, nkv), lambda b_: (b_, 0, 0, 0)),
        out_shape=jax.ShapeDtypeStruct((bsz, N_KV_HEADS, rows, nkv), F32),
        compiler_params=_params(1, blocks),
        name="attention_sample",
    )(qz, k_pad, v_pad, bias, sink_rows)
    o = jnp.stack([oz[:, c, :, c * HEAD_DIM:(c + 1) * HEAD_DIM] for c in range(N_KV_HEADS)], axis=1)
    o = o.reshape(bsz, N_KV_HEADS, ATT_GROUP, t, HEAD_DIM)
    o = jnp.transpose(o, (3, 0, 1, 2, 4)).reshape(t * bsz, nq)
    return o.astype(BF16), k_all[:, t:], v_all[:, t:]


def _mm_res_kernel(a_ref, w_ref, x_ref, g_ref, o_ref):
    y = _dot(a_ref[0], w_ref[...].astype(BF16))
    o_ref[0] = x_ref[0] + g_ref[0] * y


def _mm_res_wres_kernel(a_ref, w_ref, x_ref, g_ref, o_ref, wbf_s):
    @pl.when((pl.program_id(1) == 0) & (pl.program_id(2) == 0))
    def _():
        wbf_s[...] = w_ref[...].astype(BF16)

    o_ref[0] = x_ref[0] + g_ref[0] * _dot(a_ref[0], wbf_s[...])


def matmul_gate_residual_wres(a, w_stack, layer, x, gate, tm, tn):
    b, l, k = a.shape
    n = w_stack.shape[2]
    if gate.shape[1] == 1:
        g_spec = pl.BlockSpec((1, 1, tn), lambda j, b_, i: (b_, 0, j))
    else:
        g_spec = pl.BlockSpec((1, tm, tn), lambda j, b_, i: (b_, i, j))
    blocks = _nbytes((tm, k), a.dtype) + _nbytes((k, tn), F32) + 3 * _nbytes((tm, tn), F32)
    scratch = _nbytes((k, tn), BF16) + _nbytes((tm, tn), F32)
    return pl.pallas_call(
        _mm_res_wres_kernel,
        grid=(n // tn, b, l // tm),
        in_specs=[pl.BlockSpec((1, tm, k), lambda j, b_, i: (b_, i, 0)),
                  pl.BlockSpec((None, k, tn), lambda j, b_, i: (layer, 0, j)),
                  pl.BlockSpec((1, tm, tn), lambda j, b_, i: (b_, i, j)),
                  g_spec],
        out_specs=pl.BlockSpec((1, tm, tn), lambda j, b_, i: (b_, i, j)),
        out_shape=jax.ShapeDtypeStruct((b, l, n), F32),
        scratch_shapes=[pltpu.VMEM((k, tn), BF16)],
        compiler_params=_params(3, blocks, scratch),
        name="matmul_gate_residual_wres",
    )(a, w_stack, x, gate)


def matmul_gate_residual(a, w_stack, layer, x, gate, tm, tn):
    b, l, k = a.shape
    n = w_stack.shape[2]
    grows = 1 if gate.shape[1] == 1 else tm
    if gate.shape[1] == 1:
        g_spec = pl.BlockSpec((1, 1, tn), lambda b_, i, j: (b_, 0, j))
    else:
        g_spec = pl.BlockSpec((1, tm, tn), lambda b_, i, j: (b_, i, j))
    blocks = (_nbytes((tm, k), a.dtype) + _nbytes((k, tn), F32) + 2 * _nbytes((tm, tn), F32)
              + _nbytes((grows, tn), F32))
    scratch = _nbytes((k, tn), BF16) + _nbytes((tm, tn), F32)
    return pl.pallas_call(
        _mm_res_kernel,
        grid=(b, l // tm, n // tn),
        in_specs=[pl.BlockSpec((1, tm, k), lambda b_, i, j: (b_, i, 0)),
                  pl.BlockSpec((None, k, tn), lambda b_, i, j: (layer, 0, j)),
                  pl.BlockSpec((1, tm, tn), lambda b_, i, j: (b_, i, j)),
                  g_spec],
        out_specs=pl.BlockSpec((1, tm, tn), lambda b_, i, j: (b_, i, j)),
        out_shape=jax.ShapeDtypeStruct((b, l, n), F32),
        compiler_params=_params(3, blocks, scratch),
        name="matmul_gate_residual",
    )(a, w_stack, x, gate)


def _ffn_conv_rows(u, prev1, prev2, cw_ref, cb_ref):
    return u * cw_ref[2:3, :] + prev1 * cw_ref[1:2, :] + prev2 * cw_ref[0:1, :] + cb_ref[...]


def _ffn_up_prompt_kernel(x_ref, gain_ref, shift_ref, scale_ref, wg_ref, wv_ref, cwg_ref, cwv_ref,
                          cbg_ref, cbv_ref, o_ref, tail_ref, h_s, carry_s, ext_s):
    i = pl.program_id(1)
    j = pl.program_id(2)
    tm = h_s.shape[0]

    @pl.when(j == 0)
    def _():
        h_s[...] = _norm_mod(x_ref[0], gain_ref[...], shift_ref[0], scale_ref[0]).astype(BF16)

    @pl.when(i == 0)
    def _():
        carry_s[j] = jnp.zeros(carry_s.shape[1:], F32)

    halves = ((wg_ref, cwg_ref, cbg_ref), (wv_ref, cwv_ref, cbv_ref))
    w_bf = [w_ref[...].astype(BF16) for w_ref, _, _ in halves]
    for half in range(2):
        ext_s[half, 0:SUBLANES, :] = carry_s[j, half]
    sub = min(FFN_SUB_ROWS, tm)
    for m in range(tm // sub):
        hm = h_s[m * sub:(m + 1) * sub, :]
        r0 = SUBLANES + m * sub
        ys = []
        for half, (_, cw_ref, cb_ref) in enumerate(halves):
            u = _dot(hm, w_bf[half])
            ext_s[half, r0:r0 + sub, :] = u
            ys.append(_ffn_conv_rows(u, ext_s[half, r0 - 1:r0 - 1 + sub, :], ext_s[half, r0 - 2:r0 - 2 + sub, :],
                                     cw_ref, cb_ref))
        o_ref[0, m * sub:(m + 1) * sub, :] = (_silu(ys[0]) * ys[1]).astype(o_ref.dtype)
    for half in range(2):
        tail = ext_s[half, tm:tm + SUBLANES, :]
        carry_s[j, half] = tail
        tail_ref[0, 0, half] = tail


def ffn_up_prompt(x, gain, shift, scale, w_up, conv_w, conv_b, layer, tm, tn=512):
    b, l, d = x.shape
    dff = w_up.shape[2] // 2
    nj = dff // tn
    conv_b = conv_b.reshape(conv_b.shape[0], 1, 2 * dff)
    blocks = (_nbytes((tm, d), F32) + 2 * _nbytes((d, tn), F32) + _nbytes((tm, tn), BF16)
              + 2 * _nbytes((1, d), F32) + 8 * _nbytes((SUBLANES, tn), F32))
    scratch = (_nbytes((tm, d), BF16) + _nbytes((nj, 2, SUBLANES, tn), F32) + 2 * _nbytes((d, tn), BF16)
               + 8 * _nbytes((tm, tn), F32))
    act, tail = pl.pallas_call(
        _ffn_up_prompt_kernel,
        grid=(b, l // tm, nj),
        in_specs=[pl.BlockSpec((1, tm, d), lambda b_, i, j: (b_, i, 0)),
                  pl.BlockSpec((1, d), lambda b_, i, j: (0, 0)),
                  _mod_specs(shift, tm), _mod_specs(scale, tm),
                  pl.BlockSpec((None, d, tn), lambda b_, i, j: (layer, 0, j)),
                  pl.BlockSpec((None, d, tn), lambda b_, i, j: (layer, 0, j + nj)),
                  pl.BlockSpec((None, FFN_CONV, tn), lambda b_, i, j: (layer, 0, j)),
                  pl.BlockSpec((None, FFN_CONV, tn), lambda b_, i, j: (layer, 0, j + nj)),
                  pl.BlockSpec((None, 1, tn), lambda b_, i, j: (layer, 0, j)),
                  pl.BlockSpec((None, 1, tn), lambda b_, i, j: (layer, 0, j + nj))],
        out_specs=[pl.BlockSpec((1, tm, tn), lambda b_, i, j: (b_, i, j)),
                   pl.BlockSpec((1, 1, 2, SUBLANES, tn), lambda b_, i, j: (b_, i, 0, 0, j))],
        out_shape=[jax.ShapeDtypeStruct((b, l, dff), BF16),
                   jax.ShapeDtypeStruct((b, l // tm, 2, SUBLANES, dff), F32)],
        scratch_shapes=[pltpu.VMEM((tm, d), BF16), pltpu.VMEM((nj, 2, SUBLANES, tn), F32),
                        pltpu.VMEM((2, SUBLANES + tm, tn), F32)],
        compiler_params=_params(3, blocks, scratch),
        name="ffn_up_prompt",
    )(x, gain.reshape(1, d), shift, scale, w_up, w_up, conv_w, conv_w, conv_b, conv_b)
    keep = FFN_CONV - 1
    hist = jnp.transpose(tail[:, -1, :, SUBLANES - keep:, :], (0, 2, 1, 3)).reshape(b, keep, 2 * dff)
    return act, hist


def _ffn_up_sample_kernel(x_ref, gain_ref, shift_ref, scale_ref, wg_ref, wv_ref, cwg_ref, cwv_ref,
                          cbg_ref, cbv_ref, hg_ref, hv_ref, o_ref, ng_ref, nv_ref, h_s, *, bsz):
    @pl.when(pl.program_id(0) == 0)
    def _():
        h_s[...] = _norm_mod(x_ref[...], gain_ref[...], shift_ref[...], scale_ref[...]).astype(BF16)

    h = h_s[...]
    rows = h_s.shape[0]
    keep = FFN_CONV - 1
    ys = []
    for w_ref, cw_ref, cb_ref, hist_ref, new_ref in ((wg_ref, cwg_ref, cbg_ref, hg_ref, ng_ref),
                                                     (wv_ref, cwv_ref, cbv_ref, hv_ref, nv_ref)):
        u = _dot(h, w_ref[...].astype(BF16))
        ext = jnp.concatenate([hist_ref[...], u], axis=0)
        prev = [ext[(keep - k) * bsz:(keep - k) * bsz + rows] for k in (1, 2)]
        ys.append(_ffn_conv_rows(u, prev[0], prev[1], cw_ref, cb_ref))
        new_ref[...] = ext[rows:rows + keep * bsz]
    o_ref[...] = (_silu(ys[0]) * ys[1]).astype(o_ref.dtype)


def ffn_up_sample(x, gain, shift, scale, w_up, conv_w, conv_b, layer, hist, tn=512):
    rows, d = x.shape
    dff = w_up.shape[2] // 2
    nj = dff // tn
    keep = FFN_CONV - 1
    bsz = hist.shape[0] // keep
    conv_b = conv_b.reshape(conv_b.shape[0], 1, 2 * dff)
    col = lambda j: (0, j)
    col_hi = lambda j: (0, j + nj)
    lcol = lambda j: (layer, 0, j)
    lcol_hi = lambda j: (layer, 0, j + nj)
    blocks = (_nbytes((rows, d), F32) * 3 + 2 * _nbytes((d, tn), F32) + 5 * _nbytes((rows, tn), F32))
    scratch = _nbytes((rows, d), BF16) + 2 * _nbytes((d, tn), BF16) + 8 * _nbytes((rows, tn), F32)
    act, new_g, new_v = pl.pallas_call(
        functools.partial(_ffn_up_sample_kernel, bsz=bsz),
        grid=(nj,),
        in_specs=[pl.BlockSpec((rows, d), lambda j: (0, 0)),
                  pl.BlockSpec((1, d), lambda j: (0, 0)),
                  pl.BlockSpec((rows, d), lambda j: (0, 0)),
                  pl.BlockSpec((rows, d), lambda j: (0, 0)),
                  pl.BlockSpec((None, d, tn), lcol), pl.BlockSpec((None, d, tn), lcol_hi),
                  pl.BlockSpec((None, FFN_CONV, tn), lcol), pl.BlockSpec((None, FFN_CONV, tn), lcol_hi),
                  pl.BlockSpec((None, 1, tn), lcol), pl.BlockSpec((None, 1, tn), lcol_hi),
                  pl.BlockSpec((keep * bsz, tn), col), pl.BlockSpec((keep * bsz, tn), col_hi)],
        out_specs=[pl.BlockSpec((rows, tn), col), pl.BlockSpec((keep * bsz, tn), col),
                   pl.BlockSpec((keep * bsz, tn), col)],
        out_shape=[jax.ShapeDtypeStruct((rows, dff), BF16),
                   jax.ShapeDtypeStruct((keep * bsz, dff), F32),
                   jax.ShapeDtypeStruct((keep * bsz, dff), F32)],
        scratch_shapes=[pltpu.VMEM((rows, d), BF16)],
        compiler_params=_params(1, blocks, scratch),
        name="ffn_up_sample",
    )(x, gain.reshape(1, d), shift, scale, w_up, w_up, conv_w, conv_w, conv_b, conv_b, hist, hist)
    return act, jnp.concatenate([new_g, new_v], axis=-1)


def _gdn_in_kernel(*refs, n_qk_tiles, n_conv_tiles, bsz):
    if bsz is None:
        (x_ref, gain_ref, shift_ref, scale_ref, w_ref, wba_ref, alog_ref, dtb_ref, cw_ref,
         o_ref, bg_ref, raw_ref, h_s, carry_s) = refs
    else:
        (x_ref, gain_ref, shift_ref, scale_ref, w_ref, wba_ref, alog_ref, dtb_ref, cw_ref, hist_ref,
         o_ref, bg_ref, raw_ref, h_s) = refs
    i = pl.program_id(1)
    j = pl.program_id(2)
    tm = h_s.shape[0]
    width = GDN_CONV

    @pl.when(j == 0)
    def _():
        h = _norm_mod(x_ref[0], gain_ref[...], shift_ref[0], scale_ref[0]).astype(BF16)
        h_s[...] = h
        y = _dot_nt(h, wba_ref[...].astype(BF16))
        lane = lax.broadcasted_iota(jnp.int32, y.shape, 1) % LANES
        beta = 1.0 / (1.0 + jnp.exp(-y))
        a = y + dtb_ref[...]
        softplus = jnp.maximum(a, 0.0) + jnp.log1p(jnp.exp(-jnp.abs(a)))
        g = -jnp.exp(alog_ref[...]) * softplus
        bg_ref[0] = jnp.where(lane < GDN_VH_PER_GROUP, beta, jnp.where(lane < 2 * GDN_VH_PER_GROUP, g, 0.0))

    sub = min(GDN_IN_SUB_ROWS, tm)
    row8 = lax.broadcasted_iota(jnp.int32, (SUBLANES, 1), 0)

    def conv_tile(l2_scale):
        w_bf = w_ref[...].astype(BF16)
        jc = jnp.minimum(j, n_conv_tiles - 1)
        if bsz is None:
            @pl.when(i == 0)
            def _():
                carry_s[jc] = jnp.zeros(carry_s.shape[1:], F32)
            prev_tail = carry_s[jc]
        for m in range(tm // sub):
            rs = slice(m * sub, (m + 1) * sub)
            u = _dot_nt(h_s[rs, :], w_bf)
            y = u * cw_ref[width - 1:width, :]
            if bsz is None:
                top = u[0:SUBLANES]
                y_top = top * cw_ref[width - 1:width, :]
                for k in range(1, width):
                    wk = cw_ref[width - 1 - k:width - k, :]
                    y = y + pltpu.roll(u, k, axis=0) * wk
                    y_top = y_top + jnp.where(row8 < k, pltpu.roll(prev_tail, k, axis=0),
                                              pltpu.roll(top, k, axis=0)) * wk
                y = jnp.concatenate([y_top, y[SUBLANES:]], axis=0)
                prev_tail = u[sub - SUBLANES:sub]
            else:
                ext = jnp.concatenate([hist_ref[...], u], axis=0)
                for k in range(1, width):
                    wk = cw_ref[width - 1 - k:width - k, :]
                    y = y + ext[(width - 1 - k) * bsz:(width - 1 - k) * bsz + sub] * wk
                raw_ref[...] = ext[sub:sub + (width - 1) * bsz]
            y = _silu(y)
            if l2_scale is not None:
                heads = [y[:, hd * LANES:(hd + 1) * LANES] for hd in range(y.shape[1] // LANES)]
                y = jnp.concatenate(
                    [yh * (lax.rsqrt(jnp.sum(yh * yh, axis=-1, keepdims=True) + NORM_EPS) * l2_scale)
                     for yh in heads], axis=1)
            o_ref[0, rs, :] = y.astype(o_ref.dtype)
        if bsz is None:
            carry_s[jc] = prev_tail
            raw_ref[0, 0] = prev_tail

    @pl.when(j < n_qk_tiles)
    def _():
        conv_tile(jnp.where(j < n_qk_tiles // 2, GDN_DK ** -0.5, 1.0).astype(F32))

    @pl.when((j >= n_qk_tiles) & (j < n_conv_tiles))
    def _():
        conv_tile(None)

    @pl.when(j >= n_conv_tiles)
    def _():
        w_bf = w_ref[...].astype(BF16)
        for m in range(tm // sub):
            rs = slice(m * sub, (m + 1) * sub)
            o_ref[0, rs, :] = _silu(_dot_nt(h_s[rs, :], w_bf)).astype(o_ref.dtype)


def _group_lane_layout(vec_b, vec_a):
    lead = vec_b.shape[:-1]
    vb = vec_b.reshape(*lead, GDN_HEAD_GROUPS, GDN_VH_PER_GROUP)
    va = vec_a.reshape(*lead, GDN_HEAD_GROUPS, GDN_VH_PER_GROUP)
    pad = jnp.zeros((*lead, GDN_HEAD_GROUPS, LANES - 2 * GDN_VH_PER_GROUP), vec_b.dtype)
    return jnp.concatenate([vb, va, pad], axis=-1).reshape(*lead, GDN_HEAD_GROUPS * LANES)


def gdn_in_projection(x, gain, shift, scale, w_in, conv_w, a_log, dt_bias, tm, conv_hist=None, tn=512):
    b, l, d = x.shape
    n_main = GDN_CONV_DIM + GDN_VAL_DIM
    w_t = jnp.transpose(w_in)
    wba = _group_lane_layout(w_in[:, n_main:n_main + GDN_V_HEADS], w_in[:, n_main + GDN_V_HEADS:])
    wba_t = jnp.transpose(wba)
    zeros = jnp.zeros((1, GDN_V_HEADS), F32)
    alog = _group_lane_layout(zeros, a_log.reshape(1, -1).astype(F32))
    dtb = _group_lane_layout(zeros, dt_bias.reshape(1, -1).astype(F32))
    nbg = wba_t.shape[0]
    n_conv_tiles = GDN_CONV_DIM // tn
    n_qk_tiles = 2 * GDN_KEY_DIM // tn
    conv_col = lambda b_, i, j: (0, jnp.minimum(j, n_conv_tiles - 1))
    in_specs = [pl.BlockSpec((1, tm, d), lambda b_, i, j: (b_, i, 0)),
                pl.BlockSpec((1, d), lambda b_, i, j: (0, 0)),
                _mod_specs(shift, tm), _mod_specs(scale, tm),
                pl.BlockSpec((tn, d), lambda b_, i, j: (j, 0)),
                pl.BlockSpec((nbg, d), lambda b_, i, j: (0, 0)),
                pl.BlockSpec((1, nbg), lambda b_, i, j: (0, 0)),
                pl.BlockSpec((1, nbg), lambda b_, i, j: (0, 0)),
                pl.BlockSpec((GDN_CONV, tn), conv_col)]
    args = [x, gain.reshape(1, d), shift, scale, w_t, wba_t, alog, dtb, conv_w]
    scratch_shapes = [pltpu.VMEM((tm, d), BF16)]
    blocks = (_nbytes((tm, d), F32) + _nbytes((tn, d), F32) + _nbytes((tm, tn), BF16)
              + _nbytes((nbg, d), F32) + _nbytes((tm, nbg), F32) + 2 * _nbytes((1, d), F32))
    scratch = _nbytes((tm, d), BF16) + _nbytes((tn, d), BF16) + 6 * _nbytes((tm, nbg), F32)
    if conv_hist is None:
        bsz = None
        raw_spec = pl.BlockSpec((1, 1, SUBLANES, tn), lambda b_, i, j: (b_, i, 0, jnp.minimum(j, n_conv_tiles - 1)))
        raw_shape = jax.ShapeDtypeStruct((b, l // tm, SUBLANES, GDN_CONV_DIM), F32)
        scratch_shapes.append(pltpu.VMEM((n_conv_tiles, SUBLANES, tn), F32))
    else:
        assert b == 1 and l == tm <= GDN_IN_SUB_ROWS
        hist_rows = conv_hist.shape[0]
        bsz = hist_rows // (GDN_CONV - 1)
        args.append(conv_hist)
        in_specs.append(pl.BlockSpec((hist_rows, tn), conv_col))
        raw_spec = pl.BlockSpec((hist_rows, tn), conv_col)
        raw_shape = jax.ShapeDtypeStruct((hist_rows, GDN_CONV_DIM), F32)
        blocks += 2 * _nbytes((hist_rows, tn), F32)
    return pl.pallas_call(
        functools.partial(_gdn_in_kernel, n_qk_tiles=n_qk_tiles, n_conv_tiles=n_conv_tiles, bsz=bsz),
        grid=(b, l // tm, n_main // tn),
        in_specs=in_specs,
        out_specs=[pl.BlockSpec((1, tm, tn), lambda b_, i, j: (b_, i, j)),
                   pl.BlockSpec((1, tm, nbg), lambda b_, i, j: (b_, i, 0)),
                   raw_spec],
        out_shape=[jax.ShapeDtypeStruct((b, l, n_main), BF16),
                   jax.ShapeDtypeStruct((b, l, nbg), F32),
                   raw_shape],
        scratch_shapes=scratch_shapes,
        compiler_params=_params(3, blocks, scratch),
        name="gdn_in_projection",
    )(*args)


def _gdn_core_kernel(q_ref, k_ref, v_ref, zg_ref, bg_ref, onorm_ref, s0_ref, o_ref, s1_ref,
                     s_s, gc_s, tdec_s, mo_s, *, seq_rows):
    r = pl.program_id(2)
    n_r = pl.num_programs(2)
    rows = q_ref.shape[1]
    chunk = GDN_CHUNK
    vpg = GDN_VH_PER_GROUP
    rep = GDN_V_HEADS // GDN_K_HEADS
    n_kh = GDN_KH_PER_GROUP
    seg = chunk if seq_rows is None else seq_rows
    n_seq = chunk // seg

    def load_states():
        for s in range(n_seq):
            for kh in range(n_kh):
                s_s[s * n_kh + kh] = jnp.concatenate([s0_ref[s, kh * rep + e] for e in range(rep)], axis=1)

    if seq_rows is None:
        pl.when(r == 0)(load_states)
    else:
        load_states()

    def seg_last(x):
        if n_seq == 1:
            return x[chunk - 1:chunk, :]
        blocks = x.reshape(n_seq, seg, x.shape[1])
        return jnp.broadcast_to(blocks[:, seg - 1:seg, :], blocks.shape).reshape(x.shape)

    n_chunks = rows // chunk
    pw = GDN_PACK * chunk
    n_packs = vpg // GDN_PACK
    ri = lax.broadcasted_iota(jnp.int32, (chunk, pw), 0)
    li = lax.broadcasted_iota(jnp.int32, (chunk, pw), 1) % chunk
    seg_shift = int(math.log2(seg))
    same_seq = (ri >> seg_shift) == (li >> seg_shift)
    tril_p = (ri >= li) & same_seq
    strict_p = (ri > li) & same_seq
    eye_p = (ri == li).astype(F32)
    n_levels = seg_shift
    off_masks = [((ri >> (lvl + 1)) == (li >> (lvl + 1))) & ((ri >> lvl) != (li >> lvl))
                 for lvl in range(n_levels)]
    bd_rows = lax.broadcasted_iota(jnp.int32, (pw, pw), 0) // chunk
    bd_cols = lax.broadcasted_iota(jnp.int32, (pw, pw), 1) // chunk
    bd_mask = (bd_rows == bd_cols).astype(BF16)
    row_c = lax.broadcasted_iota(jnp.int32, (chunk, LANES), 0)
    lane_lo = lax.broadcasted_iota(jnp.int32, (chunk, LANES), 1) < chunk

    def block_diag(xp):
        return jnp.concatenate([xp.astype(BF16)] * GDN_PACK, axis=0) * bd_mask

    def pack_cols(arr, first_col):
        tiles = []
        for t in range(pw // LANES):
            even = jnp.broadcast_to(arr[:, first_col + 2 * t:first_col + 2 * t + 1], (chunk, LANES))
            odd = jnp.broadcast_to(arr[:, first_col + 2 * t + 1:first_col + 2 * t + 2], (chunk, LANES))
            tiles.append(jnp.where(lane_lo, even, odd))
        return jnp.concatenate(tiles, axis=1)

    for c0 in range(0, n_chunks, GDN_PHASE_A_CHUNKS):
        group = [(c, p) for c in range(c0, min(c0 + GDN_PHASE_A_CHUNKS, n_chunks)) for p in range(n_packs)]
        a_list, qkd, brow, dec = {}, {}, {}, {}
        for c in range(c0, min(c0 + GDN_PHASE_A_CHUNKS, n_chunks)):
            rs = slice(c * chunk, (c + 1) * chunk)
            bg = bg_ref[0, rs, :]
            cum = bg
            shift = 1
            while shift < seg:
                cum = cum + jnp.where((row_c & (seg - 1)) >= shift, pltpu.roll(cum, shift, axis=0), 0.0)
                shift *= 2
            gc_s[rs, :] = cum
            gq = []
            for kh in range(GDN_KH_PER_GROUP):
                ksl = slice(kh * LANES, (kh + 1) * LANES)
                k = k_ref[0, rs, ksl]
                kq = jnp.concatenate([k, q_ref[0, rs, ksl]], axis=0)
                kk = jnp.concatenate([k, k], axis=0)
                gq.append(_dot_nt(kq, kk))
            for p in range(n_packs):
                khs = [(p * GDN_PACK + 2 * t) // rep for t in range(pw // LANES)]
                gram = jnp.concatenate([gq[kh][:chunk] for kh in khs], axis=1)
                qk = jnp.concatenate([gq[kh][chunk:] for kh in khs], axis=1)
                beta_p = pack_cols(bg, p * GDN_PACK)
                gcol_p = pack_cols(cum, vpg + p * GDN_PACK)
                grow_p = jnp.sum(gcol_p * eye_p, axis=0, keepdims=True)
                decay = jnp.exp(jnp.where(tril_p, gcol_p - grow_p, NEG_INF))
                a_list[(c, p)] = jnp.where(strict_p, gram * beta_p * decay, 0.0)
                qkd[(c, p)] = (qk * decay).astype(BF16)
                brow[(c, p)] = jnp.sum(beta_p * eye_p, axis=0, keepdims=True)
                dec[(c, p)] = jnp.exp(seg_last(gcol_p) - gcol_p)
        inv = {cp: eye_p - jnp.where(off_masks[0], a_list[cp], 0.0) for cp in group}
        for lvl in range(1, n_levels):
            w = {cp: _dot(jnp.where(off_masks[lvl], a_list[cp], 0.0).astype(BF16), block_diag(inv[cp]))
                 for cp in group}
            inv = {cp: inv[cp] - _dot(inv[cp].astype(BF16), block_diag(w[cp])) for cp in group}
        for cp in group:
            t_beta = inv[cp] * brow[cp]
            tdec_s[cp[0], cp[1]] = block_diag(dec[cp] * t_beta)
            mo_s[cp[0], cp[1]] = block_diag(_dot(qkd[cp], block_diag(t_beta)))

    def chunk_body(c, carry):
        rs = pl.ds(pl.multiple_of(c * chunk, chunk), chunk)
        cum = gc_s[rs, :]
        eg, g_last, ks, qs = [], [], [], []
        for kh in range(n_kh):
            ksl = slice(kh * LANES, (kh + 1) * LANES)
            kq = jnp.concatenate([k_ref[0, rs, ksl], q_ref[0, rs, ksl]], axis=0)
            per_seq = [_dot(kq, s_s[s * n_kh + kh].astype(BF16)) for s in range(n_seq)]
            kq_s = jnp.concatenate([per_seq[s][half * chunk + s * seg:half * chunk + (s + 1) * seg]
                                    for half in range(2) for s in range(n_seq)], axis=0)
            for e in range(rep):
                hv = kh * rep + e
                gcol = cum[:, vpg + hv:vpg + hv + 1]
                g_last.append([gcol[(s + 1) * seg - 1:(s + 1) * seg, :] for s in range(n_seq)])
                eg.append(jnp.exp(gcol))
                ks.append(kq_s[:chunk, e * LANES:(e + 1) * LANES] * eg[hv])
                qs.append(kq_s[chunk:, e * LANES:(e + 1) * LANES] * eg[hv])
        v_dec, o_intra = [], []
        for p in range(n_packs):
            heads = range(p * GDN_PACK, (p + 1) * GDN_PACK)
            rhs = jnp.concatenate([v_ref[0, rs, hv * LANES:(hv + 1) * LANES].astype(F32) - ks[hv]
                                   for hv in heads], axis=0).astype(BF16)
            vd = _dot(tdec_s[c, p], rhs)
            oi = _dot(mo_s[c, p], rhs)
            for j in range(GDN_PACK):
                v_dec.append(vd[j * chunk:(j + 1) * chunk])
                o_intra.append(oi[j * chunk:(j + 1) * chunk])
        for hv in range(vpg):
            vsl = slice(hv * LANES, (hv + 1) * LANES)
            o = qs[hv] + o_intra[hv]
            on = o * lax.rsqrt(jnp.mean(o * o, axis=-1, keepdims=True) + NORM_EPS) * onorm_ref[...]
            o_ref[0, rs, vsl] = (on * zg_ref[0, rs, vsl].astype(F32)).astype(o_ref.dtype)
        lane_v = lax.broadcasted_iota(jnp.int32, (1, rep * LANES), 1) // LANES
        row_seq = lax.broadcasted_iota(jnp.int32, (chunk, 1), 0) >> seg_shift
        for kh in range(n_kh):
            heads = range(kh * rep, (kh + 1) * rep)
            k_bf = k_ref[0, rs, kh * LANES:(kh + 1) * LANES]
            vd_pair = jnp.concatenate([v_dec[hv] for hv in heads], axis=1)
            for s in range(n_seq):
                vd_s = vd_pair if n_seq == 1 else jnp.where(row_seq == s, vd_pair, 0.0)
                ds = _dot_tn(k_bf, vd_s.astype(BF16))
                keep = jnp.exp(g_last[kh * rep][s])
                for e in range(1, rep):
                    keep = jnp.where(lane_v == e, jnp.exp(g_last[kh * rep + e][s]), keep)
                s_s[s * n_kh + kh] = s_s[s * n_kh + kh] * keep + ds
        return carry

    lax.fori_loop(0, n_chunks, chunk_body, 0)

    def store_states():
        for s in range(n_seq):
            for hv in range(vpg):
                s1_ref[s, hv] = s_s[s * n_kh + hv // rep][:, (hv % rep) * LANES:(hv % rep + 1) * LANES]

    if seq_rows is None:
        pl.when(r == n_r - 1)(store_states)
    else:
        store_states()


def gdn_core(qkvz, bg, s0, out_norm, rows, seq_rows=None):
    b, l, _ = qkvz.shape
    n_state = 1 if seq_rows is None else GDN_CHUNK // seq_rows
    if seq_rows is None:
        state_idx = lambda b_, g, r: (b_, g, 0, 0)
    else:
        assert b == 1 and rows == GDN_CHUNK
        state_idx = lambda b_, g, r: (r, g, 0, 0)
    hg = GDN_HEAD_GROUPS
    qw = GDN_KEY_DIM // hg
    vw = GDN_VAL_DIM // hg
    kq = GDN_KEY_DIM // qw
    kv = 2 * GDN_KEY_DIM // vw
    kz = GDN_CONV_DIM // vw
    vpg = GDN_VH_PER_GROUP
    blocks = (2 * _nbytes((rows, qw), BF16) + 3 * _nbytes((rows, vw), BF16) + _nbytes((rows, LANES), F32)
              + 2 * _nbytes((n_state, vpg, GDN_DK, GDN_DV), F32))
    n_chunks = rows // GDN_CHUNK
    n_packs = vpg // GDN_PACK
    pw = GDN_PACK * GDN_CHUNK
    scratch = (_nbytes((n_state, vpg, GDN_DK, GDN_DV), F32) + _nbytes((rows, LANES), F32)
               + 2 * _nbytes((n_chunks, n_packs, pw, pw), BF16) + 6 * _nbytes((rows, vw), F32))
    return pl.pallas_call(
        functools.partial(_gdn_core_kernel, seq_rows=seq_rows),
        grid=(b, hg, l // rows),
        in_specs=[pl.BlockSpec((1, rows, qw), lambda b_, g, r: (b_, r, g)),
                  pl.BlockSpec((1, rows, qw), lambda b_, g, r: (b_, r, kq + g)),
                  pl.BlockSpec((1, rows, vw), lambda b_, g, r: (b_, r, kv + g)),
                  pl.BlockSpec((1, rows, vw), lambda b_, g, r: (b_, r, kz + g)),
                  pl.BlockSpec((1, rows, LANES), lambda b_, g, r: (b_, r, g)),
                  pl.BlockSpec((1, GDN_DV), lambda b_, g, r: (0, 0)),
                  pl.BlockSpec((n_state, vpg, GDN_DK, GDN_DV), state_idx)],
        out_specs=[pl.BlockSpec((1, rows, vw), lambda b_, g, r: (b_, r, g)),
                   pl.BlockSpec((n_state, vpg, GDN_DK, GDN_DV), state_idx)],
        out_shape=[jax.ShapeDtypeStruct((b, l, GDN_VAL_DIM), BF16),
                   jax.ShapeDtypeStruct(s0.shape, F32)],
        scratch_shapes=[pltpu.VMEM((n_state * GDN_KH_PER_GROUP, GDN_DK,
                                    (GDN_V_HEADS // GDN_K_HEADS) * GDN_DV), F32),
                        pltpu.VMEM((rows, LANES), F32),
                        pltpu.VMEM((n_chunks, n_packs, pw, pw), BF16),
                        pltpu.VMEM((n_chunks, n_packs, pw, pw), BF16)],
        compiler_params=_params(3, blocks, scratch),
        name="gdn_core",
    )(qkvz, qkvz, qkvz, qkvz, bg, out_norm.reshape(1, GDN_DV).astype(F32), s0)


def _tile_rows(l, cap):
    t = min(l, cap)
    while l % t:
        t //= 2
    return t


def _to_time_major(a):
    return jnp.transpose(a, (1, 0, 2)).reshape(a.shape[0] * a.shape[1], a.shape[2])


def _from_time_major(a, bsz):
    return jnp.transpose(a.reshape(a.shape[0] // bsz, bsz, a.shape[1]), (1, 0, 2))


def _trunk(x, mod, states, P, sample):
    bsz, l, d = x.shape
    tm = _tile_rows(l, ROW_TILE_CAP)
    out_proj = matmul_gate_residual if sample else matmul_gate_residual_wres
    new = {}
    depth = mod.shape[0]
    for layer in range(depth):
        if sample:
            parts = [mod[layer][None, :, k * d:(k + 1) * d] for k in range(6)]
        else:
            parts = [mod[layer][:, None, k * d:(k + 1) * d] for k in range(6)]
        sh_m, sc_m, g_m, sh_f, sc_f, g_f = parts
        i = layer // 2
        if layer % 2 == 0:
            qkv = qkv_projection(x, P['norm_mix'][layer], sh_m, sc_m, P['w_attn_qkv'][i],
                                 P['attn_q_norm'][i], P['attn_k_norm'][i], tm)
            nq, nkv = N_HEADS * HEAD_DIM, N_KV_HEADS * HEAD_DIM
            if sample:
                o, k_win, v_win = attention_sample(qkv[0], states['win_k'][i], states['win_v'][i],
                                                   P['attn_sinks'][i], P['rel_bias_table'])
                o = o[None]
                w_len = k_win.shape[1]
                new.setdefault('win_k', []).append(k_win.reshape(-1, w_len, N_KV_HEADS, HEAD_DIM))
                new.setdefault('win_v', []).append(v_win.reshape(-1, w_len, N_KV_HEADS, HEAD_DIM))
            else:
                o = attention_prompt(qkv, P['attn_sinks'][i], P['rel_bias_table'])
                keep = min(WINDOW, PAST_LEN)
                new.setdefault('win_k', []).append(
                    qkv[:, l - keep:, nq:nq + nkv].reshape(bsz, keep, N_KV_HEADS, HEAD_DIM))
                new.setdefault('win_v', []).append(
                    qkv[:, l - keep:, nq + nkv:].reshape(bsz, keep, N_KV_HEADS, HEAD_DIM))
            x = matmul_gate_residual(o, P['w_attn_o'], i, x, g_m, _tile_rows(l, 2 * ROW_TILE_CAP), 512)
        else:
            keep = GDN_CONV - 1
            gdn_args = (x, P['norm_mix'][layer], sh_m, sc_m, P['w_gdn_in'][i], P['gdn_conv_w'][i],
                        P['gdn_a_log'][i], P['gdn_dt_bias'][i], tm)
            if sample:
                seqs = states['gdn'].shape[1]
                t = l // seqs
                qkvz, bg, hist1 = gdn_in_projection(
                    *gdn_args, conv_hist=_to_time_major(states['gdn_conv'][i].astype(F32)), tn=2 * 512)
                new.setdefault('gdn_conv', []).append(_from_time_major(hist1, seqs))
                seq_rows = -(-t // SUBLANES) * SUBLANES
                per_seq = lambda a: jnp.pad(_from_time_major(a, seqs), ((0, 0), (0, seq_rows - t), (0, 0))
                                            ).reshape(1, seqs * seq_rows, a.shape[-1])
                o_seq, s1 = gdn_core(per_seq(qkvz[0]), per_seq(bg[0]), states['gdn'][i].astype(F32),
                                     P['gdn_out_norm'][i], GDN_CHUNK, seq_rows=seq_rows)
                o = _to_time_major(o_seq.reshape(seqs, seq_rows, GDN_VAL_DIM)[:, :t])[None]
            else:
                qkvz, bg, raw = gdn_in_projection(*gdn_args)
                s0 = jnp.zeros((bsz, GDN_V_HEADS, GDN_DK, GDN_DV), F32)
                o, s1 = gdn_core(qkvz, bg, s0, P['gdn_out_norm'][i], _tile_rows(l, GDN_ROW_TILE_CAP))
                new.setdefault('gdn_conv', []).append(raw[:, -1, SUBLANES - keep:, :])
            new.setdefault('gdn', []).append(s1)
            x = out_proj(o, P['w_gdn_out'], i, x, g_m, tm, 512)
        if sample:
            seqs = states['ffn_conv'].shape[1]
            act, f1 = ffn_up_sample(x[0], P['norm_ffn'][layer], sh_f[0], sc_f[0], P['w_ffn_up'],
                                    P['ffn_conv_w'], P['ffn_conv_b'], layer,
                                    _to_time_major(states['ffn_conv'][layer].astype(F32)))
            act, f1 = act[None], _from_time_major(f1, seqs)
        else:
            act, f1 = ffn_up_prompt(x, P['norm_ffn'][layer], sh_f, sc_f, P['w_ffn_up'],
                                    P['ffn_conv_w'], P['ffn_conv_b'], layer, tm)
        new.setdefault('ffn_conv', []).append(f1)
        if sample:
            x = matmul_gate_residual(act, P['w_ffn_down'], layer, x, g_f, tm, 512)
        else:
            x = matmul_gate_residual_wres(act, P['w_ffn_down'], layer, x, g_f, _tile_rows(l, 512), 512)
    return x, {k: jnp.stack(v) for k, v in new.items()}


def kernel(x_prompt, x_sample, c_prompt, c_sample, cache_win_k, cache_win_v, state_gdn, state_gdn_conv,
           state_ffn_conv, rel_bias_table, w_ada, b_ada, norm_mix, norm_ffn, w_attn_qkv, attn_q_norm,
           attn_k_norm, attn_sinks, w_attn_o, w_gdn_in, gdn_conv_w, gdn_a_log, gdn_dt_bias, gdn_out_norm,
           w_gdn_out, w_ffn_up, ffn_conv_w, ffn_conv_b, w_ffn_down):
    P = dict(rel_bias_table=rel_bias_table, norm_mix=norm_mix, norm_ffn=norm_ffn, w_attn_qkv=w_attn_qkv,
             attn_q_norm=attn_q_norm, attn_k_norm=attn_k_norm, attn_sinks=attn_sinks, w_attn_o=w_attn_o,
             w_gdn_in=w_gdn_in, gdn_conv_w=gdn_conv_w, gdn_a_log=gdn_a_log, gdn_dt_bias=gdn_dt_bias,
             gdn_out_norm=gdn_out_norm, w_gdn_out=w_gdn_out, w_ffn_up=w_ffn_up, ffn_conv_w=ffn_conv_w,
             ffn_conv_b=ffn_conv_b, w_ffn_down=w_ffn_down)
    bp = x_prompt.shape[0]
    bs, ts, d = x_sample.shape
    n_c = bp + bs
    c_rows = -(-n_c // SUBLANES) * SUBLANES
    c_all = jnp.pad(jnp.concatenate([c_prompt, c_sample], axis=0), ((0, c_rows - n_c), (0, 0)))
    mod = ada_modulation(c_all, w_ada, b_ada)
    mod_prompt = mod[:, :bp]
    mod_sample = jnp.tile(mod[:, bp:n_c], (1, ts, 1))

    y_p, new_p = _trunk(x_prompt, mod_prompt, None, P, sample=False)
    states = dict(win_k=cache_win_k, win_v=cache_win_v, gdn=state_gdn, gdn_conv=state_gdn_conv,
                  ffn_conv=state_ffn_conv)
    y_s, new_s = _trunk(_to_time_major(x_sample)[None], mod_sample, states, P, sample=True)
    y_s = _from_time_major(y_s[0], bs)
    return (y_p, y_s, new_p['win_k'], new_p['win_v'], new_s['win_k'], new_s['win_v'],
            new_p['gdn'], new_s['gdn'], new_p['gdn_conv'], new_s['gdn_conv'],
            new_p['ffn_conv'], new_s['ffn_conv'])
```

```python
import functools
import math

import numpy as np
import jax
import jax.numpy as jnp
from jax import lax
from jax.experimental import pallas as pl
from jax.experimental.pallas import tpu as pltpu

HEAD_DIM = 64
N_HEADS = 32
N_KV_HEADS = 4
ATT_GROUP = N_HEADS // N_KV_HEADS
WINDOW = 128
ATT_BLOCK = 128
ATT_SAMPLE_SEQS = 8
N_BUCKETS = 32
MAX_DISTANCE = 128
NEG_INF = -1e30
PAST_LEN = 16384

GDN_K_HEADS = 16
GDN_V_HEADS = 32
GDN_DK = 128
GDN_DV = 128
GDN_KEY_DIM = GDN_K_HEADS * GDN_DK
GDN_VAL_DIM = GDN_V_HEADS * GDN_DV
GDN_CONV_DIM = 2 * GDN_KEY_DIM + GDN_VAL_DIM
GDN_CONV = 4
GDN_CHUNK = 64
GDN_HEAD_GROUPS = 4
GDN_VH_PER_GROUP = GDN_V_HEADS // GDN_HEAD_GROUPS
GDN_KH_PER_GROUP = GDN_K_HEADS // GDN_HEAD_GROUPS
GDN_PACK = 4
GDN_PHASE_A_CHUNKS = 4

FFN_CONV = 3
NORM_EPS = 1e-6

LANES = 128
SUBLANES = 8
VMEM_CAP_BYTES = 60 * 1024 * 1024
VMEM_SLACK_BYTES = 8 * 1024 * 1024

ROW_TILE_CAP = 1024
GDN_IN_SUB_ROWS = 256
FFN_SUB_ROWS = 256
GDN_ROW_TILE_CAP = 1024

BF16 = jnp.bfloat16
F32 = jnp.float32


def _vmem_limit(block_bytes, scratch_bytes=0):
    est = 2 * int(block_bytes) + int(scratch_bytes) + VMEM_SLACK_BYTES
    return int(min(max(est, 16 * 1024 * 1024), VMEM_CAP_BYTES))


def _params(n_grid, block_bytes, scratch_bytes=0):
    return pltpu.CompilerParams(
        dimension_semantics=("arbitrary",) * n_grid,
        vmem_limit_bytes=_vmem_limit(block_bytes, scratch_bytes))


def _nbytes(shape, dtype):
    return int(np.prod(shape)) * jnp.dtype(dtype).itemsize


def _silu(x):
    return x * (1.0 / (1.0 + jnp.exp2(x * (-math.log2(math.e)))))


def _dot(a, b):
    return jnp.dot(a, b, preferred_element_type=F32)


def _dot_nt(a, b):
    return lax.dot_general(a, b, (((1,), (1,)), ((), ())), preferred_element_type=F32)


def _dot_tn(a, b):
    return lax.dot_general(a, b, (((0,), (0,)), ((), ())), preferred_element_type=F32)


def _norm_mod(x, gain, shift, scale):
    ms = jnp.mean(x * x, axis=-1, keepdims=True)
    y = x * lax.rsqrt(ms + NORM_EPS) * gain
    return y * (1.0 + scale) + shift


def _mod_specs(shift, tm):
    d = shift.shape[-1]
    if shift.shape[1] == 1:
        return pl.BlockSpec((1, 1, d), lambda b, i, j: (b, 0, 0))
    return pl.BlockSpec((1, tm, d), lambda b, i, j: (b, i, 0))


def _ada_kernel(c_ref, w_ref, b_ref, o_ref):
    a = _silu(c_ref[...]).astype(BF16)
    o_ref[0] = _dot(a, w_ref[0].astype(BF16)) + b_ref[0]


def ada_modulation(c_all, w_ada, b_ada, tn=1024):
    rows, d = c_all.shape
    depth, _, n = w_ada.shape
    blocks = _nbytes((rows, d), F32) + _nbytes((d, tn), F32) + _nbytes((rows, tn), F32)
    return pl.pallas_call(
        _ada_kernel,
        grid=(depth, n // tn),
        in_specs=[pl.BlockSpec((rows, d), lambda l, j: (0, 0)),
                  pl.BlockSpec((1, d, tn), lambda l, j: (l, 0, j)),
                  pl.BlockSpec((1, 1, tn), lambda l, j: (l, 0, j))],
        out_specs=pl.BlockSpec((1, rows, tn), lambda l, j: (l, 0, j)),
        out_shape=jax.ShapeDtypeStruct((depth, rows, n), F32),
        compiler_params=_params(2, blocks, _nbytes((d, tn), BF16)),
        name="ada_modulation",
    )(c_all, w_ada, b_ada.reshape(depth, 1, n))


def _qkv_kernel(x_ref, gain_ref, shift_ref, scale_ref, w_ref, hgain_ref, hflag_ref, gmat_ref,
                o_ref, h_s):
    @pl.when(pl.program_id(2) == 0)
    def _():
        h_s[...] = _norm_mod(x_ref[0], gain_ref[...], shift_ref[0], scale_ref[0]).astype(BF16)

    y = _dot(h_s[...], w_ref[...].astype(BF16))
    ms = _dot((y * y).astype(BF16), gmat_ref[...])
    yn = y * lax.rsqrt(ms + NORM_EPS) * hgain_ref[...]
    o_ref[0] = jnp.where(hflag_ref[...] > 0.0, yn, y)


def qkv_projection(x, gain, shift, scale, w, q_gain, k_gain, tm, tn=512):
    b, l, d = x.shape
    n = w.shape[1]
    nq, nk = N_HEADS * HEAD_DIM, N_KV_HEADS * HEAD_DIM
    hgain = jnp.concatenate([jnp.tile(q_gain, N_HEADS), jnp.tile(k_gain, N_KV_HEADS),
                             jnp.ones((nk,), F32)]).reshape(1, n)
    hflag = jnp.concatenate([jnp.ones((nq + nk,), F32), jnp.zeros((nk,), F32)]).reshape(1, n)
    gidx = np.arange(tn) // HEAD_DIM
    gmat = jnp.asarray((gidx[:, None] == gidx[None, :]).astype(np.float32) / HEAD_DIM, BF16)
    blocks = (_nbytes((tm, d), F32) + _nbytes((d, tn), F32) + _nbytes((tm, tn), F32)
              + 2 * _nbytes((shift.shape[1] == 1 and 1 or tm, d), F32) + _nbytes((tn, tn), BF16))
    scratch = _nbytes((tm, d), BF16) + _nbytes((d, tn), BF16) + 4 * _nbytes((tm, tn), F32)
    return pl.pallas_call(
        _qkv_kernel,
        grid=(b, l // tm, n // tn),
        in_specs=[pl.BlockSpec((1, tm, d), lambda b_, i, j: (b_, i, 0)),
                  pl.BlockSpec((1, d), lambda b_, i, j: (0, 0)),
                  _mod_specs(shift, tm), _mod_specs(scale, tm),
                  pl.BlockSpec((d, tn), lambda b_, i, j: (0, j)),
                  pl.BlockSpec((1, tn), lambda b_, i, j: (0, j)),
                  pl.BlockSpec((1, tn), lambda b_, i, j: (0, j)),
                  pl.BlockSpec((tn, tn), lambda b_, i, j: (0, 0))],
        out_specs=pl.BlockSpec((1, tm, tn), lambda b_, i, j: (b_, i, j)),
        out_shape=jax.ShapeDtypeStruct((b, l, n), F32),
        scratch_shapes=[pltpu.VMEM((tm, d), BF16)],
        compiler_params=_params(3, blocks, scratch),
        name="qkv_projection",
    )(x, gain.reshape(1, d), shift, scale, w, hgain, hflag, gmat)


def _attn_prompt_kernel(sink_ref, q_ref, cur_ref, prev_ref, bprev_ref, bcur_ref, o_ref, kz_s, vz_s):
    n = pl.program_id(1)
    blk = ATT_BLOCK
    lane = lax.broadcasted_iota(jnp.int32, (2 * blk, LANES), 1)
    lo_half = lane < HEAD_DIM
    neg_prev = jnp.where(n == 0, NEG_INF, 0.0).astype(F32)

    kv_width = N_KV_HEADS * HEAD_DIM
    for pair in range(N_KV_HEADS // 2):
        for part, dst in ((0, kz_s), (1, vz_s)):
            col = part * kv_width + pair * LANES
            both = jnp.concatenate([prev_ref[0, :, col:col + LANES], cur_ref[0, :, col:col + LANES]], axis=0)
            swapped = pltpu.roll(both, HEAD_DIM, axis=1)
            zero = jnp.zeros_like(both)
            c0, c1 = 2 * pair, 2 * pair + 1
            dst[2 * c0 + 0] = jnp.where(lo_half, both, zero).astype(BF16)
            dst[2 * c0 + 1] = jnp.where(lo_half, zero, swapped).astype(BF16)
            dst[2 * c1 + 0] = jnp.where(lo_half, swapped, zero).astype(BF16)
            dst[2 * c1 + 1] = jnp.where(lo_half, zero, both).astype(BF16)

    scale = HEAD_DIM ** -0.5 * math.log2(math.e)
    for p in range(N_HEADS // 2):
        c = (2 * p) // ATT_GROUP
        qp = (q_ref[0, :, p * LANES:(p + 1) * LANES] * scale).astype(BF16)
        o_pair = None
        for a in range(2):
            h = 2 * p + a
            sink = sink_ref[h]
            s = _dot_nt(qp, kz_s[2 * c + a])
            s_prev = s[:, :blk] + bprev_ref[h] + neg_prev
            s_cur = s[:, blk:] + bcur_ref[h]
            m = jnp.maximum(jnp.max(jnp.maximum(s_prev, s_cur), axis=-1, keepdims=True), sink)
            e_prev = jnp.exp2(s_prev - m)
            e_cur = jnp.exp2(s_cur - m)
            den = jnp.sum(e_prev + e_cur, axis=-1, keepdims=True) + jnp.exp2(sink - m)
            pm = jnp.concatenate([e_prev, e_cur], axis=1).astype(BF16)
            o_a = _dot(pm, vz_s[2 * c + a]) * (1.0 / den)
            o_pair = o_a if o_pair is None else o_pair + o_a
        o_ref[0, :, p * LANES:(p + 1) * LANES] = o_pair.astype(o_ref.dtype)


def _t5_bucket_np(dist):
    max_exact = N_BUCKETS // 2
    d = np.maximum(dist, 0)
    df = np.maximum(d, 1).astype(np.float32)
    large = max_exact + (np.log(df / np.float32(max_exact)) / np.float32(math.log(MAX_DISTANCE / max_exact))
                         * np.float32(N_BUCKETS - max_exact)).astype(np.int32)
    large = np.minimum(large, N_BUCKETS - 1)
    return np.where(d < max_exact, d, large)


def _bias_from_dist(dist, in_band, rel_table):
    onehot = (_t5_bucket_np(dist)[..., None] == np.arange(N_BUCKETS)).astype(np.float32)
    tab = jnp.einsum('qsb,bh->hqs', jnp.asarray(onehot), rel_table.astype(F32), precision=lax.Precision.HIGHEST)
    return jnp.where(jnp.asarray(in_band)[None], tab, NEG_INF)


def attention_prompt(qkv, sinks, rel_table):
    b, l, n = qkv.shape
    blk = ATT_BLOCK
    nq = N_HEADS * HEAD_DIM
    kvw = 2 * N_KV_HEADS * HEAD_DIM
    kv_blk = nq // kvw
    qi = np.arange(blk)[:, None]
    sj = np.arange(blk)[None, :]
    d_prev = qi + blk - sj
    d_cur = qi - sj
    log2e = math.log2(math.e)
    bias_prev = _bias_from_dist(d_prev, (d_prev >= 0) & (d_prev <= WINDOW), rel_table) * log2e
    bias_cur = _bias_from_dist(d_cur, (d_cur >= 0) & (d_cur <= WINDOW), rel_table) * log2e
    blocks = (_nbytes((blk, nq), F32) + 2 * _nbytes((blk, kvw), F32) + _nbytes((blk, nq), BF16))
    scratch = 2 * _nbytes((2 * N_KV_HEADS, 2 * blk, LANES), BF16) + 4 * _nbytes((N_HEADS, blk, blk), F32)
    return pl.pallas_call(
        _attn_prompt_kernel,
        grid=(b, l // blk),
        in_specs=[pl.BlockSpec(memory_space=pltpu.SMEM),
                  pl.BlockSpec((1, blk, nq), lambda b_, i: (b_, i, 0)),
                  pl.BlockSpec((1, blk, kvw), lambda b_, i: (b_, i, kv_blk)),
                  pl.BlockSpec((1, blk, kvw), lambda b_, i: (b_, jnp.maximum(i - 1, 0), kv_blk)),
                  pl.BlockSpec((N_HEADS, blk, blk), lambda b_, i: (0, 0, 0)),
                  pl.BlockSpec((N_HEADS, blk, blk), lambda b_, i: (0, 0, 0))],
        out_specs=pl.BlockSpec((1, blk, nq), lambda b_, i: (b_, i, 0)),
        out_shape=jax.ShapeDtypeStruct((b, l, nq), BF16),
        scratch_shapes=[pltpu.VMEM((2 * N_KV_HEADS, 2 * blk, LANES), BF16),
                        pltpu.VMEM((2 * N_KV_HEADS, 2 * blk, LANES), BF16)],
        compiler_params=_params(2, blocks, scratch),
        name="attention_prompt",
    )(sinks.astype(F32) * log2e, qkv, qkv, qkv, bias_prev, bias_cur)


def _attn_sample_kernel(q_ref, k_ref, v_ref, bias_ref, sink_ref, o_ref):
    pairs = [(b, c) for b in range(q_ref.shape[0]) for c in range(N_KV_HEADS)]
    k = [k_ref[b].astype(BF16) for b in range(q_ref.shape[0])]
    s = {bc: _dot_nt(q_ref[bc[0], bc[1]].astype(BF16), k[bc[0]]) + bias_ref[bc[1]] for bc in pairs}
    e, den = {}, {}
    for b, c in pairs:
        sink = sink_ref[c]
        m = jnp.maximum(jnp.max(s[(b, c)], axis=-1, keepdims=True), sink)
        e[(b, c)] = jnp.exp(s[(b, c)] - m)
        den[(b, c)] = jnp.sum(e[(b, c)], axis=-1, keepdims=True) + jnp.exp(sink - m)
    for b, c in pairs:
        o_ref[b, c] = _dot(e[(b, c)].astype(BF16), v_ref[b].astype(BF16)) * (1.0 / den[(b, c)])


def attention_sample(qkv, cache_k, cache_v, sinks, rel_table):
    bsz, w = cache_k.shape[0], cache_k.shape[1]
    t = qkv.shape[0] // bsz
    nq, nkv = N_HEADS * HEAD_DIM, N_KV_HEADS * HEAD_DIM
    keys = w + t
    keys_pad = -(-keys // 16) * 16
    rows = ATT_GROUP * t
    q = qkv[:, :nq].reshape(t, bsz, N_KV_HEADS, ATT_GROUP, HEAD_DIM) * HEAD_DIM ** -0.5
    q = jnp.transpose(q, (1, 2, 3, 0, 4)).reshape(bsz, N_KV_HEADS, rows, HEAD_DIM)
    qz = jnp.einsum('bcrd,ck->bcrkd', q, jnp.eye(N_KV_HEADS, dtype=F32)).reshape(bsz, N_KV_HEADS, rows, nkv)
    new_kv = jnp.transpose(qkv[:, nq:].reshape(t, bsz, 2 * nkv), (1, 0, 2))
    k_all = jnp.concatenate([cache_k.reshape(bsz, w, nkv), new_kv[:, :, :nkv]], axis=1)
    v_all = jnp.concatenate([cache_v.reshape(bsz, w, nkv), new_kv[:, :, nkv:]], axis=1)
    pad = ((0, 0), (0, keys_pad - keys), (0, 0))
    k_pad, v_pad = jnp.pad(k_all, pad), jnp.pad(v_all, pad)
    tq = np.arange(t)[:, None]
    sk = np.arange(keys_pad)[None, :]
    dist = tq + w - sk
    in_band = (dist >= 0) & (dist <= WINDOW) & (sk < keys)
    bias_t = _bias_from_dist(dist, in_band, rel_table)
    bias = bias_t.reshape(N_KV_HEADS, rows, keys_pad)
    sink_rows = jnp.repeat(sinks.astype(F32), t).reshape(N_KV_HEADS, rows, 1)
    sb = math.gcd(bsz, ATT_SAMPLE_SEQS)
    blocks = sb * (_nbytes((N_KV_HEADS, rows, nkv), F32) * 2 + 2 * _nbytes((keys_pad, nkv), F32))
    oz = pl.pallas_call(
        _attn_sample_kernel,
        grid=(bsz // sb,),
        in_specs=[pl.BlockSpec((sb, N_KV_HEADS, rows, nkv), lambda b_: (b_, 0, 0, 0)),
                  pl.BlockSpec((sb, keys_pad, nkv), lambda b_: (b_, 0, 0)),
                  pl.BlockSpec((sb, keys_pad, nkv), lambda b_: (b_, 0, 0)),
                  pl.BlockSpec((N_KV_HEADS, rows, keys_pad), lambda b_: (0, 0, 0)),
                  pl.BlockSpec((N_KV_HEADS, rows, 1), lambda b_: (0, 0, 0))],
        out_specs=pl.BlockSpec((sb, N_KV_HEADS, rows, nkv), lambda b_: (b_, 0, 0, 0)),
        out_shape=jax.ShapeDtypeStruct((bsz, N_KV_HEADS, rows, nkv), F32),
        compiler_params=_params(1, blocks),
        name="attention_sample",
    )(qz, k_pad, v_pad, bias, sink_rows)
    o = jnp.stack([oz[:, c, :, c * HEAD_DIM:(c + 1) * HEAD_DIM] for c in range(N_KV_HEADS)], axis=1)
    o = o.reshape(bsz, N_KV_HEADS, ATT_GROUP, t, HEAD_DIM)
    o = jnp.transpose(o, (3, 0, 1, 2, 4)).reshape(t * bsz, nq)
    return o.astype(BF16), k_all[:, t:], v_all[:, t:]


def _mm_res_kernel(a_ref, w_ref, x_ref, g_ref, o_ref):
    y = _dot(a_ref[0], w_ref[...].astype(BF16))
    o_ref[0] = x_ref[0] + g_ref[0] * y


def _mm_res_wres_kernel(a_ref, w_ref, x_ref, g_ref, o_ref, wbf_s):
    @pl.when((pl.program_id(1) == 0) & (pl.program_id(2) == 0))
    def _():
        wbf_s[...] = w_ref[...].astype(BF16)

    o_ref[0] = x_ref[0] + g_ref[0] * _dot(a_ref[0], wbf_s[...])


def matmul_gate_residual_wres(a, w_stack, layer, x, gate, tm, tn):
    b, l, k = a.shape
    n = w_stack.shape[2]
    if gate.shape[1] == 1:
        g_spec = pl.BlockSpec((1, 1, tn), lambda j, b_, i: (b_, 0, j))
    else:
        g_spec = pl.BlockSpec((1, tm, tn), lambda j, b_, i: (b_, i, j))
    blocks = _nbytes((tm, k), a.dtype) + _nbytes((k, tn), F32) + 3 * _nbytes((tm, tn), F32)
    scratch = _nbytes((k, tn), BF16) + _nbytes((tm, tn), F32)
    return pl.pallas_call(
        _mm_res_wres_kernel,
        grid=(n // tn, b, l // tm),
        in_specs=[pl.BlockSpec((1, tm, k), lambda j, b_, i: (b_, i, 0)),
                  pl.BlockSpec((None, k, tn), lambda j, b_, i: (layer, 0, j)),
                  pl.BlockSpec((1, tm, tn), lambda j, b_, i: (b_, i, j)),
                  g_spec],
        out_specs=pl.BlockSpec((1, tm, tn), lambda j, b_, i: (b_, i, j)),
        out_shape=jax.ShapeDtypeStruct((b, l, n), F32),
        scratch_shapes=[pltpu.VMEM((k, tn), BF16)],
        compiler_params=_params(3, blocks, scratch),
        name="matmul_gate_residual_wres",
    )(a, w_stack, x, gate)


def matmul_gate_residual(a, w_stack, layer, x, gate, tm, tn):
    b, l, k = a.shape
    n = w_stack.shape[2]
    grows = 1 if gate.shape[1] == 1 else tm
    if gate.shape[1] == 1:
        g_spec = pl.BlockSpec((1, 1, tn), lambda b_, i, j: (b_, 0, j))
    else:
        g_spec = pl.BlockSpec((1, tm, tn), lambda b_, i, j: (b_, i, j))
    blocks = (_nbytes((tm, k), a.dtype) + _nbytes((k, tn), F32) + 2 * _nbytes((tm, tn), F32)
              + _nbytes((grows, tn), F32))
    scratch = _nbytes((k, tn), BF16) + _nbytes((tm, tn), F32)
    return pl.pallas_call(
        _mm_res_kernel,
        grid=(b, l // tm, n // tn),
        in_specs=[pl.BlockSpec((1, tm, k), lambda b_, i, j: (b_, i, 0)),
                  pl.BlockSpec((None, k, tn), lambda b_, i, j: (layer, 0, j)),
                  pl.BlockSpec((1, tm, tn), lambda b_, i, j: (b_, i, j)),
                  g_spec],
        out_specs=pl.BlockSpec((1, tm, tn), lambda b_, i, j: (b_, i, j)),
        out_shape=jax.ShapeDtypeStruct((b, l, n), F32),
        compiler_params=_params(3, blocks, scratch),
        name="matmul_gate_residual",
    )(a, w_stack, x, gate)


def _ffn_conv_rows(u, prev1, prev2, cw_ref, cb_ref):
    return u * cw_ref[2:3, :] + prev1 * cw_ref[1:2, :] + prev2 * cw_ref[0:1, :] + cb_ref[...]


def _ffn_up_prompt_kernel(x_ref, gain_ref, shift_ref, scale_ref, wg_ref, wv_ref, cwg_ref, cwv_ref,
                          cbg_ref, cbv_ref, o_ref, tail_ref, h_s, carry_s, ext_s):
    i = pl.program_id(1)
    j = pl.program_id(2)
    tm = h_s.shape[0]

    @pl.when(j == 0)
    def _():
        h_s[...] = _norm_mod(x_ref[0], gain_ref[...], shift_ref[0], scale_ref[0]).astype(BF16)

    @pl.when(i == 0)
    def _():
        carry_s[j] = jnp.zeros(carry_s.shape[1:], F32)

    halves = ((wg_ref, cwg_ref, cbg_ref), (wv_ref, cwv_ref, cbv_ref))
    w_bf = [w_ref[...].astype(BF16) for w_ref, _, _ in halves]
    for half in range(2):
        ext_s[half, 0:SUBLANES, :] = carry_s[j, half]
    sub = min(FFN_SUB_ROWS, tm)
    for m in range(tm // sub):
        hm = h_s[m * sub:(m + 1) * sub, :]
        r0 = SUBLANES + m * sub
        ys = []
        for half, (_, cw_ref, cb_ref) in enumerate(halves):
            u = _dot(hm, w_bf[half])
            ext_s[half, r0:r0 + sub, :] = u
            ys.append(_ffn_conv_rows(u, ext_s[half, r0 - 1:r0 - 1 + sub, :], ext_s[half, r0 - 2:r0 - 2 + sub, :],
                                     cw_ref, cb_ref))
        o_ref[0, m * sub:(m + 1) * sub, :] = (_silu(ys[0]) * ys[1]).astype(o_ref.dtype)
    for half in range(2):
        tail = ext_s[half, tm:tm + SUBLANES, :]
        carry_s[j, half] = tail
        tail_ref[0, 0, half] = tail


def ffn_up_prompt(x, gain, shift, scale, w_up, conv_w, conv_b, layer, tm, tn=512):
    b, l, d = x.shape
    dff = w_up.shape[2] // 2
    nj = dff // tn
    conv_b = conv_b.reshape(conv_b.shape[0], 1, 2 * dff)
    blocks = (_nbytes((tm, d), F32) + 2 * _nbytes((d, tn), F32) + _nbytes((tm, tn), BF16)
              + 2 * _nbytes((1, d), F32) + 8 * _nbytes((SUBLANES, tn), F32))
    scratch = (_nbytes((tm, d), BF16) + _nbytes((nj, 2, SUBLANES, tn), F32) + 2 * _nbytes((d, tn), BF16)
               + 8 * _nbytes((tm, tn), F32))
    act, tail = pl.pallas_call(
        _ffn_up_prompt_kernel,
        grid=(b, l // tm, nj),
        in_specs=[pl.BlockSpec((1, tm, d), lambda b_, i, j: (b_, i, 0)),
                  pl.BlockSpec((1, d), lambda b_, i, j: (0, 0)),
                  _mod_specs(shift, tm), _mod_specs(scale, tm),
                  pl.BlockSpec((None, d, tn), lambda b_, i, j: (layer, 0, j)),
                  pl.BlockSpec((None, d, tn), lambda b_, i, j: (layer, 0, j + nj)),
                  pl.BlockSpec((None, FFN_CONV, tn), lambda b_, i, j: (layer, 0, j)),
                  pl.BlockSpec((None, FFN_CONV, tn), lambda b_, i, j: (layer, 0, j + nj)),
                  pl.BlockSpec((None, 1, tn), lambda b_, i, j: (layer, 0, j)),
                  pl.BlockSpec((None, 1, tn), lambda b_, i, j: (layer, 0, j + nj))],
        out_specs=[pl.BlockSpec((1, tm, tn), lambda b_, i, j: (b_, i, j)),
                   pl.BlockSpec((1, 1, 2, SUBLANES, tn), lambda b_, i, j: (b_, i, 0, 0, j))],
        out_shape=[jax.ShapeDtypeStruct((b, l, dff), BF16),
                   jax.ShapeDtypeStruct((b, l // tm, 2, SUBLANES, dff), F32)],
        scratch_shapes=[pltpu.VMEM((tm, d), BF16), pltpu.VMEM((nj, 2, SUBLANES, tn), F32),
                        pltpu.VMEM((2, SUBLANES + tm, tn), F32)],
        compiler_params=_params(3, blocks, scratch),
        name="ffn_up_prompt",
    )(x, gain.reshape(1, d), shift, scale, w_up, w_up, conv_w, conv_w, conv_b, conv_b)
    keep = FFN_CONV - 1
    hist = jnp.transpose(tail[:, -1, :, SUBLANES - keep:, :], (0, 2, 1, 3)).reshape(b, keep, 2 * dff)
    return act, hist


def _ffn_up_sample_kernel(x_ref, gain_ref, shift_ref, scale_ref, wg_ref, wv_ref, cwg_ref, cwv_ref,
                          cbg_ref, cbv_ref, hg_ref, hv_ref, o_ref, ng_ref, nv_ref, h_s, *, bsz):
    @pl.when(pl.program_id(0) == 0)
    def _():
        h_s[...] = _norm_mod(x_ref[...], gain_ref[...], shift_ref[...], scale_ref[...]).astype(BF16)

    h = h_s[...]
    rows = h_s.shape[0]
    keep = FFN_CONV - 1
    ys = []
    for w_ref, cw_ref, cb_ref, hist_ref, new_ref in ((wg_ref, cwg_ref, cbg_ref, hg_ref, ng_ref),
                                                     (wv_ref, cwv_ref, cbv_ref, hv_ref, nv_ref)):
        u = _dot(h, w_ref[...].astype(BF16))
        ext = jnp.concatenate([hist_ref[...], u], axis=0)
        prev = [ext[(keep - k) * bsz:(keep - k) * bsz + rows] for k in (1, 2)]
        ys.append(_ffn_conv_rows(u, prev[0], prev[1], cw_ref, cb_ref))
        new_ref[...] = ext[rows:rows + keep * bsz]
    o_ref[...] = (_silu(ys[0]) * ys[1]).astype(o_ref.dtype)


def ffn_up_sample(x, gain, shift, scale, w_up, conv_w, conv_b, layer, hist, tn=512):
    rows, d = x.shape
    dff = w_up.shape[2] // 2
    nj = dff // tn
    keep = FFN_CONV - 1
    bsz = hist.shape[0] // keep
    conv_b = conv_b.reshape(conv_b.shape[0], 1, 2 * dff)
    col = lambda j: (0, j)
    col_hi = lambda j: (0, j + nj)
    lcol = lambda j: (layer, 0, j)
    lcol_hi = lambda j: (layer, 0, j + nj)
    blocks = (_nbytes((rows, d), F32) * 3 + 2 * _nbytes((d, tn), F32) + 5 * _nbytes((rows, tn), F32))
    scratch = _nbytes((rows, d), BF16) + 2 * _nbytes((d, tn), BF16) + 8 * _nbytes((rows, tn), F32)
    act, new_g, new_v = pl.pallas_call(
        functools.partial(_ffn_up_sample_kernel, bsz=bsz),
        grid=(nj,),
        in_specs=[pl.BlockSpec((rows, d), lambda j: (0, 0)),
                  pl.BlockSpec((1, d), lambda j: (0, 0)),
                  pl.BlockSpec((rows, d), lambda j: (0, 0)),
                  pl.BlockSpec((rows, d), lambda j: (0, 0)),
                  pl.BlockSpec((None, d, tn), lcol), pl.BlockSpec((None, d, tn), lcol_hi),
                  pl.BlockSpec((None, FFN_CONV, tn), lcol), pl.BlockSpec((None, FFN_CONV, tn), lcol_hi),
                  pl.BlockSpec((None, 1, tn), lcol), pl.BlockSpec((None, 1, tn), lcol_hi),
                  pl.BlockSpec((keep * bsz, tn), col), pl.BlockSpec((keep * bsz, tn), col_hi)],
        out_specs=[pl.BlockSpec((rows, tn), col), pl.BlockSpec((keep * bsz, tn), col),
                   pl.BlockSpec((keep * bsz, tn), col)],
        out_shape=[jax.ShapeDtypeStruct((rows, dff), BF16),
                   jax.ShapeDtypeStruct((keep * bsz, dff), F32),
                   jax.ShapeDtypeStruct((keep * bsz, dff), F32)],
        scratch_shapes=[pltpu.VMEM((rows, d), BF16)],
        compiler_params=_params(1, blocks, scratch),
        name="ffn_up_sample",
    )(x, gain.reshape(1, d), shift, scale, w_up, w_up, conv_w, conv_w, conv_b, conv_b, hist, hist)
    return act, jnp.concatenate([new_g, new_v], axis=-1)


def _gdn_in_kernel(*refs, n_qk_tiles, n_conv_tiles, bsz):
    if bsz is None:
        (x_ref, gain_ref, shift_ref, scale_ref, w_ref, wba_ref, alog_ref, dtb_ref, cw_ref,
         o_ref, bg_ref, raw_ref, h_s, carry_s) = refs
    else:
        (x_ref, gain_ref, shift_ref, scale_ref, w_ref, wba_ref, alog_ref, dtb_ref, cw_ref, hist_ref,
         o_ref, bg_ref, raw_ref, h_s) = refs
    i = pl.program_id(1)
    j = pl.program_id(2)
    tm = h_s.shape[0]
    width = GDN_CONV

    @pl.when(j == 0)
    def _():
        h = _norm_mod(x_ref[0], gain_ref[...], shift_ref[0], scale_ref[0]).astype(BF16)
        h_s[...] = h
        y = _dot_nt(h, wba_ref[...].astype(BF16))
        lane = lax.broadcasted_iota(jnp.int32, y.shape, 1) % LANES
        beta = 1.0 / (1.0 + jnp.exp(-y))
        a = y + dtb_ref[...]
        softplus = jnp.maximum(a, 0.0) + jnp.log1p(jnp.exp(-jnp.abs(a)))
        g = -jnp.exp(alog_ref[...]) * softplus
        bg_ref[0] = jnp.where(lane < GDN_VH_PER_GROUP, beta, jnp.where(lane < 2 * GDN_VH_PER_GROUP, g, 0.0))

    sub = min(GDN_IN_SUB_ROWS, tm)
    row8 = lax.broadcasted_iota(jnp.int32, (SUBLANES, 1), 0)
    assert width == 4

    def shifted(x, prev, k):
        top = jnp.where(row8 < k, pltpu.roll(prev, k, axis=0), pltpu.roll(x[0:SUBLANES], k, axis=0))
        return jnp.concatenate([top, pltpu.roll(x, k, axis=0)[SUBLANES:]], axis=0)

    def conv_tile(l2_scale):
        w_bf = w_ref[...].astype(BF16)
        jc = jnp.minimum(j, n_conv_tiles - 1)
        if bsz is None:
            @pl.when(i == 0)
            def _():
                carry_s[jc] = jnp.zeros(carry_s.shape[1:], F32)
            prev_u, prev_pair = carry_s[jc, 0], carry_s[jc, 1]
        for m in range(tm // sub):
            rs = slice(m * sub, (m + 1) * sub)
            u = _dot_nt(h_s[rs, :], w_bf)
            if bsz is None:
                u1 = shifted(u, prev_u, 1)
                pair = u * cw_ref[1:2, :] + u1 * cw_ref[0:1, :]
                y = u * cw_ref[3:4, :] + u1 * cw_ref[2:3, :] + shifted(pair, prev_pair, 2)
                prev_u, prev_pair = u[sub - SUBLANES:sub], pair[sub - SUBLANES:sub]
            else:
                y = u * cw_ref[width - 1:width, :]
                ext = jnp.concatenate([hist_ref[...], u], axis=0)
                for k in range(1, width):
                    wk = cw_ref[width - 1 - k:width - k, :]
                    y = y + ext[(width - 1 - k) * bsz:(width - 1 - k) * bsz + sub] * wk
                raw_ref[...] = ext[sub:sub + (width - 1) * bsz]
            y = _silu(y)
            if l2_scale is not None:
                heads = [y[:, hd * LANES:(hd + 1) * LANES] for hd in range(y.shape[1] // LANES)]
                y = jnp.concatenate(
                    [yh * (lax.rsqrt(jnp.sum(yh * yh, axis=-1, keepdims=True) + NORM_EPS) * l2_scale)
                     for yh in heads], axis=1)
            o_ref[0, rs, :] = y.astype(o_ref.dtype)
        if bsz is None:
            carry_s[jc, 0] = prev_u
            carry_s[jc, 1] = prev_pair
            raw_ref[0, 0] = prev_u

    @pl.when(j < n_qk_tiles)
    def _():
        conv_tile(jnp.where(j < n_qk_tiles // 2, GDN_DK ** -0.5, 1.0).astype(F32))

    @pl.when((j >= n_qk_tiles) & (j < n_conv_tiles))
    def _():
        conv_tile(None)

    @pl.when(j >= n_conv_tiles)
    def _():
        w_bf = w_ref[...].astype(BF16)
        for m in range(tm // sub):
            rs = slice(m * sub, (m + 1) * sub)
            o_ref[0, rs, :] = _silu(_dot_nt(h_s[rs, :], w_bf)).astype(o_ref.dtype)


def _group_lane_layout(vec_b, vec_a):
    lead = vec_b.shape[:-1]
    vb = vec_b.reshape(*lead, GDN_HEAD_GROUPS, GDN_VH_PER_GROUP)
    va = vec_a.reshape(*lead, GDN_HEAD_GROUPS, GDN_VH_PER_GROUP)
    pad = jnp.zeros((*lead, GDN_HEAD_GROUPS, LANES - 2 * GDN_VH_PER_GROUP), vec_b.dtype)
    return jnp.concatenate([vb, va, pad], axis=-1).reshape(*lead, GDN_HEAD_GROUPS * LANES)


def gdn_in_projection(x, gain, shift, scale, w_in, conv_w, a_log, dt_bias, tm, conv_hist=None, tn=512):
    b, l, d = x.shape
    n_main = GDN_CONV_DIM + GDN_VAL_DIM
    w_t = jnp.transpose(w_in)
    wba = _group_lane_layout(w_in[:, n_main:n_main + GDN_V_HEADS], w_in[:, n_main + GDN_V_HEADS:])
    wba_t = jnp.transpose(wba)
    zeros = jnp.zeros((1, GDN_V_HEADS), F32)
    alog = _group_lane_layout(zeros, a_log.reshape(1, -1).astype(F32))
    dtb = _group_lane_layout(zeros, dt_bias.reshape(1, -1).astype(F32))
    nbg = wba_t.shape[0]
    n_conv_tiles = GDN_CONV_DIM // tn
    n_qk_tiles = 2 * GDN_KEY_DIM // tn
    conv_col = lambda b_, i, j: (0, jnp.minimum(j, n_conv_tiles - 1))
    in_specs = [pl.BlockSpec((1, tm, d), lambda b_, i, j: (b_, i, 0)),
                pl.BlockSpec((1, d), lambda b_, i, j: (0, 0)),
                _mod_specs(shift, tm), _mod_specs(scale, tm),
                pl.BlockSpec((tn, d), lambda b_, i, j: (j, 0)),
                pl.BlockSpec((nbg, d), lambda b_, i, j: (0, 0)),
                pl.BlockSpec((1, nbg), lambda b_, i, j: (0, 0)),
                pl.BlockSpec((1, nbg), lambda b_, i, j: (0, 0)),
                pl.BlockSpec((GDN_CONV, tn), conv_col)]
    args = [x, gain.reshape(1, d), shift, scale, w_t, wba_t, alog, dtb, conv_w]
    scratch_shapes = [pltpu.VMEM((tm, d), BF16)]
    blocks = (_nbytes((tm, d), F32) + _nbytes((tn, d), F32) + _nbytes((tm, tn), BF16)
              + _nbytes((nbg, d), F32) + _nbytes((tm, nbg), F32) + 2 * _nbytes((1, d), F32))
    scratch = _nbytes((tm, d), BF16) + _nbytes((tn, d), BF16) + 6 * _nbytes((tm, nbg), F32)
    if conv_hist is None:
        bsz = None
        raw_spec = pl.BlockSpec((1, 1, SUBLANES, tn), lambda b_, i, j: (b_, i, 0, jnp.minimum(j, n_conv_tiles - 1)))
        raw_shape = jax.ShapeDtypeStruct((b, l // tm, SUBLANES, GDN_CONV_DIM), F32)
        scratch_shapes.append(pltpu.VMEM((n_conv_tiles, 2, SUBLANES, tn), F32))
    else:
        assert b == 1 and l == tm <= GDN_IN_SUB_ROWS
        hist_rows = conv_hist.shape[0]
        bsz = hist_rows // (GDN_CONV - 1)
        args.append(conv_hist)
        in_specs.append(pl.BlockSpec((hist_rows, tn), conv_col))
        raw_spec = pl.BlockSpec((hist_rows, tn), conv_col)
        raw_shape = jax.ShapeDtypeStruct((hist_rows, GDN_CONV_DIM), F32)
        blocks += 2 * _nbytes((hist_rows, tn), F32)
    return pl.pallas_call(
        functools.partial(_gdn_in_kernel, n_qk_tiles=n_qk_tiles, n_conv_tiles=n_conv_tiles, bsz=bsz),
        grid=(b, l // tm, n_main // tn),
        in_specs=in_specs,
        out_specs=[pl.BlockSpec((1, tm, tn), lambda b_, i, j: (b_, i, j)),
                   pl.BlockSpec((1, tm, nbg), lambda b_, i, j: (b_, i, 0)),
                   raw_spec],
        out_shape=[jax.ShapeDtypeStruct((b, l, n_main), BF16),
                   jax.ShapeDtypeStruct((b, l, nbg), F32),
                   raw_shape],
        scratch_shapes=scratch_shapes,
        compiler_params=_params(3, blocks, scratch),
        name="gdn_in_projection",
    )(*args)


def _gdn_core_kernel(q_ref, k_ref, v_ref, zg_ref, bg_ref, onorm_ref, s0_ref, o_ref, s1_ref,
                     s_s, gc_s, tdec_s, mo_s, *, seq_rows):
    r = pl.program_id(2)
    n_r = pl.num_programs(2)
    rows = q_ref.shape[1]
    chunk = GDN_CHUNK
    vpg = GDN_VH_PER_GROUP
    rep = GDN_V_HEADS // GDN_K_HEADS
    n_kh = GDN_KH_PER_GROUP
    seg = chunk if seq_rows is None else seq_rows
    n_seq = chunk // seg

    def load_states():
        for s in range(n_seq):
            for kh in range(n_kh):
                s_s[s * n_kh + kh] = jnp.concatenate([s0_ref[s, kh * rep + e] for e in range(rep)], axis=1)

    if seq_rows is None:
        pl.when(r == 0)(load_states)
    else:
        load_states()

    def seg_last(x):
        if n_seq == 1:
            return x[chunk - 1:chunk, :]
        blocks = x.reshape(n_seq, seg, x.shape[1])
        return jnp.broadcast_to(blocks[:, seg - 1:seg, :], blocks.shape).reshape(x.shape)

    n_chunks = rows // chunk
    pw = GDN_PACK * chunk
    n_packs = vpg // GDN_PACK
    ri = lax.broadcasted_iota(jnp.int32, (chunk, pw), 0)
    li = lax.broadcasted_iota(jnp.int32, (chunk, pw), 1) % chunk
    seg_shift = int(math.log2(seg))
    same_seq = (ri >> seg_shift) == (li >> seg_shift)
    tril_p = (ri >= li) & same_seq
    strict_p = (ri > li) & same_seq
    eye_p = (ri == li).astype(F32)
    n_levels = seg_shift
    off_masks = [((ri >> (lvl + 1)) == (li >> (lvl + 1))) & ((ri >> lvl) != (li >> lvl))
                 for lvl in range(n_levels)]
    bd_rows = lax.broadcasted_iota(jnp.int32, (pw, pw), 0) // chunk
    bd_cols = lax.broadcasted_iota(jnp.int32, (pw, pw), 1) // chunk
    bd_mask = (bd_rows == bd_cols).astype(BF16)
    row_c = lax.broadcasted_iota(jnp.int32, (chunk, LANES), 0)
    lane_lo = lax.broadcasted_iota(jnp.int32, (chunk, LANES), 1) < chunk

    def block_diag(xp):
        return jnp.concatenate([xp.astype(BF16)] * GDN_PACK, axis=0) * bd_mask

    def pack_cols(arr, first_col):
        tiles = []
        for t in range(pw // LANES):
            even = jnp.broadcast_to(arr[:, first_col + 2 * t:first_col + 2 * t + 1], (chunk, LANES))
            odd = jnp.broadcast_to(arr[:, first_col + 2 * t + 1:first_col + 2 * t + 2], (chunk, LANES))
            tiles.append(jnp.where(lane_lo, even, odd))
        return jnp.concatenate(tiles, axis=1)

    for c0 in range(0, n_chunks, GDN_PHASE_A_CHUNKS):
        group = [(c, p) for c in range(c0, min(c0 + GDN_PHASE_A_CHUNKS, n_chunks)) for p in range(n_packs)]
        a_list, qkd, brow, dec = {}, {}, {}, {}
        for c in range(c0, min(c0 + GDN_PHASE_A_CHUNKS, n_chunks)):
            rs = slice(c * chunk, (c + 1) * chunk)
            bg = bg_ref[0, rs, :]
            cum = bg
            shift = 1
            while shift < seg:
                cum = cum + jnp.where((row_c & (seg - 1)) >= shift, pltpu.roll(cum, shift, axis=0), 0.0)
                shift *= 2
            gc_s[rs, :] = cum
            gq = []
            for kh in range(GDN_KH_PER_GROUP):
                ksl = slice(kh * LANES, (kh + 1) * LANES)
                k = k_ref[0, rs, ksl]
                kq = jnp.concatenate([k, q_ref[0, rs, ksl]], axis=0)
                kk = jnp.concatenate([k, k], axis=0)
                gq.append(_dot_nt(kq, kk))
            for p in range(n_packs):
                khs = [(p * GDN_PACK + 2 * t) // rep for t in range(pw // LANES)]
                gram = jnp.concatenate([gq[kh][:chunk] for kh in khs], axis=1)
                qk = jnp.concatenate([gq[kh][chunk:] for kh in khs], axis=1)
                beta_p = pack_cols(bg, p * GDN_PACK)
                gcol_p = pack_cols(cum, vpg + p * GDN_PACK)
                grow_p = jnp.sum(gcol_p * eye_p, axis=0, keepdims=True)
                decay = jnp.exp(jnp.where(tril_p, gcol_p - grow_p, NEG_INF))
                a_list[(c, p)] = jnp.where(strict_p, gram * beta_p * decay, 0.0)
                qkd[(c, p)] = (qk * decay).astype(BF16)
                brow[(c, p)] = jnp.sum(beta_p * eye_p, axis=0, keepdims=True)
                dec[(c, p)] = jnp.exp(seg_last(gcol_p) - gcol_p)
        inv = {cp: eye_p - jnp.where(off_masks[0], a_list[cp], 0.0) for cp in group}
        for lvl in range(1, n_levels):
            w = {cp: _dot(jnp.where(off_masks[lvl], a_list[cp], 0.0).astype(BF16), block_diag(inv[cp]))
                 for cp in group}
            inv = {cp: inv[cp] - _dot(inv[cp].astype(BF16), block_diag(w[cp])) for cp in group}
        for cp in group:
            t_beta = inv[cp] * brow[cp]
            tdec_s[cp[0], cp[1]] = block_diag(dec[cp] * t_beta)
            mo_s[cp[0], cp[1]] = block_diag(_dot(qkd[cp], block_diag(t_beta)))

    def chunk_body(c, carry):
        rs = pl.ds(pl.multiple_of(c * chunk, chunk), chunk)
        cum = gc_s[rs, :]
        eg, g_last, ks, qs = [], [], [], []
        for kh in range(n_kh):
            ksl = slice(kh * LANES, (kh + 1) * LANES)
            kq = jnp.concatenate([k_ref[0, rs, ksl], q_ref[0, rs, ksl]], axis=0)
            per_seq = [_dot(kq, s_s[s * n_kh + kh].astype(BF16)) for s in range(n_seq)]
            kq_s = jnp.concatenate([per_seq[s][half * chunk + s * seg:half * chunk + (s + 1) * seg]
                                    for half in range(2) for s in range(n_seq)], axis=0)
            for e in range(rep):
                hv = kh * rep + e
                gcol = cum[:, vpg + hv:vpg + hv + 1]
                g_last.append([gcol[(s + 1) * seg - 1:(s + 1) * seg, :] for s in range(n_seq)])
                eg.append(jnp.exp(gcol))
                ks.append(kq_s[:chunk, e * LANES:(e + 1) * LANES] * eg[hv])
                qs.append(kq_s[chunk:, e * LANES:(e + 1) * LANES] * eg[hv])
        v_dec, o_intra = [], []
        for p in range(n_packs):
            heads = range(p * GDN_PACK, (p + 1) * GDN_PACK)
            rhs = jnp.concatenate([v_ref[0, rs, hv * LANES:(hv + 1) * LANES].astype(F32) - ks[hv]
                                   for hv in heads], axis=0).astype(BF16)
            vd = _dot(tdec_s[c, p], rhs)
            oi = _dot(mo_s[c, p], rhs)
            for j in range(GDN_PACK):
                v_dec.append(vd[j * chunk:(j + 1) * chunk])
                o_intra.append(oi[j * chunk:(j + 1) * chunk])
        for hv in range(vpg):
            vsl = slice(hv * LANES, (hv + 1) * LANES)
            o = qs[hv] + o_intra[hv]
            on = o * lax.rsqrt(jnp.mean(o * o, axis=-1, keepdims=True) + NORM_EPS) * onorm_ref[...]
            o_ref[0, rs, vsl] = (on * zg_ref[0, rs, vsl].astype(F32)).astype(o_ref.dtype)
        lane_v = lax.broadcasted_iota(jnp.int32, (1, rep * LANES), 1) // LANES
        row_seq = lax.broadcasted_iota(jnp.int32, (chunk, 1), 0) >> seg_shift
        for kh in range(n_kh):
            heads = range(kh * rep, (kh + 1) * rep)
            k_bf = k_ref[0, rs, kh * LANES:(kh + 1) * LANES]
            vd_pair = jnp.concatenate([v_dec[hv] for hv in heads], axis=1)
            for s in range(n_seq):
                vd_s = vd_pair if n_seq == 1 else jnp.where(row_seq == s, vd_pair, 0.0)
                ds = _dot_tn(k_bf, vd_s.astype(BF16))
                keep = jnp.exp(g_last[kh * rep][s])
                for e in range(1, rep):
                    keep = jnp.where(lane_v == e, jnp.exp(g_last[kh * rep + e][s]), keep)
                s_s[s * n_kh + kh] = s_s[s * n_kh + kh] * keep + ds
        return carry

    lax.fori_loop(0, n_chunks, chunk_body, 0)

    def store_states():
        for s in range(n_seq):
            for hv in range(vpg):
                s1_ref[s, hv] = s_s[s * n_kh + hv // rep][:, (hv % rep) * LANES:(hv % rep + 1) * LANES]

    if seq_rows is None:
        pl.when(r == n_r - 1)(store_states)
    else:
        store_states()


def gdn_core(qkvz, bg, s0, out_norm, rows, seq_rows=None):
    b, l, _ = qkvz.shape
    n_state = 1 if seq_rows is None else GDN_CHUNK // seq_rows
    if seq_rows is None:
        state_idx = lambda b_, g, r: (b_, g, 0, 0)
    else:
        assert b == 1 and rows == GDN_CHUNK
        state_idx = lambda b_, g, r: (r, g, 0, 0)
    hg = GDN_HEAD_GROUPS
    qw = GDN_KEY_DIM // hg
    vw = GDN_VAL_DIM // hg
    kq = GDN_KEY_DIM // qw
    kv = 2 * GDN_KEY_DIM // vw
    kz = GDN_CONV_DIM // vw
    vpg = GDN_VH_PER_GROUP
    blocks = (2 * _nbytes((rows, qw), BF16) + 3 * _nbytes((rows, vw), BF16) + _nbytes((rows, LANES), F32)
              + 2 * _nbytes((n_state, vpg, GDN_DK, GDN_DV), F32))
    n_chunks = rows // GDN_CHUNK
    n_packs = vpg // GDN_PACK
    pw = GDN_PACK * GDN_CHUNK
    scratch = (_nbytes((n_state, vpg, GDN_DK, GDN_DV), F32) + _nbytes((rows, LANES), F32)
               + 2 * _nbytes((n_chunks, n_packs, pw, pw), BF16) + 6 * _nbytes((rows, vw), F32))
    return pl.pallas_call(
        functools.partial(_gdn_core_kernel, seq_rows=seq_rows),
        grid=(b, hg, l // rows),
        in_specs=[pl.BlockSpec((1, rows, qw), lambda b_, g, r: (b_, r, g)),
                  pl.BlockSpec((1, rows, qw), lambda b_, g, r: (b_, r, kq + g)),
                  pl.BlockSpec((1, rows, vw), lambda b_, g, r: (b_, r, kv + g)),
                  pl.BlockSpec((1, rows, vw), lambda b_, g, r: (b_, r, kz + g)),
                  pl.BlockSpec((1, rows, LANES), lambda b_, g, r: (b_, r, g)),
                  pl.BlockSpec((1, GDN_DV), lambda b_, g, r: (0, 0)),
                  pl.BlockSpec((n_state, vpg, GDN_DK, GDN_DV), state_idx)],
        out_specs=[pl.BlockSpec((1, rows, vw), lambda b_, g, r: (b_, r, g)),
                   pl.BlockSpec((n_state, vpg, GDN_DK, GDN_DV), state_idx)],
        out_shape=[jax.ShapeDtypeStruct((b, l, GDN_VAL_DIM), BF16),
                   jax.ShapeDtypeStruct(s0.shape, F32)],
        scratch_shapes=[pltpu.VMEM((n_state * GDN_KH_PER_GROUP, GDN_DK,
                                    (GDN_V_HEADS // GDN_K_HEADS) * GDN_DV), F32),
                        pltpu.VMEM((rows, LANES), F32),
                        pltpu.VMEM((n_chunks, n_packs, pw, pw), BF16),
                        pltpu.VMEM((n_chunks, n_packs, pw, pw), BF16)],
        compiler_params=_params(3, blocks, scratch),
        name="gdn_core",
    )(qkvz, qkvz, qkvz, qkvz, bg, out_norm.reshape(1, GDN_DV).astype(F32), s0)


def _tile_rows(l, cap):
    t = min(l, cap)
    while l % t:
        t //= 2
    return t


def _to_time_major(a):
    return jnp.transpose(a, (1, 0, 2)).reshape(a.shape[0] * a.shape[1], a.shape[2])


def _from_time_major(a, bsz):
    return jnp.transpose(a.reshape(a.shape[0] // bsz, bsz, a.shape[1]), (1, 0, 2))


def _trunk(x, mod, states, P, sample):
    bsz, l, d = x.shape
    tm = _tile_rows(l, ROW_TILE_CAP)
    out_proj = matmul_gate_residual if sample else matmul_gate_residual_wres
    new = {}
    depth = mod.shape[0]
    for layer in range(depth):
        if sample:
            parts = [mod[layer][None, :, k * d:(k + 1) * d] for k in range(6)]
        else:
            parts = [mod[layer][:, None, k * d:(k + 1) * d] for k in range(6)]
        sh_m, sc_m, g_m, sh_f, sc_f, g_f = parts
        i = layer // 2
        if layer % 2 == 0:
            qkv = qkv_projection(x, P['norm_mix'][layer], sh_m, sc_m, P['w_attn_qkv'][i],
                                 P['attn_q_norm'][i], P['attn_k_norm'][i], tm)
            nq, nkv = N_HEADS * HEAD_DIM, N_KV_HEADS * HEAD_DIM
            if sample:
                o, k_win, v_win = attention_sample(qkv[0], states['win_k'][i], states['win_v'][i],
                                                   P['attn_sinks'][i], P['rel_bias_table'])
                o = o[None]
                w_len = k_win.shape[1]
                new.setdefault('win_k', []).append(k_win.reshape(-1, w_len, N_KV_HEADS, HEAD_DIM))
                new.setdefault('win_v', []).append(v_win.reshape(-1, w_len, N_KV_HEADS, HEAD_DIM))
            else:
                o = attention_prompt(qkv, P['attn_sinks'][i], P['rel_bias_table'])
                keep = min(WINDOW, PAST_LEN)
                new.setdefault('win_k', []).append(
                    qkv[:, l - keep:, nq:nq + nkv].reshape(bsz, keep, N_KV_HEADS, HEAD_DIM))
                new.setdefault('win_v', []).append(
                    qkv[:, l - keep:, nq + nkv:].reshape(bsz, keep, N_KV_HEADS, HEAD_DIM))
            x = matmul_gate_residual(o, P['w_attn_o'], i, x, g_m, _tile_rows(l, 2 * ROW_TILE_CAP), 512)
        else:
            keep = GDN_CONV - 1
            gdn_args = (x, P['norm_mix'][layer], sh_m, sc_m, P['w_gdn_in'][i], P['gdn_conv_w'][i],
                        P['gdn_a_log'][i], P['gdn_dt_bias'][i], tm)
            if sample:
                seqs = states['gdn'].shape[1]
                t = l // seqs
                qkvz, bg, hist1 = gdn_in_projection(
                    *gdn_args, conv_hist=_to_time_major(states['gdn_conv'][i].astype(F32)), tn=2 * 512)
                new.setdefault('gdn_conv', []).append(_from_time_major(hist1, seqs))
                seq_rows = -(-t // SUBLANES) * SUBLANES
                per_seq = lambda a: jnp.pad(_from_time_major(a, seqs), ((0, 0), (0, seq_rows - t), (0, 0))
                                            ).reshape(1, seqs * seq_rows, a.shape[-1])
                o_seq, s1 = gdn_core(per_seq(qkvz[0]), per_seq(bg[0]), states['gdn'][i].astype(F32),
                                     P['gdn_out_norm'][i], GDN_CHUNK, seq_rows=seq_rows)
                o = _to_time_major(o_seq.reshape(seqs, seq_rows, GDN_VAL_DIM)[:, :t])[None]
            else:
                qkvz, bg, raw = gdn_in_projection(*gdn_args)
                s0 = jnp.zeros((bsz, GDN_V_HEADS, GDN_DK, GDN_DV), F32)
                o, s1 = gdn_core(qkvz, bg, s0, P['gdn_out_norm'][i], _tile_rows(l, GDN_ROW_TILE_CAP))
                new.setdefault('gdn_conv', []).append(raw[:, -1, SUBLANES - keep:, :])
            new.setdefault('gdn', []).append(s1)
            x = out_proj(o, P['w_gdn_out'], i, x, g_m, tm, 512)
        if sample:
            seqs = states['ffn_conv'].shape[1]
            act, f1 = ffn_up_sample(x[0], P['norm_ffn'][layer], sh_f[0], sc_f[0], P['w_ffn_up'],
                                    P['ffn_conv_w'], P['ffn_conv_b'], layer,
                                    _to_time_major(states['ffn_conv'][layer].astype(F32)))
            act, f1 = act[None], _from_time_major(f1, seqs)
        else:
            act, f1 = ffn_up_prompt(x, P['norm_ffn'][layer], sh_f, sc_f, P['w_ffn_up'],
                                    P['ffn_conv_w'], P['ffn_conv_b'], layer, tm)
        new.setdefault('ffn_conv', []).append(f1)
        if sample:
            x = matmul_gate_residual(act, P['w_ffn_down'], layer, x, g_f, tm, 512)
        else:
            x = matmul_gate_residual_wres(act, P['w_ffn_down'], layer, x, g_f, _tile_rows(l, 512), 512)
    return x, {k: jnp.stack(v) for k, v in new.items()}


def kernel(x_prompt, x_sample, c_prompt, c_sample, cache_win_k, cache_win_v, state_gdn, state_gdn_conv,
           state_ffn_conv, rel_bias_table, w_ada, b_ada, norm_mix, norm_ffn, w_attn_qkv, attn_q_norm,
           attn_k_norm, attn_sinks, w_attn_o, w_gdn_in, gdn_conv_w, gdn_a_log, gdn_dt_bias, gdn_out_norm,
           w_gdn_out, w_ffn_up, ffn_conv_w, ffn_conv_b, w_ffn_down):
    P = dict(rel_bias_table=rel_bias_table, norm_mix=norm_mix, norm_ffn=norm_ffn, w_attn_qkv=w_attn_qkv,
             attn_q_norm=attn_q_norm, attn_k_norm=attn_k_norm, attn_sinks=attn_sinks, w_attn_o=w_attn_o,
             w_gdn_in=w_gdn_in, gdn_conv_w=gdn_conv_w, gdn_a_log=gdn_a_log, gdn_dt_bias=gdn_dt_bias,
             gdn_out_norm=gdn_out_norm, w_gdn_out=w_gdn_out, w_ffn_up=w_ffn_up, ffn_conv_w=ffn_conv_w,
             ffn_conv_b=ffn_conv_b, w_ffn_down=w_ffn_down)
    bp = x_prompt.shape[0]
    bs, ts, d = x_sample.shape
    n_c = bp + bs
    c_rows = -(-n_c // SUBLANES) * SUBLANES
    c_all = jnp.pad(jnp.concatenate([c_prompt, c_sample], axis=0), ((0, c_rows - n_c), (0, 0)))
    mod = ada_modulation(c_all, w_ada, b_ada)
    mod_prompt = mod[:, :bp]
    mod_sample = jnp.tile(mod[:, bp:n_c], (1, ts, 1))

    y_p, new_p = _trunk(x_prompt, mod_prompt, None, P, sample=False)
    states = dict(win_k=cache_win_k, win_v=cache_win_v, gdn=state_gdn, gdn_conv=state_gdn_conv,
                  ffn_conv=state_ffn_conv)
    y_s, new_s = _trunk(_to_time_major(x_sample)[None], mod_sample, states, P, sample=True)
    y_s = _from_time_major(y_s[0], bs)
    return (y_p, y_s, new_p['win_k'], new_p['win_v'], new_s['win_k'], new_s['win_v'],
            new_p['gdn'], new_s['gdn'], new_p['gdn_conv'], new_s['gdn_conv'],
            new_p['ffn_conv'], new_s['ffn_conv'])
```

```python
import functools
import math

import numpy as np
import jax
import jax.numpy as jnp
from jax import lax
from jax.experimental import pallas as pl
from jax.experimental.pallas import tpu as pltpu

HEAD_DIM = 64
N_HEADS = 32
N_KV_HEADS = 4
ATT_GROUP = N_HEADS // N_KV_HEADS
WINDOW = 128
ATT_BLOCK = 128
ATT_SAMPLE_SEQS = 8
N_BUCKETS = 32
MAX_DISTANCE = 128
NEG_INF = -1e30
PAST_LEN = 16384

GDN_K_HEADS = 16
GDN_V_HEADS = 32
GDN_DK = 128
GDN_DV = 128
GDN_KEY_DIM = GDN_K_HEADS * GDN_DK
GDN_VAL_DIM = GDN_V_HEADS * GDN_DV
GDN_CONV_DIM = 2 * GDN_KEY_DIM + GDN_VAL_DIM
GDN_CONV = 4
GDN_CHUNK = 64
GDN_HEAD_GROUPS = 4
GDN_VH_PER_GROUP = GDN_V_HEADS // GDN_HEAD_GROUPS
GDN_KH_PER_GROUP = GDN_K_HEADS // GDN_HEAD_GROUPS
GDN_PACK = 4
GDN_PHASE_A_CHUNKS = 4

FFN_CONV = 3
NORM_EPS = 1e-6

LANES = 128
SUBLANES = 8
VMEM_CAP_BYTES = 60 * 1024 * 1024
VMEM_SLACK_BYTES = 8 * 1024 * 1024

ROW_TILE_CAP = 1024
GDN_IN_SUB_ROWS = 256
FFN_SUB_ROWS = 256
GDN_ROW_TILE_CAP = 1024

BF16 = jnp.bfloat16
F32 = jnp.float32


def _vmem_limit(block_bytes, scratch_bytes=0):
    est = 2 * int(block_bytes) + int(scratch_bytes) + VMEM_SLACK_BYTES
    return int(min(max(est, 16 * 1024 * 1024), VMEM_CAP_BYTES))


def _params(n_grid, block_bytes, scratch_bytes=0):
    return pltpu.CompilerParams(
        dimension_semantics=("arbitrary",) * n_grid,
        vmem_limit_bytes=_vmem_limit(block_bytes, scratch_bytes))


def _nbytes(shape, dtype):
    return int(np.prod(shape)) * jnp.dtype(dtype).itemsize


def _silu(x):
    return x * (1.0 / (1.0 + jnp.exp2(x * (-math.log2(math.e)))))


def _dot(a, b):
    return jnp.dot(a, b, preferred_element_type=F32)


def _dot_nt(a, b):
    return lax.dot_general(a, b, (((1,), (1,)), ((), ())), preferred_element_type=F32)


def _dot_tn(a, b):
    return lax.dot_general(a, b, (((0,), (0,)), ((), ())), preferred_element_type=F32)


def _norm_mod(x, gain, shift, scale):
    ms = jnp.mean(x * x, axis=-1, keepdims=True)
    y = x * lax.rsqrt(ms + NORM_EPS) * gain
    return y * (1.0 + scale) + shift


def _mod_specs(shift, tm):
    d = shift.shape[-1]
    if shift.shape[1] == 1:
        return pl.BlockSpec((1, 1, d), lambda b, i, j: (b, 0, 0))
    return pl.BlockSpec((1, tm, d), lambda b, i, j: (b, i, 0))


def _ada_kernel(c_ref, w_ref, b_ref, o_ref):
    a = _silu(c_ref[...]).astype(BF16)
    o_ref[0] = _dot(a, w_ref[0].astype(BF16)) + b_ref[0]


def ada_modulation(c_all, w_ada, b_ada, tn=1024):
    rows, d = c_all.shape
    depth, _, n = w_ada.shape
    blocks = _nbytes((rows, d), F32) + _nbytes((d, tn), F32) + _nbytes((rows, tn), F32)
    return pl.pallas_call(
        _ada_kernel,
        grid=(depth, n // tn),
        in_specs=[pl.BlockSpec((rows, d), lambda l, j: (0, 0)),
                  pl.BlockSpec((1, d, tn), lambda l, j: (l, 0, j)),
                  pl.BlockSpec((1, 1, tn), lambda l, j: (l, 0, j))],
        out_specs=pl.BlockSpec((1, rows, tn), lambda l, j: (l, 0, j)),
        out_shape=jax.ShapeDtypeStruct((depth, rows, n), F32),
        compiler_params=_params(2, blocks, _nbytes((d, tn), BF16)),
        name="ada_modulation",
    )(c_all, w_ada, b_ada.reshape(depth, 1, n))


def _qkv_kernel(x_ref, gain_ref, shift_ref, scale_ref, w_ref, hgain_ref, hflag_ref, gmat_ref,
                o_ref, h_s):
    @pl.when(pl.program_id(2) == 0)
    def _():
        h_s[...] = _norm_mod(x_ref[0], gain_ref[...], shift_ref[0], scale_ref[0]).astype(BF16)

    y = _dot(h_s[...], w_ref[...].astype(BF16))
    ms = _dot((y * y).astype(BF16), gmat_ref[...])
    yn = y * lax.rsqrt(ms + NORM_EPS) * hgain_ref[...]
    o_ref[0] = jnp.where(hflag_ref[...] > 0.0, yn, y)


def qkv_projection(x, gain, shift, scale, w, q_gain, k_gain, tm, tn=512):
    b, l, d = x.shape
    n = w.shape[1]
    nq, nk = N_HEADS * HEAD_DIM, N_KV_HEADS * HEAD_DIM
    hgain = jnp.concatenate([jnp.tile(q_gain, N_HEADS), jnp.tile(k_gain, N_KV_HEADS),
                             jnp.ones((nk,), F32)]).reshape(1, n)
    hflag = jnp.concatenate([jnp.ones((nq + nk,), F32), jnp.zeros((nk,), F32)]).reshape(1, n)
    gidx = np.arange(tn) // HEAD_DIM
    gmat = jnp.asarray((gidx[:, None] == gidx[None, :]).astype(np.float32) / HEAD_DIM, BF16)
    blocks = (_nbytes((tm, d), F32) + _nbytes((d, tn), F32) + _nbytes((tm, tn), F32)
              + 2 * _nbytes((shift.shape[1] == 1 and 1 or tm, d), F32) + _nbytes((tn, tn), BF16))
    scratch = _nbytes((tm, d), BF16) + _nbytes((d, tn), BF16) + 4 * _nbytes((tm, tn), F32)
    return pl.pallas_call(
        _qkv_kernel,
        grid=(b, l // tm, n // tn),
        in_specs=[pl.BlockSpec((1, tm, d), lambda b_, i, j: (b_, i, 0)),
                  pl.BlockSpec((1, d), lambda b_, i, j: (0, 0)),
                  _mod_specs(shift, tm), _mod_specs(scale, tm),
                  pl.BlockSpec((d, tn), lambda b_, i, j: (0, j)),
                  pl.BlockSpec((1, tn), lambda b_, i, j: (0, j)),
                  pl.BlockSpec((1, tn), lambda b_, i, j: (0, j)),
                  pl.BlockSpec((tn, tn), lambda b_, i, j: (0, 0))],
        out_specs=pl.BlockSpec((1, tm, tn), lambda b_, i, j: (b_, i, j)),
        out_shape=jax.ShapeDtypeStruct((b, l, n), F32),
        scratch_shapes=[pltpu.VMEM((tm, d), BF16)],
        compiler_params=_params(3, blocks, scratch),
        name="qkv_projection",
    )(x, gain.reshape(1, d), shift, scale, w, hgain, hflag, gmat)


def _attn_prompt_kernel(sink_ref, q_ref, cur_ref, prev_ref, bprev_ref, bcur_ref, o_ref, kz_s, vz_s):
    n = pl.program_id(1)
    blk = ATT_BLOCK
    lane = lax.broadcasted_iota(jnp.int32, (2 * blk, LANES), 1)
    lo_half = lane < HEAD_DIM
    neg_prev = jnp.where(n == 0, NEG_INF, 0.0).astype(F32)

    kv_width = N_KV_HEADS * HEAD_DIM
    for pair in range(N_KV_HEADS // 2):
        for part, dst in ((0, kz_s), (1, vz_s)):
            col = part * kv_width + pair * LANES
            both = jnp.concatenate([prev_ref[0, :, col:col + LANES], cur_ref[0, :, col:col + LANES]], axis=0)
            swapped = pltpu.roll(both, HEAD_DIM, axis=1)
            zero = jnp.zeros_like(both)
            c0, c1 = 2 * pair, 2 * pair + 1
            dst[2 * c0 + 0] = jnp.where(lo_half, both, zero).astype(BF16)
            dst[2 * c0 + 1] = jnp.where(lo_half, zero, swapped).astype(BF16)
            dst[2 * c1 + 0] = jnp.where(lo_half, swapped, zero).astype(BF16)
            dst[2 * c1 + 1] = jnp.where(lo_half, zero, both).astype(BF16)

    scale = HEAD_DIM ** -0.5 * math.log2(math.e)
    for p in range(N_HEADS // 2):
        c = (2 * p) // ATT_GROUP
        qp = (q_ref[0, :, p * LANES:(p + 1) * LANES] * scale).astype(BF16)
        o_pair = None
        for a in range(2):
            h = 2 * p + a
            sink = sink_ref[h]
            s = _dot_nt(qp, kz_s[2 * c + a])
            s_prev = s[:, :blk] + bprev_ref[h] + neg_prev
            s_cur = s[:, blk:] + bcur_ref[h]
            m = jnp.maximum(jnp.max(jnp.maximum(s_prev, s_cur), axis=-1, keepdims=True), sink)
            e_prev = jnp.exp2(s_prev - m)
            e_cur = jnp.exp2(s_cur - m)
            den = jnp.sum(e_prev + e_cur, axis=-1, keepdims=True) + jnp.exp2(sink - m)
            pm = jnp.concatenate([e_prev, e_cur], axis=1).astype(BF16)
            o_a = _dot(pm, vz_s[2 * c + a]) * (1.0 / den)
            o_pair = o_a if o_pair is None else o_pair + o_a
        o_ref[0, :, p * LANES:(p + 1) * LANES] = o_pair.astype(o_ref.dtype)


def _t5_bucket_np(dist):
    max_exact = N_BUCKETS // 2
    d = np.maximum(dist, 0)
    df = np.maximum(d, 1).astype(np.float32)
    large = max_exact + (np.log(df / np.float32(max_exact)) / np.float32(math.log(MAX_DISTANCE / max_exact))
                         * np.float32(N_BUCKETS - max_exact)).astype(np.int32)
    large = np.minimum(large, N_BUCKETS - 1)
    return np.where(d < max_exact, d, large)


def _bias_from_dist(dist, in_band, rel_table):
    onehot = (_t5_bucket_np(dist)[..., None] == np.arange(N_BUCKETS)).astype(np.float32)
    tab = jnp.einsum('qsb,bh->hqs', jnp.asarray(onehot), rel_table.astype(F32), precision=lax.Precision.HIGHEST)
    return jnp.where(jnp.asarray(in_band)[None], tab, NEG_INF)


def attention_prompt(qkv, sinks, rel_table):
    b, l, n = qkv.shape
    blk = ATT_BLOCK
    nq = N_HEADS * HEAD_DIM
    kvw = 2 * N_KV_HEADS * HEAD_DIM
    kv_blk = nq // kvw
    qi = np.arange(blk)[:, None]
    sj = np.arange(blk)[None, :]
    d_prev = qi + blk - sj
    d_cur = qi - sj
    log2e = math.log2(math.e)
    bias_prev = _bias_from_dist(d_prev, (d_prev >= 0) & (d_prev <= WINDOW), rel_table) * log2e
    bias_cur = _bias_from_dist(d_cur, (d_cur >= 0) & (d_cur <= WINDOW), rel_table) * log2e
    blocks = (_nbytes((blk, nq), F32) + 2 * _nbytes((blk, kvw), F32) + _nbytes((blk, nq), BF16))
    scratch = 2 * _nbytes((2 * N_KV_HEADS, 2 * blk, LANES), BF16) + 4 * _nbytes((N_HEADS, blk, blk), F32)
    return pl.pallas_call(
        _attn_prompt_kernel,
        grid=(b, l // blk),
        in_specs=[pl.BlockSpec(memory_space=pltpu.SMEM),
                  pl.BlockSpec((1, blk, nq), lambda b_, i: (b_, i, 0)),
                  pl.BlockSpec((1, blk, kvw), lambda b_, i: (b_, i, kv_blk)),
                  pl.BlockSpec((1, blk, kvw), lambda b_, i: (b_, jnp.maximum(i - 1, 0), kv_blk)),
                  pl.BlockSpec((N_HEADS, blk, blk), lambda b_, i: (0, 0, 0)),
                  pl.BlockSpec((N_HEADS, blk, blk), lambda b_, i: (0, 0, 0))],
        out_specs=pl.BlockSpec((1, blk, nq), lambda b_, i: (b_, i, 0)),
        out_shape=jax.ShapeDtypeStruct((b, l, nq), BF16),
        scratch_shapes=[pltpu.VMEM((2 * N_KV_HEADS, 2 * blk, LANES), BF16),
                        pltpu.VMEM((2 * N_KV_HEADS, 2 * blk, LANES), BF16)],
        compiler_params=_params(2, blocks, scratch),
        name="attention_prompt",
    )(sinks.astype(F32) * log2e, qkv, qkv, qkv, bias_prev, bias_cur)


def _attn_sample_kernel(q_ref, k_ref, v_ref, bias_ref, sink_ref, o_ref):
    pairs = [(b, c) for b in range(q_ref.shape[0]) for c in range(N_KV_HEADS)]
    head = lambda ref, b, c: ref[b][:, c * HEAD_DIM:(c + 1) * HEAD_DIM].astype(BF16)
    s = {(b, c): _dot_nt(q_ref[b, c].astype(BF16), head(k_ref, b, c)) + bias_ref[c] for b, c in pairs}
    e, den = {}, {}
    for b, c in pairs:
        sink = sink_ref[c]
        m = jnp.maximum(jnp.max(s[(b, c)], axis=-1, keepdims=True), sink)
        e[(b, c)] = jnp.exp(s[(b, c)] - m)
        den[(b, c)] = jnp.sum(e[(b, c)], axis=-1, keepdims=True) + jnp.exp(sink - m)
    for b, c in pairs:
        o_ref[b, c] = _dot(e[(b, c)].astype(BF16), head(v_ref, b, c)) * (1.0 / den[(b, c)])


def attention_sample(qkv, cache_k, cache_v, sinks, rel_table):
    bsz, w = cache_k.shape[0], cache_k.shape[1]
    t = qkv.shape[0] // bsz
    nq, nkv = N_HEADS * HEAD_DIM, N_KV_HEADS * HEAD_DIM
    keys = w + t
    keys_pad = -(-keys // 16) * 16
    rows = ATT_GROUP * t
    q = qkv[:, :nq].reshape(t, bsz, N_KV_HEADS, ATT_GROUP, HEAD_DIM) * HEAD_DIM ** -0.5
    q = jnp.transpose(q, (1, 2, 3, 0, 4)).reshape(bsz, N_KV_HEADS, rows, HEAD_DIM)
    new_kv = jnp.transpose(qkv[:, nq:].reshape(t, bsz, 2 * nkv), (1, 0, 2))
    k_all = jnp.concatenate([cache_k.reshape(bsz, w, nkv), new_kv[:, :, :nkv]], axis=1)
    v_all = jnp.concatenate([cache_v.reshape(bsz, w, nkv), new_kv[:, :, nkv:]], axis=1)
    pad = ((0, 0), (0, keys_pad - keys), (0, 0))
    k_pad, v_pad = jnp.pad(k_all, pad), jnp.pad(v_all, pad)
    tq = np.arange(t)[:, None]
    sk = np.arange(keys_pad)[None, :]
    dist = tq + w - sk
    in_band = (dist >= 0) & (dist <= WINDOW) & (sk < keys)
    bias_t = _bias_from_dist(dist, in_band, rel_table)
    bias = bias_t.reshape(N_KV_HEADS, rows, keys_pad)
    sink_rows = jnp.repeat(sinks.astype(F32), t).reshape(N_KV_HEADS, rows, 1)
    sb = math.gcd(bsz, ATT_SAMPLE_SEQS)
    blocks = sb * (_nbytes((N_KV_HEADS, rows, nkv), F32) * 2 + 2 * _nbytes((keys_pad, nkv), F32))
    o = pl.pallas_call(
        _attn_sample_kernel,
        grid=(bsz // sb,),
        in_specs=[pl.BlockSpec((sb, N_KV_HEADS, rows, HEAD_DIM), lambda b_: (b_, 0, 0, 0)),
                  pl.BlockSpec((sb, keys_pad, nkv), lambda b_: (b_, 0, 0)),
                  pl.BlockSpec((sb, keys_pad, nkv), lambda b_: (b_, 0, 0)),
                  pl.BlockSpec((N_KV_HEADS, rows, keys_pad), lambda b_: (0, 0, 0)),
                  pl.BlockSpec((N_KV_HEADS, rows, 1), lambda b_: (0, 0, 0))],
        out_specs=pl.BlockSpec((sb, N_KV_HEADS, rows, HEAD_DIM), lambda b_: (b_, 0, 0, 0)),
        out_shape=jax.ShapeDtypeStruct((bsz, N_KV_HEADS, rows, HEAD_DIM), F32),
        compiler_params=_params(1, blocks),
        name="attention_sample",
    )(q, k_pad, v_pad, bias, sink_rows)
    o = o.reshape(bsz, N_KV_HEADS, ATT_GROUP, t, HEAD_DIM)
    o = jnp.transpose(o, (3, 0, 1, 2, 4)).reshape(t * bsz, nq)
    return o.astype(BF16), k_all[:, t:], v_all[:, t:]


def _mm_res_kernel(a_ref, w_ref, x_ref, g_ref, o_ref):
    y = _dot(a_ref[0], w_ref[...].astype(BF16))
    o_ref[0] = x_ref[0] + g_ref[0] * y


def _mm_res_wres_kernel(a_ref, w_ref, x_ref, g_ref, o_ref, wbf_s):
    @pl.when((pl.program_id(1) == 0) & (pl.program_id(2) == 0))
    def _():
        wbf_s[...] = w_ref[...].astype(BF16)

    o_ref[0] = x_ref[0] + g_ref[0] * _dot(a_ref[0], wbf_s[...])


def matmul_gate_residual_wres(a, w_stack, layer, x, gate, tm, tn):
    b, l, k = a.shape
    n = w_stack.shape[2]
    if gate.shape[1] == 1:
        g_spec = pl.BlockSpec((1, 1, tn), lambda j, b_, i: (b_, 0, j))
    else:
        g_spec = pl.BlockSpec((1, tm, tn), lambda j, b_, i: (b_, i, j))
    blocks = _nbytes((tm, k), a.dtype) + _nbytes((k, tn), F32) + 3 * _nbytes((tm, tn), F32)
    scratch = _nbytes((k, tn), BF16) + _nbytes((tm, tn), F32)
    return pl.pallas_call(
        _mm_res_wres_kernel,
        grid=(n // tn, b, l // tm),
        in_specs=[pl.BlockSpec((1, tm, k), lambda j, b_, i: (b_, i, 0)),
                  pl.BlockSpec((None, k, tn), lambda j, b_, i: (layer, 0, j)),
                  pl.BlockSpec((1, tm, tn), lambda j, b_, i: (b_, i, j)),
                  g_spec],
        out_specs=pl.BlockSpec((1, tm, tn), lambda j, b_, i: (b_, i, j)),
        out_shape=jax.ShapeDtypeStruct((b, l, n), F32),
        scratch_shapes=[pltpu.VMEM((k, tn), BF16)],
        compiler_params=_params(3, blocks, scratch),
        name="matmul_gate_residual_wres",
    )(a, w_stack, x, gate)


def matmul_gate_residual(a, w_stack, layer, x, gate, tm, tn):
    b, l, k = a.shape
    n = w_stack.shape[2]
    grows = 1 if gate.shape[1] == 1 else tm
    if gate.shape[1] == 1:
        g_spec = pl.BlockSpec((1, 1, tn), lambda b_, i, j: (b_, 0, j))
    else:
        g_spec = pl.BlockSpec((1, tm, tn), lambda b_, i, j: (b_, i, j))
    blocks = (_nbytes((tm, k), a.dtype) + _nbytes((k, tn), F32) + 2 * _nbytes((tm, tn), F32)
              + _nbytes((grows, tn), F32))
    scratch = _nbytes((k, tn), BF16) + _nbytes((tm, tn), F32)
    return pl.pallas_call(
        _mm_res_kernel,
        grid=(b, l // tm, n // tn),
        in_specs=[pl.BlockSpec((1, tm, k), lambda b_, i, j: (b_, i, 0)),
                  pl.BlockSpec((None, k, tn), lambda b_, i, j: (layer, 0, j)),
                  pl.BlockSpec((1, tm, tn), lambda b_, i, j: (b_, i, j)),
                  g_spec],
        out_specs=pl.BlockSpec((1, tm, tn), lambda b_, i, j: (b_, i, j)),
        out_shape=jax.ShapeDtypeStruct((b, l, n), F32),
        compiler_params=_params(3, blocks, scratch),
        name="matmul_gate_residual",
    )(a, w_stack, x, gate)


def _ffn_conv_rows(u, prev1, prev2, cw_ref, cb_ref):
    return u * cw_ref[2:3, :] + prev1 * cw_ref[1:2, :] + prev2 * cw_ref[0:1, :] + cb_ref[...]


def _ffn_up_prompt_kernel(x_ref, gain_ref, shift_ref, scale_ref, wg_ref, wv_ref, cwg_ref, cwv_ref,
                          cbg_ref, cbv_ref, o_ref, tail_ref, h_s, carry_s, ext_s):
    i = pl.program_id(1)
    j = pl.program_id(2)
    tm = h_s.shape[0]

    @pl.when(j == 0)
    def _():
        h_s[...] = _norm_mod(x_ref[0], gain_ref[...], shift_ref[0], scale_ref[0]).astype(BF16)

    @pl.when(i == 0)
    def _():
        carry_s[j] = jnp.zeros(carry_s.shape[1:], F32)

    halves = ((wg_ref, cwg_ref, cbg_ref), (wv_ref, cwv_ref, cbv_ref))
    w_bf = [w_ref[...].astype(BF16) for w_ref, _, _ in halves]
    for half in range(2):
        ext_s[half, 0:SUBLANES, :] = carry_s[j, half]
    sub = min(FFN_SUB_ROWS, tm)
    for m in range(tm // sub):
        hm = h_s[m * sub:(m + 1) * sub, :]
        r0 = SUBLANES + m * sub
        ys = []
        for half, (_, cw_ref, cb_ref) in enumerate(halves):
            u = _dot(hm, w_bf[half])
            ext_s[half, r0:r0 + sub, :] = u
            ys.append(_ffn_conv_rows(u, ext_s[half, r0 - 1:r0 - 1 + sub, :], ext_s[half, r0 - 2:r0 - 2 + sub, :],
                                     cw_ref, cb_ref))
        o_ref[0, m * sub:(m + 1) * sub, :] = (_silu(ys[0]) * ys[1]).astype(o_ref.dtype)
    for half in range(2):
        tail = ext_s[half, tm:tm + SUBLANES, :]
        carry_s[j, half] = tail
        tail_ref[0, 0, half] = tail


def ffn_up_prompt(x, gain, shift, scale, w_up, conv_w, conv_b, layer, tm, tn=512):
    b, l, d = x.shape
    dff = w_up.shape[2] // 2
    nj = dff // tn
    conv_b = conv_b.reshape(conv_b.shape[0], 1, 2 * dff)
    blocks = (_nbytes((tm, d), F32) + 2 * _nbytes((d, tn), F32) + _nbytes((tm, tn), BF16)
              + 2 * _nbytes((1, d), F32) + 8 * _nbytes((SUBLANES, tn), F32))
    scratch = (_nbytes((tm, d), BF16) + _nbytes((nj, 2, SUBLANES, tn), F32) + 2 * _nbytes((d, tn), BF16)
               + 8 * _nbytes((tm, tn), F32))
    act, tail = pl.pallas_call(
        _ffn_up_prompt_kernel,
        grid=(b, l // tm, nj),
        in_specs=[pl.BlockSpec((1, tm, d), lambda b_, i, j: (b_, i, 0)),
                  pl.BlockSpec((1, d), lambda b_, i, j: (0, 0)),
                  _mod_specs(shift, tm), _mod_specs(scale, tm),
                  pl.BlockSpec((None, d, tn), lambda b_, i, j: (layer, 0, j)),
                  pl.BlockSpec((None, d, tn), lambda b_, i, j: (layer, 0, j + nj)),
                  pl.BlockSpec((None, FFN_CONV, tn), lambda b_, i, j: (layer, 0, j)),
                  pl.BlockSpec((None, FFN_CONV, tn), lambda b_, i, j: (layer, 0, j + nj)),
                  pl.BlockSpec((None, 1, tn), lambda b_, i, j: (layer, 0, j)),
                  pl.BlockSpec((None, 1, tn), lambda b_, i, j: (layer, 0, j + nj))],
        out_specs=[pl.BlockSpec((1, tm, tn), lambda b_, i, j: (b_, i, j)),
                   pl.BlockSpec((1, 1, 2, SUBLANES, tn), lambda b_, i, j: (b_, i, 0, 0, j))],
        out_shape=[jax.ShapeDtypeStruct((b, l, dff), BF16),
                   jax.ShapeDtypeStruct((b, l // tm, 2, SUBLANES, dff), F32)],
        scratch_shapes=[pltpu.VMEM((tm, d), BF16), pltpu.VMEM((nj, 2, SUBLANES, tn), F32),
                        pltpu.VMEM((2, SUBLANES + tm, tn), F32)],
        compiler_params=_params(3, blocks, scratch),
        name="ffn_up_prompt",
    )(x, gain.reshape(1, d), shift, scale, w_up, w_up, conv_w, conv_w, conv_b, conv_b)
    keep = FFN_CONV - 1
    hist = jnp.transpose(tail[:, -1, :, SUBLANES - keep:, :], (0, 2, 1, 3)).reshape(b, keep, 2 * dff)
    return act, hist


def _ffn_up_sample_kernel(x_ref, gain_ref, shift_ref, scale_ref, wg_ref, wv_ref, cwg_ref, cwv_ref,
                          cbg_ref, cbv_ref, hg_ref, hv_ref, o_ref, ng_ref, nv_ref, h_s, *, bsz):
    @pl.when(pl.program_id(0) == 0)
    def _():
        h_s[...] = _norm_mod(x_ref[...], gain_ref[...], shift_ref[...], scale_ref[...]).astype(BF16)

    h = h_s[...]
    rows = h_s.shape[0]
    keep = FFN_CONV - 1
    ys = []
    for w_ref, cw_ref, cb_ref, hist_ref, new_ref in ((wg_ref, cwg_ref, cbg_ref, hg_ref, ng_ref),
                                                     (wv_ref, cwv_ref, cbv_ref, hv_ref, nv_ref)):
        u = _dot(h, w_ref[...].astype(BF16))
        ext = jnp.concatenate([hist_ref[...], u], axis=0)
        prev = [ext[(keep - k) * bsz:(keep - k) * bsz + rows] for k in (1, 2)]
        ys.append(_ffn_conv_rows(u, prev[0], prev[1], cw_ref, cb_ref))
        new_ref[...] = ext[rows:rows + keep * bsz]
    o_ref[...] = (_silu(ys[0]) * ys[1]).astype(o_ref.dtype)


def ffn_up_sample(x, gain, shift, scale, w_up, conv_w, conv_b, layer, hist, tn=512):
    rows, d = x.shape
    dff = w_up.shape[2] // 2
    nj = dff // tn
    keep = FFN_CONV - 1
    bsz = hist.shape[0] // keep
    conv_b = conv_b.reshape(conv_b.shape[0], 1, 2 * dff)
    col = lambda j: (0, j)
    col_hi = lambda j: (0, j + nj)
    lcol = lambda j: (layer, 0, j)
    lcol_hi = lambda j: (layer, 0, j + nj)
    blocks = (_nbytes((rows, d), F32) * 3 + 2 * _nbytes((d, tn), F32) + 5 * _nbytes((rows, tn), F32))
    scratch = _nbytes((rows, d), BF16) + 2 * _nbytes((d, tn), BF16) + 8 * _nbytes((rows, tn), F32)
    act, new_g, new_v = pl.pallas_call(
        functools.partial(_ffn_up_sample_kernel, bsz=bsz),
        grid=(nj,),
        in_specs=[pl.BlockSpec((rows, d), lambda j: (0, 0)),
                  pl.BlockSpec((1, d), lambda j: (0, 0)),
                  pl.BlockSpec((rows, d), lambda j: (0, 0)),
                  pl.BlockSpec((rows, d), lambda j: (0, 0)),
                  pl.BlockSpec((None, d, tn), lcol), pl.BlockSpec((None, d, tn), lcol_hi),
                  pl.BlockSpec((None, FFN_CONV, tn), lcol), pl.BlockSpec((None, FFN_CONV, tn), lcol_hi),
                  pl.BlockSpec((None, 1, tn), lcol), pl.BlockSpec((None, 1, tn), lcol_hi),
                  pl.BlockSpec((keep * bsz, tn), col), pl.BlockSpec((keep * bsz, tn), col_hi)],
        out_specs=[pl.BlockSpec((rows, tn), col), pl.BlockSpec((keep * bsz, tn), col),
                   pl.BlockSpec((keep * bsz, tn), col)],
        out_shape=[jax.ShapeDtypeStruct((rows, dff), BF16),
                   jax.ShapeDtypeStruct((keep * bsz, dff), F32),
                   jax.ShapeDtypeStruct((keep * bsz, dff), F32)],
        scratch_shapes=[pltpu.VMEM((rows, d), BF16)],
        compiler_params=_params(1, blocks, scratch),
        name="ffn_up_sample",
    )(x, gain.reshape(1, d), shift, scale, w_up, w_up, conv_w, conv_w, conv_b, conv_b, hist, hist)
    return act, jnp.concatenate([new_g, new_v], axis=-1)


def _gdn_in_kernel(*refs, n_qk_tiles, n_conv_tiles, bsz):
    if bsz is None:
        (x_ref, gain_ref, shift_ref, scale_ref, w_ref, wba_ref, alog_ref, dtb_ref, cw_ref,
         o_ref, bg_ref, raw_ref, h_s, carry_s) = refs
    else:
        (x_ref, gain_ref, shift_ref, scale_ref, w_ref, wba_ref, alog_ref, dtb_ref, cw_ref, hist_ref,
         o_ref, bg_ref, raw_ref, h_s) = refs
    i = pl.program_id(1)
    j = pl.program_id(2)
    tm = h_s.shape[0]
    width = GDN_CONV

    @pl.when(j == 0)
    def _():
        h = _norm_mod(x_ref[0], gain_ref[...], shift_ref[0], scale_ref[0]).astype(BF16)
        h_s[...] = h
        y = _dot_nt(h, wba_ref[...].astype(BF16))
        lane = lax.broadcasted_iota(jnp.int32, y.shape, 1) % LANES
        beta = 1.0 / (1.0 + jnp.exp(-y))
        a = y + dtb_ref[...]
        softplus = jnp.maximum(a, 0.0) + jnp.log1p(jnp.exp(-jnp.abs(a)))
        g = -jnp.exp(alog_ref[...]) * softplus
        bg_ref[0] = jnp.where(lane < GDN_VH_PER_GROUP, beta, jnp.where(lane < 2 * GDN_VH_PER_GROUP, g, 0.0))

    sub = min(GDN_IN_SUB_ROWS, tm)
    row8 = lax.broadcasted_iota(jnp.int32, (SUBLANES, 1), 0)
    assert width == 4

    def shifted(x, prev, k):
        top = jnp.where(row8 < k, pltpu.roll(prev, k, axis=0), pltpu.roll(x[0:SUBLANES], k, axis=0))
        return jnp.concatenate([top, pltpu.roll(x, k, axis=0)[SUBLANES:]], axis=0)

    def conv_tile(l2_scale):
        w_bf = w_ref[...].astype(BF16)
        jc = jnp.minimum(j, n_conv_tiles - 1)
        if bsz is None:
            @pl.when(i == 0)
            def _():
                carry_s[jc] = jnp.zeros(carry_s.shape[1:], F32)
            prev_u, prev_pair = carry_s[jc, 0], carry_s[jc, 1]
        for m in range(tm // sub):
            rs = slice(m * sub, (m + 1) * sub)
            u = _dot_nt(h_s[rs, :], w_bf)
            if bsz is None:
                u1 = shifted(u, prev_u, 1)
                pair = u * cw_ref[1:2, :] + u1 * cw_ref[0:1, :]
                y = u * cw_ref[3:4, :] + u1 * cw_ref[2:3, :] + shifted(pair, prev_pair, 2)
                prev_u, prev_pair = u[sub - SUBLANES:sub], pair[sub - SUBLANES:sub]
            else:
                y = u * cw_ref[width - 1:width, :]
                ext = jnp.concatenate([hist_ref[...], u], axis=0)
                for k in range(1, width):
                    wk = cw_ref[width - 1 - k:width - k, :]
                    y = y + ext[(width - 1 - k) * bsz:(width - 1 - k) * bsz + sub] * wk
                raw_ref[...] = ext[sub:sub + (width - 1) * bsz]
            y = _silu(y)
            if l2_scale is not None:
                heads = [y[:, hd * LANES:(hd + 1) * LANES] for hd in range(y.shape[1] // LANES)]
                y = jnp.concatenate(
                    [yh * (lax.rsqrt(jnp.sum(yh * yh, axis=-1, keepdims=True) + NORM_EPS) * l2_scale)
                     for yh in heads], axis=1)
            o_ref[0, rs, :] = y.astype(o_ref.dtype)
        if bsz is None:
            carry_s[jc, 0] = prev_u
            carry_s[jc, 1] = prev_pair
            raw_ref[0, 0] = prev_u

    @pl.when(j < n_qk_tiles)
    def _():
        conv_tile(jnp.where(j < n_qk_tiles // 2, GDN_DK ** -0.5, 1.0).astype(F32))

    @pl.when((j >= n_qk_tiles) & (j < n_conv_tiles))
    def _():
        conv_tile(None)

    @pl.when(j >= n_conv_tiles)
    def _():
        w_bf = w_ref[...].astype(BF16)
        for m in range(tm // sub):
            rs = slice(m * sub, (m + 1) * sub)
            o_ref[0, rs, :] = _silu(_dot_nt(h_s[rs, :], w_bf)).astype(o_ref.dtype)


def _group_lane_layout(vec_b, vec_a):
    lead = vec_b.shape[:-1]
    vb = vec_b.reshape(*lead, GDN_HEAD_GROUPS, GDN_VH_PER_GROUP)
    va = vec_a.reshape(*lead, GDN_HEAD_GROUPS, GDN_VH_PER_GROUP)
    pad = jnp.zeros((*lead, GDN_HEAD_GROUPS, LANES - 2 * GDN_VH_PER_GROUP), vec_b.dtype)
    return jnp.concatenate([vb, va, pad], axis=-1).reshape(*lead, GDN_HEAD_GROUPS * LANES)


def gdn_in_projection(x, gain, shift, scale, w_in, conv_w, a_log, dt_bias, tm, conv_hist=None, tn=512):
    b, l, d = x.shape
    n_main = GDN_CONV_DIM + GDN_VAL_DIM
    w_t = jnp.transpose(w_in)
    wba = _group_lane_layout(w_in[:, n_main:n_main + GDN_V_HEADS], w_in[:, n_main + GDN_V_HEADS:])
    wba_t = jnp.transpose(wba)
    zeros = jnp.zeros((1, GDN_V_HEADS), F32)
    alog = _group_lane_layout(zeros, a_log.reshape(1, -1).astype(F32))
    dtb = _group_lane_layout(zeros, dt_bias.reshape(1, -1).astype(F32))
    nbg = wba_t.shape[0]
    n_conv_tiles = GDN_CONV_DIM // tn
    n_qk_tiles = 2 * GDN_KEY_DIM // tn
    conv_col = lambda b_, i, j: (0, jnp.minimum(j, n_conv_tiles - 1))
    in_specs = [pl.BlockSpec((1, tm, d), lambda b_, i, j: (b_, i, 0)),
                pl.BlockSpec((1, d), lambda b_, i, j: (0, 0)),
                _mod_specs(shift, tm), _mod_specs(scale, tm),
                pl.BlockSpec((tn, d), lambda b_, i, j: (j, 0)),
                pl.BlockSpec((nbg, d), lambda b_, i, j: (0, 0)),
                pl.BlockSpec((1, nbg), lambda b_, i, j: (0, 0)),
                pl.BlockSpec((1, nbg), lambda b_, i, j: (0, 0)),
                pl.BlockSpec((GDN_CONV, tn), conv_col)]
    args = [x, gain.reshape(1, d), shift, scale, w_t, wba_t, alog, dtb, conv_w]
    scratch_shapes = [pltpu.VMEM((tm, d), BF16)]
    blocks = (_nbytes((tm, d), F32) + _nbytes((tn, d), F32) + _nbytes((tm, tn), BF16)
              + _nbytes((nbg, d), F32) + _nbytes((tm, nbg), F32) + 2 * _nbytes((1, d), F32))
    scratch = _nbytes((tm, d), BF16) + _nbytes((tn, d), BF16) + 6 * _nbytes((tm, nbg), F32)
    if conv_hist is None:
        bsz = None
        raw_spec = pl.BlockSpec((1, 1, SUBLANES, tn), lambda b_, i, j: (b_, i, 0, jnp.minimum(j, n_conv_tiles - 1)))
        raw_shape = jax.ShapeDtypeStruct((b, l // tm, SUBLANES, GDN_CONV_DIM), F32)
        scratch_shapes.append(pltpu.VMEM((n_conv_tiles, 2, SUBLANES, tn), F32))
    else:
        assert b == 1 and l == tm <= GDN_IN_SUB_ROWS
        hist_rows = conv_hist.shape[0]
        bsz = hist_rows // (GDN_CONV - 1)
        args.append(conv_hist)
        in_specs.append(pl.BlockSpec((hist_rows, tn), conv_col))
        raw_spec = pl.BlockSpec((hist_rows, tn), conv_col)
        raw_shape = jax.ShapeDtypeStruct((hist_rows, GDN_CONV_DIM), F32)
        blocks += 2 * _nbytes((hist_rows, tn), F32)
    return pl.pallas_call(
        functools.partial(_gdn_in_kernel, n_qk_tiles=n_qk_tiles, n_conv_tiles=n_conv_tiles, bsz=bsz),
        grid=(b, l // tm, n_main // tn),
        in_specs=in_specs,
        out_specs=[pl.BlockSpec((1, tm, tn), lambda b_, i, j: (b_, i, j)),
                   pl.BlockSpec((1, tm, nbg), lambda b_, i, j: (b_, i, 0)),
                   raw_spec],
        out_shape=[jax.ShapeDtypeStruct((b, l, n_main), BF16),
                   jax.ShapeDtypeStruct((b, l, nbg), F32),
                   raw_shape],
        scratch_shapes=scratch_shapes,
        compiler_params=_params(3, blocks, scratch),
        name="gdn_in_projection",
    )(*args)


def _gdn_core_kernel(q_ref, k_ref, v_ref, zg_ref, bg_ref, onorm_ref, s0_ref, o_ref, s1_ref,
                     s_s, gc_s, tdec_s, mo_s, *, seq_rows):
    r = pl.program_id(2)
    n_r = pl.num_programs(2)
    rows = q_ref.shape[1]
    chunk = GDN_CHUNK
    vpg = GDN_VH_PER_GROUP
    rep = GDN_V_HEADS // GDN_K_HEADS
    n_kh = GDN_KH_PER_GROUP
    seg = chunk if seq_rows is None else seq_rows
    n_seq = chunk // seg

    def load_states():
        for s in range(n_seq):
            for kh in range(n_kh):
                s_s[s * n_kh + kh] = jnp.concatenate([s0_ref[s, kh * rep + e] for e in range(rep)], axis=1)

    if seq_rows is None:
        pl.when(r == 0)(load_states)
    else:
        load_states()

    def seg_last(x):
        if n_seq == 1:
            return x[chunk - 1:chunk, :]
        blocks = x.reshape(n_seq, seg, x.shape[1])
        return jnp.broadcast_to(blocks[:, seg - 1:seg, :], blocks.shape).reshape(x.shape)

    n_chunks = rows // chunk
    pw = GDN_PACK * chunk
    n_packs = vpg // GDN_PACK
    ri = lax.broadcasted_iota(jnp.int32, (chunk, pw), 0)
    li = lax.broadcasted_iota(jnp.int32, (chunk, pw), 1) % chunk
    seg_shift = int(math.log2(seg))
    same_seq = (ri >> seg_shift) == (li >> seg_shift)
    tril_p = (ri >= li) & same_seq
    strict_p = (ri > li) & same_seq
    eye_p = (ri == li).astype(F32)
    n_levels = seg_shift
    off_masks = [((ri >> (lvl + 1)) == (li >> (lvl + 1))) & ((ri >> lvl) != (li >> lvl))
                 for lvl in range(n_levels)]
    bd_rows = lax.broadcasted_iota(jnp.int32, (pw, pw), 0) // chunk
    bd_cols = lax.broadcasted_iota(jnp.int32, (pw, pw), 1) // chunk
    bd_mask = (bd_rows == bd_cols).astype(BF16)
    row_c = lax.broadcasted_iota(jnp.int32, (chunk, LANES), 0)
    lane_lo = lax.broadcasted_iota(jnp.int32, (chunk, LANES), 1) < chunk

    def block_diag(xp):
        return jnp.concatenate([xp.astype(BF16)] * GDN_PACK, axis=0) * bd_mask

    def pack_cols(arr, first_col):
        tiles = []
        for t in range(pw // LANES):
            even = jnp.broadcast_to(arr[:, first_col + 2 * t:first_col + 2 * t + 1], (chunk, LANES))
            odd = jnp.broadcast_to(arr[:, first_col + 2 * t + 1:first_col + 2 * t + 2], (chunk, LANES))
            tiles.append(jnp.where(lane_lo, even, odd))
        return jnp.concatenate(tiles, axis=1)

    for c0 in range(0, n_chunks, GDN_PHASE_A_CHUNKS):
        group = [(c, p) for c in range(c0, min(c0 + GDN_PHASE_A_CHUNKS, n_chunks)) for p in range(n_packs)]
        a_list, qkd, brow, dec = {}, {}, {}, {}
        for c in range(c0, min(c0 + GDN_PHASE_A_CHUNKS, n_chunks)):
            rs = slice(c * chunk, (c + 1) * chunk)
            bg = bg_ref[0, rs, :]
            cum = bg
            shift = 1
            while shift < seg:
                cum = cum + jnp.where((row_c & (seg - 1)) >= shift, pltpu.roll(cum, shift, axis=0), 0.0)
                shift *= 2
            gc_s[rs, :] = cum
            gq = []
            for kh in range(GDN_KH_PER_GROUP):
                ksl = slice(kh * LANES, (kh + 1) * LANES)
                k = k_ref[0, rs, ksl]
                kq = jnp.concatenate([k, q_ref[0, rs, ksl]], axis=0)
                kk = jnp.concatenate([k, k], axis=0)
                gq.append(_dot_nt(kq, kk))
            for p in range(n_packs):
                khs = [(p * GDN_PACK + 2 * t) // rep for t in range(pw // LANES)]
                gram = jnp.concatenate([gq[kh][:chunk] for kh in khs], axis=1)
                qk = jnp.concatenate([gq[kh][chunk:] for kh in khs], axis=1)
                beta_p = pack_cols(bg, p * GDN_PACK)
                gcol_p = pack_cols(cum, vpg + p * GDN_PACK)
                grow_p = jnp.sum(gcol_p * eye_p, axis=0, keepdims=True)
                decay = jnp.exp(jnp.where(tril_p, gcol_p - grow_p, NEG_INF))
                a_list[(c, p)] = jnp.where(strict_p, gram * beta_p * decay, 0.0)
                qkd[(c, p)] = (qk * decay).astype(BF16)
                brow[(c, p)] = jnp.sum(beta_p * eye_p, axis=0, keepdims=True)
                dec[(c, p)] = jnp.exp(seg_last(gcol_p) - gcol_p)
        inv = {cp: eye_p - jnp.where(off_masks[0], a_list[cp], 0.0) for cp in group}
        for lvl in range(1, n_levels):
            w = {cp: _dot(jnp.where(off_masks[lvl], a_list[cp], 0.0).astype(BF16), block_diag(inv[cp]))
                 for cp in group}
            inv = {cp: inv[cp] - _dot(inv[cp].astype(BF16), block_diag(w[cp])) for cp in group}
        for cp in group:
            t_beta = inv[cp] * brow[cp]
            tdec_s[cp[0], cp[1]] = block_diag(dec[cp] * t_beta)
            mo_s[cp[0], cp[1]] = block_diag(_dot(qkd[cp], block_diag(t_beta)))

    def chunk_body(c, carry):
        rs = pl.ds(pl.multiple_of(c * chunk, chunk), chunk)
        cum = gc_s[rs, :]
        eg, g_last, ks, qs = [], [], [], []
        for kh in range(n_kh):
            ksl = slice(kh * LANES, (kh + 1) * LANES)
            kq = jnp.concatenate([k_ref[0, rs, ksl], q_ref[0, rs, ksl]], axis=0)
            per_seq = [_dot(kq, s_s[s * n_kh + kh].astype(BF16)) for s in range(n_seq)]
            kq_s = jnp.concatenate([per_seq[s][half * chunk + s * seg:half * chunk + (s + 1) * seg]
                                    for half in range(2) for s in range(n_seq)], axis=0)
            for e in range(rep):
                hv = kh * rep + e
                gcol = cum[:, vpg + hv:vpg + hv + 1]
                g_last.append([gcol[(s + 1) * seg - 1:(s + 1) * seg, :] for s in range(n_seq)])
                eg.append(jnp.exp(gcol))
                ks.append(kq_s[:chunk, e * LANES:(e + 1) * LANES] * eg[hv])
                qs.append(kq_s[chunk:, e * LANES:(e + 1) * LANES] * eg[hv])
        v_dec, o_intra = [], []
        for p in range(n_packs):
            heads = range(p * GDN_PACK, (p + 1) * GDN_PACK)
            rhs = jnp.concatenate([v_ref[0, rs, hv * LANES:(hv + 1) * LANES].astype(F32) - ks[hv]
                                   for hv in heads], axis=0).astype(BF16)
            vd = _dot(tdec_s[c, p], rhs)
            oi = _dot(mo_s[c, p], rhs)
            for j in range(GDN_PACK):
                v_dec.append(vd[j * chunk:(j + 1) * chunk])
                o_intra.append(oi[j * chunk:(j + 1) * chunk])
        for hv in range(vpg):
            vsl = slice(hv * LANES, (hv + 1) * LANES)
            o = qs[hv] + o_intra[hv]
            on = o * lax.rsqrt(jnp.mean(o * o, axis=-1, keepdims=True) + NORM_EPS) * onorm_ref[...]
            o_ref[0, rs, vsl] = (on * zg_ref[0, rs, vsl].astype(F32)).astype(o_ref.dtype)
        lane_v = lax.broadcasted_iota(jnp.int32, (1, rep * LANES), 1) // LANES
        row_seq = lax.broadcasted_iota(jnp.int32, (chunk, 1), 0) >> seg_shift
        for kh in range(n_kh):
            heads = range(kh * rep, (kh + 1) * rep)
            k_bf = k_ref[0, rs, kh * LANES:(kh + 1) * LANES]
            vd_pair = jnp.concatenate([v_dec[hv] for hv in heads], axis=1)
            for s in range(n_seq):
                vd_s = vd_pair if n_seq == 1 else jnp.where(row_seq == s, vd_pair, 0.0)
                ds = _dot_tn(k_bf, vd_s.astype(BF16))
                keep = jnp.exp(g_last[kh * rep][s])
                for e in range(1, rep):
                    keep = jnp.where(lane_v == e, jnp.exp(g_last[kh * rep + e][s]), keep)
                s_s[s * n_kh + kh] = s_s[s * n_kh + kh] * keep + ds
        return carry

    lax.fori_loop(0, n_chunks, chunk_body, 0)

    def store_states():
        for s in range(n_seq):
            for hv in range(vpg):
                s1_ref[s, hv] = s_s[s * n_kh + hv // rep][:, (hv % rep) * LANES:(hv % rep + 1) * LANES]

    if seq_rows is None:
        pl.when(r == n_r - 1)(store_states)
    else:
        store_states()


def gdn_core(qkvz, bg, s0, out_norm, rows, seq_rows=None):
    b, l, _ = qkvz.shape
    n_state = 1 if seq_rows is None else GDN_CHUNK // seq_rows
    if seq_rows is None:
        state_idx = lambda b_, g, r: (b_, g, 0, 0)
    else:
        assert b == 1 and rows == GDN_CHUNK
        state_idx = lambda b_, g, r: (r, g, 0, 0)
    hg = GDN_HEAD_GROUPS
    qw = GDN_KEY_DIM // hg
    vw = GDN_VAL_DIM // hg
    kq = GDN_KEY_DIM // qw
    kv = 2 * GDN_KEY_DIM // vw
    kz = GDN_CONV_DIM // vw
    vpg = GDN_VH_PER_GROUP
    blocks = (2 * _nbytes((rows, qw), BF16) + 3 * _nbytes((rows, vw), BF16) + _nbytes((rows, LANES), F32)
              + 2 * _nbytes((n_state, vpg, GDN_DK, GDN_DV), F32))
    n_chunks = rows // GDN_CHUNK
    n_packs = vpg // GDN_PACK
    pw = GDN_PACK * GDN_CHUNK
    scratch = (_nbytes((n_state, vpg, GDN_DK, GDN_DV), F32) + _nbytes((rows, LANES), F32)
               + 2 * _nbytes((n_chunks, n_packs, pw, pw), BF16) + 6 * _nbytes((rows, vw), F32))
    return pl.pallas_call(
        functools.partial(_gdn_core_kernel, seq_rows=seq_rows),
        grid=(b, hg, l // rows),
        in_specs=[pl.BlockSpec((1, rows, qw), lambda b_, g, r: (b_, r, g)),
                  pl.BlockSpec((1, rows, qw), lambda b_, g, r: (b_, r, kq + g)),
                  pl.BlockSpec((1, rows, vw), lambda b_, g, r: (b_, r, kv + g)),
                  pl.BlockSpec((1, rows, vw), lambda b_, g, r: (b_, r, kz + g)),
                  pl.BlockSpec((1, rows, LANES), lambda b_, g, r: (b_, r, g)),
                  pl.BlockSpec((1, GDN_DV), lambda b_, g, r: (0, 0)),
                  pl.BlockSpec((n_state, vpg, GDN_DK, GDN_DV), state_idx)],
        out_specs=[pl.BlockSpec((1, rows, vw), lambda b_, g, r: (b_, r, g)),
                   pl.BlockSpec((n_state, vpg, GDN_DK, GDN_DV), state_idx)],
        out_shape=[jax.ShapeDtypeStruct((b, l, GDN_VAL_DIM), BF16),
                   jax.ShapeDtypeStruct(s0.shape, F32)],
        scratch_shapes=[pltpu.VMEM((n_state * GDN_KH_PER_GROUP, GDN_DK,
                                    (GDN_V_HEADS // GDN_K_HEADS) * GDN_DV), F32),
                        pltpu.VMEM((rows, LANES), F32),
                        pltpu.VMEM((n_chunks, n_packs, pw, pw), BF16),
                        pltpu.VMEM((n_chunks, n_packs, pw, pw), BF16)],
        compiler_params=_params(3, blocks, scratch),
        name="gdn_core",
    )(qkvz, qkvz, qkvz, qkvz, bg, out_norm.reshape(1, GDN_DV).astype(F32), s0)


def _tile_rows(l, cap):
    t = min(l, cap)
    while l % t:
        t //= 2
    return t


def _to_time_major(a):
    return jnp.transpose(a, (1, 0, 2)).reshape(a.shape[0] * a.shape[1], a.shape[2])


def _from_time_major(a, bsz):
    return jnp.transpose(a.reshape(a.shape[0] // bsz, bsz, a.shape[1]), (1, 0, 2))


def _trunk(x, mod, states, P, sample):
    bsz, l, d = x.shape
    tm = _tile_rows(l, ROW_TILE_CAP)
    out_proj = matmul_gate_residual if sample else matmul_gate_residual_wres
    new = {}
    depth = mod.shape[0]
    for layer in range(depth):
        if sample:
            parts = [mod[layer][None, :, k * d:(k + 1) * d] for k in range(6)]
        else:
            parts = [mod[layer][:, None, k * d:(k + 1) * d] for k in range(6)]
        sh_m, sc_m, g_m, sh_f, sc_f, g_f = parts
        i = layer // 2
        if layer % 2 == 0:
            qkv = qkv_projection(x, P['norm_mix'][layer], sh_m, sc_m, P['w_attn_qkv'][i],
                                 P['attn_q_norm'][i], P['attn_k_norm'][i], tm)
            nq, nkv = N_HEADS * HEAD_DIM, N_KV_HEADS * HEAD_DIM
            if sample:
                o, k_win, v_win = attention_sample(qkv[0], states['win_k'][i], states['win_v'][i],
                                                   P['attn_sinks'][i], P['rel_bias_table'])
                o = o[None]
                w_len = k_win.shape[1]
                new.setdefault('win_k', []).append(k_win.reshape(-1, w_len, N_KV_HEADS, HEAD_DIM))
                new.setdefault('win_v', []).append(v_win.reshape(-1, w_len, N_KV_HEADS, HEAD_DIM))
            else:
                o = attention_prompt(qkv, P['attn_sinks'][i], P['rel_bias_table'])
                keep = min(WINDOW, PAST_LEN)
                new.setdefault('win_k', []).append(
                    qkv[:, l - keep:, nq:nq + nkv].reshape(bsz, keep, N_KV_HEADS, HEAD_DIM))
                new.setdefault('win_v', []).append(
                    qkv[:, l - keep:, nq + nkv:].reshape(bsz, keep, N_KV_HEADS, HEAD_DIM))
            x = matmul_gate_residual(o, P['w_attn_o'], i, x, g_m, _tile_rows(l, 2 * ROW_TILE_CAP), 512)
        else:
            keep = GDN_CONV - 1
            gdn_args = (x, P['norm_mix'][layer], sh_m, sc_m, P['w_gdn_in'][i], P['gdn_conv_w'][i],
                        P['gdn_a_log'][i], P['gdn_dt_bias'][i], tm)
            if sample:
                seqs = states['gdn'].shape[1]
                t = l // seqs
                qkvz, bg, hist1 = gdn_in_projection(
                    *gdn_args, conv_hist=_to_time_major(states['gdn_conv'][i].astype(F32)), tn=2 * 512)
                new.setdefault('gdn_conv', []).append(_from_time_major(hist1, seqs))
                seq_rows = -(-t // SUBLANES) * SUBLANES
                per_seq = lambda a: jnp.pad(_from_time_major(a, seqs), ((0, 0), (0, seq_rows - t), (0, 0))
                                            ).reshape(1, seqs * seq_rows, a.shape[-1])
                o_seq, s1 = gdn_core(per_seq(qkvz[0]), per_seq(bg[0]), states['gdn'][i].astype(F32),
                                     P['gdn_out_norm'][i], GDN_CHUNK, seq_rows=seq_rows)
                o = _to_time_major(o_seq.reshape(seqs, seq_rows, GDN_VAL_DIM)[:, :t])[None]
            else:
                qkvz, bg, raw = gdn_in_projection(*gdn_args)
                s0 = jnp.zeros((bsz, GDN_V_HEADS, GDN_DK, GDN_DV), F32)
                o, s1 = gdn_core(qkvz, bg, s0, P['gdn_out_norm'][i], _tile_rows(l, GDN_ROW_TILE_CAP))
                new.setdefault('gdn_conv', []).append(raw[:, -1, SUBLANES - keep:, :])
            new.setdefault('gdn', []).append(s1)
            x = out_proj(o, P['w_gdn_out'], i, x, g_m, tm, 512)
        if sample:
            seqs = states['ffn_conv'].shape[1]
            act, f1 = ffn_up_sample(x[0], P['norm_ffn'][layer], sh_f[0], sc_f[0], P['w_ffn_up'],
                                    P['ffn_conv_w'], P['ffn_conv_b'], layer,
                                    _to_time_major(states['ffn_conv'][layer].astype(F32)))
            act, f1 = act[None], _from_time_major(f1, seqs)
        else:
            act, f1 = ffn_up_prompt(x, P['norm_ffn'][layer], sh_f, sc_f, P['w_ffn_up'],
                                    P['ffn_conv_w'], P['ffn_conv_b'], layer, tm)
        new.setdefault('ffn_conv', []).append(f1)
        if sample:
            x = matmul_gate_residual(act, P['w_ffn_down'], layer, x, g_f, tm, 512)
        else:
            x = matmul_gate_residual_wres(act, P['w_ffn_down'], layer, x, g_f, _tile_rows(l, 512), 512)
    return x, {k: jnp.stack(v) for k, v in new.items()}


def kernel(x_prompt, x_sample, c_prompt, c_sample, cache_win_k, cache_win_v, state_gdn, state_gdn_conv,
           state_ffn_conv, rel_bias_table, w_ada, b_ada, norm_mix, norm_ffn, w_attn_qkv, attn_q_norm,
           attn_k_norm, attn_sinks, w_attn_o, w_gdn_in, gdn_conv_w, gdn_a_log, gdn_dt_bias, gdn_out_norm,
           w_gdn_out, w_ffn_up, ffn_conv_w, ffn_conv_b, w_ffn_down):
    P = dict(rel_bias_table=rel_bias_table, norm_mix=norm_mix, norm_ffn=norm_ffn, w_attn_qkv=w_attn_qkv,
             attn_q_norm=attn_q_norm, attn_k_norm=attn_k_norm, attn_sinks=attn_sinks, w_attn_o=w_attn_o,
             w_gdn_in=w_gdn_in, gdn_conv_w=gdn_conv_w, gdn_a_log=gdn_a_log, gdn_dt_bias=gdn_dt_bias,
             gdn_out_norm=gdn_out_norm, w_gdn_out=w_gdn_out, w_ffn_up=w_ffn_up, ffn_conv_w=ffn_conv_w,
             ffn_conv_b=ffn_conv_b, w_ffn_down=w_ffn_down)
    bp = x_prompt.shape[0]
    bs, ts, d = x_sample.shape
    n_c = bp + bs
    c_rows = -(-n_c // SUBLANES) * SUBLANES
    c_all = jnp.pad(jnp.concatenate([c_prompt, c_sample], axis=0), ((0, c_rows - n_c), (0, 0)))
    mod = ada_modulation(c_all, w_ada, b_ada)
    mod_prompt = mod[:, :bp]
    mod_sample = jnp.tile(mod[:, bp:n_c], (1, ts, 1))

    y_p, new_p = _trunk(x_prompt, mod_prompt, None, P, sample=False)
    states = dict(win_k=cache_win_k, win_v=cache_win_v, gdn=state_gdn, gdn_conv=state_gdn_conv,
                  ffn_conv=state_ffn_conv)
    y_s, new_s = _trunk(_to_time_major(x_sample)[None], mod_sample, states, P, sample=True)
    y_s = _from_time_major(y_s[0], bs)
    return (y_p, y_s, new_p['win_k'], new_p['win_v'], new_s['win_k'], new_s['win_v'],
            new_p['gdn'], new_s['gdn'], new_p['gdn_conv'], new_s['gdn_conv'],
            new_p['ffn_conv'], new_s['ffn_conv'])
```

```python
import functools
import math

import numpy as np
import jax
import jax.numpy as jnp
from jax import lax
from jax.experimental import pallas as pl
from jax.experimental.pallas import tpu as pltpu

HEAD_DIM = 64
N_HEADS = 32
N_KV_HEADS = 4
ATT_GROUP = N_HEADS // N_KV_HEADS
WINDOW = 128
ATT_BLOCK = 128
ATT_SAMPLE_SEQS = 8
N_BUCKETS = 32
MAX_DISTANCE = 128
NEG_INF = -1e30
PAST_LEN = 16384

GDN_K_HEADS = 16
GDN_V_HEADS = 32
GDN_DK = 128
GDN_DV = 128
GDN_KEY_DIM = GDN_K_HEADS * GDN_DK
GDN_VAL_DIM = GDN_V_HEADS * GDN_DV
GDN_CONV_DIM = 2 * GDN_KEY_DIM + GDN_VAL_DIM
GDN_CONV = 4
GDN_CHUNK = 64
GDN_HEAD_GROUPS = 4
GDN_VH_PER_GROUP = GDN_V_HEADS // GDN_HEAD_GROUPS
GDN_KH_PER_GROUP = GDN_K_HEADS // GDN_HEAD_GROUPS
GDN_PACK = 4
GDN_PHASE_A_CHUNKS = 4

FFN_CONV = 3
NORM_EPS = 1e-6

LANES = 128
SUBLANES = 8
VMEM_CAP_BYTES = 60 * 1024 * 1024
VMEM_SLACK_BYTES = 8 * 1024 * 1024

ROW_TILE_CAP = 1024
GDN_IN_SUB_ROWS = 256
FFN_SUB_ROWS = 128
GDN_ROW_TILE_CAP = 1024

BF16 = jnp.bfloat16
F32 = jnp.float32


def _vmem_limit(block_bytes, scratch_bytes=0):
    est = 2 * int(block_bytes) + int(scratch_bytes) + VMEM_SLACK_BYTES
    return int(min(max(est, 16 * 1024 * 1024), VMEM_CAP_BYTES))


def _params(n_grid, block_bytes, scratch_bytes=0):
    return pltpu.CompilerParams(
        dimension_semantics=("arbitrary",) * n_grid,
        vmem_limit_bytes=_vmem_limit(block_bytes, scratch_bytes))


def _nbytes(shape, dtype):
    return int(np.prod(shape)) * jnp.dtype(dtype).itemsize


def _silu(x):
    return x * (1.0 / (1.0 + jnp.exp2(x * (-math.log2(math.e)))))


def _dot(a, b):
    return jnp.dot(a, b, preferred_element_type=F32)


def _dot_nt(a, b):
    return lax.dot_general(a, b, (((1,), (1,)), ((), ())), preferred_element_type=F32)


def _dot_tn(a, b):
    return lax.dot_general(a, b, (((0,), (0,)), ((), ())), preferred_element_type=F32)


def _norm_mod(x, gain, shift, scale):
    ms = jnp.mean(x * x, axis=-1, keepdims=True)
    y = x * lax.rsqrt(ms + NORM_EPS) * gain
    return y * (1.0 + scale) + shift


def _mod_specs(shift, tm):
    d = shift.shape[-1]
    if shift.shape[1] == 1:
        return pl.BlockSpec((1, 1, d), lambda b, i, j: (b, 0, 0))
    return pl.BlockSpec((1, tm, d), lambda b, i, j: (b, i, 0))


def _ada_kernel(c_ref, w_ref, b_ref, o_ref):
    a = _silu(c_ref[...]).astype(BF16)
    o_ref[0] = _dot(a, w_ref[0].astype(BF16)) + b_ref[0]


def ada_modulation(c_all, w_ada, b_ada, tn=1024):
    rows, d = c_all.shape
    depth, _, n = w_ada.shape
    blocks = _nbytes((rows, d), F32) + _nbytes((d, tn), F32) + _nbytes((rows, tn), F32)
    return pl.pallas_call(
        _ada_kernel,
        grid=(depth, n // tn),
        in_specs=[pl.BlockSpec((rows, d), lambda l, j: (0, 0)),
                  pl.BlockSpec((1, d, tn), lambda l, j: (l, 0, j)),
                  pl.BlockSpec((1, 1, tn), lambda l, j: (l, 0, j))],
        out_specs=pl.BlockSpec((1, rows, tn), lambda l, j: (l, 0, j)),
        out_shape=jax.ShapeDtypeStruct((depth, rows, n), F32),
        compiler_params=_params(2, blocks, _nbytes((d, tn), BF16)),
        name="ada_modulation",
    )(c_all, w_ada, b_ada.reshape(depth, 1, n))


def _qkv_kernel(x_ref, gain_ref, shift_ref, scale_ref, w_ref, hgain_ref, hflag_ref, gmat_ref,
                o_ref, h_s):
    @pl.when(pl.program_id(2) == 0)
    def _():
        h_s[...] = _norm_mod(x_ref[0], gain_ref[...], shift_ref[0], scale_ref[0]).astype(BF16)

    y = _dot(h_s[...], w_ref[...].astype(BF16))
    ms = _dot((y * y).astype(BF16), gmat_ref[...])
    yn = y * lax.rsqrt(ms + NORM_EPS) * hgain_ref[...]
    o_ref[0] = jnp.where(hflag_ref[...] > 0.0, yn, y)


def qkv_projection(x, gain, shift, scale, w, q_gain, k_gain, tm, tn=512):
    b, l, d = x.shape
    n = w.shape[1]
    nq, nk = N_HEADS * HEAD_DIM, N_KV_HEADS * HEAD_DIM
    hgain = jnp.concatenate([jnp.tile(q_gain, N_HEADS), jnp.tile(k_gain, N_KV_HEADS),
                             jnp.ones((nk,), F32)]).reshape(1, n)
    hflag = jnp.concatenate([jnp.ones((nq + nk,), F32), jnp.zeros((nk,), F32)]).reshape(1, n)
    gidx = np.arange(tn) // HEAD_DIM
    gmat = jnp.asarray((gidx[:, None] == gidx[None, :]).astype(np.float32) / HEAD_DIM, BF16)
    blocks = (_nbytes((tm, d), F32) + _nbytes((d, tn), F32) + _nbytes((tm, tn), F32)
              + 2 * _nbytes((shift.shape[1] == 1 and 1 or tm, d), F32) + _nbytes((tn, tn), BF16))
    scratch = _nbytes((tm, d), BF16) + _nbytes((d, tn), BF16) + 4 * _nbytes((tm, tn), F32)
    return pl.pallas_call(
        _qkv_kernel,
        grid=(b, l // tm, n // tn),
        in_specs=[pl.BlockSpec((1, tm, d), lambda b_, i, j: (b_, i, 0)),
                  pl.BlockSpec((1, d), lambda b_, i, j: (0, 0)),
                  _mod_specs(shift, tm), _mod_specs(scale, tm),
                  pl.BlockSpec((d, tn), lambda b_, i, j: (0, j)),
                  pl.BlockSpec((1, tn), lambda b_, i, j: (0, j)),
                  pl.BlockSpec((1, tn), lambda b_, i, j: (0, j)),
                  pl.BlockSpec((tn, tn), lambda b_, i, j: (0, 0))],
        out_specs=pl.BlockSpec((1, tm, tn), lambda b_, i, j: (b_, i, j)),
        out_shape=jax.ShapeDtypeStruct((b, l, n), F32),
        scratch_shapes=[pltpu.VMEM((tm, d), BF16)],
        compiler_params=_params(3, blocks, scratch),
        name="qkv_projection",
    )(x, gain.reshape(1, d), shift, scale, w, hgain, hflag, gmat)


def _attn_prompt_kernel(sink_ref, q_ref, cur_ref, prev_ref, bprev_ref, bcur_ref, o_ref, kz_s, vz_s):
    n = pl.program_id(1)
    blk = ATT_BLOCK
    lane = lax.broadcasted_iota(jnp.int32, (2 * blk, LANES), 1)
    lo_half = lane < HEAD_DIM
    neg_prev = jnp.where(n == 0, NEG_INF, 0.0).astype(F32)

    kv_width = N_KV_HEADS * HEAD_DIM
    for pair in range(N_KV_HEADS // 2):
        for part, dst in ((0, kz_s), (1, vz_s)):
            col = part * kv_width + pair * LANES
            both = jnp.concatenate([prev_ref[0, :, col:col + LANES], cur_ref[0, :, col:col + LANES]], axis=0)
            swapped = pltpu.roll(both, HEAD_DIM, axis=1)
            zero = jnp.zeros_like(both)
            c0, c1 = 2 * pair, 2 * pair + 1
            dst[2 * c0 + 0] = jnp.where(lo_half, both, zero).astype(BF16)
            dst[2 * c0 + 1] = jnp.where(lo_half, zero, swapped).astype(BF16)
            dst[2 * c1 + 0] = jnp.where(lo_half, swapped, zero).astype(BF16)
            dst[2 * c1 + 1] = jnp.where(lo_half, zero, both).astype(BF16)

    scale = HEAD_DIM ** -0.5 * math.log2(math.e)
    for p in range(N_HEADS // 2):
        c = (2 * p) // ATT_GROUP
        qp = (q_ref[0, :, p * LANES:(p + 1) * LANES] * scale).astype(BF16)
        o_pair = None
        for a in range(2):
            h = 2 * p + a
            sink = sink_ref[h]
            s = _dot_nt(qp, kz_s[2 * c + a])
            s_prev = s[:, :blk] + bprev_ref[h] + neg_prev
            s_cur = s[:, blk:] + bcur_ref[h]
            m = jnp.maximum(jnp.max(jnp.maximum(s_prev, s_cur), axis=-1, keepdims=True), sink)
            e_prev = jnp.exp2(s_prev - m)
            e_cur = jnp.exp2(s_cur - m)
            den = jnp.sum(e_prev + e_cur, axis=-1, keepdims=True) + jnp.exp2(sink - m)
            pm = jnp.concatenate([e_prev, e_cur], axis=1).astype(BF16)
            o_a = _dot(pm, vz_s[2 * c + a]) * (1.0 / den)
            o_pair = o_a if o_pair is None else o_pair + o_a
        o_ref[0, :, p * LANES:(p + 1) * LANES] = o_pair.astype(o_ref.dtype)


def _t5_bucket_np(dist):
    max_exact = N_BUCKETS // 2
    d = np.maximum(dist, 0)
    df = np.maximum(d, 1).astype(np.float32)
    large = max_exact + (np.log(df / np.float32(max_exact)) / np.float32(math.log(MAX_DISTANCE / max_exact))
                         * np.float32(N_BUCKETS - max_exact)).astype(np.int32)
    large = np.minimum(large, N_BUCKETS - 1)
    return np.where(d < max_exact, d, large)


def _bias_from_dist(dist, in_band, rel_table):
    onehot = (_t5_bucket_np(dist)[..., None] == np.arange(N_BUCKETS)).astype(np.float32)
    tab = jnp.einsum('qsb,bh->hqs', jnp.asarray(onehot), rel_table.astype(F32), precision=lax.Precision.HIGHEST)
    return jnp.where(jnp.asarray(in_band)[None], tab, NEG_INF)


def attention_prompt(qkv, sinks, rel_table):
    b, l, n = qkv.shape
    blk = ATT_BLOCK
    nq = N_HEADS * HEAD_DIM
    kvw = 2 * N_KV_HEADS * HEAD_DIM
    kv_blk = nq // kvw
    qi = np.arange(blk)[:, None]
    sj = np.arange(blk)[None, :]
    d_prev = qi + blk - sj
    d_cur = qi - sj
    log2e = math.log2(math.e)
    bias_prev = _bias_from_dist(d_prev, (d_prev >= 0) & (d_prev <= WINDOW), rel_table) * log2e
    bias_cur = _bias_from_dist(d_cur, (d_cur >= 0) & (d_cur <= WINDOW), rel_table) * log2e
    blocks = (_nbytes((blk, nq), F32) + 2 * _nbytes((blk, kvw), F32) + _nbytes((blk, nq), BF16))
    scratch = 2 * _nbytes((2 * N_KV_HEADS, 2 * blk, LANES), BF16) + 4 * _nbytes((N_HEADS, blk, blk), F32)
    return pl.pallas_call(
        _attn_prompt_kernel,
        grid=(b, l // blk),
        in_specs=[pl.BlockSpec(memory_space=pltpu.SMEM),
                  pl.BlockSpec((1, blk, nq), lambda b_, i: (b_, i, 0)),
                  pl.BlockSpec((1, blk, kvw), lambda b_, i: (b_, i, kv_blk)),
                  pl.BlockSpec((1, blk, kvw), lambda b_, i: (b_, jnp.maximum(i - 1, 0), kv_blk)),
                  pl.BlockSpec((N_HEADS, blk, blk), lambda b_, i: (0, 0, 0)),
                  pl.BlockSpec((N_HEADS, blk, blk), lambda b_, i: (0, 0, 0))],
        out_specs=pl.BlockSpec((1, blk, nq), lambda b_, i: (b_, i, 0)),
        out_shape=jax.ShapeDtypeStruct((b, l, nq), BF16),
        scratch_shapes=[pltpu.VMEM((2 * N_KV_HEADS, 2 * blk, LANES), BF16),
                        pltpu.VMEM((2 * N_KV_HEADS, 2 * blk, LANES), BF16)],
        compiler_params=_params(2, blocks, scratch),
        name="attention_prompt",
    )(sinks.astype(F32) * log2e, qkv, qkv, qkv, bias_prev, bias_cur)


def _attn_sample_kernel(q_ref, k_ref, v_ref, bias_ref, sink_ref, o_ref):
    pairs = [(b, c) for b in range(q_ref.shape[0]) for c in range(N_KV_HEADS)]
    head = lambda ref, b, c: ref[b][:, c * HEAD_DIM:(c + 1) * HEAD_DIM].astype(BF16)
    s = {(b, c): _dot_nt(q_ref[b, c].astype(BF16), head(k_ref, b, c)) + bias_ref[c] for b, c in pairs}
    e, den = {}, {}
    for b, c in pairs:
        sink = sink_ref[c]
        m = jnp.maximum(jnp.max(s[(b, c)], axis=-1, keepdims=True), sink)
        e[(b, c)] = jnp.exp(s[(b, c)] - m)
        den[(b, c)] = jnp.sum(e[(b, c)], axis=-1, keepdims=True) + jnp.exp(sink - m)
    for b, c in pairs:
        o_ref[b, c] = _dot(e[(b, c)].astype(BF16), head(v_ref, b, c)) * (1.0 / den[(b, c)])


def attention_sample(qkv, cache_k, cache_v, sinks, rel_table):
    bsz, w = cache_k.shape[0], cache_k.shape[1]
    t = qkv.shape[0] // bsz
    nq, nkv = N_HEADS * HEAD_DIM, N_KV_HEADS * HEAD_DIM
    keys = w + t
    keys_pad = -(-keys // 16) * 16
    rows = ATT_GROUP * t
    q = qkv[:, :nq].reshape(t, bsz, N_KV_HEADS, ATT_GROUP, HEAD_DIM) * HEAD_DIM ** -0.5
    q = jnp.transpose(q, (1, 2, 3, 0, 4)).reshape(bsz, N_KV_HEADS, rows, HEAD_DIM)
    new_kv = jnp.transpose(qkv[:, nq:].reshape(t, bsz, 2 * nkv), (1, 0, 2))
    k_all = jnp.concatenate([cache_k.reshape(bsz, w, nkv), new_kv[:, :, :nkv]], axis=1)
    v_all = jnp.concatenate([cache_v.reshape(bsz, w, nkv), new_kv[:, :, nkv:]], axis=1)
    pad = ((0, 0), (0, keys_pad - keys), (0, 0))
    k_pad, v_pad = jnp.pad(k_all, pad), jnp.pad(v_all, pad)
    tq = np.arange(t)[:, None]
    sk = np.arange(keys_pad)[None, :]
    dist = tq + w - sk
    in_band = (dist >= 0) & (dist <= WINDOW) & (sk < keys)
    bias_t = _bias_from_dist(dist, in_band, rel_table)
    bias = bias_t.reshape(N_KV_HEADS, rows, keys_pad)
    sink_rows = jnp.repeat(sinks.astype(F32), t).reshape(N_KV_HEADS, rows, 1)
    sb = math.gcd(bsz, ATT_SAMPLE_SEQS)
    blocks = sb * (_nbytes((N_KV_HEADS, rows, nkv), F32) * 2 + 2 * _nbytes((keys_pad, nkv), F32))
    o = pl.pallas_call(
        _attn_sample_kernel,
        grid=(bsz // sb,),
        in_specs=[pl.BlockSpec((sb, N_KV_HEADS, rows, HEAD_DIM), lambda b_: (b_, 0, 0, 0)),
                  pl.BlockSpec((sb, keys_pad, nkv), lambda b_: (b_, 0, 0)),
                  pl.BlockSpec((sb, keys_pad, nkv), lambda b_: (b_, 0, 0)),
                  pl.BlockSpec((N_KV_HEADS, rows, keys_pad), lambda b_: (0, 0, 0)),
                  pl.BlockSpec((N_KV_HEADS, rows, 1), lambda b_: (0, 0, 0))],
        out_specs=pl.BlockSpec((sb, N_KV_HEADS, rows, HEAD_DIM), lambda b_: (b_, 0, 0, 0)),
        out_shape=jax.ShapeDtypeStruct((bsz, N_KV_HEADS, rows, HEAD_DIM), F32),
        compiler_params=_params(1, blocks),
        name="attention_sample",
    )(q, k_pad, v_pad, bias, sink_rows)
    o = o.reshape(bsz, N_KV_HEADS, ATT_GROUP, t, HEAD_DIM)
    o = jnp.transpose(o, (3, 0, 1, 2, 4)).reshape(t * bsz, nq)
    return o.astype(BF16), k_all[:, t:], v_all[:, t:]


def _mm_res_kernel(a_ref, w_ref, x_ref, g_ref, o_ref):
    y = _dot(a_ref[0], w_ref[...].astype(BF16))
    o_ref[0] = x_ref[0] + g_ref[0] * y


def _mm_res_wres_kernel(a_ref, w_ref, x_ref, g_ref, o_ref, wbf_s):
    @pl.when((pl.program_id(1) == 0) & (pl.program_id(2) == 0))
    def _():
        wbf_s[...] = w_ref[...].astype(BF16)

    o_ref[0] = x_ref[0] + g_ref[0] * _dot(a_ref[0], wbf_s[...])


def matmul_gate_residual_wres(a, w_stack, layer, x, gate, tm, tn):
    b, l, k = a.shape
    n = w_stack.shape[2]
    if gate.shape[1] == 1:
        g_spec = pl.BlockSpec((1, 1, tn), lambda j, b_, i: (b_, 0, j))
    else:
        g_spec = pl.BlockSpec((1, tm, tn), lambda j, b_, i: (b_, i, j))
    blocks = _nbytes((tm, k), a.dtype) + _nbytes((k, tn), F32) + 3 * _nbytes((tm, tn), F32)
    scratch = _nbytes((k, tn), BF16) + _nbytes((tm, tn), F32)
    return pl.pallas_call(
        _mm_res_wres_kernel,
        grid=(n // tn, b, l // tm),
        in_specs=[pl.BlockSpec((1, tm, k), lambda j, b_, i: (b_, i, 0)),
                  pl.BlockSpec((None, k, tn), lambda j, b_, i: (layer, 0, j)),
                  pl.BlockSpec((1, tm, tn), lambda j, b_, i: (b_, i, j)),
                  g_spec],
        out_specs=pl.BlockSpec((1, tm, tn), lambda j, b_, i: (b_, i, j)),
        out_shape=jax.ShapeDtypeStruct((b, l, n), F32),
        scratch_shapes=[pltpu.VMEM((k, tn), BF16)],
        compiler_params=_params(3, blocks, scratch),
        name="matmul_gate_residual_wres",
    )(a, w_stack, x, gate)


def matmul_gate_residual(a, w_stack, layer, x, gate, tm, tn):
    b, l, k = a.shape
    n = w_stack.shape[2]
    grows = 1 if gate.shape[1] == 1 else tm
    if gate.shape[1] == 1:
        g_spec = pl.BlockSpec((1, 1, tn), lambda b_, i, j: (b_, 0, j))
    else:
        g_spec = pl.BlockSpec((1, tm, tn), lambda b_, i, j: (b_, i, j))
    blocks = (_nbytes((tm, k), a.dtype) + _nbytes((k, tn), F32) + 2 * _nbytes((tm, tn), F32)
              + _nbytes((grows, tn), F32))
    scratch = _nbytes((k, tn), BF16) + _nbytes((tm, tn), F32)
    return pl.pallas_call(
        _mm_res_kernel,
        grid=(b, l // tm, n // tn),
        in_specs=[pl.BlockSpec((1, tm, k), lambda b_, i, j: (b_, i, 0)),
                  pl.BlockSpec((None, k, tn), lambda b_, i, j: (layer, 0, j)),
                  pl.BlockSpec((1, tm, tn), lambda b_, i, j: (b_, i, j)),
                  g_spec],
        out_specs=pl.BlockSpec((1, tm, tn), lambda b_, i, j: (b_, i, j)),
        out_shape=jax.ShapeDtypeStruct((b, l, n), F32),
        compiler_params=_params(3, blocks, scratch),
        name="matmul_gate_residual",
    )(a, w_stack, x, gate)


def _ffn_conv_rows(u, prev1, prev2, cw_ref, cb_ref):
    return u * cw_ref[2:3, :] + prev1 * cw_ref[1:2, :] + prev2 * cw_ref[0:1, :] + cb_ref[...]


def _ffn_up_prompt_kernel(x_ref, gain_ref, shift_ref, scale_ref, wg_ref, wv_ref, cwg_ref, cwv_ref,
                          cbg_ref, cbv_ref, o_ref, tail_ref, h_s, carry_s, ext_s):
    i = pl.program_id(1)
    j = pl.program_id(2)
    tm = h_s.shape[0]

    @pl.when(j == 0)
    def _():
        h_s[...] = _norm_mod(x_ref[0], gain_ref[...], shift_ref[0], scale_ref[0]).astype(BF16)

    @pl.when(i == 0)
    def _():
        carry_s[j] = jnp.zeros(carry_s.shape[1:], F32)

    halves = ((wg_ref, cwg_ref, cbg_ref), (wv_ref, cwv_ref, cbv_ref))
    w_bf = [w_ref[...].astype(BF16) for w_ref, _, _ in halves]
    for half in range(2):
        ext_s[half, 0:SUBLANES, :] = carry_s[j, half]
    sub = min(FFN_SUB_ROWS, tm)
    for m in range(tm // sub):
        hm = h_s[m * sub:(m + 1) * sub, :]
        r0 = SUBLANES + m * sub
        ys = []
        for half, (_, cw_ref, cb_ref) in enumerate(halves):
            u = _dot(hm, w_bf[half])
            ext_s[half, r0:r0 + sub, :] = u
            ys.append(_ffn_conv_rows(u, ext_s[half, r0 - 1:r0 - 1 + sub, :], ext_s[half, r0 - 2:r0 - 2 + sub, :],
                                     cw_ref, cb_ref))
        o_ref[0, m * sub:(m + 1) * sub, :] = (_silu(ys[0]) * ys[1]).astype(o_ref.dtype)
    for half in range(2):
        tail = ext_s[half, tm:tm + SUBLANES, :]
        carry_s[j, half] = tail
        tail_ref[0, 0, half] = tail


def ffn_up_prompt(x, gain, shift, scale, w_up, conv_w, conv_b, layer, tm, tn=512):
    b, l, d = x.shape
    dff = w_up.shape[2] // 2
    nj = dff // tn
    conv_b = conv_b.reshape(conv_b.shape[0], 1, 2 * dff)
    blocks = (_nbytes((tm, d), F32) + 2 * _nbytes((d, tn), F32) + _nbytes((tm, tn), BF16)
              + 2 * _nbytes((1, d), F32) + 8 * _nbytes((SUBLANES, tn), F32))
    scratch = (_nbytes((tm, d), BF16) + _nbytes((nj, 2, SUBLANES, tn), F32) + 2 * _nbytes((d, tn), BF16)
               + 8 * _nbytes((tm, tn), F32))
    act, tail = pl.pallas_call(
        _ffn_up_prompt_kernel,
        grid=(b, l // tm, nj),
        in_specs=[pl.BlockSpec((1, tm, d), lambda b_, i, j: (b_, i, 0)),
                  pl.BlockSpec((1, d), lambda b_, i, j: (0, 0)),
                  _mod_specs(shift, tm), _mod_specs(scale, tm),
                  pl.BlockSpec((None, d, tn), lambda b_, i, j: (layer, 0, j)),
                  pl.BlockSpec((None, d, tn), lambda b_, i, j: (layer, 0, j + nj)),
                  pl.BlockSpec((None, FFN_CONV, tn), lambda b_, i, j: (layer, 0, j)),
                  pl.BlockSpec((None, FFN_CONV, tn), lambda b_, i, j: (layer, 0, j + nj)),
                  pl.BlockSpec((None, 1, tn), lambda b_, i, j: (layer, 0, j)),
                  pl.BlockSpec((None, 1, tn), lambda b_, i, j: (layer, 0, j + nj))],
        out_specs=[pl.BlockSpec((1, tm, tn), lambda b_, i, j: (b_, i, j)),
                   pl.BlockSpec((1, 1, 2, SUBLANES, tn), lambda b_, i, j: (b_, i, 0, 0, j))],
        out_shape=[jax.ShapeDtypeStruct((b, l, dff), BF16),
                   jax.ShapeDtypeStruct((b, l // tm, 2, SUBLANES, dff), F32)],
        scratch_shapes=[pltpu.VMEM((tm, d), BF16), pltpu.VMEM((nj, 2, SUBLANES, tn), F32),
                        pltpu.VMEM((2, SUBLANES + tm, tn), F32)],
        compiler_params=_params(3, blocks, scratch),
        name="ffn_up_prompt",
    )(x, gain.reshape(1, d), shift, scale, w_up, w_up, conv_w, conv_w, conv_b, conv_b)
    keep = FFN_CONV - 1
    hist = jnp.transpose(tail[:, -1, :, SUBLANES - keep:, :], (0, 2, 1, 3)).reshape(b, keep, 2 * dff)
    return act, hist


def _ffn_up_sample_kernel(x_ref, gain_ref, shift_ref, scale_ref, wg_ref, wv_ref, cwg_ref, cwv_ref,
                          cbg_ref, cbv_ref, hg_ref, hv_ref, o_ref, ng_ref, nv_ref, h_s, *, bsz):
    @pl.when(pl.program_id(0) == 0)
    def _():
        h_s[...] = _norm_mod(x_ref[...], gain_ref[...], shift_ref[...], scale_ref[...]).astype(BF16)

    h = h_s[...]
    rows = h_s.shape[0]
    keep = FFN_CONV - 1
    ys = []
    for w_ref, cw_ref, cb_ref, hist_ref, new_ref in ((wg_ref, cwg_ref, cbg_ref, hg_ref, ng_ref),
                                                     (wv_ref, cwv_ref, cbv_ref, hv_ref, nv_ref)):
        u = _dot(h, w_ref[...].astype(BF16))
        ext = jnp.concatenate([hist_ref[...], u], axis=0)
        prev = [ext[(keep - k) * bsz:(keep - k) * bsz + rows] for k in (1, 2)]
        ys.append(_ffn_conv_rows(u, prev[0], prev[1], cw_ref, cb_ref))
        new_ref[...] = ext[rows:rows + keep * bsz]
    o_ref[...] = (_silu(ys[0]) * ys[1]).astype(o_ref.dtype)


def ffn_up_sample(x, gain, shift, scale, w_up, conv_w, conv_b, layer, hist, tn=512):
    rows, d = x.shape
    dff = w_up.shape[2] // 2
    nj = dff // tn
    keep = FFN_CONV - 1
    bsz = hist.shape[0] // keep
    conv_b = conv_b.reshape(conv_b.shape[0], 1, 2 * dff)
    col = lambda j: (0, j)
    col_hi = lambda j: (0, j + nj)
    lcol = lambda j: (layer, 0, j)
    lcol_hi = lambda j: (layer, 0, j + nj)
    blocks = (_nbytes((rows, d), F32) * 3 + 2 * _nbytes((d, tn), F32) + 5 * _nbytes((rows, tn), F32))
    scratch = _nbytes((rows, d), BF16) + 2 * _nbytes((d, tn), BF16) + 8 * _nbytes((rows, tn), F32)
    act, new_g, new_v = pl.pallas_call(
        functools.partial(_ffn_up_sample_kernel, bsz=bsz),
        grid=(nj,),
        in_specs=[pl.BlockSpec((rows, d), lambda j: (0, 0)),
                  pl.BlockSpec((1, d), lambda j: (0, 0)),
                  pl.BlockSpec((rows, d), lambda j: (0, 0)),
                  pl.BlockSpec((rows, d), lambda j: (0, 0)),
                  pl.BlockSpec((None, d, tn), lcol), pl.BlockSpec((None, d, tn), lcol_hi),
                  pl.BlockSpec((None, FFN_CONV, tn), lcol), pl.BlockSpec((None, FFN_CONV, tn), lcol_hi),
                  pl.BlockSpec((None, 1, tn), lcol), pl.BlockSpec((None, 1, tn), lcol_hi),
                  pl.BlockSpec((keep * bsz, tn), col), pl.BlockSpec((keep * bsz, tn), col_hi)],
        out_specs=[pl.BlockSpec((rows, tn), col), pl.BlockSpec((keep * bsz, tn), col),
                   pl.BlockSpec((keep * bsz, tn), col)],
        out_shape=[jax.ShapeDtypeStruct((rows, dff), BF16),
                   jax.ShapeDtypeStruct((keep * bsz, dff), F32),
                   jax.ShapeDtypeStruct((keep * bsz, dff), F32)],
        scratch_shapes=[pltpu.VMEM((rows, d), BF16)],
        compiler_params=_params(1, blocks, scratch),
        name="ffn_up_sample",
    )(x, gain.reshape(1, d), shift, scale, w_up, w_up, conv_w, conv_w, conv_b, conv_b, hist, hist)
    return act, jnp.concatenate([new_g, new_v], axis=-1)


def _gdn_in_kernel(*refs, n_qk_tiles, n_conv_tiles, bsz):
    if bsz is None:
        (x_ref, gain_ref, shift_ref, scale_ref, w_ref, wba_ref, alog_ref, dtb_ref, cw_ref,
         o_ref, bg_ref, raw_ref, h_s, carry_s) = refs
    else:
        (x_ref, gain_ref, shift_ref, scale_ref, w_ref, wba_ref, alog_ref, dtb_ref, cw_ref, hist_ref,
         o_ref, bg_ref, raw_ref, h_s) = refs
    i = pl.program_id(1)
    j = pl.program_id(2)
    tm = h_s.shape[0]
    width = GDN_CONV

    @pl.when(j == 0)
    def _():
        h = _norm_mod(x_ref[0], gain_ref[...], shift_ref[0], scale_ref[0]).astype(BF16)
        h_s[...] = h
        y = _dot_nt(h, wba_ref[...].astype(BF16))
        lane = lax.broadcasted_iota(jnp.int32, y.shape, 1) % LANES
        beta = 1.0 / (1.0 + jnp.exp(-y))
        a = y + dtb_ref[...]
        softplus = jnp.maximum(a, 0.0) + jnp.log1p(jnp.exp(-jnp.abs(a)))
        g = -jnp.exp(alog_ref[...]) * softplus
        bg_ref[0] = jnp.where(lane < GDN_VH_PER_GROUP, beta, jnp.where(lane < 2 * GDN_VH_PER_GROUP, g, 0.0))

    sub = min(GDN_IN_SUB_ROWS, tm)
    row8 = lax.broadcasted_iota(jnp.int32, (SUBLANES, 1), 0)
    assert width == 4

    def shifted(x, prev, k):
        top = jnp.where(row8 < k, pltpu.roll(prev, k, axis=0), pltpu.roll(x[0:SUBLANES], k, axis=0))
        return jnp.concatenate([top, pltpu.roll(x, k, axis=0)[SUBLANES:]], axis=0)

    def conv_tile(l2_scale):
        w_bf = w_ref[...].astype(BF16)
        jc = jnp.minimum(j, n_conv_tiles - 1)
        if bsz is None:
            @pl.when(i == 0)
            def _():
                carry_s[jc] = jnp.zeros(carry_s.shape[1:], F32)
            prev_u, prev_pair = carry_s[jc, 0], carry_s[jc, 1]
        for m in range(tm // sub):
            rs = slice(m * sub, (m + 1) * sub)
            u = _dot_nt(h_s[rs, :], w_bf)
            if bsz is None:
                u1 = shifted(u, prev_u, 1)
                pair = u * cw_ref[1:2, :] + u1 * cw_ref[0:1, :]
                y = u * cw_ref[3:4, :] + u1 * cw_ref[2:3, :] + shifted(pair, prev_pair, 2)
                prev_u, prev_pair = u[sub - SUBLANES:sub], pair[sub - SUBLANES:sub]
            else:
                y = u * cw_ref[width - 1:width, :]
                ext = jnp.concatenate([hist_ref[...], u], axis=0)
                for k in range(1, width):
                    wk = cw_ref[width - 1 - k:width - k, :]
                    y = y + ext[(width - 1 - k) * bsz:(width - 1 - k) * bsz + sub] * wk
                raw_ref[...] = ext[sub:sub + (width - 1) * bsz]
            y = _silu(y)
            if l2_scale is not None:
                heads = [y[:, hd * LANES:(hd + 1) * LANES] for hd in range(y.shape[1] // LANES)]
                y = jnp.concatenate(
                    [yh * (lax.rsqrt(jnp.sum(yh * yh, axis=-1, keepdims=True) + NORM_EPS) * l2_scale)
                     for yh in heads], axis=1)
            o_ref[0, rs, :] = y.astype(o_ref.dtype)
        if bsz is None:
            carry_s[jc, 0] = prev_u
            carry_s[jc, 1] = prev_pair
            raw_ref[0, 0] = prev_u

    @pl.when(j < n_qk_tiles)
    def _():
        conv_tile(jnp.where(j < n_qk_tiles // 2, GDN_DK ** -0.5, 1.0).astype(F32))

    @pl.when((j >= n_qk_tiles) & (j < n_conv_tiles))
    def _():
        conv_tile(None)

    @pl.when(j >= n_conv_tiles)
    def _():
        w_bf = w_ref[...].astype(BF16)
        for m in range(tm // sub):
            rs = slice(m * sub, (m + 1) * sub)
            o_ref[0, rs, :] = _silu(_dot_nt(h_s[rs, :], w_bf)).astype(o_ref.dtype)


def _group_lane_layout(vec_b, vec_a):
    lead = vec_b.shape[:-1]
    vb = vec_b.reshape(*lead, GDN_HEAD_GROUPS, GDN_VH_PER_GROUP)
    va = vec_a.reshape(*lead, GDN_HEAD_GROUPS, GDN_VH_PER_GROUP)
    pad = jnp.zeros((*lead, GDN_HEAD_GROUPS, LANES - 2 * GDN_VH_PER_GROUP), vec_b.dtype)
    return jnp.concatenate([vb, va, pad], axis=-1).reshape(*lead, GDN_HEAD_GROUPS * LANES)


def gdn_in_projection(x, gain, shift, scale, w_in, conv_w, a_log, dt_bias, tm, conv_hist=None, tn=512):
    b, l, d = x.shape
    n_main = GDN_CONV_DIM + GDN_VAL_DIM
    w_t = jnp.transpose(w_in)
    wba = _group_lane_layout(w_in[:, n_main:n_main + GDN_V_HEADS], w_in[:, n_main + GDN_V_HEADS:])
    wba_t = jnp.transpose(wba)
    zeros = jnp.zeros((1, GDN_V_HEADS), F32)
    alog = _group_lane_layout(zeros, a_log.reshape(1, -1).astype(F32))
    dtb = _group_lane_layout(zeros, dt_bias.reshape(1, -1).astype(F32))
    nbg = wba_t.shape[0]
    n_conv_tiles = GDN_CONV_DIM // tn
    n_qk_tiles = 2 * GDN_KEY_DIM // tn
    conv_col = lambda b_, i, j: (0, jnp.minimum(j, n_conv_tiles - 1))
    in_specs = [pl.BlockSpec((1, tm, d), lambda b_, i, j: (b_, i, 0)),
                pl.BlockSpec((1, d), lambda b_, i, j: (0, 0)),
                _mod_specs(shift, tm), _mod_specs(scale, tm),
                pl.BlockSpec((tn, d), lambda b_, i, j: (j, 0)),
                pl.BlockSpec((nbg, d), lambda b_, i, j: (0, 0)),
                pl.BlockSpec((1, nbg), lambda b_, i, j: (0, 0)),
                pl.BlockSpec((1, nbg), lambda b_, i, j: (0, 0)),
                pl.BlockSpec((GDN_CONV, tn), conv_col)]
    args = [x, gain.reshape(1, d), shift, scale, w_t, wba_t, alog, dtb, conv_w]
    scratch_shapes = [pltpu.VMEM((tm, d), BF16)]
    blocks = (_nbytes((tm, d), F32) + _nbytes((tn, d), F32) + _nbytes((tm, tn), BF16)
              + _nbytes((nbg, d), F32) + _nbytes((tm, nbg), F32) + 2 * _nbytes((1, d), F32))
    scratch = _nbytes((tm, d), BF16) + _nbytes((tn, d), BF16) + 6 * _nbytes((tm, nbg), F32)
    if conv_hist is None:
        bsz = None
        raw_spec = pl.BlockSpec((1, 1, SUBLANES, tn), lambda b_, i, j: (b_, i, 0, jnp.minimum(j, n_conv_tiles - 1)))
        raw_shape = jax.ShapeDtypeStruct((b, l // tm, SUBLANES, GDN_CONV_DIM), F32)
        scratch_shapes.append(pltpu.VMEM((n_conv_tiles, 2, SUBLANES, tn), F32))
    else:
        assert b == 1 and l == tm <= GDN_IN_SUB_ROWS
        hist_rows = conv_hist.shape[0]
        bsz = hist_rows // (GDN_CONV - 1)
        args.append(conv_hist)
        in_specs.append(pl.BlockSpec((hist_rows, tn), conv_col))
        raw_spec = pl.BlockSpec((hist_rows, tn), conv_col)
        raw_shape = jax.ShapeDtypeStruct((hist_rows, GDN_CONV_DIM), F32)
        blocks += 2 * _nbytes((hist_rows, tn), F32)
    return pl.pallas_call(
        functools.partial(_gdn_in_kernel, n_qk_tiles=n_qk_tiles, n_conv_tiles=n_conv_tiles, bsz=bsz),
        grid=(b, l // tm, n_main // tn),
        in_specs=in_specs,
        out_specs=[pl.BlockSpec((1, tm, tn), lambda b_, i, j: (b_, i, j)),
                   pl.BlockSpec((1, tm, nbg), lambda b_, i, j: (b_, i, 0)),
                   raw_spec],
        out_shape=[jax.ShapeDtypeStruct((b, l, n_main), BF16),
                   jax.ShapeDtypeStruct((b, l, nbg), F32),
                   raw_shape],
        scratch_shapes=scratch_shapes,
        compiler_params=_params(3, blocks, scratch),
        name="gdn_in_projection",
    )(*args)


def _gdn_core_kernel(q_ref, k_ref, v_ref, zg_ref, bg_ref, onorm_ref, s0_ref, o_ref, s1_ref,
                     s_s, gc_s, tdec_s, mo_s, *, seq_rows):
    r = pl.program_id(2)
    n_r = pl.num_programs(2)
    rows = q_ref.shape[1]
    chunk = GDN_CHUNK
    vpg = GDN_VH_PER_GROUP
    rep = GDN_V_HEADS // GDN_K_HEADS
    n_kh = GDN_KH_PER_GROUP
    seg = chunk if seq_rows is None else seq_rows
    n_seq = chunk // seg

    def load_states():
        for s in range(n_seq):
            for kh in range(n_kh):
                s_s[s * n_kh + kh] = jnp.concatenate([s0_ref[s, kh * rep + e] for e in range(rep)], axis=1)

    if seq_rows is None:
        pl.when(r == 0)(load_states)
    else:
        load_states()

    def seg_last(x):
        if n_seq == 1:
            return x[chunk - 1:chunk, :]
        blocks = x.reshape(n_seq, seg, x.shape[1])
        return jnp.broadcast_to(blocks[:, seg - 1:seg, :], blocks.shape).reshape(x.shape)

    n_chunks = rows // chunk
    pw = GDN_PACK * chunk
    n_packs = vpg // GDN_PACK
    ri = lax.broadcasted_iota(jnp.int32, (chunk, pw), 0)
    li = lax.broadcasted_iota(jnp.int32, (chunk, pw), 1) % chunk
    seg_shift = int(math.log2(seg))
    same_seq = (ri >> seg_shift) == (li >> seg_shift)
    tril_p = (ri >= li) & same_seq
    strict_p = (ri > li) & same_seq
    eye_p = (ri == li).astype(F32)
    n_levels = seg_shift
    off_masks = [((ri >> (lvl + 1)) == (li >> (lvl + 1))) & ((ri >> lvl) != (li >> lvl))
                 for lvl in range(n_levels)]
    bd_rows = lax.broadcasted_iota(jnp.int32, (pw, pw), 0) // chunk
    bd_cols = lax.broadcasted_iota(jnp.int32, (pw, pw), 1) // chunk
    bd_mask = (bd_rows == bd_cols).astype(BF16)
    row_c = lax.broadcasted_iota(jnp.int32, (chunk, LANES), 0)
    lane_lo = lax.broadcasted_iota(jnp.int32, (chunk, LANES), 1) < chunk

    def block_diag(xp):
        return jnp.concatenate([xp.astype(BF16)] * GDN_PACK, axis=0) * bd_mask

    def pack_cols(arr, first_col):
        tiles = []
        for t in range(pw // LANES):
            even = jnp.broadcast_to(arr[:, first_col + 2 * t:first_col + 2 * t + 1], (chunk, LANES))
            odd = jnp.broadcast_to(arr[:, first_col + 2 * t + 1:first_col + 2 * t + 2], (chunk, LANES))
            tiles.append(jnp.where(lane_lo, even, odd))
        return jnp.concatenate(tiles, axis=1)

    for c0 in range(0, n_chunks, GDN_PHASE_A_CHUNKS):
        group = [(c, p) for c in range(c0, min(c0 + GDN_PHASE_A_CHUNKS, n_chunks)) for p in range(n_packs)]
        a_list, qkd, brow, dec = {}, {}, {}, {}
        for c in range(c0, min(c0 + GDN_PHASE_A_CHUNKS, n_chunks)):
            rs = slice(c * chunk, (c + 1) * chunk)
            bg = bg_ref[0, rs, :]
            cum = bg
            shift = 1
            while shift < seg:
                cum = cum + jnp.where((row_c & (seg - 1)) >= shift, pltpu.roll(cum, shift, axis=0), 0.0)
                shift *= 2
            gc_s[rs, :] = cum
            gq = []
            for kh in range(GDN_KH_PER_GROUP):
                ksl = slice(kh * LANES, (kh + 1) * LANES)
                k = k_ref[0, rs, ksl]
                kq = jnp.concatenate([k, q_ref[0, rs, ksl]], axis=0)
                kk = jnp.concatenate([k, k], axis=0)
                gq.append(_dot_nt(kq, kk))
            for p in range(n_packs):
                khs = [(p * GDN_PACK + 2 * t) // rep for t in range(pw // LANES)]
                gram = jnp.concatenate([gq[kh][:chunk] for kh in khs], axis=1)
                qk = jnp.concatenate([gq[kh][chunk:] for kh in khs], axis=1)
                beta_p = pack_cols(bg, p * GDN_PACK)
                gcol_p = pack_cols(cum, vpg + p * GDN_PACK)
                grow_p = jnp.sum(gcol_p * eye_p, axis=0, keepdims=True)
                decay = jnp.exp(jnp.where(tril_p, gcol_p - grow_p, NEG_INF))
                a_list[(c, p)] = jnp.where(strict_p, gram * beta_p * decay, 0.0)
                qkd[(c, p)] = (qk * decay).astype(BF16)
                brow[(c, p)] = jnp.sum(beta_p * eye_p, axis=0, keepdims=True)
                dec[(c, p)] = jnp.exp(seg_last(gcol_p) - gcol_p)
        inv = {cp: eye_p - jnp.where(off_masks[0], a_list[cp], 0.0) for cp in group}
        for lvl in range(1, n_levels):
            w = {cp: _dot(jnp.where(off_masks[lvl], a_list[cp], 0.0).astype(BF16), block_diag(inv[cp]))
                 for cp in group}
            inv = {cp: inv[cp] - _dot(inv[cp].astype(BF16), block_diag(w[cp])) for cp in group}
        for cp in group:
            t_beta = inv[cp] * brow[cp]
            tdec_s[cp[0], cp[1]] = block_diag(dec[cp] * t_beta)
            mo_s[cp[0], cp[1]] = block_diag(_dot(qkd[cp], block_diag(t_beta)))

    def chunk_body(c, carry):
        rs = pl.ds(pl.multiple_of(c * chunk, chunk), chunk)
        cum = gc_s[rs, :]
        eg, g_last, ks, qs = [], [], [], []
        for kh in range(n_kh):
            ksl = slice(kh * LANES, (kh + 1) * LANES)
            kq = jnp.concatenate([k_ref[0, rs, ksl], q_ref[0, rs, ksl]], axis=0)
            per_seq = [_dot(kq, s_s[s * n_kh + kh].astype(BF16)) for s in range(n_seq)]
            kq_s = jnp.concatenate([per_seq[s][half * chunk + s * seg:half * chunk + (s + 1) * seg]
                                    for half in range(2) for s in range(n_seq)], axis=0)
            for e in range(rep):
                hv = kh * rep + e
                gcol = cum[:, vpg + hv:vpg + hv + 1]
                g_last.append([gcol[(s + 1) * seg - 1:(s + 1) * seg, :] for s in range(n_seq)])
                eg.append(jnp.exp(gcol))
                ks.append(kq_s[:chunk, e * LANES:(e + 1) * LANES] * eg[hv])
                qs.append(kq_s[chunk:, e * LANES:(e + 1) * LANES] * eg[hv])
        v_dec, o_intra = [], []
        for p in range(n_packs):
            heads = range(p * GDN_PACK, (p + 1) * GDN_PACK)
            rhs = jnp.concatenate([v_ref[0, rs, hv * LANES:(hv + 1) * LANES].astype(F32) - ks[hv]
                                   for hv in heads], axis=0).astype(BF16)
            vd = _dot(tdec_s[c, p], rhs)
            oi = _dot(mo_s[c, p], rhs)
            for j in range(GDN_PACK):
                v_dec.append(vd[j * chunk:(j + 1) * chunk])
                o_intra.append(oi[j * chunk:(j + 1) * chunk])
        for hv in range(vpg):
            vsl = slice(hv * LANES, (hv + 1) * LANES)
            o = qs[hv] + o_intra[hv]
            on = o * lax.rsqrt(jnp.mean(o * o, axis=-1, keepdims=True) + NORM_EPS) * onorm_ref[...]
            o_ref[0, rs, vsl] = (on * zg_ref[0, rs, vsl].astype(F32)).astype(o_ref.dtype)
        lane_v = lax.broadcasted_iota(jnp.int32, (1, rep * LANES), 1) // LANES
        row_seq = lax.broadcasted_iota(jnp.int32, (chunk, 1), 0) >> seg_shift
        for kh in range(n_kh):
            heads = range(kh * rep, (kh + 1) * rep)
            k_bf = k_ref[0, rs, kh * LANES:(kh + 1) * LANES]
            vd_pair = jnp.concatenate([v_dec[hv] for hv in heads], axis=1)
            for s in range(n_seq):
                vd_s = vd_pair if n_seq == 1 else jnp.where(row_seq == s, vd_pair, 0.0)
                ds = _dot_tn(k_bf, vd_s.astype(BF16))
                keep = jnp.exp(g_last[kh * rep][s])
                for e in range(1, rep):
                    keep = jnp.where(lane_v == e, jnp.exp(g_last[kh * rep + e][s]), keep)
                s_s[s * n_kh + kh] = s_s[s * n_kh + kh] * keep + ds
        return carry

    lax.fori_loop(0, n_chunks, chunk_body, 0)

    def store_states():
        for s in range(n_seq):
            for hv in range(vpg):
                s1_ref[s, hv] = s_s[s * n_kh + hv // rep][:, (hv % rep) * LANES:(hv % rep + 1) * LANES]

    if seq_rows is None:
        pl.when(r == n_r - 1)(store_states)
    else:
        store_states()


def gdn_core(qkvz, bg, s0, out_norm, rows, seq_rows=None):
    b, l, _ = qkvz.shape
    n_state = 1 if seq_rows is None else GDN_CHUNK // seq_rows
    if seq_rows is None:
        state_idx = lambda b_, g, r: (b_, g, 0, 0)
    else:
        assert b == 1 and rows == GDN_CHUNK
        state_idx = lambda b_, g, r: (r, g, 0, 0)
    hg = GDN_HEAD_GROUPS
    qw = GDN_KEY_DIM // hg
    vw = GDN_VAL_DIM // hg
    kq = GDN_KEY_DIM // qw
    kv = 2 * GDN_KEY_DIM // vw
    kz = GDN_CONV_DIM // vw
    vpg = GDN_VH_PER_GROUP
    blocks = (2 * _nbytes((rows, qw), BF16) + 3 * _nbytes((rows, vw), BF16) + _nbytes((rows, LANES), F32)
              + 2 * _nbytes((n_state, vpg, GDN_DK, GDN_DV), F32))
    n_chunks = rows // GDN_CHUNK
    n_packs = vpg // GDN_PACK
    pw = GDN_PACK * GDN_CHUNK
    scratch = (_nbytes((n_state, vpg, GDN_DK, GDN_DV), F32) + _nbytes((rows, LANES), F32)
               + 2 * _nbytes((n_chunks, n_packs, pw, pw), BF16) + 6 * _nbytes((rows, vw), F32))
    return pl.pallas_call(
        functools.partial(_gdn_core_kernel, seq_rows=seq_rows),
        grid=(b, hg, l // rows),
        in_specs=[pl.BlockSpec((1, rows, qw), lambda b_, g, r: (b_, r, g)),
                  pl.BlockSpec((1, rows, qw), lambda b_, g, r: (b_, r, kq + g)),
                  pl.BlockSpec((1, rows, vw), lambda b_, g, r: (b_, r, kv + g)),
                  pl.BlockSpec((1, rows, vw), lambda b_, g, r: (b_, r, kz + g)),
                  pl.BlockSpec((1, rows, LANES), lambda b_, g, r: (b_, r, g)),
                  pl.BlockSpec((1, GDN_DV), lambda b_, g, r: (0, 0)),
                  pl.BlockSpec((n_state, vpg, GDN_DK, GDN_DV), state_idx)],
        out_specs=[pl.BlockSpec((1, rows, vw), lambda b_, g, r: (b_, r, g)),
                   pl.BlockSpec((n_state, vpg, GDN_DK, GDN_DV), state_idx)],
        out_shape=[jax.ShapeDtypeStruct((b, l, GDN_VAL_DIM), BF16),
                   jax.ShapeDtypeStruct(s0.shape, F32)],
        scratch_shapes=[pltpu.VMEM((n_state * GDN_KH_PER_GROUP, GDN_DK,
                                    (GDN_V_HEADS // GDN_K_HEADS) * GDN_DV), F32),
                        pltpu.VMEM((rows, LANES), F32),
                        pltpu.VMEM((n_chunks, n_packs, pw, pw), BF16),
                        pltpu.VMEM((n_chunks, n_packs, pw, pw), BF16)],
        compiler_params=_params(3, blocks, scratch),
        name="gdn_core",
    )(qkvz, qkvz, qkvz, qkvz, bg, out_norm.reshape(1, GDN_DV).astype(F32), s0)


def _tile_rows(l, cap):
    t = min(l, cap)
    while l % t:
        t //= 2
    return t


def _to_time_major(a):
    return jnp.transpose(a, (1, 0, 2)).reshape(a.shape[0] * a.shape[1], a.shape[2])


def _from_time_major(a, bsz):
    return jnp.transpose(a.reshape(a.shape[0] // bsz, bsz, a.shape[1]), (1, 0, 2))


def _trunk(x, mod, states, P, sample):
    bsz, l, d = x.shape
    tm = _tile_rows(l, ROW_TILE_CAP)
    out_proj = matmul_gate_residual if sample else matmul_gate_residual_wres
    new = {}
    depth = mod.shape[0]
    for layer in range(depth):
        if sample:
            parts = [mod[layer][None, :, k * d:(k + 1) * d] for k in range(6)]
        else:
            parts = [mod[layer][:, None, k * d:(k + 1) * d] for k in range(6)]
        sh_m, sc_m, g_m, sh_f, sc_f, g_f = parts
        i = layer // 2
        if layer % 2 == 0:
            qkv = qkv_projection(x, P['norm_mix'][layer], sh_m, sc_m, P['w_attn_qkv'][i],
                                 P['attn_q_norm'][i], P['attn_k_norm'][i], tm)
            nq, nkv = N_HEADS * HEAD_DIM, N_KV_HEADS * HEAD_DIM
            if sample:
                o, k_win, v_win = attention_sample(qkv[0], states['win_k'][i], states['win_v'][i],
                                                   P['attn_sinks'][i], P['rel_bias_table'])
                o = o[None]
                w_len = k_win.shape[1]
                new.setdefault('win_k', []).append(k_win.reshape(-1, w_len, N_KV_HEADS, HEAD_DIM))
                new.setdefault('win_v', []).append(v_win.reshape(-1, w_len, N_KV_HEADS, HEAD_DIM))
            else:
                o = attention_prompt(qkv, P['attn_sinks'][i], P['rel_bias_table'])
                keep = min(WINDOW, PAST_LEN)
                new.setdefault('win_k', []).append(
                    qkv[:, l - keep:, nq:nq + nkv].reshape(bsz, keep, N_KV_HEADS, HEAD_DIM))
                new.setdefault('win_v', []).append(
                    qkv[:, l - keep:, nq + nkv:].reshape(bsz, keep, N_KV_HEADS, HEAD_DIM))
            x = matmul_gate_residual(o, P['w_attn_o'], i, x, g_m, _tile_rows(l, 2 * ROW_TILE_CAP), 512)
        else:
            keep = GDN_CONV - 1
            gdn_args = (x, P['norm_mix'][layer], sh_m, sc_m, P['w_gdn_in'][i], P['gdn_conv_w'][i],
                        P['gdn_a_log'][i], P['gdn_dt_bias'][i], tm)
            if sample:
                seqs = states['gdn'].shape[1]
                t = l // seqs
                qkvz, bg, hist1 = gdn_in_projection(
                    *gdn_args, conv_hist=_to_time_major(states['gdn_conv'][i].astype(F32)), tn=2 * 512)
                new.setdefault('gdn_conv', []).append(_from_time_major(hist1, seqs))
                seq_rows = -(-t // SUBLANES) * SUBLANES
                per_seq = lambda a: jnp.pad(_from_time_major(a, seqs), ((0, 0), (0, seq_rows - t), (0, 0))
                                            ).reshape(1, seqs * seq_rows, a.shape[-1])
                o_seq, s1 = gdn_core(per_seq(qkvz[0]), per_seq(bg[0]), states['gdn'][i].astype(F32),
                                     P['gdn_out_norm'][i], GDN_CHUNK, seq_rows=seq_rows)
                o = _to_time_major(o_seq.reshape(seqs, seq_rows, GDN_VAL_DIM)[:, :t])[None]
            else:
                qkvz, bg, raw = gdn_in_projection(*gdn_args)
                s0 = jnp.zeros((bsz, GDN_V_HEADS, GDN_DK, GDN_DV), F32)
                o, s1 = gdn_core(qkvz, bg, s0, P['gdn_out_norm'][i], _tile_rows(l, GDN_ROW_TILE_CAP))
                new.setdefault('gdn_conv', []).append(raw[:, -1, SUBLANES - keep:, :])
            new.setdefault('gdn', []).append(s1)
            x = out_proj(o, P['w_gdn_out'], i, x, g_m, tm, 512)
        if sample:
            seqs = states['ffn_conv'].shape[1]
            act, f1 = ffn_up_sample(x[0], P['norm_ffn'][layer], sh_f[0], sc_f[0], P['w_ffn_up'],
                                    P['ffn_conv_w'], P['ffn_conv_b'], layer,
                                    _to_time_major(states['ffn_conv'][layer].astype(F32)))
            act, f1 = act[None], _from_time_major(f1, seqs)
        else:
            act, f1 = ffn_up_prompt(x, P['norm_ffn'][layer], sh_f, sc_f, P['w_ffn_up'],
                                    P['ffn_conv_w'], P['ffn_conv_b'], layer, tm)
        new.setdefault('ffn_conv', []).append(f1)
        if sample:
            x = matmul_gate_residual(act, P['w_ffn_down'], layer, x, g_f, tm, 512)
        else:
            x = matmul_gate_residual_wres(act, P['w_ffn_down'], layer, x, g_f, _tile_rows(l, 512), 512)
    return x, {k: jnp.stack(v) for k, v in new.items()}


def kernel(x_prompt, x_sample, c_prompt, c_sample, cache_win_k, cache_win_v, state_gdn, state_gdn_conv,
           state_ffn_conv, rel_bias_table, w_ada, b_ada, norm_mix, norm_ffn, w_attn_qkv, attn_q_norm,
           attn_k_norm, attn_sinks, w_attn_o, w_gdn_in, gdn_conv_w, gdn_a_log, gdn_dt_bias, gdn_out_norm,
           w_gdn_out, w_ffn_up, ffn_conv_w, ffn_conv_b, w_ffn_down):
    P = dict(rel_bias_table=rel_bias_table, norm_mix=norm_mix, norm_ffn=norm_ffn, w_attn_qkv=w_attn_qkv,
             attn_q_norm=attn_q_norm, attn_k_norm=attn_k_norm, attn_sinks=attn_sinks, w_attn_o=w_attn_o,
             w_gdn_in=w_gdn_in, gdn_conv_w=gdn_conv_w, gdn_a_log=gdn_a_log, gdn_dt_bias=gdn_dt_bias,
             gdn_out_norm=gdn_out_norm, w_gdn_out=w_gdn_out, w_ffn_up=w_ffn_up, ffn_conv_w=ffn_conv_w,
             ffn_conv_b=ffn_conv_b, w_ffn_down=w_ffn_down)
    bp = x_prompt.shape[0]
    bs, ts, d = x_sample.shape
    n_c = bp + bs
    c_rows = -(-n_c // SUBLANES) * SUBLANES
    c_all = jnp.pad(jnp.concatenate([c_prompt, c_sample], axis=0), ((0, c_rows - n_c), (0, 0)))
    mod = ada_modulation(c_all, w_ada, b_ada)
    mod_prompt = mod[:, :bp]
    mod_sample = jnp.tile(mod[:, bp:n_c], (1, ts, 1))

    y_p, new_p = _trunk(x_prompt, mod_prompt, None, P, sample=False)
    states = dict(win_k=cache_win_k, win_v=cache_win_v, gdn=state_gdn, gdn_conv=state_gdn_conv,
                  ffn_conv=state_ffn_conv)
    y_s, new_s = _trunk(_to_time_major(x_sample)[None], mod_sample, states, P, sample=True)
    y_s = _from_time_major(y_s[0], bs)
    return (y_p, y_s, new_p['win_k'], new_p['win_v'], new_s['win_k'], new_s['win_v'],
            new_p['gdn'], new_s['gdn'], new_p['gdn_conv'], new_s['gdn_conv'],
            new_p['ffn_conv'], new_s['ffn_conv'])
```

```python
import functools
import math

import numpy as np
import jax
import jax.numpy as jnp
from jax import lax
from jax.experimental import pallas as pl
from jax.experimental.pallas import tpu as pltpu

HEAD_DIM = 64
N_HEADS = 32
N_KV_HEADS = 4
ATT_GROUP = N_HEADS // N_KV_HEADS
WINDOW = 128
ATT_BLOCK = 128
ATT_SAMPLE_SEQS = 8
N_BUCKETS = 32
MAX_DISTANCE = 128
NEG_INF = -1e30
PAST_LEN = 16384

GDN_K_HEADS = 16
GDN_V_HEADS = 32
GDN_DK = 128
GDN_DV = 128
GDN_KEY_DIM = GDN_K_HEADS * GDN_DK
GDN_VAL_DIM = GDN_V_HEADS * GDN_DV
GDN_CONV_DIM = 2 * GDN_KEY_DIM + GDN_VAL_DIM
GDN_CONV = 4
GDN_CHUNK = 64
GDN_HEAD_GROUPS = 4
GDN_VH_PER_GROUP = GDN_V_HEADS // GDN_HEAD_GROUPS
GDN_KH_PER_GROUP = GDN_K_HEADS // GDN_HEAD_GROUPS
GDN_PACK = 4
GDN_PHASE_A_CHUNKS = 4

FFN_CONV = 3
NORM_EPS = 1e-6

LANES = 128
SUBLANES = 8
VMEM_CAP_BYTES = 60 * 1024 * 1024
VMEM_SLACK_BYTES = 8 * 1024 * 1024

ROW_TILE_CAP = 1024
GDN_IN_SUB_ROWS = 256
FFN_SUB_ROWS = 512
GDN_ROW_TILE_CAP = 1024

BF16 = jnp.bfloat16
F32 = jnp.float32


def _vmem_limit(block_bytes, scratch_bytes=0):
    est = 2 * int(block_bytes) + int(scratch_bytes) + VMEM_SLACK_BYTES
    return int(min(max(est, 16 * 1024 * 1024), VMEM_CAP_BYTES))


def _params(n_grid, block_bytes, scratch_bytes=0):
    return pltpu.CompilerParams(
        dimension_semantics=("arbitrary",) * n_grid,
        vmem_limit_bytes=_vmem_limit(block_bytes, scratch_bytes))


def _nbytes(shape, dtype):
    return int(np.prod(shape)) * jnp.dtype(dtype).itemsize


def _silu(x):
    return x * (1.0 / (1.0 + jnp.exp2(x * (-math.log2(math.e)))))


def _dot(a, b):
    return jnp.dot(a, b, preferred_element_type=F32)


def _dot_nt(a, b):
    return lax.dot_general(a, b, (((1,), (1,)), ((), ())), preferred_element_type=F32)


def _dot_tn(a, b):
    return lax.dot_general(a, b, (((0,), (0,)), ((), ())), preferred_element_type=F32)


def _norm_mod(x, gain, shift, scale):
    ms = jnp.mean(x * x, axis=-1, keepdims=True)
    y = x * lax.rsqrt(ms + NORM_EPS) * gain
    return y * (1.0 + scale) + shift


def _mod_specs(shift, tm):
    d = shift.shape[-1]
    if shift.shape[1] == 1:
        return pl.BlockSpec((1, 1, d), lambda b, i, j: (b, 0, 0))
    return pl.BlockSpec((1, tm, d), lambda b, i, j: (b, i, 0))


def _ada_kernel(c_ref, w_ref, b_ref, o_ref):
    a = _silu(c_ref[...]).astype(BF16)
    o_ref[0] = _dot(a, w_ref[0].astype(BF16)) + b_ref[0]


def ada_modulation(c_all, w_ada, b_ada, tn=1024):
    rows, d = c_all.shape
    depth, _, n = w_ada.shape
    blocks = _nbytes((rows, d), F32) + _nbytes((d, tn), F32) + _nbytes((rows, tn), F32)
    return pl.pallas_call(
        _ada_kernel,
        grid=(depth, n // tn),
        in_specs=[pl.BlockSpec((rows, d), lambda l, j: (0, 0)),
                  pl.BlockSpec((1, d, tn), lambda l, j: (l, 0, j)),
                  pl.BlockSpec((1, 1, tn), lambda l, j: (l, 0, j))],
        out_specs=pl.BlockSpec((1, rows, tn), lambda l, j: (l, 0, j)),
        out_shape=jax.ShapeDtypeStruct((depth, rows, n), F32),
        compiler_params=_params(2, blocks, _nbytes((d, tn), BF16)),
        name="ada_modulation",
    )(c_all, w_ada, b_ada.reshape(depth, 1, n))


def _qkv_kernel(x_ref, gain_ref, shift_ref, scale_ref, w_ref, hgain_ref, hflag_ref, gmat_ref,
                o_ref, h_s):
    @pl.when(pl.program_id(2) == 0)
    def _():
        h_s[...] = _norm_mod(x_ref[0], gain_ref[...], shift_ref[0], scale_ref[0]).astype(BF16)

    y = _dot(h_s[...], w_ref[...].astype(BF16))
    ms = _dot((y * y).astype(BF16), gmat_ref[...])
    yn = y * lax.rsqrt(ms + NORM_EPS) * hgain_ref[...]
    o_ref[0] = jnp.where(hflag_ref[...] > 0.0, yn, y)


def qkv_projection(x, gain, shift, scale, w, q_gain, k_gain, tm, tn=512):
    b, l, d = x.shape
    n = w.shape[1]
    nq, nk = N_HEADS * HEAD_DIM, N_KV_HEADS * HEAD_DIM
    hgain = jnp.concatenate([jnp.tile(q_gain, N_HEADS), jnp.tile(k_gain, N_KV_HEADS),
                             jnp.ones((nk,), F32)]).reshape(1, n)
    hflag = jnp.concatenate([jnp.ones((nq + nk,), F32), jnp.zeros((nk,), F32)]).reshape(1, n)
    gidx = np.arange(tn) // HEAD_DIM
    gmat = jnp.asarray((gidx[:, None] == gidx[None, :]).astype(np.float32) / HEAD_DIM, BF16)
    blocks = (_nbytes((tm, d), F32) + _nbytes((d, tn), F32) + _nbytes((tm, tn), F32)
              + 2 * _nbytes((shift.shape[1] == 1 and 1 or tm, d), F32) + _nbytes((tn, tn), BF16))
    scratch = _nbytes((tm, d), BF16) + _nbytes((d, tn), BF16) + 4 * _nbytes((tm, tn), F32)
    return pl.pallas_call(
        _qkv_kernel,
        grid=(b, l // tm, n // tn),
        in_specs=[pl.BlockSpec((1, tm, d), lambda b_, i, j: (b_, i, 0)),
                  pl.BlockSpec((1, d), lambda b_, i, j: (0, 0)),
                  _mod_specs(shift, tm), _mod_specs(scale, tm),
                  pl.BlockSpec((d, tn), lambda b_, i, j: (0, j)),
                  pl.BlockSpec((1, tn), lambda b_, i, j: (0, j)),
                  pl.BlockSpec((1, tn), lambda b_, i, j: (0, j)),
                  pl.BlockSpec((tn, tn), lambda b_, i, j: (0, 0))],
        out_specs=pl.BlockSpec((1, tm, tn), lambda b_, i, j: (b_, i, j)),
        out_shape=jax.ShapeDtypeStruct((b, l, n), F32),
        scratch_shapes=[pltpu.VMEM((tm, d), BF16)],
        compiler_params=_params(3, blocks, scratch),
        name="qkv_projection",
    )(x, gain.reshape(1, d), shift, scale, w, hgain, hflag, gmat)


def _attn_prompt_kernel(sink_ref, q_ref, cur_ref, prev_ref, bprev_ref, bcur_ref, o_ref, kz_s, vz_s):
    n = pl.program_id(1)
    blk = ATT_BLOCK
    lane = lax.broadcasted_iota(jnp.int32, (2 * blk, LANES), 1)
    lo_half = lane < HEAD_DIM
    neg_prev = jnp.where(n == 0, NEG_INF, 0.0).astype(F32)

    kv_width = N_KV_HEADS * HEAD_DIM
    for pair in range(N_KV_HEADS // 2):
        for part, dst in ((0, kz_s), (1, vz_s)):
            col = part * kv_width + pair * LANES
            both = jnp.concatenate([prev_ref[0, :, col:col + LANES], cur_ref[0, :, col:col + LANES]], axis=0)
            swapped = pltpu.roll(both, HEAD_DIM, axis=1)
            zero = jnp.zeros_like(both)
            c0, c1 = 2 * pair, 2 * pair + 1
            dst[2 * c0 + 0] = jnp.where(lo_half, both, zero).astype(BF16)
            dst[2 * c0 + 1] = jnp.where(lo_half, zero, swapped).astype(BF16)
            dst[2 * c1 + 0] = jnp.where(lo_half, swapped, zero).astype(BF16)
            dst[2 * c1 + 1] = jnp.where(lo_half, zero, both).astype(BF16)

    scale = HEAD_DIM ** -0.5 * math.log2(math.e)
    for p in range(N_HEADS // 2):
        c = (2 * p) // ATT_GROUP
        qp = (q_ref[0, :, p * LANES:(p + 1) * LANES] * scale).astype(BF16)
        o_pair = None
        for a in range(2):
            h = 2 * p + a
            sink = sink_ref[h]
            s = _dot_nt(qp, kz_s[2 * c + a])
            s_prev = s[:, :blk] + bprev_ref[h] + neg_prev
            s_cur = s[:, blk:] + bcur_ref[h]
            m = jnp.maximum(jnp.max(jnp.maximum(s_prev, s_cur), axis=-1, keepdims=True), sink)
            e_prev = jnp.exp2(s_prev - m)
            e_cur = jnp.exp2(s_cur - m)
            den = jnp.sum(e_prev + e_cur, axis=-1, keepdims=True) + jnp.exp2(sink - m)
            pm = jnp.concatenate([e_prev, e_cur], axis=1).astype(BF16)
            o_a = _dot(pm, vz_s[2 * c + a]) * (1.0 / den)
            o_pair = o_a if o_pair is None else o_pair + o_a
        o_ref[0, :, p * LANES:(p + 1) * LANES] = o_pair.astype(o_ref.dtype)


def _t5_bucket_np(dist):
    max_exact = N_BUCKETS // 2
    d = np.maximum(dist, 0)
    df = np.maximum(d, 1).astype(np.float32)
    large = max_exact + (np.log(df / np.float32(max_exact)) / np.float32(math.log(MAX_DISTANCE / max_exact))
                         * np.float32(N_BUCKETS - max_exact)).astype(np.int32)
    large = np.minimum(large, N_BUCKETS - 1)
    return np.where(d < max_exact, d, large)


def _bias_from_dist(dist, in_band, rel_table):
    onehot = (_t5_bucket_np(dist)[..., None] == np.arange(N_BUCKETS)).astype(np.float32)
    tab = jnp.einsum('qsb,bh->hqs', jnp.asarray(onehot), rel_table.astype(F32), precision=lax.Precision.HIGHEST)
    return jnp.where(jnp.asarray(in_band)[None], tab, NEG_INF)


def attention_prompt(qkv, sinks, rel_table):
    b, l, n = qkv.shape
    blk = ATT_BLOCK
    nq = N_HEADS * HEAD_DIM
    kvw = 2 * N_KV_HEADS * HEAD_DIM
    kv_blk = nq // kvw
    qi = np.arange(blk)[:, None]
    sj = np.arange(blk)[None, :]
    d_prev = qi + blk - sj
    d_cur = qi - sj
    log2e = math.log2(math.e)
    bias_prev = _bias_from_dist(d_prev, (d_prev >= 0) & (d_prev <= WINDOW), rel_table) * log2e
    bias_cur = _bias_from_dist(d_cur, (d_cur >= 0) & (d_cur <= WINDOW), rel_table) * log2e
    blocks = (_nbytes((blk, nq), F32) + 2 * _nbytes((blk, kvw), F32) + _nbytes((blk, nq), BF16))
    scratch = 2 * _nbytes((2 * N_KV_HEADS, 2 * blk, LANES), BF16) + 4 * _nbytes((N_HEADS, blk, blk), F32)
    return pl.pallas_call(
        _attn_prompt_kernel,
        grid=(b, l // blk),
        in_specs=[pl.BlockSpec(memory_space=pltpu.SMEM),
                  pl.BlockSpec((1, blk, nq), lambda b_, i: (b_, i, 0)),
                  pl.BlockSpec((1, blk, kvw), lambda b_, i: (b_, i, kv_blk)),
                  pl.BlockSpec((1, blk, kvw), lambda b_, i: (b_, jnp.maximum(i - 1, 0), kv_blk)),
                  pl.BlockSpec((N_HEADS, blk, blk), lambda b_, i: (0, 0, 0)),
                  pl.BlockSpec((N_HEADS, blk, blk), lambda b_, i: (0, 0, 0))],
        out_specs=pl.BlockSpec((1, blk, nq), lambda b_, i: (b_, i, 0)),
        out_shape=jax.ShapeDtypeStruct((b, l, nq), BF16),
        scratch_shapes=[pltpu.VMEM((2 * N_KV_HEADS, 2 * blk, LANES), BF16),
                        pltpu.VMEM((2 * N_KV_HEADS, 2 * blk, LANES), BF16)],
        compiler_params=_params(2, blocks, scratch),
        name="attention_prompt",
    )(sinks.astype(F32) * log2e, qkv, qkv, qkv, bias_prev, bias_cur)


def _attn_sample_kernel(q_ref, k_ref, v_ref, bias_ref, sink_ref, o_ref):
    pairs = [(b, c) for b in range(q_ref.shape[0]) for c in range(N_KV_HEADS)]
    head = lambda ref, b, c: ref[b][:, c * HEAD_DIM:(c + 1) * HEAD_DIM].astype(BF16)
    s = {(b, c): _dot_nt(q_ref[b, c].astype(BF16), head(k_ref, b, c)) + bias_ref[c] for b, c in pairs}
    e, den = {}, {}
    for b, c in pairs:
        sink = sink_ref[c]
        m = jnp.maximum(jnp.max(s[(b, c)], axis=-1, keepdims=True), sink)
        e[(b, c)] = jnp.exp(s[(b, c)] - m)
        den[(b, c)] = jnp.sum(e[(b, c)], axis=-1, keepdims=True) + jnp.exp(sink - m)
    for b, c in pairs:
        o_ref[b, c] = _dot(e[(b, c)].astype(BF16), head(v_ref, b, c)) * (1.0 / den[(b, c)])


def attention_sample(qkv, cache_k, cache_v, sinks, rel_table):
    bsz, w = cache_k.shape[0], cache_k.shape[1]
    t = qkv.shape[0] // bsz
    nq, nkv = N_HEADS * HEAD_DIM, N_KV_HEADS * HEAD_DIM
    keys = w + t
    keys_pad = -(-keys // 16) * 16
    rows = ATT_GROUP * t
    q = qkv[:, :nq].reshape(t, bsz, N_KV_HEADS, ATT_GROUP, HEAD_DIM) * HEAD_DIM ** -0.5
    q = jnp.transpose(q, (1, 2, 3, 0, 4)).reshape(bsz, N_KV_HEADS, rows, HEAD_DIM)
    new_kv = jnp.transpose(qkv[:, nq:].reshape(t, bsz, 2 * nkv), (1, 0, 2))
    k_all = jnp.concatenate([cache_k.reshape(bsz, w, nkv), new_kv[:, :, :nkv]], axis=1)
    v_all = jnp.concatenate([cache_v.reshape(bsz, w, nkv), new_kv[:, :, nkv:]], axis=1)
    pad = ((0, 0), (0, keys_pad - keys), (0, 0))
    k_pad, v_pad = jnp.pad(k_all, pad), jnp.pad(v_all, pad)
    tq = np.arange(t)[:, None]
    sk = np.arange(keys_pad)[None, :]
    dist = tq + w - sk
    in_band = (dist >= 0) & (dist <= WINDOW) & (sk < keys)
    bias_t = _bias_from_dist(dist, in_band, rel_table)
    bias = bias_t.reshape(N_KV_HEADS, rows, keys_pad)
    sink_rows = jnp.repeat(sinks.astype(F32), t).reshape(N_KV_HEADS, rows, 1)
    sb = math.gcd(bsz, ATT_SAMPLE_SEQS)
    blocks = sb * (_nbytes((N_KV_HEADS, rows, nkv), F32) * 2 + 2 * _nbytes((keys_pad, nkv), F32))
    o = pl.pallas_call(
        _attn_sample_kernel,
        grid=(bsz // sb,),
        in_specs=[pl.BlockSpec((sb, N_KV_HEADS, rows, HEAD_DIM), lambda b_: (b_, 0, 0, 0)),
                  pl.BlockSpec((sb, keys_pad, nkv), lambda b_: (b_, 0, 0)),
                  pl.BlockSpec((sb, keys_pad, nkv), lambda b_: (b_, 0, 0)),
                  pl.BlockSpec((N_KV_HEADS, rows, keys_pad), lambda b_: (0, 0, 0)),
                  pl.BlockSpec((N_KV_HEADS, rows, 1), lambda b_: (0, 0, 0))],
        out_specs=pl.BlockSpec((sb, N_KV_HEADS, rows, HEAD_DIM), lambda b_: (b_, 0, 0, 0)),
        out_shape=jax.ShapeDtypeStruct((bsz, N_KV_HEADS, rows, HEAD_DIM), F32),
        compiler_params=_params(1, blocks),
        name="attention_sample",
    )(q, k_pad, v_pad, bias, sink_rows)
    o = o.reshape(bsz, N_KV_HEADS, ATT_GROUP, t, HEAD_DIM)
    o = jnp.transpose(o, (3, 0, 1, 2, 4)).reshape(t * bsz, nq)
    return o.astype(BF16), k_all[:, t:], v_all[:, t:]


def _mm_res_kernel(a_ref, w_ref, x_ref, g_ref, o_ref):
    y = _dot(a_ref[0], w_ref[...].astype(BF16))
    o_ref[0] = x_ref[0] + g_ref[0] * y


def _mm_res_wres_kernel(a_ref, w_ref, x_ref, g_ref, o_ref, wbf_s):
    @pl.when((pl.program_id(1) == 0) & (pl.program_id(2) == 0))
    def _():
        wbf_s[...] = w_ref[...].astype(BF16)

    o_ref[0] = x_ref[0] + g_ref[0] * _dot(a_ref[0], wbf_s[...])


def matmul_gate_residual_wres(a, w_stack, layer, x, gate, tm, tn):
    b, l, k = a.shape
    n = w_stack.shape[2]
    if gate.shape[1] == 1:
        g_spec = pl.BlockSpec((1, 1, tn), lambda j, b_, i: (b_, 0, j))
    else:
        g_spec = pl.BlockSpec((1, tm, tn), lambda j, b_, i: (b_, i, j))
    blocks = _nbytes((tm, k), a.dtype) + _nbytes((k, tn), F32) + 3 * _nbytes((tm, tn), F32)
    scratch = _nbytes((k, tn), BF16) + _nbytes((tm, tn), F32)
    return pl.pallas_call(
        _mm_res_wres_kernel,
        grid=(n // tn, b, l // tm),
        in_specs=[pl.BlockSpec((1, tm, k), lambda j, b_, i: (b_, i, 0)),
                  pl.BlockSpec((None, k, tn), lambda j, b_, i: (layer, 0, j)),
                  pl.BlockSpec((1, tm, tn), lambda j, b_, i: (b_, i, j)),
                  g_spec],
        out_specs=pl.BlockSpec((1, tm, tn), lambda j, b_, i: (b_, i, j)),
        out_shape=jax.ShapeDtypeStruct((b, l, n), F32),
        scratch_shapes=[pltpu.VMEM((k, tn), BF16)],
        compiler_params=_params(3, blocks, scratch),
        name="matmul_gate_residual_wres",
    )(a, w_stack, x, gate)


def matmul_gate_residual(a, w_stack, layer, x, gate, tm, tn):
    b, l, k = a.shape
    n = w_stack.shape[2]
    grows = 1 if gate.shape[1] == 1 else tm
    if gate.shape[1] == 1:
        g_spec = pl.BlockSpec((1, 1, tn), lambda b_, i, j: (b_, 0, j))
    else:
        g_spec = pl.BlockSpec((1, tm, tn), lambda b_, i, j: (b_, i, j))
    blocks = (_nbytes((tm, k), a.dtype) + _nbytes((k, tn), F32) + 2 * _nbytes((tm, tn), F32)
              + _nbytes((grows, tn), F32))
    scratch = _nbytes((k, tn), BF16) + _nbytes((tm, tn), F32)
    return pl.pallas_call(
        _mm_res_kernel,
        grid=(b, l // tm, n // tn),
        in_specs=[pl.BlockSpec((1, tm, k), lambda b_, i, j: (b_, i, 0)),
                  pl.BlockSpec((None, k, tn), lambda b_, i, j: (layer, 0, j)),
                  pl.BlockSpec((1, tm, tn), lambda b_, i, j: (b_, i, j)),
                  g_spec],
        out_specs=pl.BlockSpec((1, tm, tn), lambda b_, i, j: (b_, i, j)),
        out_shape=jax.ShapeDtypeStruct((b, l, n), F32),
        compiler_params=_params(3, blocks, scratch),
        name="matmul_gate_residual",
    )(a, w_stack, x, gate)


def _ffn_conv_rows(u, prev1, prev2, cw_ref, cb_ref):
    return u * cw_ref[2:3, :] + prev1 * cw_ref[1:2, :] + prev2 * cw_ref[0:1, :] + cb_ref[...]


def _ffn_up_prompt_kernel(x_ref, gain_ref, shift_ref, scale_ref, wg_ref, wv_ref, cwg_ref, cwv_ref,
                          cbg_ref, cbv_ref, o_ref, tail_ref, h_s, carry_s, ext_s):
    i = pl.program_id(1)
    j = pl.program_id(2)
    tm = h_s.shape[0]

    @pl.when(j == 0)
    def _():
        h_s[...] = _norm_mod(x_ref[0], gain_ref[...], shift_ref[0], scale_ref[0]).astype(BF16)

    @pl.when(i == 0)
    def _():
        carry_s[j] = jnp.zeros(carry_s.shape[1:], F32)

    halves = ((wg_ref, cwg_ref, cbg_ref), (wv_ref, cwv_ref, cbv_ref))
    w_bf = [w_ref[...].astype(BF16) for w_ref, _, _ in halves]
    for half in range(2):
        ext_s[half, 0:SUBLANES, :] = carry_s[j, half]
    sub = min(FFN_SUB_ROWS, tm)
    for m in range(tm // sub):
        hm = h_s[m * sub:(m + 1) * sub, :]
        r0 = SUBLANES + m * sub
        ys = []
        for half, (_, cw_ref, cb_ref) in enumerate(halves):
            u = _dot(hm, w_bf[half])
            ext_s[half, r0:r0 + sub, :] = u
            ys.append(_ffn_conv_rows(u, ext_s[half, r0 - 1:r0 - 1 + sub, :], ext_s[half, r0 - 2:r0 - 2 + sub, :],
                                     cw_ref, cb_ref))
        o_ref[0, m * sub:(m + 1) * sub, :] = (_silu(ys[0]) * ys[1]).astype(o_ref.dtype)
    for half in range(2):
        tail = ext_s[half, tm:tm + SUBLANES, :]
        carry_s[j, half] = tail
        tail_ref[0, 0, half] = tail


def ffn_up_prompt(x, gain, shift, scale, w_up, conv_w, conv_b, layer, tm, tn=512):
    b, l, d = x.shape
    dff = w_up.shape[2] // 2
    nj = dff // tn
    conv_b = conv_b.reshape(conv_b.shape[0], 1, 2 * dff)
    blocks = (_nbytes((tm, d), F32) + 2 * _nbytes((d, tn), F32) + _nbytes((tm, tn), BF16)
              + 2 * _nbytes((1, d), F32) + 8 * _nbytes((SUBLANES, tn), F32))
    scratch = (_nbytes((tm, d), BF16) + _nbytes((nj, 2, SUBLANES, tn), F32) + 2 * _nbytes((d, tn), BF16)
               + 8 * _nbytes((tm, tn), F32))
    act, tail = pl.pallas_call(
        _ffn_up_prompt_kernel,
        grid=(b, l // tm, nj),
        in_specs=[pl.BlockSpec((1, tm, d), lambda b_, i, j: (b_, i, 0)),
                  pl.BlockSpec((1, d), lambda b_, i, j: (0, 0)),
                  _mod_specs(shift, tm), _mod_specs(scale, tm),
                  pl.BlockSpec((None, d, tn), lambda b_, i, j: (layer, 0, j)),
                  pl.BlockSpec((None, d, tn), lambda b_, i, j: (layer, 0, j + nj)),
                  pl.BlockSpec((None, FFN_CONV, tn), lambda b_, i, j: (layer, 0, j)),
                  pl.BlockSpec((None, FFN_CONV, tn), lambda b_, i, j: (layer, 0, j + nj)),
                  pl.BlockSpec((None, 1, tn), lambda b_, i, j: (layer, 0, j)),
                  pl.BlockSpec((None, 1, tn), lambda b_, i, j: (layer, 0, j + nj))],
        out_specs=[pl.BlockSpec((1, tm, tn), lambda b_, i, j: (b_, i, j)),
                   pl.BlockSpec((1, 1, 2, SUBLANES, tn), lambda b_, i, j: (b_, i, 0, 0, j))],
        out_shape=[jax.ShapeDtypeStruct((b, l, dff), BF16),
                   jax.ShapeDtypeStruct((b, l // tm, 2, SUBLANES, dff), F32)],
        scratch_shapes=[pltpu.VMEM((tm, d), BF16), pltpu.VMEM((nj, 2, SUBLANES, tn), F32),
                        pltpu.VMEM((2, SUBLANES + tm, tn), F32)],
        compiler_params=_params(3, blocks, scratch),
        name="ffn_up_prompt",
    )(x, gain.reshape(1, d), shift, scale, w_up, w_up, conv_w, conv_w, conv_b, conv_b)
    keep = FFN_CONV - 1
    hist = jnp.transpose(tail[:, -1, :, SUBLANES - keep:, :], (0, 2, 1, 3)).reshape(b, keep, 2 * dff)
    return act, hist


def _ffn_up_sample_kernel(x_ref, gain_ref, shift_ref, scale_ref, wg_ref, wv_ref, cwg_ref, cwv_ref,
                          cbg_ref, cbv_ref, hg_ref, hv_ref, o_ref, ng_ref, nv_ref, h_s, *, bsz):
    @pl.when(pl.program_id(0) == 0)
    def _():
        h_s[...] = _norm_mod(x_ref[...], gain_ref[...], shift_ref[...], scale_ref[...]).astype(BF16)

    h = h_s[...]
    rows = h_s.shape[0]
    keep = FFN_CONV - 1
    ys = []
    for w_ref, cw_ref, cb_ref, hist_ref, new_ref in ((wg_ref, cwg_ref, cbg_ref, hg_ref, ng_ref),
                                                     (wv_ref, cwv_ref, cbv_ref, hv_ref, nv_ref)):
        u = _dot(h, w_ref[...].astype(BF16))
        ext = jnp.concatenate([hist_ref[...], u], axis=0)
        prev = [ext[(keep - k) * bsz:(keep - k) * bsz + rows] for k in (1, 2)]
        ys.append(_ffn_conv_rows(u, prev[0], prev[1], cw_ref, cb_ref))
        new_ref[...] = ext[rows:rows + keep * bsz]
    o_ref[...] = (_silu(ys[0]) * ys[1]).astype(o_ref.dtype)


def ffn_up_sample(x, gain, shift, scale, w_up, conv_w, conv_b, layer, hist, tn=512):
    rows, d = x.shape
    dff = w_up.shape[2] // 2
    nj = dff // tn
    keep = FFN_CONV - 1
    bsz = hist.shape[0] // keep
    conv_b = conv_b.reshape(conv_b.shape[0], 1, 2 * dff)
    col = lambda j: (0, j)
    col_hi = lambda j: (0, j + nj)
    lcol = lambda j: (layer, 0, j)
    lcol_hi = lambda j: (layer, 0, j + nj)
    blocks = (_nbytes((rows, d), F32) * 3 + 2 * _nbytes((d, tn), F32) + 5 * _nbytes((rows, tn), F32))
    scratch = _nbytes((rows, d), BF16) + 2 * _nbytes((d, tn), BF16) + 8 * _nbytes((rows, tn), F32)
    act, new_g, new_v = pl.pallas_call(
        functools.partial(_ffn_up_sample_kernel, bsz=bsz),
        grid=(nj,),
        in_specs=[pl.BlockSpec((rows, d), lambda j: (0, 0)),
                  pl.BlockSpec((1, d), lambda j: (0, 0)),
                  pl.BlockSpec((rows, d), lambda j: (0, 0)),
                  pl.BlockSpec((rows, d), lambda j: (0, 0)),
                  pl.BlockSpec((None, d, tn), lcol), pl.BlockSpec((None, d, tn), lcol_hi),
                  pl.BlockSpec((None, FFN_CONV, tn), lcol), pl.BlockSpec((None, FFN_CONV, tn), lcol_hi),
                  pl.BlockSpec((None, 1, tn), lcol), pl.BlockSpec((None, 1, tn), lcol_hi),
                  pl.BlockSpec((keep * bsz, tn), col), pl.BlockSpec((keep * bsz, tn), col_hi)],
        out_specs=[pl.BlockSpec((rows, tn), col), pl.BlockSpec((keep * bsz, tn), col),
                   pl.BlockSpec((keep * bsz, tn), col)],
        out_shape=[jax.ShapeDtypeStruct((rows, dff), BF16),
                   jax.ShapeDtypeStruct((keep * bsz, dff), F32),
                   jax.ShapeDtypeStruct((keep * bsz, dff), F32)],
        scratch_shapes=[pltpu.VMEM((rows, d), BF16)],
        compiler_params=_params(1, blocks, scratch),
        name="ffn_up_sample",
    )(x, gain.reshape(1, d), shift, scale, w_up, w_up, conv_w, conv_w, conv_b, conv_b, hist, hist)
    return act, jnp.concatenate([new_g, new_v], axis=-1)


def _gdn_in_kernel(*refs, n_qk_tiles, n_conv_tiles, bsz):
    if bsz is None:
        (x_ref, gain_ref, shift_ref, scale_ref, w_ref, wba_ref, alog_ref, dtb_ref, cw_ref,
         o_ref, bg_ref, raw_ref, h_s, carry_s) = refs
    else:
        (x_ref, gain_ref, shift_ref, scale_ref, w_ref, wba_ref, alog_ref, dtb_ref, cw_ref, hist_ref,
         o_ref, bg_ref, raw_ref, h_s) = refs
    i = pl.program_id(1)
    j = pl.program_id(2)
    tm = h_s.shape[0]
    width = GDN_CONV

    @pl.when(j == 0)
    def _():
        h = _norm_mod(x_ref[0], gain_ref[...], shift_ref[0], scale_ref[0]).astype(BF16)
        h_s[...] = h
        y = _dot_nt(h, wba_ref[...].astype(BF16))
        lane = lax.broadcasted_iota(jnp.int32, y.shape, 1) % LANES
        beta = 1.0 / (1.0 + jnp.exp(-y))
        a = y + dtb_ref[...]
        softplus = jnp.maximum(a, 0.0) + jnp.log1p(jnp.exp(-jnp.abs(a)))
        g = -jnp.exp(alog_ref[...]) * softplus
        bg_ref[0] = jnp.where(lane < GDN_VH_PER_GROUP, beta, jnp.where(lane < 2 * GDN_VH_PER_GROUP, g, 0.0))

    sub = min(GDN_IN_SUB_ROWS, tm)
    row8 = lax.broadcasted_iota(jnp.int32, (SUBLANES, 1), 0)
    assert width == 4

    def shifted(x, prev, k):
        top = jnp.where(row8 < k, pltpu.roll(prev, k, axis=0), pltpu.roll(x[0:SUBLANES], k, axis=0))
        return jnp.concatenate([top, pltpu.roll(x, k, axis=0)[SUBLANES:]], axis=0)

    def conv_tile(l2_scale):
        w_bf = w_ref[...].astype(BF16)
        jc = jnp.minimum(j, n_conv_tiles - 1)
        if bsz is None:
            @pl.when(i == 0)
            def _():
                carry_s[jc] = jnp.zeros(carry_s.shape[1:], F32)
            prev_u, prev_pair = carry_s[jc, 0], carry_s[jc, 1]
        for m in range(tm // sub):
            rs = slice(m * sub, (m + 1) * sub)
            u = _dot_nt(h_s[rs, :], w_bf)
            if bsz is None:
                u1 = shifted(u, prev_u, 1)
                pair = u * cw_ref[1:2, :] + u1 * cw_ref[0:1, :]
                y = u * cw_ref[3:4, :] + u1 * cw_ref[2:3, :] + shifted(pair, prev_pair, 2)
                prev_u, prev_pair = u[sub - SUBLANES:sub], pair[sub - SUBLANES:sub]
            else:
                y = u * cw_ref[width - 1:width, :]
                ext = jnp.concatenate([hist_ref[...], u], axis=0)
                for k in range(1, width):
                    wk = cw_ref[width - 1 - k:width - k, :]
                    y = y + ext[(width - 1 - k) * bsz:(width - 1 - k) * bsz + sub] * wk
                raw_ref[...] = ext[sub:sub + (width - 1) * bsz]
            y = _silu(y)
            if l2_scale is not None:
                heads = [y[:, hd * LANES:(hd + 1) * LANES] for hd in range(y.shape[1] // LANES)]
                y = jnp.concatenate(
                    [yh * (lax.rsqrt(jnp.sum(yh * yh, axis=-1, keepdims=True) + NORM_EPS) * l2_scale)
                     for yh in heads], axis=1)
            o_ref[0, rs, :] = y.astype(o_ref.dtype)
        if bsz is None:
            carry_s[jc, 0] = prev_u
            carry_s[jc, 1] = prev_pair
            raw_ref[0, 0] = prev_u

    @pl.when(j < n_qk_tiles)
    def _():
        conv_tile(jnp.where(j < n_qk_tiles // 2, GDN_DK ** -0.5, 1.0).astype(F32))

    @pl.when((j >= n_qk_tiles) & (j < n_conv_tiles))
    def _():
        conv_tile(None)

    @pl.when(j >= n_conv_tiles)
    def _():
        w_bf = w_ref[...].astype(BF16)
        for m in range(tm // sub):
            rs = slice(m * sub, (m + 1) * sub)
            o_ref[0, rs, :] = _silu(_dot_nt(h_s[rs, :], w_bf)).astype(o_ref.dtype)


def _group_lane_layout(vec_b, vec_a):
    lead = vec_b.shape[:-1]
    vb = vec_b.reshape(*lead, GDN_HEAD_GROUPS, GDN_VH_PER_GROUP)
    va = vec_a.reshape(*lead, GDN_HEAD_GROUPS, GDN_VH_PER_GROUP)
    pad = jnp.zeros((*lead, GDN_HEAD_GROUPS, LANES - 2 * GDN_VH_PER_GROUP), vec_b.dtype)
    return jnp.concatenate([vb, va, pad], axis=-1).reshape(*lead, GDN_HEAD_GROUPS * LANES)


def gdn_in_projection(x, gain, shift, scale, w_in, conv_w, a_log, dt_bias, tm, conv_hist=None, tn=512):
    b, l, d = x.shape
    n_main = GDN_CONV_DIM + GDN_VAL_DIM
    w_t = jnp.transpose(w_in)
    wba = _group_lane_layout(w_in[:, n_main:n_main + GDN_V_HEADS], w_in[:, n_main + GDN_V_HEADS:])
    wba_t = jnp.transpose(wba)
    zeros = jnp.zeros((1, GDN_V_HEADS), F32)
    alog = _group_lane_layout(zeros, a_log.reshape(1, -1).astype(F32))
    dtb = _group_lane_layout(zeros, dt_bias.reshape(1, -1).astype(F32))
    nbg = wba_t.shape[0]
    n_conv_tiles = GDN_CONV_DIM // tn
    n_qk_tiles = 2 * GDN_KEY_DIM // tn
    conv_col = lambda b_, i, j: (0, jnp.minimum(j, n_conv_tiles - 1))
    in_specs = [pl.BlockSpec((1, tm, d), lambda b_, i, j: (b_, i, 0)),
                pl.BlockSpec((1, d), lambda b_, i, j: (0, 0)),
                _mod_specs(shift, tm), _mod_specs(scale, tm),
                pl.BlockSpec((tn, d), lambda b_, i, j: (j, 0)),
                pl.BlockSpec((nbg, d), lambda b_, i, j: (0, 0)),
                pl.BlockSpec((1, nbg), lambda b_, i, j: (0, 0)),
                pl.BlockSpec((1, nbg), lambda b_, i, j: (0, 0)),
                pl.BlockSpec((GDN_CONV, tn), conv_col)]
    args = [x, gain.reshape(1, d), shift, scale, w_t, wba_t, alog, dtb, conv_w]
    scratch_shapes = [pltpu.VMEM((tm, d), BF16)]
    blocks = (_nbytes((tm, d), F32) + _nbytes((tn, d), F32) + _nbytes((tm, tn), BF16)
              + _nbytes((nbg, d), F32) + _nbytes((tm, nbg), F32) + 2 * _nbytes((1, d), F32))
    scratch = _nbytes((tm, d), BF16) + _nbytes((tn, d), BF16) + 6 * _nbytes((tm, nbg), F32)
    if conv_hist is None:
        bsz = None
        raw_spec = pl.BlockSpec((1, 1, SUBLANES, tn), lambda b_, i, j: (b_, i, 0, jnp.minimum(j, n_conv_tiles - 1)))
        raw_shape = jax.ShapeDtypeStruct((b, l // tm, SUBLANES, GDN_CONV_DIM), F32)
        scratch_shapes.append(pltpu.VMEM((n_conv_tiles, 2, SUBLANES, tn), F32))
    else:
        assert b == 1 and l == tm <= GDN_IN_SUB_ROWS
        hist_rows = conv_hist.shape[0]
        bsz = hist_rows // (GDN_CONV - 1)
        args.append(conv_hist)
        in_specs.append(pl.BlockSpec((hist_rows, tn), conv_col))
        raw_spec = pl.BlockSpec((hist_rows, tn), conv_col)
        raw_shape = jax.ShapeDtypeStruct((hist_rows, GDN_CONV_DIM), F32)
        blocks += 2 * _nbytes((hist_rows, tn), F32)
    return pl.pallas_call(
        functools.partial(_gdn_in_kernel, n_qk_tiles=n_qk_tiles, n_conv_tiles=n_conv_tiles, bsz=bsz),
        grid=(b, l // tm, n_main // tn),
        in_specs=in_specs,
        out_specs=[pl.BlockSpec((1, tm, tn), lambda b_, i, j: (b_, i, j)),
                   pl.BlockSpec((1, tm, nbg), lambda b_, i, j: (b_, i, 0)),
                   raw_spec],
        out_shape=[jax.ShapeDtypeStruct((b, l, n_main), BF16),
                   jax.ShapeDtypeStruct((b, l, nbg), F32),
                   raw_shape],
        scratch_shapes=scratch_shapes,
        compiler_params=_params(3, blocks, scratch),
        name="gdn_in_projection",
    )(*args)


def _gdn_core_kernel(q_ref, k_ref, v_ref, zg_ref, bg_ref, onorm_ref, s0_ref, o_ref, s1_ref,
                     s_s, gc_s, tdec_s, mo_s, *, seq_rows):
    r = pl.program_id(2)
    n_r = pl.num_programs(2)
    rows = q_ref.shape[1]
    chunk = GDN_CHUNK
    vpg = GDN_VH_PER_GROUP
    rep = GDN_V_HEADS // GDN_K_HEADS
    n_kh = GDN_KH_PER_GROUP
    seg = chunk if seq_rows is None else seq_rows
    n_seq = chunk // seg

    def load_states():
        for s in range(n_seq):
            for kh in range(n_kh):
                s_s[s * n_kh + kh] = jnp.concatenate([s0_ref[s, kh * rep + e] for e in range(rep)], axis=1)

    if seq_rows is None:
        pl.when(r == 0)(load_states)
    else:
        load_states()

    def seg_last(x):
        if n_seq == 1:
            return x[chunk - 1:chunk, :]
        blocks = x.reshape(n_seq, seg, x.shape[1])
        return jnp.broadcast_to(blocks[:, seg - 1:seg, :], blocks.shape).reshape(x.shape)

    n_chunks = rows // chunk
    pw = GDN_PACK * chunk
    n_packs = vpg // GDN_PACK
    ri = lax.broadcasted_iota(jnp.int32, (chunk, pw), 0)
    li = lax.broadcasted_iota(jnp.int32, (chunk, pw), 1) % chunk
    seg_shift = int(math.log2(seg))
    same_seq = (ri >> seg_shift) == (li >> seg_shift)
    tril_p = (ri >= li) & same_seq
    strict_p = (ri > li) & same_seq
    eye_p = (ri == li).astype(F32)
    n_levels = seg_shift
    off_masks = [((ri >> (lvl + 1)) == (li >> (lvl + 1))) & ((ri >> lvl) != (li >> lvl))
                 for lvl in range(n_levels)]
    bd_rows = lax.broadcasted_iota(jnp.int32, (pw, pw), 0) // chunk
    bd_cols = lax.broadcasted_iota(jnp.int32, (pw, pw), 1) // chunk
    bd_mask = (bd_rows == bd_cols).astype(BF16)
    row_c = lax.broadcasted_iota(jnp.int32, (chunk, LANES), 0)
    lane_lo = lax.broadcasted_iota(jnp.int32, (chunk, LANES), 1) < chunk

    def block_diag(xp):
        return jnp.concatenate([xp.astype(BF16)] * GDN_PACK, axis=0) * bd_mask

    def pack_cols(arr, first_col):
        tiles = []
        for t in range(pw // LANES):
            even = jnp.broadcast_to(arr[:, first_col + 2 * t:first_col + 2 * t + 1], (chunk, LANES))
            odd = jnp.broadcast_to(arr[:, first_col + 2 * t + 1:first_col + 2 * t + 2], (chunk, LANES))
            tiles.append(jnp.where(lane_lo, even, odd))
        return jnp.concatenate(tiles, axis=1)

    for c0 in range(0, n_chunks, GDN_PHASE_A_CHUNKS):
        group = [(c, p) for c in range(c0, min(c0 + GDN_PHASE_A_CHUNKS, n_chunks)) for p in range(n_packs)]
        a_list, qkd, brow, dec = {}, {}, {}, {}
        for c in range(c0, min(c0 + GDN_PHASE_A_CHUNKS, n_chunks)):
            rs = slice(c * chunk, (c + 1) * chunk)
            bg = bg_ref[0, rs, :]
            cum = bg
            shift = 1
            while shift < seg:
                cum = cum + jnp.where((row_c & (seg - 1)) >= shift, pltpu.roll(cum, shift, axis=0), 0.0)
                shift *= 2
            gc_s[rs, :] = cum
            gq = []
            for kh in range(GDN_KH_PER_GROUP):
                ksl = slice(kh * LANES, (kh + 1) * LANES)
                k = k_ref[0, rs, ksl]
                kq = jnp.concatenate([k, q_ref[0, rs, ksl]], axis=0)
                kk = jnp.concatenate([k, k], axis=0)
                gq.append(_dot_nt(kq, kk))
            for p in range(n_packs):
                khs = [(p * GDN_PACK + 2 * t) // rep for t in range(pw // LANES)]
                gram = jnp.concatenate([gq[kh][:chunk] for kh in khs], axis=1)
                qk = jnp.concatenate([gq[kh][chunk:] for kh in khs], axis=1)
                beta_p = pack_cols(bg, p * GDN_PACK)
                gcol_p = pack_cols(cum, vpg + p * GDN_PACK)
                grow_p = jnp.sum(gcol_p * eye_p, axis=0, keepdims=True)
                decay = jnp.exp(jnp.where(tril_p, gcol_p - grow_p, NEG_INF))
                a_list[(c, p)] = jnp.where(strict_p, gram * beta_p * decay, 0.0)
                qkd[(c, p)] = (qk * decay).astype(BF16)
                brow[(c, p)] = jnp.sum(beta_p * eye_p, axis=0, keepdims=True)
                dec[(c, p)] = jnp.exp(seg_last(gcol_p) - gcol_p)
        inv = {cp: eye_p - jnp.where(off_masks[0], a_list[cp], 0.0) for cp in group}
        for lvl in range(1, n_levels):
            w = {cp: _dot(jnp.where(off_masks[lvl], a_list[cp], 0.0).astype(BF16), block_diag(inv[cp]))
                 for cp in group}
            inv = {cp: inv[cp] - _dot(inv[cp].astype(BF16), block_diag(w[cp])) for cp in group}
        for cp in group:
            t_beta = inv[cp] * brow[cp]
            tdec_s[cp[0], cp[1]] = block_diag(dec[cp] * t_beta)
            mo_s[cp[0], cp[1]] = block_diag(_dot(qkd[cp], block_diag(t_beta)))

    def chunk_body(c, carry):
        rs = pl.ds(pl.multiple_of(c * chunk, chunk), chunk)
        cum = gc_s[rs, :]
        eg, g_last, ks, qs = [], [], [], []
        for kh in range(n_kh):
            ksl = slice(kh * LANES, (kh + 1) * LANES)
            kq = jnp.concatenate([k_ref[0, rs, ksl], q_ref[0, rs, ksl]], axis=0)
            per_seq = [_dot(kq, s_s[s * n_kh + kh].astype(BF16)) for s in range(n_seq)]
            kq_s = jnp.concatenate([per_seq[s][half * chunk + s * seg:half * chunk + (s + 1) * seg]
                                    for half in range(2) for s in range(n_seq)], axis=0)
            for e in range(rep):
                hv = kh * rep + e
                gcol = cum[:, vpg + hv:vpg + hv + 1]
                g_last.append([gcol[(s + 1) * seg - 1:(s + 1) * seg, :] for s in range(n_seq)])
                eg.append(jnp.exp(gcol))
                ks.append(kq_s[:chunk, e * LANES:(e + 1) * LANES] * eg[hv])
                qs.append(kq_s[chunk:, e * LANES:(e + 1) * LANES] * eg[hv])
        v_dec, o_intra = [], []
        for p in range(n_packs):
            heads = range(p * GDN_PACK, (p + 1) * GDN_PACK)
            rhs = jnp.concatenate([v_ref[0, rs, hv * LANES:(hv + 1) * LANES].astype(F32) - ks[hv]
                                   for hv in heads], axis=0).astype(BF16)
            vd = _dot(tdec_s[c, p], rhs)
            oi = _dot(mo_s[c, p], rhs)
            for j in range(GDN_PACK):
                v_dec.append(vd[j * chunk:(j + 1) * chunk])
                o_intra.append(oi[j * chunk:(j + 1) * chunk])
        for hv in range(vpg):
            vsl = slice(hv * LANES, (hv + 1) * LANES)
            o = qs[hv] + o_intra[hv]
            on = o * lax.rsqrt(jnp.mean(o * o, axis=-1, keepdims=True) + NORM_EPS) * onorm_ref[...]
            o_ref[0, rs, vsl] = (on * zg_ref[0, rs, vsl].astype(F32)).astype(o_ref.dtype)
        lane_v = lax.broadcasted_iota(jnp.int32, (1, rep * LANES), 1) // LANES
        row_seq = lax.broadcasted_iota(jnp.int32, (chunk, 1), 0) >> seg_shift
        for kh in range(n_kh):
            heads = range(kh * rep, (kh + 1) * rep)
            k_bf = k_ref[0, rs, kh * LANES:(kh + 1) * LANES]
            vd_pair = jnp.concatenate([v_dec[hv] for hv in heads], axis=1)
            for s in range(n_seq):
                vd_s = vd_pair if n_seq == 1 else jnp.where(row_seq == s, vd_pair, 0.0)
                ds = _dot_tn(k_bf, vd_s.astype(BF16))
                keep = jnp.exp(g_last[kh * rep][s])
                for e in range(1, rep):
                    keep = jnp.where(lane_v == e, jnp.exp(g_last[kh * rep + e][s]), keep)
                s_s[s * n_kh + kh] = s_s[s * n_kh + kh] * keep + ds
        return carry

    lax.fori_loop(0, n_chunks, chunk_body, 0)

    def store_states():
        for s in range(n_seq):
            for hv in range(vpg):
                s1_ref[s, hv] = s_s[s * n_kh + hv // rep][:, (hv % rep) * LANES:(hv % rep + 1) * LANES]

    if seq_rows is None:
        pl.when(r == n_r - 1)(store_states)
    else:
        store_states()


def gdn_core(qkvz, bg, s0, out_norm, rows, seq_rows=None):
    b, l, _ = qkvz.shape
    n_state = 1 if seq_rows is None else GDN_CHUNK // seq_rows
    if seq_rows is None:
        state_idx = lambda b_, g, r: (b_, g, 0, 0)
    else:
        assert b == 1 and rows == GDN_CHUNK
        state_idx = lambda b_, g, r: (r, g, 0, 0)
    hg = GDN_HEAD_GROUPS
    qw = GDN_KEY_DIM // hg
    vw = GDN_VAL_DIM // hg
    kq = GDN_KEY_DIM // qw
    kv = 2 * GDN_KEY_DIM // vw
    kz = GDN_CONV_DIM // vw
    vpg = GDN_VH_PER_GROUP
    blocks = (2 * _nbytes((rows, qw), BF16) + 3 * _nbytes((rows, vw), BF16) + _nbytes((rows, LANES), F32)
              + 2 * _nbytes((n_state, vpg, GDN_DK, GDN_DV), F32))
    n_chunks = rows // GDN_CHUNK
    n_packs = vpg // GDN_PACK
    pw = GDN_PACK * GDN_CHUNK
    scratch = (_nbytes((n_state, vpg, GDN_DK, GDN_DV), F32) + _nbytes((rows, LANES), F32)
               + 2 * _nbytes((n_chunks, n_packs, pw, pw), BF16) + 6 * _nbytes((rows, vw), F32))
    return pl.pallas_call(
        functools.partial(_gdn_core_kernel, seq_rows=seq_rows),
        grid=(b, hg, l // rows),
        in_specs=[pl.BlockSpec((1, rows, qw), lambda b_, g, r: (b_, r, g)),
                  pl.BlockSpec((1, rows, qw), lambda b_, g, r: (b_, r, kq + g)),
                  pl.BlockSpec((1, rows, vw), lambda b_, g, r: (b_, r, kv + g)),
                  pl.BlockSpec((1, rows, vw), lambda b_, g, r: (b_, r, kz + g)),
                  pl.BlockSpec((1, rows, LANES), lambda b_, g, r: (b_, r, g)),
                  pl.BlockSpec((1, GDN_DV), lambda b_, g, r: (0, 0)),
                  pl.BlockSpec((n_state, vpg, GDN_DK, GDN_DV), state_idx)],
        out_specs=[pl.BlockSpec((1, rows, vw), lambda b_, g, r: (b_, r, g)),
                   pl.BlockSpec((n_state, vpg, GDN_DK, GDN_DV), state_idx)],
        out_shape=[jax.ShapeDtypeStruct((b, l, GDN_VAL_DIM), BF16),
                   jax.ShapeDtypeStruct(s0.shape, F32)],
        scratch_shapes=[pltpu.VMEM((n_state * GDN_KH_PER_GROUP, GDN_DK,
                                    (GDN_V_HEADS // GDN_K_HEADS) * GDN_DV), F32),
                        pltpu.VMEM((rows, LANES), F32),
                        pltpu.VMEM((n_chunks, n_packs, pw, pw), BF16),
                        pltpu.VMEM((n_chunks, n_packs, pw, pw), BF16)],
        compiler_params=_params(3, blocks, scratch),
        name="gdn_core",
    )(qkvz, qkvz, qkvz, qkvz, bg, out_norm.reshape(1, GDN_DV).astype(F32), s0)


def _tile_rows(l, cap):
    t = min(l, cap)
    while l % t:
        t //= 2
    return t


def _to_time_major(a):
    return jnp.transpose(a, (1, 0, 2)).reshape(a.shape[0] * a.shape[1], a.shape[2])


def _from_time_major(a, bsz):
    return jnp.transpose(a.reshape(a.shape[0] // bsz, bsz, a.shape[1]), (1, 0, 2))


def _trunk(x, mod, states, P, sample):
    bsz, l, d = x.shape
    tm = _tile_rows(l, ROW_TILE_CAP)
    out_proj = matmul_gate_residual if sample else matmul_gate_residual_wres
    new = {}
    depth = mod.shape[0]
    for layer in range(depth):
        if sample:
            parts = [mod[layer][None, :, k * d:(k + 1) * d] for k in range(6)]
        else:
            parts = [mod[layer][:, None, k * d:(k + 1) * d] for k in range(6)]
        sh_m, sc_m, g_m, sh_f, sc_f, g_f = parts
        i = layer // 2
        if layer % 2 == 0:
            qkv = qkv_projection(x, P['norm_mix'][layer], sh_m, sc_m, P['w_attn_qkv'][i],
                                 P['attn_q_norm'][i], P['attn_k_norm'][i], tm)
            nq, nkv = N_HEADS * HEAD_DIM, N_KV_HEADS * HEAD_DIM
            if sample:
                o, k_win, v_win = attention_sample(qkv[0], states['win_k'][i], states['win_v'][i],
                                                   P['attn_sinks'][i], P['rel_bias_table'])
                o = o[None]
                w_len = k_win.shape[1]
                new.setdefault('win_k', []).append(k_win.reshape(-1, w_len, N_KV_HEADS, HEAD_DIM))
                new.setdefault('win_v', []).append(v_win.reshape(-1, w_len, N_KV_HEADS, HEAD_DIM))
            else:
                o = attention_prompt(qkv, P['attn_sinks'][i], P['rel_bias_table'])
                keep = min(WINDOW, PAST_LEN)
                new.setdefault('win_k', []).append(
                    qkv[:, l - keep:, nq:nq + nkv].reshape(bsz, keep, N_KV_HEADS, HEAD_DIM))
                new.setdefault('win_v', []).append(
                    qkv[:, l - keep:, nq + nkv:].reshape(bsz, keep, N_KV_HEADS, HEAD_DIM))
            x = matmul_gate_residual(o, P['w_attn_o'], i, x, g_m, _tile_rows(l, 2 * ROW_TILE_CAP), 512)
        else:
            keep = GDN_CONV - 1
            gdn_args = (x, P['norm_mix'][layer], sh_m, sc_m, P['w_gdn_in'][i], P['gdn_conv_w'][i],
                        P['gdn_a_log'][i], P['gdn_dt_bias'][i], tm)
            if sample:
                seqs = states['gdn'].shape[1]
                t = l // seqs
                qkvz, bg, hist1 = gdn_in_projection(
                    *gdn_args, conv_hist=_to_time_major(states['gdn_conv'][i].astype(F32)), tn=2 * 512)
                new.setdefault('gdn_conv', []).append(_from_time_major(hist1, seqs))
                seq_rows = -(-t // SUBLANES) * SUBLANES
                per_seq = lambda a: jnp.pad(_from_time_major(a, seqs), ((0, 0), (0, seq_rows - t), (0, 0))
                                            ).reshape(1, seqs * seq_rows, a.shape[-1])
                o_seq, s1 = gdn_core(per_seq(qkvz[0]), per_seq(bg[0]), states['gdn'][i].astype(F32),
                                     P['gdn_out_norm'][i], GDN_CHUNK, seq_rows=seq_rows)
                o = _to_time_major(o_seq.reshape(seqs, seq_rows, GDN_VAL_DIM)[:, :t])[None]
            else:
                qkvz, bg, raw = gdn_in_projection(*gdn_args)
                s0 = jnp.zeros((bsz, GDN_V_HEADS, GDN_DK, GDN_DV), F32)
                o, s1 = gdn_core(qkvz, bg, s0, P['gdn_out_norm'][i], _tile_rows(l, GDN_ROW_TILE_CAP))
                new.setdefault('gdn_conv', []).append(raw[:, -1, SUBLANES - keep:, :])
            new.setdefault('gdn', []).append(s1)
            x = out_proj(o, P['w_gdn_out'], i, x, g_m, tm, 512)
        if sample:
            seqs = states['ffn_conv'].shape[1]
            act, f1 = ffn_up_sample(x[0], P['norm_ffn'][layer], sh_f[0], sc_f[0], P['w_ffn_up'],
                                    P['ffn_conv_w'], P['ffn_conv_b'], layer,
                                    _to_time_major(states['ffn_conv'][layer].astype(F32)))
            act, f1 = act[None], _from_time_major(f1, seqs)
        else:
            act, f1 = ffn_up_prompt(x, P['norm_ffn'][layer], sh_f, sc_f, P['w_ffn_up'],
                                    P['ffn_conv_w'], P['ffn_conv_b'], layer, tm)
        new.setdefault('ffn_conv', []).append(f1)
        if sample:
            x = matmul_gate_residual(act, P['w_ffn_down'], layer, x, g_f, tm, 512)
        else:
            x = matmul_gate_residual_wres(act, P['w_ffn_down'], layer, x, g_f, _tile_rows(l, 512), 512)
    return x, {k: jnp.stack(v) for k, v in new.items()}


def kernel(x_prompt, x_sample, c_prompt, c_sample, cache_win_k, cache_win_v, state_gdn, state_gdn_conv,
           state_ffn_conv, rel_bias_table, w_ada, b_ada, norm_mix, norm_ffn, w_attn_qkv, attn_q_norm,
           attn_k_norm, attn_sinks, w_attn_o, w_gdn_in, gdn_conv_w, gdn_a_log, gdn_dt_bias, gdn_out_norm,
           w_gdn_out, w_ffn_up, ffn_conv_w, ffn_conv_b, w_ffn_down):
    P = dict(rel_bias_table=rel_bias_table, norm_mix=norm_mix, norm_ffn=norm_ffn, w_attn_qkv=w_attn_qkv,
             attn_q_norm=attn_q_norm, attn_k_norm=attn_k_norm, attn_sinks=attn_sinks, w_attn_o=w_attn_o,
             w_gdn_in=w_gdn_in, gdn_conv_w=gdn_conv_w, gdn_a_log=gdn_a_log, gdn_dt_bias=gdn_dt_bias,
             gdn_out_norm=gdn_out_norm, w_gdn_out=w_gdn_out, w_ffn_up=w_ffn_up, ffn_conv_w=ffn_conv_w,
             ffn_conv_b=ffn_conv_b, w_ffn_down=w_ffn_down)
    bp = x_prompt.shape[0]
    bs, ts, d = x_sample.shape
    n_c = bp + bs
    c_rows = -(-n_c // SUBLANES) * SUBLANES
    c_all = jnp.pad(jnp.concatenate([c_prompt, c_sample], axis=0), ((0, c_rows - n_c), (0, 0)))
    mod = ada_modulation(c_all, w_ada, b_ada)
    mod_prompt = mod[:, :bp]
    mod_sample = jnp.tile(mod[:, bp:n_c], (1, ts, 1))

    y_p, new_p = _trunk(x_prompt, mod_prompt, None, P, sample=False)
    states = dict(win_k=cache_win_k, win_v=cache_win_v, gdn=state_gdn, gdn_conv=state_gdn_conv,
                  ffn_conv=state_ffn_conv)
    y_s, new_s = _trunk(_to_time_major(x_sample)[None], mod_sample, states, P, sample=True)
    y_s = _from_time_major(y_s[0], bs)
    return (y_p, y_s, new_p['win_k'], new_p['win_v'], new_s['win_k'], new_s['win_v'],
            new_p['gdn'], new_s['gdn'], new_p['gdn_conv'], new_s['gdn_conv'],
            new_p['ffn_conv'], new_s['ffn_conv'])
```
